```python
import math
import jax, jax.numpy as jnp
from jax import lax
import numpy as np

D_MODEL = 2048
BATCH = 32
SEQ = 256
DEPTH = 2
DEC_BATCH = 8
DEC_SEQ = 4096
PAST_LEN = 512

GRID_W = 64
MIX_W = D_MODEL // 4
HEAD_DIM = 64
D_FF = 4 * D_MODEL
NORM_EPS = 1e-6

GLA_H = MIX_W // HEAD_DIM
GLA_DK = HEAD_DIM // 2
GLA_DV = HEAD_DIM
GLA_RANK = 16
GLA_TAU = 16.0
GLA_CHUNK = 64

DIFF_H = MIX_W // HEAD_DIM
DIFF_DK = HEAD_DIM // 2
DIFF_DV = HEAD_DIM
Q_BLOCK = 128
ROPE_BASE = 10000.0

SSD_H = MIX_W // HEAD_DIM
SSD_P = HEAD_DIM
SSD_N = 64
SSD_G = 2
SSD_CONV = 3
SSD_CHUNK = 128

HY_CH = MIX_W
HY_SHORT = 3
HY_BANDS = 16
HY_EMB = 2 * HY_BANDS + 1
HY_HID = 64

GLA_QK = GLA_H * GLA_DK
GLA_V = GLA_H * GLA_DV
DIFF_QK = DIFF_H * 2 * DIFF_DK
DIFF_V = DIFF_H * DIFF_DV
SSD_DI = SSD_H * SSD_P
SSD_BC = SSD_G * SSD_N
SSD_XBC = SSD_DI + 2 * SSD_BC
HY_W = 3 * HY_CH
IN_SIZES = (GLA_QK, GLA_QK, GLA_V, GLA_V, 2 * GLA_RANK,
            DIFF_QK, DIFF_QK, DIFF_V,
            SSD_DI, SSD_XBC, 2 * SSD_H,
            HY_W)
IN_W = sum(IN_SIZES)

kernel_name = 'hybrid_prefix_diffusion_step'

f32 = jnp.float32


def rms_norm(x, g):
    xf = x.astype(f32)
    y = xf * lax.rsqrt(jnp.mean(xf * xf, axis=-1, keepdims=True) + NORM_EPS)
    return (y * g.astype(f32)).astype(x.dtype)


def split_cols(t, sizes):
    offs = np.cumsum(sizes)[:-1].tolist()
    return jnp.split(t, offs, axis=-1)


def flip(t):
    return jnp.flip(t, axis=1)


def dwconv_centred(x, w, b):
    K = w.shape[0]
    L = x.shape[1]
    xp = jnp.pad(x, ((0, 0), (K // 2, K // 2), (0, 0)))
    return sum(xp[:, i:i + L] * w[i] for i in range(K)) + b


def adaln(cvec, w, b):
    m = jax.nn.silu(cvec) @ w + b
    return jnp.split(m[:, None, :], 6, axis=-1)


def axial_rope(L):
    rows = L // GRID_W
    r, col = jnp.meshgrid(jnp.arange(rows), jnp.arange(GRID_W), indexing='ij')
    r = r.reshape(-1).astype(f32)
    col = col.reshape(-1).astype(f32)
    nf = DIFF_DK // 4
    inv = ROPE_BASE ** (-jnp.arange(nf, dtype=f32) / nf)
    ang = jnp.concatenate([r[:, None] * inv, col[:, None] * inv], axis=-1)
    return jnp.cos(ang), jnp.sin(ang)


def apply_rope(x, cos, sin):
    half = x.shape[-1] // 2
    x1, x2 = x[..., :half], x[..., half:]
    c = cos[:, None, None, :]
    s = sin[:, None, None, :]
    return jnp.concatenate([x1 * c - x2 * s, x1 * s + x2 * c], axis=-1).astype(x.dtype)


def gla_scan(q, k, v, log_a, s0):
    B, L, H, dk = q.shape
    dv = v.shape[-1]
    C = GLA_CHUNK
    nc = L // C
    qc = q.astype(f32).reshape(B, nc, C, H, dk)
    kc = k.astype(f32).reshape(B, nc, C, H, dk)
    vc = v.astype(f32).reshape(B, nc, C, H, dv)
    gc = log_a.astype(f32).reshape(B, nc, C, H, dk)
    b = jnp.cumsum(gc, axis=2)
    btot = b[:, :, -1]
    q_dec = qc * jnp.exp(b)
    att = jnp.einsum('bnihd,bnjhd->bnhij', q_dec, kc * jnp.exp(-b))
    att = jnp.where(jnp.tril(jnp.ones((C, C), bool)), att, 0.0)
    o_intra = jnp.einsum('bnhij,bnjhv->bnihv', att, vc)
    kv = jnp.einsum('bnjhd,bnjhv->bnhdv', kc * jnp.exp(btot[:, :, None] - b), vc)

    def step(S, xs):
        qd, kvc, bt = xs
        o = jnp.einsum('bihd,bhdv->bihv', qd, S)
        return jnp.exp(bt)[..., None] * S + kvc, o

    S_fin, o_inter = lax.scan(step, s0.astype(f32),
                              (jnp.moveaxis(q_dec, 1, 0), jnp.moveaxis(kv, 1, 0), jnp.moveaxis(btot, 1, 0)))
    o = o_intra + jnp.moveaxis(o_inter, 0, 1)
    return o.reshape(B, L, H, dv).astype(v.dtype), S_fin


def ssd_scan(x, dt, a, bm, cm, s0):
    B, L, H, P = x.shape
    N = bm.shape[-1]
    C = SSD_CHUNK
    nc = L // C
    rep = H // bm.shape[2]
    bh = jnp.repeat(bm, rep, axis=2).astype(f32).reshape(B, nc, C, H, N)
    ch = jnp.repeat(cm, rep, axis=2).astype(f32).reshape(B, nc, C, H, N)
    xc = x.astype(f32).reshape(B, nc, C, H, P)
    dtc = dt.astype(f32).reshape(B, nc, C, H)
    cum = jnp.cumsum(dtc * a, axis=2)
    causal = jnp.tril(jnp.ones((C, C), bool))[None, None, :, :, None]
    seg = cum[:, :, :, None, :] - cum[:, :, None, :, :]
    lmat = jnp.exp(jnp.where(causal, seg, -jnp.inf))
    w = jnp.einsum('bnihd,bnjhd->bnijh', ch, bh) * lmat * dtc[:, :, None, :, :]
    y_intra = jnp.einsum('bnijh,bnjhp->bnihp', w, xc)
    tail = jnp.exp(cum[:, :, -1:, :] - cum) * dtc
    st = jnp.einsum('bnjhd,bnjh,bnjhp->bnhpd', bh, tail, xc)
    chunk_decay = jnp.exp(cum[:, :, -1])
    c_dec = ch * jnp.exp(cum)[..., None]

    def step(S, xs):
        cd, stc, dec = xs
        y = jnp.einsum('bihd,bhpd->bihp', cd, S)
        return dec[:, :, None, None] * S + stc, y

    S_fin, y_inter = lax.scan(step, s0.astype(f32),
                              (jnp.moveaxis(c_dec, 1, 0), jnp.moveaxis(st, 1, 0), jnp.moveaxis(chunk_decay, 1, 0)))
    y = (y_intra + jnp.moveaxis(y_inter, 0, 1)).reshape(B, L, H, P)
    return y.astype(x.dtype), S_fin


def diff_attention(q, k, v, lam):
    B, Lq, H, _, dk = q.shape
    nb = Lq // Q_BLOCK
    qb = jnp.moveaxis(q.reshape(B, nb, Q_BLOCK, H, 2, dk), 1, 0)
    kf = k.astype(f32) * dk ** -0.5
    vf = v.astype(f32)

    def block(qblk):
        s = jnp.einsum('bqhcd,bkhcd->bhcqk', qblk.astype(f32), kf)
        pr = jax.nn.softmax(s, axis=-1)
        amap = pr[:, :, 0] - lam * pr[:, :, 1]
        return jnp.einsum('bhqk,bkhd->bqhd', amap, vf)

    o = lax.map(block, qb)
    return jnp.moveaxis(o, 0, 1).reshape(B, Lq, H, v.shape[-1]).astype(v.dtype)


def hyena_filters(L, p):
    t = (jnp.arange(L, dtype=f32) / L)[:, None]
    bands = jnp.arange(1, HY_BANDS + 1, dtype=f32)
    ang = 2.0 * math.pi * t * bands
    z = jnp.concatenate([t, jnp.cos(ang), jnp.sin(ang)], axis=-1)
    hdn = jnp.sin(p['hy_sin_w'] * (z @ p['hy_f_w1'] + p['hy_f_b1']))
    hdn = jnp.sin(p['hy_sin_w'] * (hdn @ p['hy_f_w2'] + p['hy_f_b2']))
    h = (hdn @ p['hy_f_w3'] + p['hy_f_b3']).astype(f32) * jnp.exp(-t * jnp.abs(p['hy_decay'].astype(f32)))
    h = h.reshape(L, 2, 2, HY_CH)
    return h / (jnp.sum(jnp.abs(h), axis=(0, 2), keepdims=True) + NORM_EPS)


def fft_long_conv(u, hf, hb, skip):
    L = u.shape[1]
    kern = jnp.concatenate([hf, jnp.zeros_like(hf[:1]), hb[:0:-1]], axis=0)
    kf = jnp.fft.rfft(kern, n=2 * L, axis=0)
    uf = jnp.fft.rfft(u.astype(f32), n=2 * L, axis=1)
    y = jnp.fft.irfft(uf * kf[None], n=2 * L, axis=1)[:, :L]
    return (y + u.astype(f32) * skip.astype(f32)).astype(u.dtype)


def mixer_block(u, p, lam_init, rope, ctx):
    B, L, _ = u.shape
    (gq, gk, gv, gg, gr, dq, dkk, dvv, sz, sxbc, sdt, hy) = split_cols(u @ p['w_in'], IN_SIZES)
    if ctx is None:
        gla_s0 = jnp.zeros((B, 2, GLA_H, GLA_DK, GLA_DV), f32)
        ssd_s0 = jnp.zeros((B, 2, SSD_H, SSD_P, SSD_N), f32)
    else:
        ctx_k, ctx_v, gla_s0, ssd_s0 = ctx

    q = gq.reshape(B, L, GLA_H, GLA_DK) * GLA_DK ** -0.5
    k = gk.reshape(B, L, GLA_H, GLA_DK)
    v = gv.reshape(B, L, GLA_H, GLA_DV)
    logits = jnp.einsum('bldr,drk->bldk', gr.reshape(B, L, 2, GLA_RANK), p['gla_gate_w']) + p['gla_gate_b']
    log_a = (jax.nn.log_sigmoid(logits.astype(f32)) / GLA_TAU).reshape(B, L, 2, GLA_H, GLA_DK)
    o_f, gsf = gla_scan(q, k, v, log_a[:, :, 0], gla_s0[:, 0])
    o_b, gsb = gla_scan(flip(q), flip(k), flip(v), flip(log_a[:, :, 1]), gla_s0[:, 1])
    o_gla = rms_norm(o_f + flip(o_b), p['gla_norm_g']).reshape(B, L, GLA_V) * jax.nn.silu(gg)

    q = dq.reshape(B, L, DIFF_H, 2, DIFF_DK)
    k = dkk.reshape(B, L, DIFF_H, 2, DIFF_DK)
    v = dvv.reshape(B, L, DIFF_H, DIFF_DV)
    if rope is not None:
        q = apply_rope(q, rope[0], rope[1])
        k = apply_rope(k, rope[0], rope[1])
    if ctx is None:
        keys, vals = k, v
    else:
        keys = jnp.concatenate([k, ctx_k.reshape(B, -1, DIFF_H, 2, DIFF_DK).astype(k.dtype)], axis=1)
        vals = jnp.concatenate([v, ctx_v.astype(v.dtype)], axis=1)
    lp = p['diff_lambda'].astype(f32)
    lam = jnp.exp(jnp.sum(lp[0] * lp[1])) - jnp.exp(jnp.sum(lp[2] * lp[3])) + lam_init
    o_diff = rms_norm(diff_attention(q, keys, vals, lam), p['diff_norm_g']) * (1.0 - lam_init)
    o_diff = o_diff.reshape(B, L, DIFF_V)

    xbc = jax.nn.silu(dwconv_centred(sxbc, p['ssd_conv_w'], p['ssd_conv_b']))
    xs, bm, cm = split_cols(xbc, (SSD_DI, SSD_BC, SSD_BC))
    xs = xs.reshape(B, L, SSD_H, SSD_P)
    bm = bm.reshape(B, L, SSD_G, SSD_N)
    cm = cm.reshape(B, L, SSD_G, SSD_N)
    dt = jax.nn.softplus(sdt.reshape(B, L, 2, SSD_H).astype(f32) + p['ssd_dt_bias'].astype(f32))
    a = -jnp.exp(p['ssd_a_log'].astype(f32))
    y_f, ssf = ssd_scan(xs, dt[:, :, 0], a[0], bm, cm, ssd_s0[:, 0])
    y_b, ssb = ssd_scan(flip(xs), flip(dt[:, :, 1]), a[1], flip(bm), flip(cm), ssd_s0[:, 1])
    y = y_f + flip(y_b) + xs * p['ssd_d'][:, None]
    o_ssd = rms_norm(y.reshape(B, L, SSD_DI) * jax.nn.silu(sz), p['ssd_norm_g'])

    hv, hx1, hx2 = split_cols(dwconv_centred(hy, p['hy_conv_w'], p['hy_conv_b']), (HY_CH, HY_CH, HY_CH))
    filt = hyena_filters(L, p)
    zz = hx1 * fft_long_conv(hv, filt[:, 0, 0], filt[:, 0, 1], p['hy_skip'][0])
    o_hy = hx2 * fft_long_conv(zz, filt[:, 1, 0], filt[:, 1, 1], p['hy_skip'][1])

    out = jnp.concatenate([o_gla, o_diff, o_ssd, o_hy], axis=-1) @ p['w_out']
    new_ctx = (k.reshape(B, L, DIFF_H, 2 * DIFF_DK), v,
               jnp.stack([gsf, gsb], axis=1), jnp.stack([ssf, ssb], axis=1))
    return out, new_ctx


def trunk_layer(x, cvec, p, lam_init, rope, ctx):
    sh1, sc1, g1, sh2, sc2, g2 = adaln(cvec, p['ada_w'], p['ada_b'])
    u = rms_norm(x, p['norm1_g']) * (1.0 + sc1) + sh1
    mix, new_ctx = mixer_block(u, p, lam_init, rope, ctx)
    x = x + g1 * mix
    u = rms_norm(x, p['norm2_g']) * (1.0 + sc2) + sh2
    x = x + g2 * (jnp.square(jax.nn.relu(u @ p['mlp_w1'])) @ p['mlp_w2'])
    return x, new_ctx


def setup_inputs(seed: int = 0) -> dict:
    key = jax.random.key(seed)
    ks = iter(jax.random.split(key, 48))

    def nrm(shape, scale):
        return jax.random.normal(next(ks), shape, f32) * scale

    def unif(shape, lo, hi):
        return jax.random.uniform(next(ks), shape, f32, lo, hi)

    dt0 = jnp.exp(unif((DEPTH, 2, SSD_H), math.log(1e-3), math.log(1e-1)))
    return {
        'x_prompt': nrm((BATCH, SEQ, D_MODEL), 1.0),
        'x_sample': nrm((DEC_BATCH, DEC_SEQ, D_MODEL), 1.0),
        'cache_diff_k': nrm((DEC_BATCH, DEPTH, PAST_LEN, DIFF_H, 2 * DIFF_DK), 1.0),
        'cache_diff_v': nrm((DEC_BATCH, DEPTH, PAST_LEN, DIFF_H, DIFF_DV), 1.0),
        'state_gla': nrm((DEC_BATCH, DEPTH, 2, GLA_H, GLA_DK, GLA_DV), 0.5),
        'state_ssd': nrm((DEC_BATCH, DEPTH, 2, SSD_H, SSD_P, SSD_N), 0.5),
        'c': nrm((DEC_BATCH, D_MODEL), 1.0),
        'c_ctx': nrm((D_MODEL,), 1.0),
        'ada_w': nrm((DEPTH, D_MODEL, 6 * D_MODEL), D_MODEL ** -0.5),
        'ada_b': nrm((DEPTH, 6 * D_MODEL), 0.02),
        'norm1_g': 1.0 + nrm((DEPTH, D_MODEL), 0.02),
        'norm2_g': 1.0 + nrm((DEPTH, D_MODEL), 0.02),
        'w_in': nrm((DEPTH, D_MODEL, IN_W), D_MODEL ** -0.5),
        'w_out': nrm((DEPTH, D_MODEL, D_MODEL), D_MODEL ** -0.5),
        'gla_gate_w': nrm((DEPTH, 2, GLA_RANK, GLA_QK), GLA_RANK ** -0.5),
        'gla_gate_b': nrm((DEPTH, 2, GLA_QK), 0.1),
        'gla_norm_g': 1.0 + nrm((DEPTH, GLA_DV), 0.02),
        'diff_lambda': nrm((DEPTH, 4, DIFF_DK), 0.1),
        'diff_norm_g': 1.0 + nrm((DEPTH, DIFF_DV), 0.02),
        'ssd_conv_w': nrm((DEPTH, SSD_CONV, SSD_XBC), SSD_CONV ** -0.5),
        'ssd_conv_b': nrm((DEPTH, SSD_XBC), 0.02),
        'ssd_dt_bias': dt0 + jnp.log(-jnp.expm1(-dt0)),
        'ssd_a_log': jnp.log(unif((DEPTH, 2, SSD_H), 1.0, 16.0)),
        'ssd_d': 1.0 + nrm((DEPTH, SSD_H), 0.1),
        'ssd_norm_g': 1.0 + nrm((DEPTH, SSD_DI), 0.02),
        'hy_conv_w': nrm((DEPTH, HY_SHORT, HY_W), HY_SHORT ** -0.5),
        'hy_conv_b': nrm((DEPTH, HY_W), 0.02),
        'hy_f_w1': nrm((DEPTH, HY_EMB, HY_HID), HY_EMB ** -0.5),
        'hy_f_b1': nrm((DEPTH, HY_HID), 0.1),
        'hy_f_w2': nrm((DEPTH, HY_HID, HY_HID), HY_HID ** -0.5),
        'hy_f_b2': nrm((DEPTH, HY_HID), 0.1),
        'hy_f_w3': nrm((DEPTH, HY_HID, 4 * HY_CH), HY_HID ** -0.5),
        'hy_f_b3': nrm((DEPTH, 4 * HY_CH), 0.1),
        'hy_sin_w': 1.0 + nrm((DEPTH, HY_HID), 0.1),
        'hy_decay': unif((DEPTH, 4 * HY_CH), 3.0, 15.0),
        'hy_skip': nrm((DEPTH, 2, HY_CH), 0.1),
        'mlp_w1': nrm((DEPTH, D_MODEL, D_FF), D_MODEL ** -0.5),
        'mlp_w2': nrm((DEPTH, D_FF, D_MODEL), D_FF ** -0.5),
        'final_g': 1.0 + nrm((D_MODEL,), 0.02),
    }


def reference(x_prompt, x_sample, cache_diff_k, cache_diff_v, state_gla, state_ssd, c, c_ctx,
              ada_w, ada_b, norm1_g, norm2_g, w_in, w_out,
              gla_gate_w, gla_gate_b, gla_norm_g, diff_lambda, diff_norm_g,
              ssd_conv_w, ssd_conv_b, ssd_dt_bias, ssd_a_log, ssd_d, ssd_norm_g,
              hy_conv_w, hy_conv_b, hy_f_w1, hy_f_b1, hy_f_w2, hy_f_b2, hy_f_w3, hy_f_b3,
              hy_sin_w, hy_decay, hy_skip, mlp_w1, mlp_w2, final_g):
    rope = axial_rope(x_sample.shape[1])
    hp = x_prompt
    hs = x_sample
    ks_, vs_, gs_, ss_ = [], [], [], []
    for l in range(DEPTH):
        p = dict(ada_w=ada_w[l], ada_b=ada_b[l], norm1_g=norm1_g[l], norm2_g=norm2_g[l],
                 w_in=w_in[l], w_out=w_out[l],
                 gla_gate_w=gla_gate_w[l], gla_gate_b=gla_gate_b[l], gla_norm_g=gla_norm_g[l],
                 diff_lambda=diff_lambda[l], diff_norm_g=diff_norm_g[l],
                 ssd_conv_w=ssd_conv_w[l], ssd_conv_b=ssd_conv_b[l], ssd_dt_bias=ssd_dt_bias[l],
                 ssd_a_log=ssd_a_log[l], ssd_d=ssd_d[l], ssd_norm_g=ssd_norm_g[l],
                 hy_conv_w=hy_conv_w[l], hy_conv_b=hy_conv_b[l], hy_f_w1=hy_f_w1[l], hy_f_b1=hy_f_b1[l],
                 hy_f_w2=hy_f_w2[l], hy_f_b2=hy_f_b2[l], hy_f_w3=hy_f_w3[l], hy_f_b3=hy_f_b3[l],
                 hy_sin_w=hy_sin_w[l], hy_decay=hy_decay[l], hy_skip=hy_skip[l],
                 mlp_w1=mlp_w1[l], mlp_w2=mlp_w2[l])
        lam_init = 0.8 - 0.6 * math.exp(-0.3 * l)
        hp, (k_l, v_l, g_l, s_l) = trunk_layer(hp, c_ctx[None], p, lam_init, None, None)
        ks_.append(k_l)
        vs_.append(v_l)
        gs_.append(g_l)
        ss_.append(s_l)
        hs, _ = trunk_layer(hs, c, p, lam_init, rope,
                            (cache_diff_k[:, l], cache_diff_v[:, l], state_gla[:, l], state_ssd[:, l]))
    y_prompt = rms_norm(hp, final_g)
    y_sample = rms_norm(hs, final_g)
    new_diff_k = jnp.stack(ks_, axis=1)
    new_diff_v = jnp.stack(vs_, axis=1)
    new_gla = jnp.stack(gs_, axis=1)
    new_ssd = jnp.stack(ss_, axis=1)
    return (y_prompt, y_sample, new_diff_k, new_diff_v, new_gla, new_ssd)
```

```python
import functools
import math

import numpy as np
import jax
import jax.numpy as jnp
from jax import lax
from jax.experimental import pallas as pl
from jax.experimental.pallas import tpu as pltpu

f32 = jnp.float32
bf16 = jnp.bfloat16
HI = lax.Precision.HIGHEST

LANES = 128
VMEM_LIMIT = 56 * 2**20

D_MODEL = 2048
GRID_W = 64
MIX_W = D_MODEL // 4
HEAD_DIM = 64
NH = MIX_W // HEAD_DIM
D_FF = 4 * D_MODEL
NORM_EPS = 1e-6
GLA_DK = HEAD_DIM // 2
GLA_RANK = 16
GLA_TAU = 16.0
GLA_CHUNK = 64
DIFF_DK = HEAD_DIM // 2
ROPE_BASE = 10000.0
SSD_N = 64
SSD_G = 2
SSD_CHUNK = 128
HY_BANDS = 16
HY_EMB = 2 * HY_BANDS + 1
HY_EMB_PAD = 40
HY_HID = 64
HY_TB = 128

PT = 512
(T_GQK, T_GV, T_GG, T_DQ, T_DK, T_DV, T_SZ, T_SX, T_MISC) = range(9)
N_PT = 9
MISC_GR = 256
MISC_DT = 384
SH1, SC1, G1, SH2, SC2, G2 = range(6)

NT_DIMS = (((1,), (1,)), ((), ()))
TN_DIMS = (((0,), (0,)), ((), ()))


def _cparams(*sem):
    return pltpu.CompilerParams(dimension_semantics=sem, vmem_limit_bytes=VMEM_LIMIT)


def _sigmoid(x):
    return 1.0 / (1.0 + jnp.exp(-x))


def _silu(x):
    return x * _sigmoid(x)


def _softplus(x):
    return jnp.maximum(x, 0.0) + jnp.log1p(jnp.exp(-jnp.abs(x)))


def _log_sigmoid(x):
    return jnp.minimum(x, 0.0) - jnp.log1p(jnp.exp(-jnp.abs(x)))


def _tri_mask(rows, c, sgn):
    r_i = lax.broadcasted_iota(jnp.int32, (rows, c), 0) & (c - 1)
    c_i = lax.broadcasted_iota(jnp.int32, (rows, c), 1)
    return (r_i - c_i) * sgn >= 0


def _dot(a, b, dims=None, precision=None):
    if dims is None:
        return jnp.dot(a, b, preferred_element_type=f32, precision=precision)
    return lax.dot_general(a, b, dims, preferred_element_type=f32, precision=precision)


def _ada_kernel(c_ref, w_ref, b_ref, o_ref):
    s = _silu(c_ref[...]).astype(bf16)
    o_ref[...] = _dot(s, w_ref[...].astype(bf16)) + b_ref[...]


def ada_mod(cvec, w, b, tn=1024):
    m, d = cvec.shape
    n = w.shape[1]
    return pl.pallas_call(
        _ada_kernel,
        out_shape=jax.ShapeDtypeStruct((m, n), f32),
        grid=(n // tn,),
        in_specs=[pl.BlockSpec((m, d), lambda j: (0, 0)),
                  pl.BlockSpec((d, tn), lambda j: (0, j)),
                  pl.BlockSpec((1, tn), lambda j: (0, j))],
        out_specs=pl.BlockSpec((m, tn), lambda j: (0, j)),
        compiler_params=_cparams("arbitrary"),
        name="ada_mod",
    )(cvec, w, b.reshape(1, n))


def _norm_mod(x, g, sc, sh):
    ms = jnp.mean(x * x, axis=-1, keepdims=True)
    return (x * lax.rsqrt(ms + NORM_EPS) * g) * (1.0 + sc) + sh


def _inproj_kernel(x_ref, mod_ref, g_ref, wt_ref, wh_ref, p_ref, h_ref, u_scr):
    j = pl.program_id(1)

    @pl.when(j == 0)
    def _():
        u = _norm_mod(x_ref[...], g_ref[...], mod_ref[SC1:SC1 + 1, :], mod_ref[SH1:SH1 + 1, :])
        u_scr[...] = u.astype(bf16)

    @pl.when(j < N_PT)
    def _():
        p_ref[...] = _dot(u_scr[...], wt_ref[...])

    @pl.when(j >= N_PT)
    def _():
        h_ref[...] = _dot(wh_ref[...], u_scr[...], NT_DIMS)


def in_proj(x, mod, g, w_tok, w_hyT, rows_per_mod, tm=512):
    t, d = x.shape
    n_hy = w_hyT.shape[0] // PT
    return pl.pallas_call(
        _inproj_kernel,
        out_shape=(jax.ShapeDtypeStruct((t, N_PT * PT), f32), jax.ShapeDtypeStruct((n_hy * PT, t), f32)),
        grid=(t // tm, N_PT + n_hy),
        in_specs=[pl.BlockSpec((tm, d), lambda i, j: (i, 0)),
                  pl.BlockSpec((None, 6, d), lambda i, j: ((i * tm) // rows_per_mod, 0, 0)),
                  pl.BlockSpec((1, d), lambda i, j: (0, 0)),
                  pl.BlockSpec((d, PT), lambda i, j: (0, jnp.minimum(j, N_PT - 1))),
                  pl.BlockSpec((PT, d), lambda i, j: (jnp.maximum(j - N_PT, 0), 0))],
        out_specs=(pl.BlockSpec((tm, PT), lambda i, j: (i, jnp.minimum(j, N_PT - 1))),
                   pl.BlockSpec((PT, tm), lambda i, j: (jnp.maximum(j - N_PT, 0), i))),
        scratch_shapes=[pltpu.VMEM((tm, d), bf16)],
        compiler_params=_cparams("parallel", "arbitrary"),
        name="in_proj",
    )(x, mod, g.reshape(1, d), w_tok, w_hyT)


def _outproj_kernel(x_ref, mod_ref, a_ref, b_ref, c_ref, dT_ref, w_ref, o_ref):
    acc = _dot(a_ref[...], w_ref[0:MIX_W, :])
    acc += _dot(b_ref[...], w_ref[MIX_W:2 * MIX_W, :])
    acc += _dot(c_ref[...], w_ref[2 * MIX_W:3 * MIX_W, :])
    acc += _dot(dT_ref[...].astype(bf16), w_ref[3 * MIX_W:4 * MIX_W, :], TN_DIMS)
    o_ref[...] = x_ref[...] + mod_ref[G1:G1 + 1, :] * acc


def out_proj(x, mod, o_a, o_b, o_c, o_dT, w_out, rows_per_mod, tm=512):
    t, d = x.shape
    tok = lambda i: (i, 0)
    return pl.pallas_call(
        _outproj_kernel,
        out_shape=jax.ShapeDtypeStruct((t, d), f32),
        grid=(t // tm,),
        in_specs=[pl.BlockSpec((tm, d), tok),
                  pl.BlockSpec((None, 6, d), lambda i: ((i * tm) // rows_per_mod, 0, 0)),
                  pl.BlockSpec((tm, MIX_W), tok), pl.BlockSpec((tm, MIX_W), tok), pl.BlockSpec((tm, MIX_W), tok),
                  pl.BlockSpec((MIX_W, tm), lambda i: (0, i)),
                  pl.BlockSpec((d, d), lambda i: (0, 0))],
        out_specs=pl.BlockSpec((tm, d), tok),
        compiler_params=_cparams("parallel"),
        name="out_proj",
    )(x, mod, o_a, o_b, o_c, o_dT, w_out)


def _mlp_kernel(x_ref, mod_ref, g_ref, w1_ref, w2_ref, fg_ref, o_ref, u_scr, acc_scr, *, final_norm):
    j = pl.program_id(1)

    @pl.when(j == 0)
    def _():
        u = _norm_mod(x_ref[...], g_ref[...], mod_ref[SC2:SC2 + 1, :], mod_ref[SH2:SH2 + 1, :])
        u_scr[...] = u.astype(bf16)
        acc_scr[...] = jnp.zeros_like(acc_scr)

    h = jnp.maximum(_dot(u_scr[...], w1_ref[...]), 0.0)
    acc_scr[...] += _dot((h * h).astype(bf16), w2_ref[...])

    @pl.when(j == pl.num_programs(1) - 1)
    def _():
        y = x_ref[...] + mod_ref[G2:G2 + 1, :] * acc_scr[...]
        if final_norm:
            ms = jnp.mean(y * y, axis=-1, keepdims=True)
            y = y * lax.rsqrt(ms + NORM_EPS) * fg_ref[...]
        o_ref[...] = y


def mlp(x, mod, g, w1, w2, final_g, rows_per_mod, final_norm, tm=512, tf=1024):
    t, d = x.shape
    dff = w1.shape[1]
    return pl.pallas_call(
        functools.partial(_mlp_kernel, final_norm=final_norm),
        out_shape=jax.ShapeDtypeStruct((t, d), f32),
        grid=(t // tm, dff // tf),
        in_specs=[pl.BlockSpec((tm, d), lambda i, j: (i, 0)),
                  pl.BlockSpec((None, 6, d), lambda i, j: ((i * tm) // rows_per_mod, 0, 0)),
                  pl.BlockSpec((1, d), lambda i, j: (0, 0)),
                  pl.BlockSpec((d, tf), lambda i, j: (0, j)),
                  pl.BlockSpec((tf, d), lambda i, j: (j, 0)),
                  pl.BlockSpec((1, d), lambda i, j: (0, 0))],
        out_specs=pl.BlockSpec((tm, d), lambda i, j: (i, 0)),
        scratch_shapes=[pltpu.VMEM((tm, d), bf16), pltpu.VMEM((tm, d), f32)],
        compiler_params=_cparams("parallel", "arbitrary"),
        name="mlp",
    )(x, mod, g.reshape(1, d), w1, w2, final_g.reshape(1, d))


def _gla_kernel(qk_ref, v_ref, gg_ref, misc_ref, gw_ref, gb_ref, ng_ref, hsum_ref, s0_ref,
                o_ref, sfin_ref, of_scr, st_scr, la_scr, *, nblk, tl):
    ph = pl.program_id(1)
    i = pl.program_id(2)
    fwd = ph == 0
    c = GLA_CHUNK
    nch = tl // c
    qkw = NH * GLA_DK

    @pl.when(i == 0)
    def _():
        st_scr[...] = s0_ref[...]

    logits = _dot(misc_ref[:, MISC_GR:MISC_GR + LANES], gw_ref[...], precision=HI) + gb_ref[...]
    la_all = _log_sigmoid(logits) * (1.0 / GLA_TAU)
    la_scr[...] = jnp.where(fwd, la_all[:, :qkw], la_all[:, qkw:])

    sgn = jnp.where(fwd, 1, -1)
    tri = _tri_mask(c, c, sgn)
    trif = tri.astype(f32)
    tri8 = _tri_mask(NH * c, c, sgn)
    qk_head = lax.broadcasted_iota(jnp.int32, (1, qkw), 1) // GLA_DK
    v_head = lax.broadcasted_iota(jnp.int32, (1, MIX_W), 1) // HEAD_DIM
    bd = (lax.broadcasted_iota(jnp.int32, (MIX_W, qkw), 0) // HEAD_DIM
          == lax.broadcasted_iota(jnp.int32, (MIX_W, qkw), 1) // GLA_DK)
    blk = jnp.where(fwd, i, nblk - 1 - i)

    def chunk(j, carry):
        cj = jnp.where(fwd, j, nch - 1 - j)
        off = pl.multiple_of(cj * c, c)
        q = qk_ref[pl.ds(off, c), 0:qkw] * (GLA_DK ** -0.5)
        k = qk_ref[pl.ds(off, c), qkw:2 * qkw]
        vb = v_ref[pl.ds(off, c), :].astype(bf16)
        b = _dot(trif, la_scr[pl.ds(off, c), :], precision=HI)
        btot = jnp.where(fwd, b[c - 1:c, :], b[0:1, :])
        qd = q * jnp.exp(b)
        kd = (k * jnp.exp(-b)).astype(bf16)
        kt = (k * jnp.exp(btot - b)).astype(bf16)
        qst = jnp.concatenate([jnp.where(qk_head == h, qd, 0.0) for h in range(NH)], axis=0).astype(bf16)
        att = jnp.where(tri8, _dot(qst, kd, NT_DIMS), 0.0)
        r = _dot(att.astype(bf16), vb)
        o = jnp.where(v_head == 0, r[0:c], 0.0)
        for h in range(1, NH):
            o += jnp.where(v_head == h, r[h * c:(h + 1) * c], 0.0)
        st = st_scr[...]
        o += _dot(qd.astype(bf16), st.astype(bf16), NT_DIMS)
        kv = _dot(vb, kt, TN_DIMS)
        st_scr[...] = st * jnp.exp(btot) + jnp.where(bd, kv, 0.0)
        row = pl.multiple_of(blk * tl + off, c)

        @pl.when(fwd)
        def _():
            of_scr[pl.ds(row, c), :] = o

        @pl.when(ph == 1)
        def _():
            ot = of_scr[pl.ds(row, c), :] + o
            ss = _dot(ot * ot, hsum_ref[...], precision=HI) * (1.0 / HEAD_DIM)
            g = gg_ref[pl.ds(off, c), :]
            o_ref[pl.ds(off, c), :] = (ot * lax.rsqrt(ss + NORM_EPS) * ng_ref[...] * _silu(g)).astype(bf16)

        return carry

    lax.fori_loop(0, nch, chunk, 0)

    @pl.when(i == nblk - 1)
    def _():
        sfin_ref[...] = st_scr[...]


def gla_mixer(p, gw, gb, ng, hsum, s0, bsz, seq, tl=256):
    nblk = seq // tl
    qkw = NH * GLA_DK
    blk = lambda ph, i: jnp.where(ph == 0, i, nblk - 1 - i)
    tile = lambda col: pl.BlockSpec((tl, PT), lambda b, ph, i: (b * nblk + blk(ph, i), col))
    const = lambda shape: pl.BlockSpec(shape, lambda b, ph, i: (0,) * len(shape))
    state = pl.BlockSpec((None, None, MIX_W, qkw), lambda b, ph, i: (b, ph, 0, 0))
    return pl.pallas_call(
        functools.partial(_gla_kernel, nblk=nblk, tl=tl),
        out_shape=(jax.ShapeDtypeStruct((bsz * seq, MIX_W), bf16),
                   jax.ShapeDtypeStruct((bsz, 2, MIX_W, qkw), f32)),
        grid=(bsz, 2, nblk),
        in_specs=[tile(T_GQK), tile(T_GV), tile(T_GG), tile(T_MISC),
                  const((LANES, 2 * qkw)), const((1, 2 * qkw)), const((1, MIX_W)), const((MIX_W, MIX_W)), state],
        out_specs=(pl.BlockSpec((tl, MIX_W),
                                lambda b, ph, i: (b * nblk + jnp.where(ph == 0, nblk - 1, nblk - 1 - i), 0)),
                   state),
        scratch_shapes=[pltpu.VMEM((seq, MIX_W), f32), pltpu.VMEM((MIX_W, qkw), f32), pltpu.VMEM((tl, qkw), f32)],
        compiler_params=_cparams("parallel", "arbitrary", "arbitrary"),
        name="gla_mixer",
    )(p, p, p, p, gw, gb, ng, hsum, s0)


def _rope(x, cos, sin_signed):
    lane = lax.broadcasted_iota(jnp.int32, (1, LANES), 1)
    half = DIFF_DK // 2
    rot = jnp.where((lane % DIFF_DK) < half, pltpu.roll(x, LANES - half, 1), pltpu.roll(x, half, 1))
    return x * cos + rot * sin_signed


def _diff_kernel(*refs, seq, ctx_len, rope, lam_init):
    if rope:
        (q_ref, k_ref, v_ref, ck_ref, cv_ref, cq_ref, sq_ref, ckk_ref, skk_ref, lam_ref, ng_ref,
         o_ref, kb_scr, vb_scr) = refs
    else:
        q_ref, k_ref, v_ref, lam_ref, ng_ref, o_ref, kb_scr, vb_scr = refs
    qi = pl.program_id(2)

    @pl.when(qi == 0)
    def _():
        k = k_ref[...]
        if rope:
            k = _rope(k, ckk_ref[...], skk_ref[...])
        kb_scr[0:seq, :] = k.astype(bf16)
        vb_scr[0:seq, :] = v_ref[...].astype(bf16)
        if ctx_len:
            kb_scr[seq:seq + ctx_len, :] = ck_ref[...].astype(bf16)
            vb_scr[seq:seq + ctx_len, :] = cv_ref[...].astype(bf16)

    q = q_ref[...]
    if rope:
        q = _rope(q, cq_ref[...], sq_ref[...])
    q = q * (DIFF_DK ** -0.5)
    lane = lax.broadcasted_iota(jnp.int32, (1, LANES), 1)
    kb = kb_scr[...]
    vb = vb_scr[...]
    res = []
    for g in range(4):
        qm = jnp.where(lane // DIFF_DK == g, q, 0.0).astype(bf16)
        s = _dot(qm, kb, NT_DIMS)
        p = jnp.exp(s - jnp.max(s, axis=-1, keepdims=True))
        l = jnp.sum(p, axis=-1, keepdims=True)
        res.append(_dot(p.astype(bf16), vb) / l)
    lp = lam_ref[...]
    lam = (jnp.exp(jnp.sum(lp[0:1] * lp[1:2], keepdims=True)) - jnp.exp(jnp.sum(lp[2:3] * lp[3:4], keepdims=True))
           + lam_init)
    first = lane < HEAD_DIM
    o = jnp.where(first, res[0] - lam * res[1], res[2] - lam * res[3])
    o2 = o * o
    ss_a = jnp.sum(jnp.where(first, o2, 0.0), axis=-1, keepdims=True)
    ss_b = jnp.sum(jnp.where(first, 0.0, o2), axis=-1, keepdims=True)
    inv = jnp.where(first, lax.rsqrt(ss_a * (1.0 / HEAD_DIM) + NORM_EPS), lax.rsqrt(ss_b * (1.0 / HEAD_DIM) + NORM_EPS))
    o_ref[...] = (o * inv * ng_ref[...] * (1.0 - lam_init)).astype(bf16)


def diff_mixer(p, lam_p, ng, bsz, seq, lam_init, ctx=None, rope_tabs=None, tq=256):
    nq = seq // tq
    npair = MIX_W // LANES
    rope = rope_tabs is not None
    ctx_len = ctx[0].shape[1] if ctx is not None else 0
    assert rope == (ctx is not None)
    q_spec = pl.BlockSpec((tq, LANES), lambda b, hp, i: (b * nq + i, T_DQ * (PT // LANES) + hp))
    k_spec = pl.BlockSpec((seq, LANES), lambda b, hp, i: (b, T_DK * (PT // LANES) + hp))
    v_spec = pl.BlockSpec((seq, LANES), lambda b, hp, i: (b, T_DV * (PT // LANES) + hp))
    const = lambda shape: pl.BlockSpec(shape, lambda b, hp, i: (0,) * len(shape))
    args = [p, p, p]
    specs = [q_spec, k_spec, v_spec]
    if rope:
        c_spec = pl.BlockSpec((None, ctx_len, LANES), lambda b, hp, i: (b, 0, hp))
        tab_q = pl.BlockSpec((tq, LANES), lambda b, hp, i: (i, 0))
        args += [ctx[0], ctx[1], rope_tabs[0], rope_tabs[1], rope_tabs[0], rope_tabs[1]]
        specs += [c_spec, c_spec, tab_q, tab_q, const((seq, LANES)), const((seq, LANES))]
    args += [lam_p, ng]
    specs += [const((8, LANES)), const((1, LANES))]
    return pl.pallas_call(
        functools.partial(_diff_kernel, seq=seq, ctx_len=ctx_len, rope=rope, lam_init=lam_init),
        out_shape=jax.ShapeDtypeStruct((bsz * seq, MIX_W), bf16),
        grid=(bsz, npair, nq),
        in_specs=specs,
        out_specs=pl.BlockSpec((tq, LANES), lambda b, hp, i: (b * nq + i, hp)),
        scratch_shapes=[pltpu.VMEM((seq + ctx_len, LANES), bf16), pltpu.VMEM((seq + ctx_len, LANES), bf16)],
        compiler_params=_cparams("parallel", "parallel", "arbitrary"),
        name="diff_mixer",
    )(*args)


def _dwconv_kernel(x_ref, w_ref, b_ref, o_ref):
    x = x_ref[...]
    n = x.shape[0]
    row = lax.broadcasted_iota(jnp.int32, x.shape, 0)
    xm = jnp.where(row == 0, 0.0, pltpu.roll(x, 1, 0))
    xp = jnp.where(row == n - 1, 0.0, pltpu.roll(x, n - 1, 0))
    y = xm * w_ref[0:1, :] + x * w_ref[1:2, :] + xp * w_ref[2:3, :] + b_ref[...]
    o_ref[...] = _silu(y)


def ssd_conv(p, w, b, bsz, seq):
    ncol = w.shape[1] // LANES
    per = PT // LANES
    col = lambda j: jnp.where(j < per, T_SX * per + j, T_MISC * per + (j - per))
    return pl.pallas_call(
        _dwconv_kernel,
        out_shape=jax.ShapeDtypeStruct((bsz * seq, w.shape[1]), f32),
        grid=(bsz, ncol),
        in_specs=[pl.BlockSpec((seq, LANES), lambda bb, j: (bb, col(j))),
                  pl.BlockSpec((3, LANES), lambda bb, j: (0, j)),
                  pl.BlockSpec((1, LANES), lambda bb, j: (0, j))],
        out_specs=pl.BlockSpec((seq, LANES), lambda bb, j: (bb, j)),
        compiler_params=_cparams("parallel", "parallel"),
        name="ssd_conv",
    )(p, w, b.reshape(1, -1))


def _ssd_kernel(xs_ref, bc_ref, z_ref, dt_ref, dtb_ref, alog_ref, dexp_ref, ng_ref, exp_ref, selb_ref, selc_ref,
                s0_ref, o_ref, sfin_ref, yf_scr, st_scr, dt_scr, dta_scr, *, nblk, tl):
    ph = pl.program_id(1)
    i = pl.program_id(2)
    fwd = ph == 0
    c = SSD_CHUNK
    nch = tl // c
    rep = NH // SSD_G

    @pl.when(i == 0)
    def _():
        st_scr[...] = s0_ref[...]

    dtv = _softplus(dt_ref[...] + dtb_ref[...])
    dtv = jnp.where(fwd, dtv, pltpu.roll(dtv, LANES - NH, 1))
    av = -jnp.exp(alog_ref[...])
    av = jnp.where(fwd, av, pltpu.roll(av, LANES - NH, 1))
    dt_scr[...] = dtv
    dta_scr[...] = dtv * av

    tri = _tri_mask(c, c, jnp.where(fwd, 1, -1))
    trif = tri.astype(f32)
    head = lax.broadcasted_iota(jnp.int32, (1, MIX_W), 1) // HEAD_DIM
    bd = (lax.broadcasted_iota(jnp.int32, (MIX_W, MIX_W), 0) // SSD_N
          == lax.broadcasted_iota(jnp.int32, (MIX_W, MIX_W), 1) // HEAD_DIM)
    blk = jnp.where(fwd, i, nblk - 1 - i)

    def chunk(j, carry):
        cj = jnp.where(fwd, j, nch - 1 - j)
        off = pl.multiple_of(cj * c, c)
        dtc = dt_scr[pl.ds(off, c), :]
        cum = _dot(trif, dta_scr[pl.ds(off, c), :], precision=HI)
        cum_t = cum.T
        dt_t = dtc.T
        earg = _dot(cum, exp_ref[...], precision=HI)
        elast = jnp.where(fwd, earg[c - 1:c, :], earg[0:1, :])
        dtx = _dot(dtc, exp_ref[...], precision=HI)
        bc = bc_ref[pl.ds(off, c), :]
        bt = _dot(bc, selb_ref[...], precision=HI)
        ct = _dot(bc, selc_ref[...], precision=HI)
        x = xs_ref[pl.ds(off, c), :]
        xb = x.astype(bf16)
        st = st_scr[...]
        y = _dot((ct * jnp.exp(earg)).astype(bf16), st.astype(bf16))
        btail = (bt * jnp.exp(elast - earg) * dtx).astype(bf16)
        new = _dot(btail, xb, TN_DIMS)
        st_scr[...] = st * jnp.exp(elast) + jnp.where(bd, new, 0.0)
        btb = bt.astype(bf16)
        cb = [_dot(jnp.where(head == g * rep, ct, 0.0).astype(bf16), btb, NT_DIMS) for g in range(SSD_G)]
        ws = []
        for h in range(NH):
            seg = cum[:, h:h + 1] - cum_t[h:h + 1, :]
            lm = jnp.exp(jnp.where(tri, seg, -jnp.inf))
            ws.append((cb[h // rep] * lm * dt_t[h:h + 1, :]).astype(bf16))
        r = _dot(jnp.concatenate(ws, axis=0), xb)
        for h in range(NH):
            y += jnp.where(head == h, r[h * c:(h + 1) * c], 0.0)
        row = pl.multiple_of(blk * tl + off, c)

        @pl.when(fwd)
        def _():
            yf_scr[pl.ds(row, c), :] = y

        @pl.when(ph == 1)
        def _():
            yt = yf_scr[pl.ds(row, c), :] + y + x * dexp_ref[...]
            t = yt * _silu(z_ref[pl.ds(off, c), :])
            ms = jnp.mean(t * t, axis=-1, keepdims=True)
            o_ref[pl.ds(off, c), :] = (t * lax.rsqrt(ms + NORM_EPS) * ng_ref[...]).astype(bf16)

        return carry

    lax.fori_loop(0, nch, chunk, 0)

    @pl.when(i == nblk - 1)
    def _():
        sfin_ref[...] = st_scr[...]


def ssd_mixer(p, xbc, dtb, alog, dexp, ng, expand, selb, selc, s0, bsz, seq, tl=256):
    nblk = seq // tl
    blk = lambda ph, i: jnp.where(ph == 0, i, nblk - 1 - i)
    rows = lambda b, ph, i: b * nblk + blk(ph, i)
    const = lambda shape: pl.BlockSpec(shape, lambda b, ph, i: (0,) * len(shape))
    state = pl.BlockSpec((None, None, MIX_W, MIX_W), lambda b, ph, i: (b, ph, 0, 0))
    return pl.pallas_call(
        functools.partial(_ssd_kernel, nblk=nblk, tl=tl),
        out_shape=(jax.ShapeDtypeStruct((bsz * seq, MIX_W), bf16),
                   jax.ShapeDtypeStruct((bsz, 2, MIX_W, MIX_W), f32)),
        grid=(bsz, 2, nblk),
        in_specs=[pl.BlockSpec((tl, MIX_W), lambda b, ph, i: (rows(b, ph, i), 0)),
                  pl.BlockSpec((tl, 2 * LANES), lambda b, ph, i: (rows(b, ph, i), MIX_W // (2 * LANES))),
                  pl.BlockSpec((tl, PT), lambda b, ph, i: (rows(b, ph, i), T_SZ)),
                  pl.BlockSpec((tl, LANES), lambda b, ph, i: (rows(b, ph, i), (T_MISC * PT + MISC_DT) // LANES)),
                  const((1, LANES)), const((1, LANES)), const((1, MIX_W)), const((1, MIX_W)),
                  const((LANES, MIX_W)), const((2 * LANES, MIX_W)), const((2 * LANES, MIX_W)), state],
        out_specs=(pl.BlockSpec((tl, MIX_W),
                                lambda b, ph, i: (b * nblk + jnp.where(ph == 0, nblk - 1, nblk - 1 - i), 0)),
                   state),
        scratch_shapes=[pltpu.VMEM((seq, MIX_W), f32), pltpu.VMEM((MIX_W, MIX_W), f32),
                        pltpu.VMEM((tl, LANES), f32), pltpu.VMEM((tl, LANES), f32)],
        compiler_params=_cparams("parallel", "arbitrary", "arbitrary"),
        name="ssd_mixer",
    )(xbc, xbc, p, p, dtb, alog, dexp, ng, expand, selb, selc, s0)


def _hy_filter_kernel(zf_ref, zb_ref, tf_ref, tb_ref, w1_ref, b1_ref, sw_ref, w2_ref, b2_ref,
                      w3f_ref, w3b_ref, b3f_ref, b3b_ref, df_ref, db_ref, skip_ref, r_ref, hf_scr, hb_scr, *, seq):
    first = jnp.logical_and(pl.program_id(0) == 0, pl.program_id(1) == 0)

    @pl.when(first)
    def _():
        for z_ref, h_scr in ((zf_ref, hf_scr), (zb_ref, hb_scr)):
            h = jnp.sin(sw_ref[...] * (_dot(w1_ref[...], z_ref[...], precision=HI) + b1_ref[...]))
            h_scr[...] = jnp.sin(sw_ref[...] * (_dot(w2_ref[...], h, precision=HI) + b2_ref[...]))

    hf = (_dot(w3f_ref[...], hf_scr[...], precision=HI) + b3f_ref[...]) * jnp.exp(-tf_ref[...] * jnp.abs(df_ref[...]))
    hb = (_dot(w3b_ref[...], hb_scr[...], precision=HI) + b3b_ref[...]) * jnp.exp(-tb_ref[...] * jnp.abs(db_ref[...]))
    den = (jnp.sum(jnp.abs(hf), axis=-1, keepdims=True) + jnp.sum(jnp.abs(hb), axis=-1, keepdims=True)) + NORM_EPS
    col = lax.broadcasted_iota(jnp.int32, (1, seq), 1)
    r_ref[:, 0:seq] = jnp.where(col == 0, 0.0, hb / den)
    r_ref[:, seq:2 * seq] = hf / den + jnp.where(col == 0, skip_ref[...], 0.0)


def hy_filters(fp, seq, cbf=128):
    t = jnp.arange(seq, dtype=f32) / seq
    t_rev = jnp.concatenate([t[:1], t[:0:-1]])

    def feats(tt):
        tc = tt[:, None]
        ang = 2.0 * math.pi * tc * jnp.arange(1, HY_BANDS + 1, dtype=f32)
        z = jnp.concatenate([tc, jnp.cos(ang), jnp.sin(ang)], axis=-1)
        return jnp.pad(z, ((0, 0), (0, HY_EMB_PAD - HY_EMB))).T

    ch = MIX_W
    nb = ch // cbf
    col = lambda a: a.reshape(-1, 1)
    w3t = fp['w3'].T.reshape(2, 2, ch, HY_HID)
    b3 = fp['b3'].reshape(2, 2, ch, 1)
    dec = fp['decay'].reshape(2, 2, ch, 1)
    const = lambda shape: pl.BlockSpec(shape, lambda o, j: (0,) * len(shape))
    sel = lambda d, last: pl.BlockSpec((None, None, cbf, last), lambda o, j: (o, d, j, 0))
    return pl.pallas_call(
        functools.partial(_hy_filter_kernel, seq=seq),
        out_shape=jax.ShapeDtypeStruct((2, ch, 2 * seq), f32),
        grid=(2, nb),
        in_specs=[const((HY_EMB_PAD, seq)), const((HY_EMB_PAD, seq)), const((1, seq)), const((1, seq)),
                  const((HY_HID, HY_EMB_PAD)), const((HY_HID, 1)), const((HY_HID, 1)),
                  const((HY_HID, HY_HID)), const((HY_HID, 1)),
                  sel(0, HY_HID), sel(1, HY_HID), sel(0, 1), sel(1, 1), sel(0, 1), sel(1, 1),
                  pl.BlockSpec((None, cbf, 1), lambda o, j: (o, j, 0))],
        out_specs=pl.BlockSpec((None, cbf, 2 * seq), lambda o, j: (o, j, 0)),
        scratch_shapes=[pltpu.VMEM((HY_HID, seq), f32), pltpu.VMEM((HY_HID, seq), f32)],
        compiler_params=_cparams("arbitrary", "arbitrary"),
        name="hy_filters",
    )(feats(t), feats(t_rev), t.reshape(1, seq), t_rev.reshape(1, seq),
      jnp.pad(fp['w1'], ((0, HY_EMB_PAD - HY_EMB), (0, 0))).T, col(fp['b1']), col(fp['sin_w']),
      fp['w2'].T, col(fp['b2']), w3t, w3t, b3, b3, dec, dec, fp['skip'].reshape(2, ch, 1))


def _hyena_kernel(cw_ref, cb_ref, hv_ref, h1_ref, h2_ref, r_ref, o_ref, acc_scr, *, bsz, n_i, cb_n):
    tb = HY_TB
    seq = n_i * tb
    rows = n_i * bsz
    base = pl.program_id(0) * cb_n
    n_ch = MIX_W
    lane = lax.broadcasted_iota(jnp.int32, (1, tb), 1)
    zblk = jnp.zeros((bsz, tb), f32)

    def load(ref, ch):
        return jnp.concatenate([ref[ch, pl.ds(ib, bsz, stride=n_i), :] for ib in range(n_i)], axis=0)

    def short_conv(a, stream, ch):
        idx = stream * n_ch + base + ch
        prev = jnp.concatenate([zblk, a[:rows - bsz]], axis=0)
        nxt = jnp.concatenate([a[bsz:], zblk], axis=0)
        am = pltpu.roll(jnp.where(lane == tb - 1, prev, a), 1, 1)
        ap = pltpu.roll(jnp.where(lane == 0, nxt, a), tb - 1, 1)
        return (am * cw_ref[idx] + a * cw_ref[3 * n_ch + idx] + ap * cw_ref[6 * n_ch + idx]) + cb_ref[idx]

    def long_conv(u, order, ch):
        rrow = r_ref[order, pl.ds(ch, 1), :]
        g = pltpu.roll(jnp.broadcast_to(rrow, (tb, 2 * seq)), 0, 1, stride=1, stride_axis=0).astype(bf16)
        acc_scr[...] = jnp.zeros_like(acc_scr)
        for d in range(-(n_i - 1), n_i):
            n = n_i - abs(d)
            src = max(0, -d) * bsz
            dst = max(0, d) * bsz
            m = g[:, (d + n_i) * tb:(d + n_i + 1) * tb]
            acc_scr[dst:dst + n * bsz, :] += _dot(u[src:src + n * bsz].astype(bf16), m)
        return acc_scr[...]

    def body(ch, carry):
        hv = short_conv(load(hv_ref, ch), 0, ch)
        hx1 = short_conv(load(h1_ref, ch), 1, ch)
        hx2 = short_conv(load(h2_ref, ch), 2, ch)
        zz = hx1 * long_conv(hv, 0, ch)
        out = hx2 * long_conv(zz, 1, ch)
        for ib in range(n_i):
            o_ref[ch, pl.ds(ib, bsz, stride=n_i), :] = out[ib * bsz:(ib + 1) * bsz]
        return carry

    lax.fori_loop(0, cb_n, body, 0)


def hyena_mixer(hy_t, r, cw, cb, bsz, seq, cb_n=8):
    n_i = seq // HY_TB
    ch = MIX_W
    nblk = ch // cb_n
    x3 = hy_t.reshape(3 * ch, bsz * n_i, HY_TB)
    stream = lambda s: pl.BlockSpec((cb_n, bsz * n_i, HY_TB), lambda j: (s * nblk + j, 0, 0))
    smem = pl.BlockSpec(memory_space=pltpu.SMEM)
    out = pl.pallas_call(
        functools.partial(_hyena_kernel, bsz=bsz, n_i=n_i, cb_n=cb_n),
        out_shape=jax.ShapeDtypeStruct((ch, bsz * n_i, HY_TB), f32),
        grid=(nblk,),
        in_specs=[smem, smem, stream(0), stream(1), stream(2),
                  pl.BlockSpec((2, cb_n, 2 * seq), lambda j: (0, j, 0))],
        out_specs=pl.BlockSpec((cb_n, bsz * n_i, HY_TB), lambda j: (j, 0, 0)),
        scratch_shapes=[pltpu.VMEM((bsz * n_i, HY_TB), f32)],
        compiler_params=_cparams("parallel"),
        name="hyena_mixer",
    )(cw.reshape(-1), cb, x3, x3, x3, r)
    return out.reshape(ch, bsz * seq)


_IN_SIZES = (NH * GLA_DK, NH * GLA_DK, MIX_W, MIX_W, 2 * GLA_RANK, MIX_W, MIX_W, MIX_W,
             MIX_W, MIX_W + 2 * SSD_G * SSD_N, 2 * NH, 3 * MIX_W)
_IN_OFFS = np.concatenate([[0], np.cumsum(_IN_SIZES)]).tolist()


def _prep_w_in(w_in):
    (gq, gk, gv, gg, gr, dq, dk, dv, sz, sxbc, sdt, hy) = [w_in[:, _IN_OFFS[i]:_IN_OFFS[i + 1]]
                                                          for i in range(len(_IN_SIZES))]
    d = w_in.shape[0]
    zeros = lambda n: jnp.zeros((d, n), w_in.dtype)
    misc = jnp.concatenate([sxbc[:, MIX_W:], gr, zeros(LANES - 2 * GLA_RANK), sdt, zeros(LANES - 2 * NH)], axis=1)
    w_tok = jnp.concatenate([gq, gk, gv, gg, dq, dk, dv, sz, sxbc[:, :MIX_W], misc], axis=1)
    return w_tok.astype(bf16), hy.T.astype(bf16)


def _rope_tables(seq):
    rows = seq // GRID_W
    r, col = jnp.meshgrid(jnp.arange(rows), jnp.arange(GRID_W), indexing='ij')
    r = r.reshape(-1).astype(f32)
    col = col.reshape(-1).astype(f32)
    nf = DIFF_DK // 4
    inv = ROPE_BASE ** (-jnp.arange(nf, dtype=f32) / nf)
    ang = jnp.concatenate([r[:, None] * inv, col[:, None] * inv], axis=-1)
    cos, sin = jnp.cos(ang), jnp.sin(ang)
    reps = LANES // DIFF_DK
    return jnp.tile(jnp.concatenate([cos, cos], axis=-1), (1, reps)), jnp.tile(jnp.concatenate([-sin, sin], axis=-1), (1, reps))


def _const_tables():
    hsum = np.kron(np.eye(NH, dtype=np.float32), np.ones((HEAD_DIM, HEAD_DIM), np.float32))
    expand = np.zeros((LANES, MIX_W), np.float32)
    selb = np.zeros((2 * LANES, MIX_W), np.float32)
    selc = np.zeros((2 * LANES, MIX_W), np.float32)
    rep = NH // SSD_G
    for h in range(NH):
        expand[h, h * HEAD_DIM:(h + 1) * HEAD_DIM] = 1.0
        g = h // rep
        for n in range(SSD_N):
            selb[g * SSD_N + n, h * SSD_N + n] = 1.0
            selc[SSD_G * SSD_N + g * SSD_N + n, h * SSD_N + n] = 1.0
    return jnp.asarray(hsum), jnp.asarray(expand), jnp.asarray(selb), jnp.asarray(selc)


def _embed_diag(s, k_last):
    eye = jnp.eye(NH, dtype=s.dtype)
    bsz = s.shape[0]
    return jnp.einsum('bdhkv,hg->bdhvgk', s, eye).reshape(bsz, 2, NH * s.shape[4], NH * k_last)


def _extract_diag(st, a, b):
    bsz = st.shape[0]
    st6 = st.reshape(bsz, 2, NH, b, NH, a)
    return jnp.stack([st6[:, :, h, :, h, :] for h in range(NH)], axis=2).swapaxes(-1, -2)


def _layer(x, mod, rows_per_mod, lw, consts, bsz, seq, lam_init, r_filt, ctx, rope_tabs, final_g, final_norm):
    hsum, expand, selb, selc = consts
    p, hy_t = in_proj(x, mod, lw['norm1_g'], lw['w_tok'], lw['w_hyT'], rows_per_mod)
    if ctx is None:
        gla_s0 = jnp.zeros((bsz, 2, MIX_W, NH * GLA_DK), f32)
        ssd_s0 = jnp.zeros((bsz, 2, MIX_W, MIX_W), f32)
        dctx = None
    else:
        ctx_k, ctx_v, gla_s, ssd_s = ctx
        gla_s0 = _embed_diag(gla_s, GLA_DK)
        ssd_s0 = _embed_diag(ssd_s, HEAD_DIM)
        dctx = (ctx_k.reshape(bsz, -1, MIX_W), ctx_v.reshape(bsz, -1, MIX_W))
    o_gla, gla_fin = gla_mixer(p, lw['gla_gw'], lw['gla_gb'], lw['gla_ng'], hsum, gla_s0, bsz, seq)
    o_diff = diff_mixer(p, lw['lam_p'], lw['diff_ng'], bsz, seq, lam_init, dctx, rope_tabs)
    xbc = ssd_conv(p, lw['ssd_conv_w'], lw['ssd_conv_b'], bsz, seq)
    o_ssd, ssd_fin = ssd_mixer(p, xbc, lw['ssd_dtb'], lw['ssd_alog'], lw['ssd_dexp'], lw['ssd_ng'],
                               expand, selb, selc, ssd_s0, bsz, seq)
    o_hy = hyena_mixer(hy_t, r_filt, lw['hy_cw'], lw['hy_cb'], bsz, seq)
    x = out_proj(x, mod, o_gla, o_diff, o_ssd, o_hy, lw['w_out'], rows_per_mod)
    x = mlp(x, mod, lw['norm2_g'], lw['mlp_w1'], lw['mlp_w2'], final_g, rows_per_mod, final_norm)
    return x, p, gla_fin, ssd_fin


def kernel(x_prompt, x_sample, cache_diff_k, cache_diff_v, state_gla, state_ssd, c, c_ctx, ada_w, ada_b, norm1_g, norm2_g, w_in, w_out, gla_gate_w, gla_gate_b, gla_norm_g, diff_lambda, diff_norm_g, ssd_conv_w, ssd_conv_b, ssd_dt_bias, ssd_a_log, ssd_d, ssd_norm_g, hy_conv_w, hy_conv_b, hy_f_w1, hy_f_b1, hy_f_w2, hy_f_b2, hy_f_w3, hy_f_b3, hy_sin_w, hy_decay, hy_skip, mlp_w1, mlp_w2, final_g):
    bp, lp, d = x_prompt.shape
    bs, ls, _ = x_sample.shape
    depth = w_in.shape[0]
    consts = _const_tables()
    rope_tabs = _rope_tables(ls)
    n_c = 16
    cvec = jnp.concatenate([c_ctx[None], c, jnp.zeros((n_c - 1 - bs, d), f32)], axis=0)
    hp = x_prompt.reshape(bp * lp, d)
    hs = x_sample.reshape(bs * ls, d)
    ks_, vs_, gs_, ss_ = [], [], [], []
    for l in range(depth):
        w_tok, w_hyT = _prep_w_in(w_in[l])
        pad_l = lambda a: jnp.pad(a, (0, LANES - a.shape[0])).reshape(1, LANES)
        gw = jnp.zeros((LANES, 2 * NH * GLA_DK), f32)
        gw = gw.at[0:GLA_RANK, 0:NH * GLA_DK].set(gla_gate_w[l, 0])
        gw = gw.at[GLA_RANK:2 * GLA_RANK, NH * GLA_DK:].set(gla_gate_w[l, 1])
        lw = dict(
            norm1_g=norm1_g[l], norm2_g=norm2_g[l], w_tok=w_tok, w_hyT=w_hyT, w_out=w_out[l].astype(bf16),
            mlp_w1=mlp_w1[l].astype(bf16), mlp_w2=mlp_w2[l].astype(bf16),
            gla_gw=gw, gla_gb=gla_gate_b[l].reshape(1, -1), gla_ng=jnp.tile(gla_norm_g[l], NH).reshape(1, MIX_W),
            lam_p=jnp.pad(diff_lambda[l], ((0, 4), (0, LANES - DIFF_DK))),
            diff_ng=jnp.tile(diff_norm_g[l], LANES // HEAD_DIM).reshape(1, LANES),
            ssd_conv_w=ssd_conv_w[l], ssd_conv_b=ssd_conv_b[l],
            ssd_dtb=pad_l(ssd_dt_bias[l].reshape(-1)), ssd_alog=pad_l(ssd_a_log[l].reshape(-1)),
            ssd_dexp=jnp.repeat(ssd_d[l], HEAD_DIM).reshape(1, MIX_W), ssd_ng=ssd_norm_g[l].reshape(1, MIX_W),
            hy_cw=hy_conv_w[l], hy_cb=hy_conv_b[l],
        )
        fp = dict(w1=hy_f_w1[l], b1=hy_f_b1[l], w2=hy_f_w2[l], b2=hy_f_b2[l], w3=hy_f_w3[l], b3=hy_f_b3[l],
                  sin_w=hy_sin_w[l], decay=hy_decay[l], skip=hy_skip[l])
        mod = ada_mod(cvec, ada_w[l], ada_b[l]).reshape(n_c, 6, d)
        lam_init = 0.8 - 0.6 * math.exp(-0.3 * l)
        last = l == depth - 1
        hp, p_p, g_l, s_l = _layer(hp, mod[0:1], bp * lp, lw, consts, bp, lp, lam_init, hy_filters(fp, lp),
                                   None, None, final_g, last)
        ks_.append(p_p[:, T_DK * PT:(T_DK + 1) * PT].reshape(bp, lp, NH, 2 * DIFF_DK))
        vs_.append(p_p[:, T_DV * PT:(T_DV + 1) * PT].reshape(bp, lp, NH, HEAD_DIM))
        gs_.append(_extract_diag(g_l, GLA_DK, HEAD_DIM))
        ss_.append(_extract_diag(s_l, HEAD_DIM, SSD_N))
        hs, _, _, _ = _layer(hs, mod[1:1 + bs], ls, lw, consts, bs, ls, lam_init, hy_filters(fp, ls),
                             (cache_diff_k[:, l], cache_diff_v[:, l], state_gla[:, l], state_ssd[:, l]),
                             rope_tabs, final_g, last)
    return (hp.reshape(bp, lp, d), hs.reshape(bs, ls, d), jnp.stack(ks_, axis=1), jnp.stack(vs_, axis=1),
            jnp.stack(gs_, axis=1), jnp.stack(ss_, axis=1))
```

```python
import functools
import math

import numpy as np
import jax
import jax.numpy as jnp
from jax import lax
from jax.experimental import pallas as pl
from jax.experimental.pallas import tpu as pltpu

f32 = jnp.float32
bf16 = jnp.bfloat16
HI = lax.Precision.HIGHEST

LANES = 128
VMEM_LIMIT = 56 * 2**20

D_MODEL = 2048
GRID_W = 64
MIX_W = D_MODEL // 4
HEAD_DIM = 64
NH = MIX_W // HEAD_DIM
D_FF = 4 * D_MODEL
NORM_EPS = 1e-6
LOG2E = 1.0 / math.log(2.0)
GLA_DK = HEAD_DIM // 2
GLA_RANK = 16
GLA_TAU = 16.0
GLA_CHUNK = 64
DIFF_DK = HEAD_DIM // 2
ROPE_BASE = 10000.0
SSD_N = 64
SSD_G = 2
SSD_CHUNK = 128
HY_BANDS = 16
HY_EMB = 2 * HY_BANDS + 1
HY_EMB_PAD = 40
HY_HID = 64
HY_TB = 256

PT = 512
(T_GQK, T_GV, T_GG, T_DQ, T_DK, T_DV, T_SZ, T_SX, T_MISC) = range(9)
N_PT = 9
MISC_GR = 256
MISC_DT = 384
SH1, SC1, G1, SH2, SC2, G2 = range(6)

NT_DIMS = (((1,), (1,)), ((), ()))
TN_DIMS = (((0,), (0,)), ((), ()))


def _cparams(*sem):
    return pltpu.CompilerParams(dimension_semantics=sem, vmem_limit_bytes=VMEM_LIMIT)


def _sigmoid(x):
    return 1.0 / (1.0 + jnp.exp(-x))


def _silu(x):
    return x * _sigmoid(x)


def _softplus(x):
    return jnp.maximum(x, 0.0) + jnp.log1p(jnp.exp(-jnp.abs(x)))


def _log_sigmoid(x):
    return jnp.minimum(x, 0.0) - jnp.log1p(jnp.exp(-jnp.abs(x)))


def _tri_mask(rows, c, sgn):
    r_i = lax.broadcasted_iota(jnp.int32, (rows, c), 0) & (c - 1)
    c_i = lax.broadcasted_iota(jnp.int32, (rows, c), 1)
    return (r_i - c_i) * sgn >= 0


def _expand_state(s, row_w, col_w):
    rows, cols = NH * row_w, NH * col_w
    bd = (lax.broadcasted_iota(jnp.int32, (rows, cols), 0) // row_w
          == lax.broadcasted_iota(jnp.int32, (rows, cols), 1) // col_w)
    return jnp.where(bd, jnp.concatenate([s] * NH, axis=0), 0.0)


def _compact_state(st):
    row_w = st.shape[0] // NH
    out = st[0:row_w]
    for h in range(1, NH):
        out += st[h * row_w:(h + 1) * row_w]
    return out


def _split(x, n):
    parts = []
    for _ in range(n - 1):
        hi = x.astype(bf16)
        parts.append(hi)
        x = x - hi.astype(f32)
    parts.append(x.astype(bf16))
    return parts


def _sel_dot(a, b, dims=None, terms=3):
    if a.dtype == bf16:
        prods = [_dot(a, p, dims) for p in _split(b, terms)]
    else:
        prods = [_dot(p, b, dims) for p in _split(a, terms)]
    return functools.reduce(lambda x, y: x + y, prods)


def _dot(a, b, dims=None, precision=None):
    if dims is None:
        return jnp.dot(a, b, preferred_element_type=f32, precision=precision)
    return lax.dot_general(a, b, dims, preferred_element_type=f32, precision=precision)


def _ada_kernel(c_ref, w_ref, b_ref, o_ref):
    s = _silu(c_ref[...]).astype(bf16)
    o_ref[...] = _dot(s, w_ref[...].astype(bf16)) + b_ref[...]


def ada_mod(cvec, w, b, layer, tn=1024):
    m, d = cvec.shape
    n = w.shape[2]
    return pl.pallas_call(
        _ada_kernel,
        out_shape=jax.ShapeDtypeStruct((m, n), f32),
        grid=(n // tn,),
        in_specs=[pl.BlockSpec((m, d), lambda j: (0, 0)),
                  pl.BlockSpec((None, d, tn), lambda j: (layer, 0, j)),
                  pl.BlockSpec((None, 1, tn), lambda j: (layer, 0, j))],
        out_specs=pl.BlockSpec((m, tn), lambda j: (0, j)),
        compiler_params=_cparams("arbitrary"),
        name="ada_mod",
    )(cvec, w, b.reshape(b.shape[0], 1, n))


def _norm_mod(x, g, sc, sh):
    ms = jnp.mean(x * x, axis=-1, keepdims=True)
    return (x * lax.rsqrt(ms + NORM_EPS) * g) * (1.0 + sc) + sh


def _inproj_kernel(x_ref, mod_ref, g_ref, wt_ref, wh_ref, p_ref, h_ref, u_scr):
    j = pl.program_id(1)

    @pl.when(j == 0)
    def _():
        u = _norm_mod(x_ref[...], g_ref[...], mod_ref[SC1:SC1 + 1, :], mod_ref[SH1:SH1 + 1, :])
        u_scr[...] = u.astype(bf16)

    @pl.when(j < N_PT)
    def _():
        p_ref[...] = _dot(u_scr[...], wt_ref[...])

    @pl.when(j >= N_PT)
    def _():
        h_ref[...] = _dot(wh_ref[...], u_scr[...], NT_DIMS)


def in_proj(x, mod, g, w_tok, w_hyT, rows_per_mod, tm=1024):
    t, d = x.shape
    assert t % tm == 0 and rows_per_mod % tm == 0, (t, rows_per_mod, tm)
    n_hy = w_hyT.shape[0] // PT
    return pl.pallas_call(
        _inproj_kernel,
        out_shape=(jax.ShapeDtypeStruct((t, N_PT * PT), f32), jax.ShapeDtypeStruct((n_hy * PT, t), f32)),
        grid=(t // tm, N_PT + n_hy),
        in_specs=[pl.BlockSpec((tm, d), lambda i, j: (i, 0)),
                  pl.BlockSpec((None, 6, d), lambda i, j: ((i * tm) // rows_per_mod, 0, 0)),
                  pl.BlockSpec((1, d), lambda i, j: (0, 0)),
                  pl.BlockSpec((d, PT), lambda i, j: (0, jnp.minimum(j, N_PT - 1))),
                  pl.BlockSpec((PT, d), lambda i, j: (jnp.maximum(j - N_PT, 0), 0))],
        out_specs=(pl.BlockSpec((tm, PT), lambda i, j: (i, jnp.minimum(j, N_PT - 1))),
                   pl.BlockSpec((PT, tm), lambda i, j: (jnp.maximum(j - N_PT, 0), i))),
        scratch_shapes=[pltpu.VMEM((tm, d), bf16)],
        compiler_params=_cparams("parallel", "arbitrary"),
        name="in_proj",
    )(x, mod, g.reshape(1, d), w_tok, w_hyT)


def _outproj_kernel(x_ref, mod_ref, a_ref, b_ref, c_ref, dT_ref, w_ref, o_ref):
    acc = _dot(a_ref[...], w_ref[0:MIX_W, :])
    acc += _dot(b_ref[...], w_ref[MIX_W:2 * MIX_W, :])
    acc += _dot(c_ref[...], w_ref[2 * MIX_W:3 * MIX_W, :])
    acc += _dot(dT_ref[...].astype(bf16), w_ref[3 * MIX_W:4 * MIX_W, :], TN_DIMS)
    o_ref[...] = x_ref[...] + mod_ref[G1:G1 + 1, :] * acc


def out_proj(x, mod, o_a, o_b, o_c, o_dT, w_out, layer, rows_per_mod, tm=512):
    t, d = x.shape
    assert t % tm == 0 and rows_per_mod % tm == 0, (t, rows_per_mod, tm)
    tok = lambda i: (i, 0)
    return pl.pallas_call(
        _outproj_kernel,
        out_shape=jax.ShapeDtypeStruct((t, d), f32),
        grid=(t // tm,),
        in_specs=[pl.BlockSpec((tm, d), tok),
                  pl.BlockSpec((None, 6, d), lambda i: ((i * tm) // rows_per_mod, 0, 0)),
                  pl.BlockSpec((tm, MIX_W), tok), pl.BlockSpec((tm, MIX_W), tok), pl.BlockSpec((tm, MIX_W), tok),
                  pl.BlockSpec((MIX_W, tm), lambda i: (0, i)),
                  pl.BlockSpec((None, d, d), lambda i: (layer, 0, 0))],
        out_specs=pl.BlockSpec((tm, d), tok),
        compiler_params=_cparams("parallel"),
        name="out_proj",
    )(x, mod, o_a, o_b, o_c, o_dT, w_out)


def _mlp_kernel(x_ref, mod_ref, g_ref, w1_ref, w2_ref, fg_ref, o_ref, u_scr, *, final_norm):
    j = pl.program_id(1)

    @pl.when(j == 0)
    def _():
        u = _norm_mod(x_ref[...], g_ref[...], mod_ref[SC2:SC2 + 1, :], mod_ref[SH2:SH2 + 1, :])
        u_scr[...] = u.astype(bf16)
        o_ref[...] = jnp.zeros_like(o_ref)

    h = jnp.maximum(_dot(u_scr[...], w1_ref[...]), 0.0)
    hb = (h * h).astype(bf16)
    d = o_ref.shape[1]
    for n in range(d // PT):
        o_ref[:, n * PT:(n + 1) * PT] += _dot(hb, w2_ref[:, n * PT:(n + 1) * PT])

    @pl.when(j == pl.num_programs(1) - 1)
    def _():
        y = x_ref[...] + mod_ref[G2:G2 + 1, :] * o_ref[...]
        if final_norm:
            ms = jnp.mean(y * y, axis=-1, keepdims=True)
            y = y * lax.rsqrt(ms + NORM_EPS) * fg_ref[...]
        o_ref[...] = y


def mlp(x, mod, g, w1, w2, layer, final_g, rows_per_mod, final_norm, tm=1024, tf=512):
    t, d = x.shape
    assert t % tm == 0 and rows_per_mod % tm == 0, (t, rows_per_mod, tm)
    dff = w1.shape[2]
    return pl.pallas_call(
        functools.partial(_mlp_kernel, final_norm=final_norm),
        out_shape=jax.ShapeDtypeStruct((t, d), f32),
        grid=(t // tm, dff // tf),
        in_specs=[pl.BlockSpec((tm, d), lambda i, j: (i, 0)),
                  pl.BlockSpec((None, 6, d), lambda i, j: ((i * tm) // rows_per_mod, 0, 0)),
                  pl.BlockSpec((1, d), lambda i, j: (0, 0)),
                  pl.BlockSpec((None, d, tf), lambda i, j: (layer, 0, j)),
                  pl.BlockSpec((None, tf, d), lambda i, j: (layer, j, 0)),
                  pl.BlockSpec((1, d), lambda i, j: (0, 0))],
        out_specs=pl.BlockSpec((tm, d), lambda i, j: (i, 0)),
        scratch_shapes=[pltpu.VMEM((tm, d), bf16)],
        compiler_params=_cparams("parallel", "arbitrary"),
        name="mlp",
    )(x, mod, g.reshape(1, d), w1, w2, final_g.reshape(1, d))


def _gla_kernel(qk_ref, v_ref, gg_ref, misc_ref, gwh_ref, gwl_ref, gb_ref, ng_ref, hsum_ref, s0_ref,
                o_ref, sfin_ref, of_scr, st_scr, la_scr, *, nblk, tl):
    ph = pl.program_id(1)
    i = pl.program_id(2)
    fwd = ph == 0
    c = GLA_CHUNK
    nch = tl // c
    qkw = NH * GLA_DK

    @pl.when(i == 0)
    def _():
        st_scr[...] = _expand_state(s0_ref[...], HEAD_DIM, GLA_DK)

    gr_hi, gr_lo = _split(misc_ref[:, MISC_GR:MISC_GR + LANES], 2)
    logits = (_dot(gr_hi, gwh_ref[...]) + _dot(gr_lo, gwh_ref[...]) + _dot(gr_hi, gwl_ref[...])) + gb_ref[...]
    la_scr[...] = _log_sigmoid(logits) * (1.0 / GLA_TAU)

    sgn = jnp.where(fwd, 1, -1)
    trib = jnp.where(_tri_mask(c, c, sgn), 1.0, 0.0).astype(bf16)
    tri8 = _tri_mask(NH * c, c, sgn)
    qk_head = lax.broadcasted_iota(jnp.int32, (1, qkw), 1) // GLA_DK
    v_head = lax.broadcasted_iota(jnp.int32, (1, MIX_W), 1) // HEAD_DIM
    bd = (lax.broadcasted_iota(jnp.int32, (MIX_W, qkw), 0) // HEAD_DIM
          == lax.broadcasted_iota(jnp.int32, (MIX_W, qkw), 1) // GLA_DK)
    blk = jnp.where(fwd, i, nblk - 1 - i)

    def chunk(j, carry):
        cj = jnp.where(fwd, j, nch - 1 - j)
        off = pl.multiple_of(cj * c, c)
        q = qk_ref[pl.ds(off, c), 0:qkw] * (GLA_DK ** -0.5)
        k = qk_ref[pl.ds(off, c), qkw:2 * qkw]
        vb = v_ref[pl.ds(off, c), :].astype(bf16)
        b = _sel_dot(trib, la_scr[pl.ds(off, c), :])
        btot = jnp.where(fwd, b[c - 1:c, :], b[0:1, :])
        qd = q * jnp.exp(b)
        kd = (k * jnp.exp(-b)).astype(bf16)
        kt = (k * jnp.exp(btot - b)).astype(bf16)
        qst = jnp.concatenate([jnp.where(qk_head == h, qd, 0.0) for h in range(NH)], axis=0).astype(bf16)
        att = jnp.where(tri8, _dot(qst, kd, NT_DIMS), 0.0)
        r = _dot(att.astype(bf16), vb)
        o = jnp.where(v_head == 0, r[0:c], 0.0)
        for h in range(1, NH):
            o += jnp.where(v_head == h, r[h * c:(h + 1) * c], 0.0)
        st = st_scr[...]
        o += _dot(qd.astype(bf16), st.astype(bf16), NT_DIMS)
        kv = _dot(vb, kt, TN_DIMS)
        st_scr[...] = st * jnp.exp(btot) + jnp.where(bd, kv, 0.0)
        row = pl.multiple_of(blk * tl + off, c)

        @pl.when(fwd)
        def _():
            of_scr[pl.ds(row, c), :] = o

        @pl.when(ph == 1)
        def _():
            ot = of_scr[pl.ds(row, c), :] + o
            ss = _sel_dot(ot * ot, hsum_ref[...], terms=2) * (1.0 / HEAD_DIM)
            g = gg_ref[pl.ds(off, c), :]
            o_ref[pl.ds(off, c), :] = (ot * lax.rsqrt(ss + NORM_EPS) * ng_ref[...] * _silu(g)).astype(bf16)

        return carry

    lax.fori_loop(0, nch, chunk, 0)

    @pl.when(i == nblk - 1)
    def _():
        sfin_ref[...] = _compact_state(st_scr[...])


def gla_mixer(p, gw_hi, gw_lo, gb, ng, hsum, s0, bsz, seq, tl=256):
    nblk = seq // tl
    qkw = NH * GLA_DK
    blk = lambda ph, i: jnp.where(ph == 0, i, nblk - 1 - i)
    tile = lambda col: pl.BlockSpec((tl, PT), lambda b, ph, i: (b * nblk + blk(ph, i), col))
    const = lambda shape: pl.BlockSpec(shape, lambda b, ph, i: (0,) * len(shape))
    per_dir = lambda rows: pl.BlockSpec((None, rows, qkw), lambda b, ph, i: (ph, 0, 0))
    state = pl.BlockSpec((None, None, HEAD_DIM, qkw), lambda b, ph, i: (b, ph, 0, 0))
    return pl.pallas_call(
        functools.partial(_gla_kernel, nblk=nblk, tl=tl),
        out_shape=(jax.ShapeDtypeStruct((bsz * seq, MIX_W), bf16),
                   jax.ShapeDtypeStruct((bsz, 2, HEAD_DIM, qkw), f32)),
        grid=(bsz, 2, nblk),
        in_specs=[tile(T_GQK), tile(T_GV), tile(T_GG), tile(T_MISC),
                  per_dir(LANES), per_dir(LANES), per_dir(1), const((1, MIX_W)), const((MIX_W, MIX_W)), state],
        out_specs=(pl.BlockSpec((tl, MIX_W),
                                lambda b, ph, i: (b * nblk + jnp.where(ph == 0, nblk - 1, nblk - 1 - i), 0)),
                   state),
        scratch_shapes=[pltpu.VMEM((seq, MIX_W), f32), pltpu.VMEM((MIX_W, qkw), f32), pltpu.VMEM((tl, qkw), f32)],
        compiler_params=_cparams("parallel", "arbitrary", "arbitrary"),
        name="gla_mixer",
    )(p, p, p, p, gw_hi, gw_lo, gb, ng, hsum, s0)


def _rope(x, cos, sin_signed):
    lane = lax.broadcasted_iota(jnp.int32, (1, LANES), 1)
    half = DIFF_DK // 2
    rot = jnp.where((lane % DIFF_DK) < half, pltpu.roll(x, LANES - half, 1), pltpu.roll(x, half, 1))
    return x * cos + rot * sin_signed


def _diff_kernel(*refs, seq, ctx_len, rope, lam_init):
    if rope:
        (q_ref, k_ref, v_ref, ck_ref, cv_ref, cq_ref, sq_ref, ckk_ref, skk_ref, lam_ref, ng_ref,
         o_ref, kb_scr, va_scr, vb_scr) = refs
    else:
        q_ref, k_ref, v_ref, lam_ref, ng_ref, o_ref, kb_scr, va_scr, vb_scr = refs
    qi = pl.program_id(2)
    lane = lax.broadcasted_iota(jnp.int32, (1, LANES), 1)
    first = lane < HEAD_DIM

    def put_v(v, lo, hi):
        va_scr[lo:hi, :] = jnp.where(first, v, 1.0).astype(bf16)
        vb_scr[lo:hi, :] = jnp.where(first, pltpu.roll(v, HEAD_DIM, 1), 1.0).astype(bf16)

    @pl.when(qi == 0)
    def _():
        k = k_ref[...]
        if rope:
            k = _rope(k, ckk_ref[...], skk_ref[...])
        kb_scr[0:seq, :] = k.astype(bf16)
        put_v(v_ref[...], 0, seq)
        if ctx_len:
            kb_scr[seq:seq + ctx_len, :] = ck_ref[...].astype(bf16)
            put_v(cv_ref[...], seq, seq + ctx_len)

    q = q_ref[...]
    if rope:
        q = _rope(q, cq_ref[...], sq_ref[...])
    q = q * (DIFF_DK ** -0.5 * LOG2E)
    kb = kb_scr[...]
    vs = (va_scr[...], vb_scr[...])
    res = []
    for g in range(4):
        qm = jnp.where(lane // DIFF_DK == g, q, 0.0).astype(bf16)
        s = _dot(qm, kb, NT_DIMS)
        p = jnp.exp2(s - jnp.max(s, axis=-1, keepdims=True))
        ov = _dot(p.astype(bf16), vs[g // 2])
        res.append(ov / pltpu.roll(ov, HEAD_DIM, 1))
    lp = lam_ref[...]
    lam = (jnp.exp(jnp.sum(lp[0:1] * lp[1:2], keepdims=True)) - jnp.exp(jnp.sum(lp[2:3] * lp[3:4], keepdims=True))
           + lam_init)
    o = jnp.where(first, res[0] - lam * res[1], pltpu.roll(res[2] - lam * res[3], HEAD_DIM, 1))
    o2 = o * o
    ss_a = jnp.sum(jnp.where(first, o2, 0.0), axis=-1, keepdims=True)
    ss_b = jnp.sum(jnp.where(first, 0.0, o2), axis=-1, keepdims=True)
    inv = jnp.where(first, lax.rsqrt(ss_a * (1.0 / HEAD_DIM) + NORM_EPS), lax.rsqrt(ss_b * (1.0 / HEAD_DIM) + NORM_EPS))
    o_ref[...] = (o * inv * ng_ref[...] * (1.0 - lam_init)).astype(bf16)


def diff_mixer(p, lam_p, ng, bsz, seq, lam_init, ctx=None, rope_tabs=None, tq=256):
    nq = seq // tq
    npair = MIX_W // LANES
    rope = rope_tabs is not None
    ctx_len = ctx[0].shape[1] if ctx is not None else 0
    assert rope == (ctx is not None)
    q_spec = pl.BlockSpec((tq, LANES), lambda b, hp, i: (b * nq + i, T_DQ * (PT // LANES) + hp))
    k_spec = pl.BlockSpec((seq, LANES), lambda b, hp, i: (b, T_DK * (PT // LANES) + hp))
    v_spec = pl.BlockSpec((seq, LANES), lambda b, hp, i: (b, T_DV * (PT // LANES) + hp))
    const = lambda shape: pl.BlockSpec(shape, lambda b, hp, i: (0,) * len(shape))
    args = [p, p, p]
    specs = [q_spec, k_spec, v_spec]
    if rope:
        c_spec = pl.BlockSpec((None, ctx_len, LANES), lambda b, hp, i: (b, 0, hp))
        tab_q = pl.BlockSpec((tq, LANES), lambda b, hp, i: (i, 0))
        args += [ctx[0], ctx[1], rope_tabs[0], rope_tabs[1], rope_tabs[0], rope_tabs[1]]
        specs += [c_spec, c_spec, tab_q, tab_q, const((seq, LANES)), const((seq, LANES))]
    args += [lam_p, ng]
    specs += [const((8, LANES)), const((1, LANES))]
    return pl.pallas_call(
        functools.partial(_diff_kernel, seq=seq, ctx_len=ctx_len, rope=rope, lam_init=lam_init),
        out_shape=jax.ShapeDtypeStruct((bsz * seq, MIX_W), bf16),
        grid=(bsz, npair, nq),
        in_specs=specs,
        out_specs=pl.BlockSpec((tq, LANES), lambda b, hp, i: (b * nq + i, hp)),
        scratch_shapes=[pltpu.VMEM((seq + ctx_len, LANES), bf16)] * 3,
        compiler_params=_cparams("parallel", "parallel", "arbitrary"),
        name="diff_mixer",
    )(*args)


def _dwconv_kernel(x_ref, w_ref, b_ref, o_ref):
    x = x_ref[...]
    n = x.shape[0]
    row = lax.broadcasted_iota(jnp.int32, x.shape, 0)
    xm = jnp.where(row == 0, 0.0, pltpu.roll(x, 1, 0))
    xp = jnp.where(row == n - 1, 0.0, pltpu.roll(x, n - 1, 0))
    y = xm * w_ref[0:1, :] + x * w_ref[1:2, :] + xp * w_ref[2:3, :] + b_ref[...]
    o_ref[...] = _silu(y)


def ssd_conv(p, w, b, bsz, seq):
    ncol = w.shape[1] // LANES
    per = PT // LANES
    col = lambda j: jnp.where(j < per, T_SX * per + j, T_MISC * per + (j - per))
    return pl.pallas_call(
        _dwconv_kernel,
        out_shape=jax.ShapeDtypeStruct((bsz * seq, w.shape[1]), f32),
        grid=(bsz, ncol),
        in_specs=[pl.BlockSpec((seq, LANES), lambda bb, j: (bb, col(j))),
                  pl.BlockSpec((3, LANES), lambda bb, j: (0, j)),
                  pl.BlockSpec((1, LANES), lambda bb, j: (0, j))],
        out_specs=pl.BlockSpec((seq, LANES), lambda bb, j: (bb, j)),
        compiler_params=_cparams("parallel", "parallel"),
        name="ssd_conv",
    )(p, w, b.reshape(1, -1))


def _ssd_kernel(xs_ref, bc_ref, z_ref, dt_ref, dtb_ref, alog_ref, dexp_ref, ng_ref, exp_ref, selb_ref, selc_ref,
                s0_ref, o_ref, sfin_ref, yf_scr, st_scr, dt_scr, dta_scr, *, nblk, tl):
    ph = pl.program_id(1)
    i = pl.program_id(2)
    fwd = ph == 0
    c = SSD_CHUNK
    nch = tl // c
    rep = NH // SSD_G

    @pl.when(i == 0)
    def _():
        st_scr[...] = _expand_state(s0_ref[...], SSD_N, HEAD_DIM)

    dtv = _softplus(dt_ref[...] + dtb_ref[...])
    dtv = jnp.where(fwd, dtv, pltpu.roll(dtv, LANES - NH, 1))
    av = -jnp.exp(alog_ref[...])
    av = jnp.where(fwd, av, pltpu.roll(av, LANES - NH, 1))
    dt_scr[...] = dtv
    dta_scr[...] = dtv * av

    tri = _tri_mask(c, c, jnp.where(fwd, 1, -1))
    trib = jnp.where(tri, 1.0, 0.0).astype(bf16)
    head = lax.broadcasted_iota(jnp.int32, (1, MIX_W), 1) // HEAD_DIM
    bd = (lax.broadcasted_iota(jnp.int32, (MIX_W, MIX_W), 0) // SSD_N
          == lax.broadcasted_iota(jnp.int32, (MIX_W, MIX_W), 1) // HEAD_DIM)
    blk = jnp.where(fwd, i, nblk - 1 - i)

    def chunk(j, carry):
        cj = jnp.where(fwd, j, nch - 1 - j)
        off = pl.multiple_of(cj * c, c)
        dtc = dt_scr[pl.ds(off, c), :]
        cum = _sel_dot(trib, dta_scr[pl.ds(off, c), :])
        cum_t = cum.T
        dt_t = dtc.T
        earg = _sel_dot(cum, exp_ref[...])
        elast = jnp.where(fwd, earg[c - 1:c, :], earg[0:1, :])
        dtx = _sel_dot(dtc, exp_ref[...])
        bcb = bc_ref[pl.ds(off, c), :].astype(bf16)
        bt = _dot(bcb, selb_ref[...])
        ct = _dot(bcb, selc_ref[...])
        x = xs_ref[pl.ds(off, c), :]
        xb = x.astype(bf16)
        st = st_scr[...]
        y = _dot((ct * jnp.exp(earg)).astype(bf16), st.astype(bf16))
        btail = (bt * jnp.exp(elast - earg) * dtx).astype(bf16)
        new = _dot(btail, xb, TN_DIMS)
        st_scr[...] = st * jnp.exp(elast) + jnp.where(bd, new, 0.0)
        btb = bt.astype(bf16)
        cb = [_dot(jnp.where(head == g * rep, ct, 0.0).astype(bf16), btb, NT_DIMS) for g in range(SSD_G)]
        ws = []
        for h in range(NH):
            seg = cum[:, h:h + 1] - cum_t[h:h + 1, :]
            lm = jnp.exp(jnp.where(tri, seg, -jnp.inf))
            ws.append((cb[h // rep] * lm * dt_t[h:h + 1, :]).astype(bf16))
        r = _dot(jnp.concatenate(ws, axis=0), xb)
        for h in range(NH):
            y += jnp.where(head == h, r[h * c:(h + 1) * c], 0.0)
        row = pl.multiple_of(blk * tl + off, c)

        @pl.when(fwd)
        def _():
            yf_scr[pl.ds(row, c), :] = y

        @pl.when(ph == 1)
        def _():
            yt = yf_scr[pl.ds(row, c), :] + y + x * dexp_ref[...]
            t = yt * _silu(z_ref[pl.ds(off, c), :])
            ms = jnp.mean(t * t, axis=-1, keepdims=True)
            o_ref[pl.ds(off, c), :] = (t * lax.rsqrt(ms + NORM_EPS) * ng_ref[...]).astype(bf16)

        return carry

    lax.fori_loop(0, nch, chunk, 0)

    @pl.when(i == nblk - 1)
    def _():
        sfin_ref[...] = _compact_state(st_scr[...])


def ssd_mixer(p, xbc, dtb, alog, dexp, ng, expand, selb, selc, s0, bsz, seq, tl=256):
    nblk = seq // tl
    blk = lambda ph, i: jnp.where(ph == 0, i, nblk - 1 - i)
    rows = lambda b, ph, i: b * nblk + blk(ph, i)
    const = lambda shape: pl.BlockSpec(shape, lambda b, ph, i: (0,) * len(shape))
    state = pl.BlockSpec((None, None, SSD_N, MIX_W), lambda b, ph, i: (b, ph, 0, 0))
    return pl.pallas_call(
        functools.partial(_ssd_kernel, nblk=nblk, tl=tl),
        out_shape=(jax.ShapeDtypeStruct((bsz * seq, MIX_W), bf16),
                   jax.ShapeDtypeStruct((bsz, 2, SSD_N, MIX_W), f32)),
        grid=(bsz, 2, nblk),
        in_specs=[pl.BlockSpec((tl, MIX_W), lambda b, ph, i: (rows(b, ph, i), 0)),
                  pl.BlockSpec((tl, 2 * LANES), lambda b, ph, i: (rows(b, ph, i), MIX_W // (2 * LANES))),
                  pl.BlockSpec((tl, PT), lambda b, ph, i: (rows(b, ph, i), T_SZ)),
                  pl.BlockSpec((tl, LANES), lambda b, ph, i: (rows(b, ph, i), (T_MISC * PT + MISC_DT) // LANES)),
                  const((1, LANES)), const((1, LANES)), const((1, MIX_W)), const((1, MIX_W)),
                  const((LANES, MIX_W)), const((2 * LANES, MIX_W)), const((2 * LANES, MIX_W)), state],
        out_specs=(pl.BlockSpec((tl, MIX_W),
                                lambda b, ph, i: (b * nblk + jnp.where(ph == 0, nblk - 1, nblk - 1 - i), 0)),
                   state),
        scratch_shapes=[pltpu.VMEM((seq, MIX_W), f32), pltpu.VMEM((MIX_W, MIX_W), f32),
                        pltpu.VMEM((tl, LANES), f32), pltpu.VMEM((tl, LANES), f32)],
        compiler_params=_cparams("parallel", "arbitrary", "arbitrary"),
        name="ssd_mixer",
    )(xbc, xbc, p, p, dtb, alog, dexp, ng, expand, selb, selc, s0)


def _hy_filter_kernel(zf_ref, zb_ref, tf_ref, tb_ref, w1_ref, b1_ref, sw_ref, w2_ref, b2_ref,
                      w3f_ref, w3b_ref, b3f_ref, b3b_ref, df_ref, db_ref, skip_ref, r_ref, hf_scr, hb_scr, *, seq):
    first = jnp.logical_and(pl.program_id(0) == 0, pl.program_id(1) == 0)

    @pl.when(first)
    def _():
        for z_ref, h_scr in ((zf_ref, hf_scr), (zb_ref, hb_scr)):
            h = jnp.sin(sw_ref[...] * (_dot(w1_ref[...], z_ref[...], precision=HI) + b1_ref[...]))
            h_scr[...] = jnp.sin(sw_ref[...] * (_dot(w2_ref[...], h, precision=HI) + b2_ref[...]))

    hf = (_dot(w3f_ref[...], hf_scr[...], precision=HI) + b3f_ref[...]) * jnp.exp(-tf_ref[...] * jnp.abs(df_ref[...]))
    hb = (_dot(w3b_ref[...], hb_scr[...], precision=HI) + b3b_ref[...]) * jnp.exp(-tb_ref[...] * jnp.abs(db_ref[...]))
    den = (jnp.sum(jnp.abs(hf), axis=-1, keepdims=True) + jnp.sum(jnp.abs(hb), axis=-1, keepdims=True)) + NORM_EPS
    col = lax.broadcasted_iota(jnp.int32, (1, seq), 1)
    r_ref[:, 0:seq] = jnp.where(col == 0, 0.0, hb / den)
    r_ref[:, seq:2 * seq] = hf / den + jnp.where(col == 0, skip_ref[...], 0.0)


def hy_filters(fp, seq, cbf=128):
    t = jnp.arange(seq, dtype=f32) / seq
    t_rev = jnp.concatenate([t[:1], t[:0:-1]])

    def feats(tt):
        tc = tt[:, None]
        ang = 2.0 * math.pi * tc * jnp.arange(1, HY_BANDS + 1, dtype=f32)
        z = jnp.concatenate([tc, jnp.cos(ang), jnp.sin(ang)], axis=-1)
        return jnp.pad(z, ((0, 0), (0, HY_EMB_PAD - HY_EMB))).T

    ch = MIX_W
    nb = ch // cbf
    col = lambda a: a.reshape(-1, 1)
    w3t = fp['w3'].T.reshape(2, 2, ch, HY_HID)
    b3 = fp['b3'].reshape(2, 2, ch, 1)
    dec = fp['decay'].reshape(2, 2, ch, 1)
    const = lambda shape: pl.BlockSpec(shape, lambda o, j: (0,) * len(shape))
    sel = lambda d, last: pl.BlockSpec((None, None, cbf, last), lambda o, j: (o, d, j, 0))
    return pl.pallas_call(
        functools.partial(_hy_filter_kernel, seq=seq),
        out_shape=jax.ShapeDtypeStruct((2, ch, 2 * seq), f32),
        grid=(2, nb),
        in_specs=[const((HY_EMB_PAD, seq)), const((HY_EMB_PAD, seq)), const((1, seq)), const((1, seq)),
                  const((HY_HID, HY_EMB_PAD)), const((HY_HID, 1)), const((HY_HID, 1)),
                  const((HY_HID, HY_HID)), const((HY_HID, 1)),
                  sel(0, HY_HID), sel(1, HY_HID), sel(0, 1), sel(1, 1), sel(0, 1), sel(1, 1),
                  pl.BlockSpec((None, cbf, 1), lambda o, j: (o, j, 0))],
        out_specs=pl.BlockSpec((None, cbf, 2 * seq), lambda o, j: (o, j, 0)),
        scratch_shapes=[pltpu.VMEM((HY_HID, seq), f32), pltpu.VMEM((HY_HID, seq), f32)],
        compiler_params=_cparams("arbitrary", "arbitrary"),
        name="hy_filters",
    )(feats(t), feats(t_rev), t.reshape(1, seq), t_rev.reshape(1, seq),
      jnp.pad(fp['w1'], ((0, HY_EMB_PAD - HY_EMB), (0, 0))).T, col(fp['b1']), col(fp['sin_w']),
      fp['w2'].T, col(fp['b2']), w3t, w3t, b3, b3, dec, dec, fp['skip'].reshape(2, ch, 1))


def _hyena_kernel(cw_ref, cb_ref, hv_ref, h1_ref, h2_ref, r_ref, o_ref, acc_scr, *, bsz, n_i, cb_n):
    tb = HY_TB
    gr = tb // 2
    sub = tb // LANES
    seq = n_i * tb
    rows = n_i * bsz
    base = pl.program_id(0) * cb_n
    n_ch = MIX_W
    lane = lax.broadcasted_iota(jnp.int32, (1, tb), 1)
    zblk = jnp.zeros((bsz, tb), f32)

    def load(ref, ch):
        return jnp.concatenate(
            [jnp.concatenate([ref[ch, pl.ds(sub * ib + k, bsz, stride=sub * n_i), :] for k in range(sub)], axis=1)
             for ib in range(n_i)], axis=0)

    def short_conv(a, stream, ch):
        idx = stream * n_ch + base + ch
        prev = jnp.concatenate([zblk, a[:rows - bsz]], axis=0) if n_i > 1 else zblk
        nxt = jnp.concatenate([a[bsz:], zblk], axis=0) if n_i > 1 else zblk
        am = pltpu.roll(jnp.where(lane == tb - 1, prev, a), 1, 1)
        ap = pltpu.roll(jnp.where(lane == 0, nxt, a), tb - 1, 1)
        return (am * cw_ref[idx] + a * cw_ref[3 * n_ch + idx] + ap * cw_ref[6 * n_ch + idx]) + cb_ref[idx]

    def long_conv(u, order, ch):
        rrow = r_ref[order, pl.ds(ch, 1), :]
        g = pltpu.roll(jnp.broadcast_to(rrow, (gr, 2 * seq)), 0, 1, stride=1, stride_axis=0).astype(bf16)
        for d in [0] + [e for e in range(-(n_i - 1), n_i) if e != 0]:
            n = n_i - abs(d)
            src = max(0, -d) * bsz
            dst = max(0, d) * bsz
            c0 = seq + d * tb
            m = jnp.concatenate([g[:, c0:c0 + tb], g[:, c0 - gr:c0 - gr + tb]], axis=0)
            part = _dot(u[src:src + n * bsz].astype(bf16), m)
            if d == 0:
                acc_scr[...] = part
            else:
                acc_scr[dst:dst + n * bsz, :] += part
        return acc_scr[...]

    def body(ch, carry):
        hv = short_conv(load(hv_ref, ch), 0, ch)
        hx1 = short_conv(load(h1_ref, ch), 1, ch)
        hx2 = short_conv(load(h2_ref, ch), 2, ch)
        zz = hx1 * long_conv(hv, 0, ch)
        out = hx2 * long_conv(zz, 1, ch)
        for ib in range(n_i):
            for k in range(sub):
                o_ref[ch, pl.ds(sub * ib + k, bsz, stride=sub * n_i), :] = (
                    out[ib * bsz:(ib + 1) * bsz, k * LANES:(k + 1) * LANES])
        return carry

    lax.fori_loop(0, cb_n, body, 0)


def hyena_mixer(hy_t, r, cw, cb, bsz, seq, cb_n=8):
    n_i = seq // HY_TB
    ch = MIX_W
    nblk = ch // cb_n
    n_rows = bsz * seq // LANES
    x3 = hy_t.reshape(3 * ch, n_rows, LANES)
    stream = lambda s: pl.BlockSpec((cb_n, n_rows, LANES), lambda j: (s * nblk + j, 0, 0))
    smem = pl.BlockSpec(memory_space=pltpu.SMEM)
    out = pl.pallas_call(
        functools.partial(_hyena_kernel, bsz=bsz, n_i=n_i, cb_n=cb_n),
        out_shape=jax.ShapeDtypeStruct((ch, n_rows, LANES), f32),
        grid=(nblk,),
        in_specs=[smem, smem, stream(0), stream(1), stream(2),
                  pl.BlockSpec((2, cb_n, 2 * seq), lambda j: (0, j, 0))],
        out_specs=pl.BlockSpec((cb_n, n_rows, LANES), lambda j: (j, 0, 0)),
        scratch_shapes=[pltpu.VMEM((bsz * n_i, HY_TB), f32)],
        compiler_params=_cparams("parallel"),
        name="hyena_mixer",
    )(cw.reshape(-1), cb, x3, x3, x3, r)
    return out.reshape(ch, bsz * seq)


_IN_SIZES = (NH * GLA_DK, NH * GLA_DK, MIX_W, MIX_W, 2 * GLA_RANK, MIX_W, MIX_W, MIX_W,
             MIX_W, MIX_W + 2 * SSD_G * SSD_N, 2 * NH, 3 * MIX_W)
_IN_OFFS = np.concatenate([[0], np.cumsum(_IN_SIZES)]).tolist()


def _prep_w_in(w_in):
    (gq, gk, gv, gg, gr, dq, dk, dv, sz, sxbc, sdt, hy) = [w_in[:, _IN_OFFS[i]:_IN_OFFS[i + 1]]
                                                          for i in range(len(_IN_SIZES))]
    d = w_in.shape[0]
    zeros = lambda n: jnp.zeros((d, n), w_in.dtype)
    misc = jnp.concatenate([sxbc[:, MIX_W:], gr, zeros(LANES - 2 * GLA_RANK), sdt, zeros(LANES - 2 * NH)], axis=1)
    w_tok = jnp.concatenate([gq, gk, gv, gg, dq, dk, dv, sz, sxbc[:, :MIX_W], misc], axis=1)
    return w_tok.astype(bf16), hy.T.astype(bf16)


def _prep_gate_w(gate_w):
    w = jnp.zeros((2, LANES, gate_w.shape[2]), f32)
    for d in range(2):
        w = w.at[d, d * GLA_RANK:(d + 1) * GLA_RANK, :].set(gate_w[d])
    hi = w.astype(bf16)
    return hi, (w - hi.astype(f32)).astype(bf16)


def _rope_tables(seq):
    rows = seq // GRID_W
    r, col = jnp.meshgrid(jnp.arange(rows), jnp.arange(GRID_W), indexing='ij')
    r = r.reshape(-1).astype(f32)
    col = col.reshape(-1).astype(f32)
    nf = DIFF_DK // 4
    inv = ROPE_BASE ** (-jnp.arange(nf, dtype=f32) / nf)
    ang = jnp.concatenate([r[:, None] * inv, col[:, None] * inv], axis=-1)
    cos, sin = jnp.cos(ang), jnp.sin(ang)
    reps = LANES // DIFF_DK
    return jnp.tile(jnp.concatenate([cos, cos], axis=-1), (1, reps)), jnp.tile(jnp.concatenate([-sin, sin], axis=-1), (1, reps))


def _const_tables():
    hsum = np.kron(np.eye(NH, dtype=np.float32), np.ones((HEAD_DIM, HEAD_DIM), np.float32))
    expand = np.zeros((LANES, MIX_W), np.float32)
    selb = np.zeros((2 * LANES, MIX_W), np.float32)
    selc = np.zeros((2 * LANES, MIX_W), np.float32)
    rep = NH // SSD_G
    for h in range(NH):
        expand[h, h * HEAD_DIM:(h + 1) * HEAD_DIM] = 1.0
        g = h // rep
        for n in range(SSD_N):
            selb[g * SSD_N + n, h * SSD_N + n] = 1.0
            selc[SSD_G * SSD_N + g * SSD_N + n, h * SSD_N + n] = 1.0
    return tuple(jnp.asarray(a, dtype=bf16) for a in (hsum, expand, selb, selc))


def _pack_state(s):
    bsz, _, h, a, b = s.shape
    return s.transpose(0, 1, 4, 2, 3).reshape(bsz, 2, b, h * a)


def _unpack_state(st, a):
    bsz, _, b, _ = st.shape
    return st.reshape(bsz, 2, b, NH, a).transpose(0, 1, 3, 4, 2)


def _layer(x, mod, rows_per_mod, lw, layer, consts, bsz, seq, lam_init, r_filt, ctx, rope_tabs, final_g,
           final_norm):
    hsum, expand, selb, selc = consts
    p, hy_t = in_proj(x, mod, lw['norm1_g'], lw['w_tok'], lw['w_hyT'], rows_per_mod)
    if ctx is None:
        gla_s0 = jnp.zeros((bsz, 2, HEAD_DIM, NH * GLA_DK), f32)
        ssd_s0 = jnp.zeros((bsz, 2, SSD_N, MIX_W), f32)
        dctx = None
    else:
        ctx_k, ctx_v, gla_s, ssd_s = ctx
        gla_s0 = _pack_state(gla_s)
        ssd_s0 = _pack_state(ssd_s)
        dctx = (ctx_k.reshape(bsz, -1, MIX_W), ctx_v.reshape(bsz, -1, MIX_W))
    o_gla, gla_fin = gla_mixer(p, lw['gla_gw_hi'], lw['gla_gw_lo'], lw['gla_gb'], lw['gla_ng'], hsum, gla_s0,
                               bsz, seq)
    o_diff = diff_mixer(p, lw['lam_p'], lw['diff_ng'], bsz, seq, lam_init, dctx, rope_tabs)
    xbc = ssd_conv(p, lw['ssd_conv_w'], lw['ssd_conv_b'], bsz, seq)
    o_ssd, ssd_fin = ssd_mixer(p, xbc, lw['ssd_dtb'], lw['ssd_alog'], lw['ssd_dexp'], lw['ssd_ng'],
                               expand, selb, selc, ssd_s0, bsz, seq)
    o_hy = hyena_mixer(hy_t, r_filt, lw['hy_cw'], lw['hy_cb'], bsz, seq)
    x = out_proj(x, mod, o_gla, o_diff, o_ssd, o_hy, lw['w_out'], layer, rows_per_mod)
    x = mlp(x, mod, lw['norm2_g'], lw['mlp_w1'], lw['mlp_w2'], layer, final_g, rows_per_mod, final_norm)
    return x, p, gla_fin, ssd_fin


def kernel(x_prompt, x_sample, cache_diff_k, cache_diff_v, state_gla, state_ssd, c, c_ctx, ada_w, ada_b, norm1_g, norm2_g, w_in, w_out, gla_gate_w, gla_gate_b, gla_norm_g, diff_lambda, diff_norm_g, ssd_conv_w, ssd_conv_b, ssd_dt_bias, ssd_a_log, ssd_d, ssd_norm_g, hy_conv_w, hy_conv_b, hy_f_w1, hy_f_b1, hy_f_w2, hy_f_b2, hy_f_w3, hy_f_b3, hy_sin_w, hy_decay, hy_skip, mlp_w1, mlp_w2, final_g):
    bp, lp, d = x_prompt.shape
    bs, ls, _ = x_sample.shape
    depth = w_in.shape[0]
    consts = _const_tables()
    rope_tabs = _rope_tables(ls)
    n_c = 16
    cvec = jnp.concatenate([c_ctx[None], c, jnp.zeros((n_c - 1 - bs, d), f32)], axis=0)
    hp = x_prompt.reshape(bp * lp, d)
    hs = x_sample.reshape(bs * ls, d)
    ks_, vs_, gs_, ss_ = [], [], [], []
    w_out_b, mlp_w1_b, mlp_w2_b = w_out.astype(bf16), mlp_w1.astype(bf16), mlp_w2.astype(bf16)
    for l in range(depth):
        w_tok, w_hyT = _prep_w_in(w_in[l])
        pad_l = lambda a: jnp.pad(a, (0, LANES - a.shape[0])).reshape(1, LANES)
        gw_hi, gw_lo = _prep_gate_w(gla_gate_w[l])
        lw = dict(
            norm1_g=norm1_g[l], norm2_g=norm2_g[l], w_tok=w_tok, w_hyT=w_hyT, w_out=w_out_b,
            mlp_w1=mlp_w1_b, mlp_w2=mlp_w2_b,
            gla_gw_hi=gw_hi, gla_gw_lo=gw_lo, gla_gb=gla_gate_b[l].reshape(2, 1, -1), gla_ng=jnp.tile(gla_norm_g[l], NH).reshape(1, MIX_W),
            lam_p=jnp.pad(diff_lambda[l], ((0, 4), (0, LANES - DIFF_DK))),
            diff_ng=jnp.tile(diff_norm_g[l], LANES // HEAD_DIM).reshape(1, LANES),
            ssd_conv_w=ssd_conv_w[l], ssd_conv_b=ssd_conv_b[l],
            ssd_dtb=pad_l(ssd_dt_bias[l].reshape(-1)), ssd_alog=pad_l(ssd_a_log[l].reshape(-1)),
            ssd_dexp=jnp.repeat(ssd_d[l], HEAD_DIM).reshape(1, MIX_W), ssd_ng=ssd_norm_g[l].reshape(1, MIX_W),
            hy_cw=hy_conv_w[l], hy_cb=hy_conv_b[l],
        )
        fp = dict(w1=hy_f_w1[l], b1=hy_f_b1[l], w2=hy_f_w2[l], b2=hy_f_b2[l], w3=hy_f_w3[l], b3=hy_f_b3[l],
                  sin_w=hy_sin_w[l], decay=hy_decay[l], skip=hy_skip[l])
        mod = ada_mod(cvec, ada_w, ada_b, l).reshape(n_c, 6, d)
        lam_init = 0.8 - 0.6 * math.exp(-0.3 * l)
        last = l == depth - 1
        hp, p_p, g_l, s_l = _layer(hp, mod[0:1], bp * lp, lw, l, consts, bp, lp, lam_init, hy_filters(fp, lp),
                                   None, None, final_g, last)
        ks_.append(p_p[:, T_DK * PT:(T_DK + 1) * PT].reshape(bp, lp, NH, 2 * DIFF_DK))
        vs_.append(p_p[:, T_DV * PT:(T_DV + 1) * PT].reshape(bp, lp, NH, HEAD_DIM))
        gs_.append(_unpack_state(g_l, GLA_DK))
        ss_.append(_unpack_state(s_l, HEAD_DIM))
        hs, _, _, _ = _layer(hs, mod[1:1 + bs], ls, lw, l, consts, bs, ls, lam_init, hy_filters(fp, ls),
                             (cache_diff_k[:, l], cache_diff_v[:, l], state_gla[:, l], state_ssd[:, l]),
                             rope_tabs, final_g, last)
    return (hp.reshape(bp, lp, d), hs.reshape(bs, ls, d), jnp.stack(ks_, axis=1), jnp.stack(vs_, axis=1),
            jnp.stack(gs_, axis=1), jnp.stack(ss_, axis=1))
```

```python
import functools
import math

import numpy as np
import jax
import jax.numpy as jnp
from jax import lax
from jax.experimental import pallas as pl
from jax.experimental.pallas import tpu as pltpu

f32 = jnp.float32
bf16 = jnp.bfloat16
HI = lax.Precision.HIGHEST

LANES = 128
VMEM_LIMIT = 56 * 2**20

D_MODEL = 2048
GRID_W = 64
MIX_W = D_MODEL // 4
HEAD_DIM = 64
NH = MIX_W // HEAD_DIM
D_FF = 4 * D_MODEL
NORM_EPS = 1e-6
LOG2E = 1.0 / math.log(2.0)
GLA_DK = HEAD_DIM // 2
GLA_RANK = 16
GLA_TAU = 16.0
GLA_CHUNK = 64
DIFF_DK = HEAD_DIM // 2
ROPE_BASE = 10000.0
SSD_N = 64
SSD_G = 2
SSD_CHUNK = 128
HY_BANDS = 16
HY_EMB = 2 * HY_BANDS + 1
HY_EMB_PAD = 40
HY_HID = 64
DIFF_KEY_CHUNK = 512
DIFF_Q_SUB = 512
MLP_TF = 512
HY_TB = 256

PT = 512
(T_GQK, T_GV, T_GG, T_DQ, T_DK, T_DV, T_SZ, T_SX, T_MISC) = range(9)
N_PT = 9
MISC_GR = 256
MISC_DT = 384
SH1, SC1, G1, SH2, SC2, G2 = range(6)

NT_DIMS = (((1,), (1,)), ((), ()))
TN_DIMS = (((0,), (0,)), ((), ()))


def _cparams(*sem):
    return pltpu.CompilerParams(dimension_semantics=sem, vmem_limit_bytes=VMEM_LIMIT)


def _sigmoid(x):
    return 1.0 / (1.0 + jnp.exp(-x))


def _silu(x):
    return x * _sigmoid(x)


def _softplus(x):
    return jnp.maximum(x, 0.0) + jnp.log1p(jnp.exp(-jnp.abs(x)))


def _log_sigmoid(x):
    return jnp.minimum(x, 0.0) - jnp.log1p(jnp.exp(-jnp.abs(x)))


def _tri_mask(rows, c, sgn):
    r_i = lax.broadcasted_iota(jnp.int32, (rows, c), 0) & (c - 1)
    c_i = lax.broadcasted_iota(jnp.int32, (rows, c), 1)
    return (r_i - c_i) * sgn >= 0


def _expand_state(s, row_w, col_w):
    rows, cols = NH * row_w, NH * col_w
    bd = (lax.broadcasted_iota(jnp.int32, (rows, cols), 0) // row_w
          == lax.broadcasted_iota(jnp.int32, (rows, cols), 1) // col_w)
    return jnp.where(bd, jnp.concatenate([s] * NH, axis=0), 0.0)


def _compact_state(st):
    row_w = st.shape[0] // NH
    out = st[0:row_w]
    for h in range(1, NH):
        out += st[h * row_w:(h + 1) * row_w]
    return out


def _split(x, n):
    parts = []
    for _ in range(n - 1):
        hi = x.astype(bf16)
        parts.append(hi)
        x = x - hi.astype(f32)
    parts.append(x.astype(bf16))
    return parts


def _sel_dot(a, b, dims=None, terms=3):
    if a.dtype == bf16:
        prods = [_dot(a, p, dims) for p in _split(b, terms)]
    else:
        prods = [_dot(p, b, dims) for p in _split(a, terms)]
    return functools.reduce(lambda x, y: x + y, prods)


def _dot(a, b, dims=None, precision=None):
    if dims is None:
        return jnp.dot(a, b, preferred_element_type=f32, precision=precision)
    return lax.dot_general(a, b, dims, preferred_element_type=f32, precision=precision)


def _ada_kernel(c_ref, w_ref, b_ref, o_ref):
    s = _silu(c_ref[...]).astype(bf16)
    o_ref[...] = _dot(s, w_ref[...].astype(bf16)) + b_ref[...]


def ada_mod(cvec, w, b, layer, tn=1024):
    m, d = cvec.shape
    n = w.shape[2]
    return pl.pallas_call(
        _ada_kernel,
        out_shape=jax.ShapeDtypeStruct((m, n), f32),
        grid=(n // tn,),
        in_specs=[pl.BlockSpec((m, d), lambda j: (0, 0)),
                  pl.BlockSpec((None, d, tn), lambda j: (layer, 0, j)),
                  pl.BlockSpec((None, 1, tn), lambda j: (layer, 0, j))],
        out_specs=pl.BlockSpec((m, tn), lambda j: (0, j)),
        compiler_params=_cparams("arbitrary"),
        name="ada_mod",
    )(cvec, w, b.reshape(b.shape[0], 1, n))


def _norm_mod(x, g, sc, sh):
    ms = jnp.mean(x * x, axis=-1, keepdims=True)
    return (x * lax.rsqrt(ms + NORM_EPS) * g) * (1.0 + sc) + sh


def _inproj_kernel(x_ref, mod_ref, g_ref, wt_ref, wh_ref, p_ref, h_ref, u_scr):
    j = pl.program_id(1)

    @pl.when(j == 0)
    def _():
        u = _norm_mod(x_ref[...], g_ref[...], mod_ref[SC1:SC1 + 1, :], mod_ref[SH1:SH1 + 1, :])
        u_scr[...] = u.astype(bf16)

    @pl.when(j < N_PT)
    def _():
        p_ref[...] = _dot(u_scr[...], wt_ref[...])

    @pl.when(j >= N_PT)
    def _():
        h_ref[...] = _dot(wh_ref[...], u_scr[...], NT_DIMS)


def in_proj(x, mod, g, w_tok, w_hyT, rows_per_mod, tm=1024):
    t, d = x.shape
    assert t % tm == 0 and rows_per_mod % tm == 0, (t, rows_per_mod, tm)
    n_hy = w_hyT.shape[0] // PT
    return pl.pallas_call(
        _inproj_kernel,
        out_shape=(jax.ShapeDtypeStruct((N_PT, t, PT), f32), jax.ShapeDtypeStruct((n_hy * PT, t), f32)),
        grid=(t // tm, N_PT + n_hy),
        in_specs=[pl.BlockSpec((tm, d), lambda i, j: (i, 0)),
                  pl.BlockSpec((None, 6, d), lambda i, j: ((i * tm) // rows_per_mod, 0, 0)),
                  pl.BlockSpec((1, d), lambda i, j: (0, 0)),
                  pl.BlockSpec((None, d, PT), lambda i, j: (jnp.minimum(j, N_PT - 1), 0, 0)),
                  pl.BlockSpec((PT, d), lambda i, j: (jnp.maximum(j - N_PT, 0), 0))],
        out_specs=(pl.BlockSpec((None, tm, PT), lambda i, j: (jnp.minimum(j, N_PT - 1), i, 0)),
                   pl.BlockSpec((PT, tm), lambda i, j: (jnp.maximum(j - N_PT, 0), i))),
        scratch_shapes=[pltpu.VMEM((tm, d), bf16)],
        compiler_params=_cparams("parallel", "arbitrary"),
        name="in_proj",
    )(x, mod, g.reshape(1, d), w_tok, w_hyT)


def _outproj_kernel(x_ref, mod_ref, a_ref, b_ref, c_ref, dT_ref, w_ref, o_ref):
    acc = _dot(a_ref[...], w_ref[0:MIX_W, :])
    acc += _dot(b_ref[...], w_ref[MIX_W:2 * MIX_W, :])
    acc += _dot(c_ref[...], w_ref[2 * MIX_W:3 * MIX_W, :])
    acc += _dot(dT_ref[...].astype(bf16), w_ref[3 * MIX_W:4 * MIX_W, :], TN_DIMS)
    o_ref[...] = x_ref[...] + mod_ref[G1:G1 + 1, :] * acc


def out_proj(x, mod, o_a, o_b, o_c, o_dT, w_out, layer, rows_per_mod, tm=512):
    t, d = x.shape
    assert t % tm == 0 and rows_per_mod % tm == 0, (t, rows_per_mod, tm)
    tok = lambda i: (i, 0)
    return pl.pallas_call(
        _outproj_kernel,
        out_shape=jax.ShapeDtypeStruct((t, d), f32),
        grid=(t // tm,),
        in_specs=[pl.BlockSpec((tm, d), tok),
                  pl.BlockSpec((None, 6, d), lambda i: ((i * tm) // rows_per_mod, 0, 0)),
                  pl.BlockSpec((tm, MIX_W), tok), pl.BlockSpec((tm, MIX_W), tok), pl.BlockSpec((tm, MIX_W), tok),
                  pl.BlockSpec((MIX_W, tm), lambda i: (0, i)),
                  pl.BlockSpec((None, d, d), lambda i: (layer, 0, 0))],
        out_specs=pl.BlockSpec((tm, d), tok),
        compiler_params=_cparams("parallel"),
        name="out_proj",
    )(x, mod, o_a, o_b, o_c, o_dT, w_out)


def _mlp_kernel(x_ref, mod_ref, g_ref, w1_ref, w2_ref, fg_ref, o_ref, u_scr, *, final_norm):
    j = pl.program_id(1)

    @pl.when(j == 0)
    def _():
        u = _norm_mod(x_ref[...], g_ref[...], mod_ref[SC2:SC2 + 1, :], mod_ref[SH2:SH2 + 1, :])
        u_scr[...] = u.astype(bf16)
        o_ref[...] = jnp.zeros_like(o_ref)

    h = jnp.maximum(_dot(u_scr[...], w1_ref[...]), 0.0)
    hb = (h * h).astype(bf16)
    d = o_ref.shape[1]
    for n in range(d // PT):
        o_ref[:, n * PT:(n + 1) * PT] += _dot(hb, w2_ref[:, n * PT:(n + 1) * PT])

    @pl.when(j == pl.num_programs(1) - 1)
    def _():
        y = x_ref[...] + mod_ref[G2:G2 + 1, :] * o_ref[...]
        if final_norm:
            ms = jnp.mean(y * y, axis=-1, keepdims=True)
            y = y * lax.rsqrt(ms + NORM_EPS) * fg_ref[...]
        o_ref[...] = y


def mlp(x, mod, g, w1, w2, layer, final_g, rows_per_mod, final_norm, tm=1024):
    t, d = x.shape
    assert t % tm == 0 and rows_per_mod % tm == 0, (t, rows_per_mod, tm)
    n_f, tf = w1.shape[1], w1.shape[3]
    return pl.pallas_call(
        functools.partial(_mlp_kernel, final_norm=final_norm),
        out_shape=jax.ShapeDtypeStruct((t, d), f32),
        grid=(t // tm, n_f),
        in_specs=[pl.BlockSpec((tm, d), lambda i, j: (i, 0)),
                  pl.BlockSpec((None, 6, d), lambda i, j: ((i * tm) // rows_per_mod, 0, 0)),
                  pl.BlockSpec((1, d), lambda i, j: (0, 0)),
                  pl.BlockSpec((None, None, d, tf), lambda i, j: (layer, j, 0, 0)),
                  pl.BlockSpec((None, tf, d), lambda i, j: (layer, j, 0)),
                  pl.BlockSpec((1, d), lambda i, j: (0, 0))],
        out_specs=pl.BlockSpec((tm, d), lambda i, j: (i, 0)),
        scratch_shapes=[pltpu.VMEM((tm, d), bf16)],
        compiler_params=_cparams("parallel", "arbitrary"),
        name="mlp",
    )(x, mod, g.reshape(1, d), w1, w2, final_g.reshape(1, d))


def _gla_kernel(qk_ref, v_ref, gg_ref, misc_ref, gwh_ref, gwl_ref, gb_ref, ng_ref, hsum_ref, s0_ref,
                o_ref, sfin_ref, of_scr, st_scr, la_scr, *, nblk, tl):
    ph = pl.program_id(1)
    i = pl.program_id(2)
    fwd = ph == 0
    c = GLA_CHUNK
    nch = tl // c
    qkw = NH * GLA_DK

    @pl.when(i == 0)
    def _():
        st_scr[...] = _expand_state(s0_ref[...], HEAD_DIM, GLA_DK)

    gr_hi, gr_lo = _split(misc_ref[:, MISC_GR:MISC_GR + LANES], 2)
    logits = (_dot(gr_hi, gwh_ref[...]) + _dot(gr_lo, gwh_ref[...]) + _dot(gr_hi, gwl_ref[...])) + gb_ref[...]
    la_scr[...] = _log_sigmoid(logits) * (1.0 / GLA_TAU)

    sgn = jnp.where(fwd, 1, -1)
    trib = jnp.where(_tri_mask(c, c, sgn), 1.0, 0.0).astype(bf16)
    tri8 = _tri_mask(NH * c, c, sgn)
    qk_head = lax.broadcasted_iota(jnp.int32, (1, qkw), 1) // GLA_DK
    v_head = lax.broadcasted_iota(jnp.int32, (1, MIX_W), 1) // HEAD_DIM
    bd = (lax.broadcasted_iota(jnp.int32, (MIX_W, qkw), 0) // HEAD_DIM
          == lax.broadcasted_iota(jnp.int32, (MIX_W, qkw), 1) // GLA_DK)
    brow = pl.multiple_of(jnp.where(fwd, i, nblk - 1 - i) * tl, tl)

    def chunk(j, carry):
        cj = jnp.where(fwd, j, nch - 1 - j)
        off = pl.multiple_of(cj * c, c)
        q = qk_ref[pl.ds(off, c), 0:qkw] * (GLA_DK ** -0.5)
        k = qk_ref[pl.ds(off, c), qkw:2 * qkw]
        vb = v_ref[pl.ds(off, c), :].astype(bf16)
        b = _sel_dot(trib, la_scr[pl.ds(off, c), :])
        btot = jnp.where(fwd, b[c - 1:c, :], b[0:1, :])
        qd = q * jnp.exp(b)
        kd = (k * jnp.exp(-b)).astype(bf16)
        kt = (k * jnp.exp(btot - b)).astype(bf16)
        qst = jnp.concatenate([jnp.where(qk_head == h, qd, 0.0) for h in range(NH)], axis=0).astype(bf16)
        att = jnp.where(tri8, _dot(qst, kd, NT_DIMS), 0.0)
        r = _dot(att.astype(bf16), vb)
        o = jnp.where(v_head == 0, r[0:c], 0.0)
        for h in range(1, NH):
            o += jnp.where(v_head == h, r[h * c:(h + 1) * c], 0.0)
        st = st_scr[...]
        o += _dot(qd.astype(bf16), st.astype(bf16), NT_DIMS)
        kv = _dot(vb, kt, TN_DIMS)
        st_scr[...] = st * jnp.exp(btot) + jnp.where(bd, kv, 0.0)
        of_scr[pl.ds(pl.multiple_of(brow + off, c), c), :] += o
        return carry

    @pl.when(fwd)
    def _():
        of_scr[pl.ds(brow, tl), :] = jnp.zeros((tl, MIX_W), f32)

    lax.fori_loop(0, nch, chunk, 0, unroll=True)

    @pl.when(ph == 1)
    def _():
        ot = of_scr[pl.ds(brow, tl), :]
        ss = _sel_dot(ot * ot, hsum_ref[...], terms=2) * (1.0 / HEAD_DIM)
        o_ref[...] = (ot * lax.rsqrt(ss + NORM_EPS) * ng_ref[...] * _silu(gg_ref[...])).astype(bf16)

    @pl.when(i == nblk - 1)
    def _():
        sfin_ref[...] = _compact_state(st_scr[...])


def gla_mixer(p, gw_hi, gw_lo, gb, ng, hsum, s0, bsz, seq, tl=256):
    nblk = seq // tl
    qkw = NH * GLA_DK
    blk = lambda ph, i: jnp.where(ph == 0, i, nblk - 1 - i)
    tile = lambda col: pl.BlockSpec((None, tl, PT), lambda b, ph, i: (col, b * nblk + blk(ph, i), 0))
    const = lambda shape: pl.BlockSpec(shape, lambda b, ph, i: (0,) * len(shape))
    per_dir = lambda rows: pl.BlockSpec((None, rows, qkw), lambda b, ph, i: (ph, 0, 0))
    state = pl.BlockSpec((None, None, HEAD_DIM, qkw), lambda b, ph, i: (b, ph, 0, 0))
    return pl.pallas_call(
        functools.partial(_gla_kernel, nblk=nblk, tl=tl),
        out_shape=(jax.ShapeDtypeStruct((bsz * seq, MIX_W), bf16),
                   jax.ShapeDtypeStruct((bsz, 2, HEAD_DIM, qkw), f32)),
        grid=(bsz, 2, nblk),
        in_specs=[tile(T_GQK), tile(T_GV), tile(T_GG), tile(T_MISC),
                  per_dir(LANES), per_dir(LANES), per_dir(1), const((1, MIX_W)), const((MIX_W, MIX_W)), state],
        out_specs=(pl.BlockSpec((tl, MIX_W),
                                lambda b, ph, i: (b * nblk + jnp.where(ph == 0, nblk - 1, nblk - 1 - i), 0)),
                   state),
        scratch_shapes=[pltpu.VMEM((seq, MIX_W), f32), pltpu.VMEM((MIX_W, qkw), f32), pltpu.VMEM((tl, qkw), f32)],
        compiler_params=_cparams("parallel", "arbitrary", "arbitrary"),
        name="gla_mixer",
    )(p, p, p, p, gw_hi, gw_lo, gb, ng, hsum, s0)


def _rope(x, cos, sin_signed):
    lane = lax.broadcasted_iota(jnp.int32, (1, LANES), 1)
    half = DIFF_DK // 2
    rot = jnp.where((lane % DIFF_DK) < half, pltpu.roll(x, LANES - half, 1), pltpu.roll(x, half, 1))
    return x * cos + rot * sin_signed


def _diff_kernel(*refs, seq, ctx_len, rope, lam_init):
    if rope:
        (q_ref, k_ref, v_ref, ck_ref, cv_ref, cq_ref, sq_ref, ckk_ref, skk_ref, lam_ref, ng_ref,
         o_ref, kb_scr, vt_scr) = refs
    else:
        q_ref, k_ref, v_ref, lam_ref, ng_ref, o_ref, kb_scr, vt_scr = refs
    qi = pl.program_id(2)
    lane = lax.broadcasted_iota(jnp.int32, (1, LANES), 1)
    first = lane < HEAD_DIM

    def put_v(v, lo, hi):
        vt = v.T
        ones = jnp.ones((HEAD_DIM, hi - lo), f32)
        vt_scr[0, :, lo:hi] = jnp.concatenate([vt[0:HEAD_DIM], ones], axis=0).astype(bf16)
        vt_scr[1, :, lo:hi] = jnp.concatenate([vt[HEAD_DIM:], ones], axis=0).astype(bf16)

    @pl.when(qi == 0)
    def _():
        k = k_ref[...]
        if rope:
            k = _rope(k, ckk_ref[...], skk_ref[...])
        kb_scr[0:seq, :] = k.astype(bf16)
        put_v(v_ref[...], 0, seq)
        if ctx_len:
            kb_scr[seq:seq + ctx_len, :] = ck_ref[...].astype(bf16)
            put_v(cv_ref[...], seq, seq + ctx_len)

    q = q_ref[...]
    if rope:
        q = _rope(q, cq_ref[...], sq_ref[...])
    q = q * (DIFF_DK ** -0.5 * LOG2E)
    n_keys = seq + ctx_len
    ck = max(w for w in range(LANES, DIFF_KEY_CHUNK + 1, LANES) if n_keys % w == 0)
    nck = n_keys // ck
    tq = q.shape[0]
    sub = min(DIFF_Q_SUB, tq)
    jobs = [(qs, g) for qs in range(tq // sub) for g in range(4)]
    lp = lam_ref[...]
    lam = (jnp.exp(jnp.sum(lp[0:1] * lp[1:2], keepdims=True)) - jnp.exp(jnp.sum(lp[2:3] * lp[3:4], keepdims=True))
           + lam_init)

    def finish(qs, res):
        o = jnp.concatenate([res[0] - lam * res[1], res[2] - lam * res[3]], axis=0).T
        o2 = o * o
        ss_a = jnp.sum(jnp.where(first, o2, 0.0), axis=-1, keepdims=True)
        ss_b = jnp.sum(jnp.where(first, 0.0, o2), axis=-1, keepdims=True)
        inv = jnp.where(first, lax.rsqrt(ss_a * (1.0 / HEAD_DIM) + NORM_EPS),
                        lax.rsqrt(ss_b * (1.0 / HEAD_DIM) + NORM_EPS))
        o_ref[qs * sub:(qs + 1) * sub, :] = (o * inv * ng_ref[...] * (1.0 - lam_init)).astype(bf16)

    res = []
    s_prev = m_prev = None
    for j in range(len(jobs) + 1):
        s_cur, mvec, acc = [], None, None
        if j < len(jobs):
            qs, g = jobs[j]
            qm = jnp.where(lane // DIFF_DK == g, q[qs * sub:(qs + 1) * sub], 0.0).astype(bf16)
        for c in range(nck):
            if j < len(jobs):
                s = _dot(qm, kb_scr[c * ck:(c + 1) * ck, :], NT_DIMS)
                s_cur.append(s)
                for k in range(ck // LANES):
                    blk = s[:, k * LANES:(k + 1) * LANES]
                    mvec = blk if mvec is None else jnp.maximum(mvec, blk)
            if j > 0:
                p = jnp.exp2(s_prev[c] - m_prev).astype(bf16)
                part = _dot(vt_scr[jobs[j - 1][1] // 2, :, c * ck:(c + 1) * ck], p, NT_DIMS)
                acc = part if acc is None else acc + part
        if j > 0:
            res.append(acc[0:HEAD_DIM] / acc[HEAD_DIM:HEAD_DIM + 1])
            if len(res) == 4:
                finish(jobs[j - 1][0], res)
                res = []
        if j < len(jobs):
            s_prev, m_prev = s_cur, jnp.max(mvec, axis=-1, keepdims=True)


def diff_mixer(p, lam_p, ng, bsz, seq, lam_init, ctx=None, rope_tabs=None, tq=512):
    tq = min(tq, seq)
    nq = seq // tq
    npair = MIX_W // LANES
    rope = rope_tabs is not None
    ctx_len = ctx[0].shape[1] if ctx is not None else 0
    assert rope == (ctx is not None)
    q_spec = pl.BlockSpec((None, tq, LANES), lambda b, hp, i: (T_DQ, b * nq + i, hp))
    k_spec = pl.BlockSpec((None, seq, LANES), lambda b, hp, i: (T_DK, b, hp))
    v_spec = pl.BlockSpec((None, seq, LANES), lambda b, hp, i: (T_DV, b, hp))
    const = lambda shape: pl.BlockSpec(shape, lambda b, hp, i: (0,) * len(shape))
    args = [p, p, p]
    specs = [q_spec, k_spec, v_spec]
    if rope:
        c_spec = pl.BlockSpec((None, ctx_len, LANES), lambda b, hp, i: (b, 0, hp))
        tab_q = pl.BlockSpec((tq, LANES), lambda b, hp, i: (i, 0))
        args += [ctx[0], ctx[1], rope_tabs[0], rope_tabs[1], rope_tabs[0], rope_tabs[1]]
        specs += [c_spec, c_spec, tab_q, tab_q, const((seq, LANES)), const((seq, LANES))]
    args += [lam_p, ng]
    specs += [const((8, LANES)), const((1, LANES))]
    return pl.pallas_call(
        functools.partial(_diff_kernel, seq=seq, ctx_len=ctx_len, rope=rope, lam_init=lam_init),
        out_shape=jax.ShapeDtypeStruct((bsz * seq, MIX_W), bf16),
        grid=(bsz, npair, nq),
        in_specs=specs,
        out_specs=pl.BlockSpec((tq, LANES), lambda b, hp, i: (b * nq + i, hp)),
        scratch_shapes=[pltpu.VMEM((seq + ctx_len, LANES), bf16), pltpu.VMEM((2, LANES, seq + ctx_len), bf16)],
        compiler_params=_cparams("parallel", "parallel", "arbitrary"),
        name="diff_mixer",
    )(*args)


def _dwconv_kernel(x_ref, w_ref, b_ref, o_ref):
    x = x_ref[...]
    n = x.shape[0]
    row = lax.broadcasted_iota(jnp.int32, x.shape, 0)
    xm = jnp.where(row == 0, 0.0, pltpu.roll(x, 1, 0))
    xp = jnp.where(row == n - 1, 0.0, pltpu.roll(x, n - 1, 0))
    y = xm * w_ref[0:1, :] + x * w_ref[1:2, :] + xp * w_ref[2:3, :] + b_ref[...]
    o_ref[...] = _silu(y)


def ssd_conv(p, w, b, bsz, seq):
    ncol = w.shape[1] // LANES
    per = PT // LANES
    return pl.pallas_call(
        _dwconv_kernel,
        out_shape=jax.ShapeDtypeStruct((bsz * seq, w.shape[1]), f32),
        grid=(bsz, ncol),
        in_specs=[pl.BlockSpec((None, seq, LANES),
                               lambda bb, j: (jnp.where(j < per, T_SX, T_MISC), bb, jnp.where(j < per, j, j - per))),
                  pl.BlockSpec((3, LANES), lambda bb, j: (0, j)),
                  pl.BlockSpec((1, LANES), lambda bb, j: (0, j))],
        out_specs=pl.BlockSpec((seq, LANES), lambda bb, j: (bb, j)),
        compiler_params=_cparams("parallel", "parallel"),
        name="ssd_conv",
    )(p, w, b.reshape(1, -1))


def _ssd_kernel(xs_ref, bc_ref, z_ref, dt_ref, dtb_ref, alog_ref, dexp_ref, ng_ref, exp_ref, selb_ref, selc_ref,
                s0_ref, o_ref, sfin_ref, yf_scr, st_scr, dt_scr, dta_scr, *, nblk, tl):
    ph = pl.program_id(1)
    i = pl.program_id(2)
    fwd = ph == 0
    c = SSD_CHUNK
    nch = tl // c
    rep = NH // SSD_G

    @pl.when(i == 0)
    def _():
        st_scr[...] = _expand_state(s0_ref[...], SSD_N, HEAD_DIM)

    dtv = _softplus(dt_ref[...] + dtb_ref[...])
    dtv = jnp.where(fwd, dtv, pltpu.roll(dtv, LANES - NH, 1))
    av = -jnp.exp(alog_ref[...])
    av = jnp.where(fwd, av, pltpu.roll(av, LANES - NH, 1))
    dt_scr[...] = dtv
    dta_scr[...] = dtv * av

    tri = _tri_mask(c, c, jnp.where(fwd, 1, -1))
    trib = jnp.where(tri, 1.0, 0.0).astype(bf16)
    head = lax.broadcasted_iota(jnp.int32, (1, MIX_W), 1) // HEAD_DIM
    bd = (lax.broadcasted_iota(jnp.int32, (MIX_W, MIX_W), 0) // SSD_N
          == lax.broadcasted_iota(jnp.int32, (MIX_W, MIX_W), 1) // HEAD_DIM)
    brow = pl.multiple_of(jnp.where(fwd, i, nblk - 1 - i) * tl, tl)

    def chunk(j, carry):
        cj = jnp.where(fwd, j, nch - 1 - j)
        off = pl.multiple_of(cj * c, c)
        dtc = dt_scr[pl.ds(off, c), :]
        cum = _sel_dot(trib, dta_scr[pl.ds(off, c), :])
        cum_t = cum.T
        dt_t = dtc.T
        earg = _sel_dot(cum, exp_ref[...])
        elast = jnp.where(fwd, earg[c - 1:c, :], earg[0:1, :])
        dtx = _sel_dot(dtc, exp_ref[...])
        bcb = bc_ref[pl.ds(off, c), :].astype(bf16)
        bt = _dot(bcb, selb_ref[...])
        ct = _dot(bcb, selc_ref[...])
        x = xs_ref[pl.ds(off, c), :]
        xb = x.astype(bf16)
        st = st_scr[...]
        y = _dot((ct * jnp.exp(earg)).astype(bf16), st.astype(bf16))
        btail = (bt * jnp.exp(elast - earg) * dtx).astype(bf16)
        new = _dot(btail, xb, TN_DIMS)
        st_scr[...] = st * jnp.exp(elast) + jnp.where(bd, new, 0.0)
        btb = bt.astype(bf16)
        cb = [_dot(jnp.where(head == g * rep, ct, 0.0).astype(bf16), btb, NT_DIMS) for g in range(SSD_G)]
        ws = []
        for h in range(NH):
            seg = cum[:, h:h + 1] - cum_t[h:h + 1, :]
            lm = jnp.exp(jnp.where(tri, seg, -jnp.inf))
            ws.append((cb[h // rep] * lm * dt_t[h:h + 1, :]).astype(bf16))
        r = _dot(jnp.concatenate(ws, axis=0), xb)
        for h in range(NH):
            y += jnp.where(head == h, r[h * c:(h + 1) * c], 0.0)
        yf_scr[pl.ds(pl.multiple_of(brow + off, c), c), :] += y
        return carry

    @pl.when(fwd)
    def _():
        yf_scr[pl.ds(brow, tl), :] = jnp.zeros((tl, MIX_W), f32)

    lax.fori_loop(0, nch, chunk, 0, unroll=True)

    @pl.when(ph == 1)
    def _():
        yt = yf_scr[pl.ds(brow, tl), :] + xs_ref[...] * dexp_ref[...]
        t = yt * _silu(z_ref[...])
        ms = jnp.mean(t * t, axis=-1, keepdims=True)
        o_ref[...] = (t * lax.rsqrt(ms + NORM_EPS) * ng_ref[...]).astype(bf16)

    @pl.when(i == nblk - 1)
    def _():
        sfin_ref[...] = _compact_state(st_scr[...])


def ssd_mixer(p, xbc, dtb, alog, dexp, ng, expand, selb, selc, s0, bsz, seq, tl=256):
    nblk = seq // tl
    blk = lambda ph, i: jnp.where(ph == 0, i, nblk - 1 - i)
    rows = lambda b, ph, i: b * nblk + blk(ph, i)
    const = lambda shape: pl.BlockSpec(shape, lambda b, ph, i: (0,) * len(shape))
    state = pl.BlockSpec((None, None, SSD_N, MIX_W), lambda b, ph, i: (b, ph, 0, 0))
    return pl.pallas_call(
        functools.partial(_ssd_kernel, nblk=nblk, tl=tl),
        out_shape=(jax.ShapeDtypeStruct((bsz * seq, MIX_W), bf16),
                   jax.ShapeDtypeStruct((bsz, 2, SSD_N, MIX_W), f32)),
        grid=(bsz, 2, nblk),
        in_specs=[pl.BlockSpec((tl, MIX_W), lambda b, ph, i: (rows(b, ph, i), 0)),
                  pl.BlockSpec((tl, 2 * LANES), lambda b, ph, i: (rows(b, ph, i), MIX_W // (2 * LANES))),
                  pl.BlockSpec((None, tl, PT), lambda b, ph, i: (T_SZ, rows(b, ph, i), 0)),
                  pl.BlockSpec((None, tl, LANES), lambda b, ph, i: (T_MISC, rows(b, ph, i), MISC_DT // LANES)),
                  const((1, LANES)), const((1, LANES)), const((1, MIX_W)), const((1, MIX_W)),
                  const((LANES, MIX_W)), const((2 * LANES, MIX_W)), const((2 * LANES, MIX_W)), state],
        out_specs=(pl.BlockSpec((tl, MIX_W),
                                lambda b, ph, i: (b * nblk + jnp.where(ph == 0, nblk - 1, nblk - 1 - i), 0)),
                   state),
        scratch_shapes=[pltpu.VMEM((seq, MIX_W), f32), pltpu.VMEM((MIX_W, MIX_W), f32),
                        pltpu.VMEM((tl, LANES), f32), pltpu.VMEM((tl, LANES), f32)],
        compiler_params=_cparams("parallel", "arbitrary", "arbitrary"),
        name="ssd_mixer",
    )(xbc, xbc, p, p, dtb, alog, dexp, ng, expand, selb, selc, s0)


def _hy_filter_kernel(zf_ref, zb_ref, tf_ref, tb_ref, w1_ref, b1_ref, sw_ref, w2_ref, b2_ref,
                      w3f_ref, w3b_ref, b3f_ref, b3b_ref, df_ref, db_ref, skip_ref, r_ref, hf_scr, hb_scr, *, seq):
    first = jnp.logical_and(pl.program_id(0) == 0, pl.program_id(1) == 0)

    @pl.when(first)
    def _():
        for z_ref, h_scr in ((zf_ref, hf_scr), (zb_ref, hb_scr)):
            h = jnp.sin(sw_ref[...] * (_dot(w1_ref[...], z_ref[...], precision=HI) + b1_ref[...]))
            h_scr[...] = jnp.sin(sw_ref[...] * (_dot(w2_ref[...], h, precision=HI) + b2_ref[...]))

    hf = (_dot(w3f_ref[...], hf_scr[...], precision=HI) + b3f_ref[...]) * jnp.exp(-tf_ref[...] * jnp.abs(df_ref[...]))
    hb = (_dot(w3b_ref[...], hb_scr[...], precision=HI) + b3b_ref[...]) * jnp.exp(-tb_ref[...] * jnp.abs(db_ref[...]))
    den = (jnp.sum(jnp.abs(hf), axis=-1, keepdims=True) + jnp.sum(jnp.abs(hb), axis=-1, keepdims=True)) + NORM_EPS
    col = lax.broadcasted_iota(jnp.int32, (1, seq), 1)
    r_ref[:, 0:seq] = jnp.where(col == 0, 0.0, hb / den)
    r_ref[:, seq:2 * seq] = hf / den + jnp.where(col == 0, skip_ref[...], 0.0)


def hy_filters(fp, seq, cbf=128):
    t = jnp.arange(seq, dtype=f32) / seq
    t_rev = jnp.concatenate([t[:1], t[:0:-1]])

    def feats(tt):
        tc = tt[:, None]
        ang = 2.0 * math.pi * tc * jnp.arange(1, HY_BANDS + 1, dtype=f32)
        z = jnp.concatenate([tc, jnp.cos(ang), jnp.sin(ang)], axis=-1)
        return jnp.pad(z, ((0, 0), (0, HY_EMB_PAD - HY_EMB))).T

    ch = MIX_W
    nb = ch // cbf
    col = lambda a: a.reshape(-1, 1)
    w3t = fp['w3'].T.reshape(2, 2, ch, HY_HID)
    b3 = fp['b3'].reshape(2, 2, ch, 1)
    dec = fp['decay'].reshape(2, 2, ch, 1)
    const = lambda shape: pl.BlockSpec(shape, lambda o, j: (0,) * len(shape))
    sel = lambda d, last: pl.BlockSpec((None, None, cbf, last), lambda o, j: (o, d, j, 0))
    return pl.pallas_call(
        functools.partial(_hy_filter_kernel, seq=seq),
        out_shape=jax.ShapeDtypeStruct((2, ch, 2 * seq), f32),
        grid=(2, nb),
        in_specs=[const((HY_EMB_PAD, seq)), const((HY_EMB_PAD, seq)), const((1, seq)), const((1, seq)),
                  const((HY_HID, HY_EMB_PAD)), const((HY_HID, 1)), const((HY_HID, 1)),
                  const((HY_HID, HY_HID)), const((HY_HID, 1)),
                  sel(0, HY_HID), sel(1, HY_HID), sel(0, 1), sel(1, 1), sel(0, 1), sel(1, 1),
                  pl.BlockSpec((None, cbf, 1), lambda o, j: (o, j, 0))],
        out_specs=pl.BlockSpec((None, cbf, 2 * seq), lambda o, j: (o, j, 0)),
        scratch_shapes=[pltpu.VMEM((HY_HID, seq), f32), pltpu.VMEM((HY_HID, seq), f32)],
        compiler_params=_cparams("arbitrary", "arbitrary"),
        name="hy_filters",
    )(feats(t), feats(t_rev), t.reshape(1, seq), t_rev.reshape(1, seq),
      jnp.pad(fp['w1'], ((0, HY_EMB_PAD - HY_EMB), (0, 0))).T, col(fp['b1']), col(fp['sin_w']),
      fp['w2'].T, col(fp['b2']), w3t, w3t, b3, b3, dec, dec, fp['skip'].reshape(2, ch, 1))


def _hyena_kernel(cw_ref, cb_ref, hv_ref, h1_ref, h2_ref, r_ref, o_ref, acc_scr, *, bsz, n_i, cb_n):
    tb = HY_TB
    gr = tb // 2
    sub = tb // LANES
    seq = n_i * tb
    rows = n_i * bsz
    base = pl.program_id(0) * cb_n
    n_ch = MIX_W
    lane = lax.broadcasted_iota(jnp.int32, (1, tb), 1)
    zblk = jnp.zeros((bsz, tb), f32)

    def load(ref, ch):
        return jnp.concatenate(
            [jnp.concatenate([ref[ch, pl.ds(sub * ib + k, bsz, stride=sub * n_i), :] for k in range(sub)], axis=1)
             for ib in range(n_i)], axis=0)

    def short_conv(a, stream, ch):
        idx = stream * n_ch + base + ch
        prev = jnp.concatenate([zblk, a[:rows - bsz]], axis=0) if n_i > 1 else zblk
        nxt = jnp.concatenate([a[bsz:], zblk], axis=0) if n_i > 1 else zblk
        am = pltpu.roll(jnp.where(lane == tb - 1, prev, a), 1, 1)
        ap = pltpu.roll(jnp.where(lane == 0, nxt, a), tb - 1, 1)
        return (am * cw_ref[idx] + a * cw_ref[3 * n_ch + idx] + ap * cw_ref[6 * n_ch + idx]) + cb_ref[idx]

    def long_conv(u, order, ch):
        rrow = r_ref[order, pl.ds(ch, 1), :]
        bits = lambda a: lax.bitcast_convert_type(a.astype(bf16).astype(f32), jnp.int32)
        word = (bits(pltpu.roll(rrow, 1, 1)) & jnp.int32(-65536)) | lax.shift_right_logical(bits(rrow), 16)
        g = pltpu.bitcast(
            pltpu.roll(jnp.broadcast_to(word, (gr // 2, 2 * seq)), 0, 1, stride=2, stride_axis=0), bf16)
        for d in [0] + [e for e in range(-(n_i - 1), n_i) if e != 0]:
            n = n_i - abs(d)
            src = max(0, -d) * bsz
            dst = max(0, d) * bsz
            c0 = seq + d * tb
            m = jnp.concatenate([g[:, c0:c0 + tb], g[:, c0 - gr:c0 - gr + tb]], axis=0)
            part = _dot(u[src:src + n * bsz].astype(bf16), m)
            if d == 0:
                acc_scr[order] = part
            else:
                acc_scr[order, dst:dst + n * bsz, :] += part
        return acc_scr[order]

    def body(ch, carry):
        hv = short_conv(load(hv_ref, ch), 0, ch)
        hx1 = short_conv(load(h1_ref, ch), 1, ch)
        hx2 = short_conv(load(h2_ref, ch), 2, ch)
        zz = hx1 * long_conv(hv, 0, ch)
        out = hx2 * long_conv(zz, 1, ch)
        for ib in range(n_i):
            for k in range(sub):
                o_ref[ch, pl.ds(sub * ib + k, bsz, stride=sub * n_i), :] = (
                    out[ib * bsz:(ib + 1) * bsz, k * LANES:(k + 1) * LANES])
        return carry

    lax.fori_loop(0, cb_n, body, 0, unroll=2)


def hyena_mixer(hy_t, r, cw, cb, bsz, seq, cb_n=8):
    n_i = seq // HY_TB
    ch = MIX_W
    nblk = ch // cb_n
    n_rows = bsz * seq // LANES
    x3 = hy_t.reshape(3 * ch, n_rows, LANES)
    stream = lambda s: pl.BlockSpec((cb_n, n_rows, LANES), lambda j: (s * nblk + j, 0, 0))
    smem = pl.BlockSpec(memory_space=pltpu.SMEM)
    out = pl.pallas_call(
        functools.partial(_hyena_kernel, bsz=bsz, n_i=n_i, cb_n=cb_n),
        out_shape=jax.ShapeDtypeStruct((ch, n_rows, LANES), f32),
        grid=(nblk,),
        in_specs=[smem, smem, stream(0), stream(1), stream(2),
                  pl.BlockSpec((2, cb_n, 2 * seq), lambda j: (0, j, 0))],
        out_specs=pl.BlockSpec((cb_n, n_rows, LANES), lambda j: (j, 0, 0)),
        scratch_shapes=[pltpu.VMEM((2, bsz * n_i, HY_TB), f32)],
        compiler_params=_cparams("parallel"),
        name="hyena_mixer",
    )(cw.reshape(-1), cb, x3, x3, x3, r)
    return out.reshape(ch, bsz * seq)


_IN_SIZES = (NH * GLA_DK, NH * GLA_DK, MIX_W, MIX_W, 2 * GLA_RANK, MIX_W, MIX_W, MIX_W,
             MIX_W, MIX_W + 2 * SSD_G * SSD_N, 2 * NH, 3 * MIX_W)
_IN_OFFS = np.concatenate([[0], np.cumsum(_IN_SIZES)]).tolist()


def _prep_w_in(w_in):
    (gq, gk, gv, gg, gr, dq, dk, dv, sz, sxbc, sdt, hy) = [w_in[:, _IN_OFFS[i]:_IN_OFFS[i + 1]]
                                                          for i in range(len(_IN_SIZES))]
    d = w_in.shape[0]
    zeros = lambda n: jnp.zeros((d, n), w_in.dtype)
    misc = jnp.concatenate([sxbc[:, MIX_W:], gr, zeros(LANES - 2 * GLA_RANK), sdt, zeros(LANES - 2 * NH)], axis=1)
    w_tok = jnp.concatenate([gq, gk, gv, gg, dq, dk, dv, sz, sxbc[:, :MIX_W], misc], axis=1)
    w_tok = w_tok.astype(bf16).reshape(d, N_PT, PT).transpose(1, 0, 2)
    return w_tok, hy.T.astype(bf16)


def _prep_gate_w(gate_w):
    w = jnp.zeros((2, LANES, gate_w.shape[2]), f32)
    for d in range(2):
        w = w.at[d, d * GLA_RANK:(d + 1) * GLA_RANK, :].set(gate_w[d])
    hi = w.astype(bf16)
    return hi, (w - hi.astype(f32)).astype(bf16)


def _rope_tables(seq):
    rows = seq // GRID_W
    r, col = jnp.meshgrid(jnp.arange(rows), jnp.arange(GRID_W), indexing='ij')
    r = r.reshape(-1).astype(f32)
    col = col.reshape(-1).astype(f32)
    nf = DIFF_DK // 4
    inv = ROPE_BASE ** (-jnp.arange(nf, dtype=f32) / nf)
    ang = jnp.concatenate([r[:, None] * inv, col[:, None] * inv], axis=-1)
    cos, sin = jnp.cos(ang), jnp.sin(ang)
    reps = LANES // DIFF_DK
    return jnp.tile(jnp.concatenate([cos, cos], axis=-1), (1, reps)), jnp.tile(jnp.concatenate([-sin, sin], axis=-1), (1, reps))


def _const_tables():
    hsum = np.kron(np.eye(NH, dtype=np.float32), np.ones((HEAD_DIM, HEAD_DIM), np.float32))
    expand = np.zeros((LANES, MIX_W), np.float32)
    selb = np.zeros((2 * LANES, MIX_W), np.float32)
    selc = np.zeros((2 * LANES, MIX_W), np.float32)
    rep = NH // SSD_G
    for h in range(NH):
        expand[h, h * HEAD_DIM:(h + 1) * HEAD_DIM] = 1.0
        g = h // rep
        for n in range(SSD_N):
            selb[g * SSD_N + n, h * SSD_N + n] = 1.0
            selc[SSD_G * SSD_N + g * SSD_N + n, h * SSD_N + n] = 1.0
    return tuple(jnp.asarray(a, dtype=bf16) for a in (hsum, expand, selb, selc))


def _pack_state(s):
    bsz, _, h, a, b = s.shape
    return s.transpose(0, 1, 4, 2, 3).reshape(bsz, 2, b, h * a)


def _unpack_state(st, a):
    bsz, _, b, _ = st.shape
    return st.reshape(bsz, 2, b, NH, a).transpose(0, 1, 3, 4, 2)


def _layer(x, mod, rows_per_mod, lw, layer, consts, bsz, seq, lam_init, r_filt, ctx, rope_tabs, final_g,
           final_norm):
    hsum, expand, selb, selc = consts
    p, hy_t = in_proj(x, mod, lw['norm1_g'], lw['w_tok'], lw['w_hyT'], rows_per_mod)
    if ctx is None:
        gla_s0 = jnp.zeros((bsz, 2, HEAD_DIM, NH * GLA_DK), f32)
        ssd_s0 = jnp.zeros((bsz, 2, SSD_N, MIX_W), f32)
        dctx = None
    else:
        ctx_k, ctx_v, gla_s, ssd_s = ctx
        gla_s0 = _pack_state(gla_s)
        ssd_s0 = _pack_state(ssd_s)
        dctx = (ctx_k.reshape(bsz, -1, MIX_W), ctx_v.reshape(bsz, -1, MIX_W))
    o_gla, gla_fin = gla_mixer(p, lw['gla_gw_hi'], lw['gla_gw_lo'], lw['gla_gb'], lw['gla_ng'], hsum, gla_s0,
                               bsz, seq)
    o_diff = diff_mixer(p, lw['lam_p'], lw['diff_ng'], bsz, seq, lam_init, dctx, rope_tabs)
    xbc = ssd_conv(p, lw['ssd_conv_w'], lw['ssd_conv_b'], bsz, seq)
    o_ssd, ssd_fin = ssd_mixer(p, xbc, lw['ssd_dtb'], lw['ssd_alog'], lw['ssd_dexp'], lw['ssd_ng'],
                               expand, selb, selc, ssd_s0, bsz, seq)
    o_hy = hyena_mixer(hy_t, r_filt, lw['hy_cw'], lw['hy_cb'], bsz, seq)
    x = out_proj(x, mod, o_gla, o_diff, o_ssd, o_hy, lw['w_out'], layer, rows_per_mod)
    x = mlp(x, mod, lw['norm2_g'], lw['mlp_w1'], lw['mlp_w2'], layer, final_g, rows_per_mod, final_norm)
    return x, p, gla_fin, ssd_fin


def kernel(x_prompt, x_sample, cache_diff_k, cache_diff_v, state_gla, state_ssd, c, c_ctx, ada_w, ada_b, norm1_g, norm2_g, w_in, w_out, gla_gate_w, gla_gate_b, gla_norm_g, diff_lambda, diff_norm_g, ssd_conv_w, ssd_conv_b, ssd_dt_bias, ssd_a_log, ssd_d, ssd_norm_g, hy_conv_w, hy_conv_b, hy_f_w1, hy_f_b1, hy_f_w2, hy_f_b2, hy_f_w3, hy_f_b3, hy_sin_w, hy_decay, hy_skip, mlp_w1, mlp_w2, final_g):
    bp, lp, d = x_prompt.shape
    bs, ls, _ = x_sample.shape
    depth = w_in.shape[0]
    consts = _const_tables()
    rope_tabs = _rope_tables(ls)
    n_c = 16
    cvec = jnp.concatenate([c_ctx[None], c, jnp.zeros((n_c - 1 - bs, d), f32)], axis=0)
    hp = x_prompt.reshape(bp * lp, d)
    hs = x_sample.reshape(bs * ls, d)
    ks_, vs_, gs_, ss_ = [], [], [], []
    w_out_b, mlp_w2_b = w_out.astype(bf16), mlp_w2.astype(bf16)
    mlp_w1_b = mlp_w1.astype(bf16).reshape(depth, d, -1, MLP_TF).transpose(0, 2, 1, 3)
    for l in range(depth):
        w_tok, w_hyT = _prep_w_in(w_in[l])
        pad_l = lambda a: jnp.pad(a, (0, LANES - a.shape[0])).reshape(1, LANES)
        gw_hi, gw_lo = _prep_gate_w(gla_gate_w[l])
        lw = dict(
            norm1_g=norm1_g[l], norm2_g=norm2_g[l], w_tok=w_tok, w_hyT=w_hyT, w_out=w_out_b,
            mlp_w1=mlp_w1_b, mlp_w2=mlp_w2_b,
            gla_gw_hi=gw_hi, gla_gw_lo=gw_lo, gla_gb=gla_gate_b[l].reshape(2, 1, -1), gla_ng=jnp.tile(gla_norm_g[l], NH).reshape(1, MIX_W),
            lam_p=jnp.pad(diff_lambda[l], ((0, 4), (0, LANES - DIFF_DK))),
            diff_ng=jnp.tile(diff_norm_g[l], LANES // HEAD_DIM).reshape(1, LANES),
            ssd_conv_w=ssd_conv_w[l], ssd_conv_b=ssd_conv_b[l],
            ssd_dtb=pad_l(ssd_dt_bias[l].reshape(-1)), ssd_alog=pad_l(ssd_a_log[l].reshape(-1)),
            ssd_dexp=jnp.repeat(ssd_d[l], HEAD_DIM).reshape(1, MIX_W), ssd_ng=ssd_norm_g[l].reshape(1, MIX_W),
            hy_cw=hy_conv_w[l], hy_cb=hy_conv_b[l],
        )
        fp = dict(w1=hy_f_w1[l], b1=hy_f_b1[l], w2=hy_f_w2[l], b2=hy_f_b2[l], w3=hy_f_w3[l], b3=hy_f_b3[l],
                  sin_w=hy_sin_w[l], decay=hy_decay[l], skip=hy_skip[l])
        mod = ada_mod(cvec, ada_w, ada_b, l).reshape(n_c, 6, d)
        lam_init = 0.8 - 0.6 * math.exp(-0.3 * l)
        last = l == depth - 1
        hp, p_p, g_l, s_l = _layer(hp, mod[0:1], bp * lp, lw, l, consts, bp, lp, lam_init, hy_filters(fp, lp),
                                   None, None, final_g, last)
        ks_.append(p_p[T_DK].reshape(bp, lp, NH, 2 * DIFF_DK))
        vs_.append(p_p[T_DV].reshape(bp, lp, NH, HEAD_DIM))
        gs_.append(_unpack_state(g_l, GLA_DK))
        ss_.append(_unpack_state(s_l, HEAD_DIM))
        hs, _, _, _ = _layer(hs, mod[1:1 + bs], ls, lw, l, consts, bs, ls, lam_init, hy_filters(fp, ls),
                             (cache_diff_k[:, l], cache_diff_v[:, l], state_gla[:, l], state_ssd[:, l]),
                             rope_tabs, final_g, last)
    return (hp.reshape(bp, lp, d), hs.reshape(bs, ls, d), jnp.stack(ks_, axis=1), jnp.stack(vs_, axis=1),
            jnp.stack(gs_, axis=1), jnp.stack(ss_, axis=1))
```

```python
import functools
import math

import numpy as np
import jax
import jax.numpy as jnp
from jax import lax
from jax.experimental import pallas as pl
from jax.experimental.pallas import tpu as pltpu

f32 = jnp.float32
bf16 = jnp.bfloat16
HI = lax.Precision.HIGHEST

LANES = 128
VMEM_LIMIT = 56 * 2**20

D_MODEL = 2048
GRID_W = 64
MIX_W = D_MODEL // 4
HEAD_DIM = 64
NH = MIX_W // HEAD_DIM
D_FF = 4 * D_MODEL
NORM_EPS = 1e-6
LOG2E = 1.0 / math.log(2.0)
GLA_DK = HEAD_DIM // 2
GLA_RANK = 16
GLA_TAU = 16.0
GLA_CHUNK = 64
DIFF_DK = HEAD_DIM // 2
ROPE_BASE = 10000.0
SSD_N = 64
SSD_G = 2
SSD_CHUNK = 128
HY_BANDS = 16
HY_EMB = 2 * HY_BANDS + 1
HY_EMB_PAD = 40
HY_HID = 64
DIFF_KEY_CHUNK = 512
DIFF_Q_SUB = 512
MLP_TF = 512
HY_TB = 256

PT = 512
(T_GQK, T_GV, T_GG, T_DQ, T_DK, T_DV, T_SZ, T_SX, T_MISC) = range(9)
N_PT = 9
MISC_GR = 256
MISC_DT = 384
SH1, SC1, G1, SH2, SC2, G2 = range(6)

NT_DIMS = (((1,), (1,)), ((), ()))
TN_DIMS = (((0,), (0,)), ((), ()))


def _cparams(*sem):
    return pltpu.CompilerParams(dimension_semantics=sem, vmem_limit_bytes=VMEM_LIMIT)


def _sigmoid(x):
    return 1.0 / (1.0 + jnp.exp(-x))


def _silu(x):
    return x * _sigmoid(x)


def _softplus(x):
    return jnp.maximum(x, 0.0) + jnp.log1p(jnp.exp(-jnp.abs(x)))


def _log_sigmoid(x):
    return jnp.minimum(x, 0.0) - jnp.log1p(jnp.exp(-jnp.abs(x)))


def _tri_mask(rows, c, sgn):
    r_i = lax.broadcasted_iota(jnp.int32, (rows, c), 0) & (c - 1)
    c_i = lax.broadcasted_iota(jnp.int32, (rows, c), 1)
    return (r_i - c_i) * sgn >= 0


def _expand_state(s, row_w, col_w):
    rows, cols = NH * row_w, NH * col_w
    bd = (lax.broadcasted_iota(jnp.int32, (rows, cols), 0) // row_w
          == lax.broadcasted_iota(jnp.int32, (rows, cols), 1) // col_w)
    return jnp.where(bd, jnp.concatenate([s] * NH, axis=0), 0.0)


def _compact_state(st):
    row_w = st.shape[0] // NH
    out = st[0:row_w]
    for h in range(1, NH):
        out += st[h * row_w:(h + 1) * row_w]
    return out


def _split(x, n):
    parts = []
    for _ in range(n - 1):
        hi = x.astype(bf16)
        parts.append(hi)
        x = x - hi.astype(f32)
    parts.append(x.astype(bf16))
    return parts


def _sel_dot(a, b, dims=None, terms=3):
    if a.dtype == bf16:
        prods = [_dot(a, p, dims) for p in _split(b, terms)]
    else:
        prods = [_dot(p, b, dims) for p in _split(a, terms)]
    return functools.reduce(lambda x, y: x + y, prods)


def _dot(a, b, dims=None, precision=None):
    if dims is None:
        return jnp.dot(a, b, preferred_element_type=f32, precision=precision)
    return lax.dot_general(a, b, dims, preferred_element_type=f32, precision=precision)


def _ada_kernel(c_ref, w_ref, b_ref, o_ref):
    s = _silu(c_ref[...]).astype(bf16)
    o_ref[...] = _dot(s, w_ref[...].astype(bf16)) + b_ref[...]


def ada_mod(cvec, w, b, layer, tn=1024):
    m, d = cvec.shape
    n = w.shape[2]
    return pl.pallas_call(
        _ada_kernel,
        out_shape=jax.ShapeDtypeStruct((m, n), f32),
        grid=(n // tn,),
        in_specs=[pl.BlockSpec((m, d), lambda j: (0, 0)),
                  pl.BlockSpec((None, d, tn), lambda j: (layer, 0, j)),
                  pl.BlockSpec((None, 1, tn), lambda j: (layer, 0, j))],
        out_specs=pl.BlockSpec((m, tn), lambda j: (0, j)),
        compiler_params=_cparams("arbitrary"),
        name="ada_mod",
    )(cvec, w, b.reshape(b.shape[0], 1, n))


def _norm_mod(x, g, sc, sh):
    ms = jnp.mean(x * x, axis=-1, keepdims=True)
    return (x * lax.rsqrt(ms + NORM_EPS) * g) * (1.0 + sc) + sh


def _inproj_kernel(x_ref, mod_ref, g_ref, wt_ref, wh_ref, p_ref, h_ref, u_scr):
    j = pl.program_id(1)

    @pl.when(j == 0)
    def _():
        u = _norm_mod(x_ref[...], g_ref[...], mod_ref[SC1:SC1 + 1, :], mod_ref[SH1:SH1 + 1, :])
        u_scr[...] = u.astype(bf16)

    @pl.when(j < N_PT)
    def _():
        p_ref[...] = _dot(u_scr[...], wt_ref[...])

    @pl.when(j >= N_PT)
    def _():
        h_ref[...] = _dot(wh_ref[...], u_scr[...], NT_DIMS)


def in_proj(x, mod, g, w_tok, w_hyT, rows_per_mod, tm=1024):
    t, d = x.shape
    assert t % tm == 0 and rows_per_mod % tm == 0, (t, rows_per_mod, tm)
    n_hy = w_hyT.shape[0] // PT
    return pl.pallas_call(
        _inproj_kernel,
        out_shape=(jax.ShapeDtypeStruct((N_PT, t, PT), f32), jax.ShapeDtypeStruct((n_hy * PT, t), f32)),
        grid=(t // tm, N_PT + n_hy),
        in_specs=[pl.BlockSpec((tm, d), lambda i, j: (i, 0)),
                  pl.BlockSpec((None, 6, d), lambda i, j: ((i * tm) // rows_per_mod, 0, 0)),
                  pl.BlockSpec((1, d), lambda i, j: (0, 0)),
                  pl.BlockSpec((None, d, PT), lambda i, j: (jnp.minimum(j, N_PT - 1), 0, 0)),
                  pl.BlockSpec((PT, d), lambda i, j: (jnp.maximum(j - N_PT, 0), 0))],
        out_specs=(pl.BlockSpec((None, tm, PT), lambda i, j: (jnp.minimum(j, N_PT - 1), i, 0)),
                   pl.BlockSpec((PT, tm), lambda i, j: (jnp.maximum(j - N_PT, 0), i))),
        scratch_shapes=[pltpu.VMEM((tm, d), bf16)],
        compiler_params=_cparams("parallel", "arbitrary"),
        name="in_proj",
    )(x, mod, g.reshape(1, d), w_tok, w_hyT)


def _outproj_kernel(x_ref, mod_ref, a_ref, b_ref, c_ref, dT_ref, w_ref, o_ref):
    acc = _dot(a_ref[...], w_ref[0:MIX_W, :])
    acc += _dot(b_ref[...], w_ref[MIX_W:2 * MIX_W, :])
    acc += _dot(c_ref[...], w_ref[2 * MIX_W:3 * MIX_W, :])
    acc += _dot(dT_ref[...].astype(bf16), w_ref[3 * MIX_W:4 * MIX_W, :], TN_DIMS)
    o_ref[...] = x_ref[...] + mod_ref[G1:G1 + 1, :] * acc


def out_proj(x, mod, o_a, o_b, o_c, o_dT, w_out, layer, rows_per_mod, tm=512):
    t, d = x.shape
    assert t % tm == 0 and rows_per_mod % tm == 0, (t, rows_per_mod, tm)
    tok = lambda i: (i, 0)
    return pl.pallas_call(
        _outproj_kernel,
        out_shape=jax.ShapeDtypeStruct((t, d), f32),
        grid=(t // tm,),
        in_specs=[pl.BlockSpec((tm, d), tok),
                  pl.BlockSpec((None, 6, d), lambda i: ((i * tm) // rows_per_mod, 0, 0)),
                  pl.BlockSpec((tm, MIX_W), tok), pl.BlockSpec((tm, MIX_W), tok), pl.BlockSpec((tm, MIX_W), tok),
                  pl.BlockSpec((MIX_W, tm), lambda i: (0, i)),
                  pl.BlockSpec((None, d, d), lambda i: (layer, 0, 0))],
        out_specs=pl.BlockSpec((tm, d), tok),
        compiler_params=_cparams("parallel"),
        name="out_proj",
    )(x, mod, o_a, o_b, o_c, o_dT, w_out)


def _mlp_kernel(x_ref, mod_ref, g_ref, w1_ref, w2_ref, fg_ref, o_ref, u_scr, *, final_norm):
    j = pl.program_id(1)

    @pl.when(j == 0)
    def _():
        u = _norm_mod(x_ref[...], g_ref[...], mod_ref[SC2:SC2 + 1, :], mod_ref[SH2:SH2 + 1, :])
        u_scr[...] = u.astype(bf16)
        o_ref[...] = jnp.zeros_like(o_ref)

    h = jnp.maximum(_dot(u_scr[...], w1_ref[...]), 0.0)
    hb = (h * h).astype(bf16)
    d = o_ref.shape[1]
    for n in range(d // PT):
        o_ref[:, n * PT:(n + 1) * PT] += _dot(hb, w2_ref[:, n * PT:(n + 1) * PT])

    @pl.when(j == pl.num_programs(1) - 1)
    def _():
        y = x_ref[...] + mod_ref[G2:G2 + 1, :] * o_ref[...]
        if final_norm:
            ms = jnp.mean(y * y, axis=-1, keepdims=True)
            y = y * lax.rsqrt(ms + NORM_EPS) * fg_ref[...]
        o_ref[...] = y


def mlp(x, mod, g, w1, w2, layer, final_g, rows_per_mod, final_norm, tm=1024, tf=MLP_TF):
    t, d = x.shape
    assert t % tm == 0 and rows_per_mod % tm == 0, (t, rows_per_mod, tm)
    n_f = w1.shape[2] // tf
    return pl.pallas_call(
        functools.partial(_mlp_kernel, final_norm=final_norm),
        out_shape=jax.ShapeDtypeStruct((t, d), f32),
        grid=(t // tm, n_f),
        in_specs=[pl.BlockSpec((tm, d), lambda i, j: (i, 0)),
                  pl.BlockSpec((None, 6, d), lambda i, j: ((i * tm) // rows_per_mod, 0, 0)),
                  pl.BlockSpec((1, d), lambda i, j: (0, 0)),
                  pl.BlockSpec((None, d, tf), lambda i, j: (layer, 0, j)),
                  pl.BlockSpec((None, tf, d), lambda i, j: (layer, j, 0)),
                  pl.BlockSpec((1, d), lambda i, j: (0, 0))],
        out_specs=pl.BlockSpec((tm, d), lambda i, j: (i, 0)),
        scratch_shapes=[pltpu.VMEM((tm, d), bf16)],
        compiler_params=_cparams("parallel", "arbitrary"),
        name="mlp",
    )(x, mod, g.reshape(1, d), w1, w2, final_g.reshape(1, d))


def _gla_kernel(qk_ref, v_ref, gg_ref, misc_ref, gwh_ref, gwl_ref, gb_ref, ng_ref, hsum_ref, s0_ref,
                o_ref, sfin_ref, of_scr, st_scr, la_scr, *, nblk, tl):
    ph = pl.program_id(1)
    i = pl.program_id(2)
    fwd = ph == 0
    c = GLA_CHUNK
    nch = tl // c
    qkw = NH * GLA_DK

    @pl.when(i == 0)
    def _():
        st_scr[...] = _expand_state(s0_ref[...], HEAD_DIM, GLA_DK)

    gr_hi, gr_lo = _split(misc_ref[:, MISC_GR:MISC_GR + LANES], 2)
    logits = (_dot(gr_hi, gwh_ref[...]) + _dot(gr_lo, gwh_ref[...]) + _dot(gr_hi, gwl_ref[...])) + gb_ref[...]
    la_scr[...] = _log_sigmoid(logits) * (1.0 / GLA_TAU)

    sgn = jnp.where(fwd, 1, -1)
    trib = jnp.where(_tri_mask(c, c, sgn), 1.0, 0.0).astype(bf16)
    tri8 = _tri_mask(NH * c, c, sgn)
    qk_head = lax.broadcasted_iota(jnp.int32, (1, qkw), 1) // GLA_DK
    v_head = lax.broadcasted_iota(jnp.int32, (1, MIX_W), 1) // HEAD_DIM
    bd = (lax.broadcasted_iota(jnp.int32, (MIX_W, qkw), 0) // HEAD_DIM
          == lax.broadcasted_iota(jnp.int32, (MIX_W, qkw), 1) // GLA_DK)
    brow = pl.multiple_of(jnp.where(fwd, i, nblk - 1 - i) * tl, tl)

    def chunk(j, carry):
        cj = jnp.where(fwd, j, nch - 1 - j)
        off = pl.multiple_of(cj * c, c)
        q = qk_ref[pl.ds(off, c), 0:qkw] * (GLA_DK ** -0.5)
        k = qk_ref[pl.ds(off, c), qkw:2 * qkw]
        vb = v_ref[pl.ds(off, c), :].astype(bf16)
        b = _sel_dot(trib, la_scr[pl.ds(off, c), :])
        btot = jnp.where(fwd, b[c - 1:c, :], b[0:1, :])
        qd = q * jnp.exp(b)
        kd = (k * jnp.exp(-b)).astype(bf16)
        kt = (k * jnp.exp(btot - b)).astype(bf16)
        qst = jnp.concatenate([jnp.where(qk_head == h, qd, 0.0) for h in range(NH)], axis=0).astype(bf16)
        att = jnp.where(tri8, _dot(qst, kd, NT_DIMS), 0.0)
        r = _dot(att.astype(bf16), vb)
        o = jnp.where(v_head == 0, r[0:c], 0.0)
        for h in range(1, NH):
            o += jnp.where(v_head == h, r[h * c:(h + 1) * c], 0.0)
        st = st_scr[...]
        o += _dot(qd.astype(bf16), st.astype(bf16), NT_DIMS)
        kv = _dot(vb, kt, TN_DIMS)
        st_scr[...] = st * jnp.exp(btot) + jnp.where(bd, kv, 0.0)
        of_scr[pl.ds(pl.multiple_of(brow + off, c), c), :] += o
        return carry

    @pl.when(fwd)
    def _():
        of_scr[pl.ds(brow, tl), :] = jnp.zeros((tl, MIX_W), f32)

    lax.fori_loop(0, nch, chunk, 0, unroll=True)

    @pl.when(ph == 1)
    def _():
        ot = of_scr[pl.ds(brow, tl), :]
        ss = _sel_dot(ot * ot, hsum_ref[...], terms=2) * (1.0 / HEAD_DIM)
        o_ref[...] = (ot * lax.rsqrt(ss + NORM_EPS) * ng_ref[...] * _silu(gg_ref[...])).astype(bf16)

    @pl.when(i == nblk - 1)
    def _():
        sfin_ref[...] = _compact_state(st_scr[...])


def gla_mixer(p, gw_hi, gw_lo, gb, ng, hsum, s0, bsz, seq, tl=1024):
    tl = min(tl, seq)
    nblk = seq // tl
    qkw = NH * GLA_DK
    blk = lambda ph, i: jnp.where(ph == 0, i, nblk - 1 - i)
    tile = lambda col: pl.BlockSpec((None, tl, PT), lambda b, ph, i: (col, b * nblk + blk(ph, i), 0))
    const = lambda shape: pl.BlockSpec(shape, lambda b, ph, i: (0,) * len(shape))
    per_dir = lambda rows: pl.BlockSpec((None, rows, qkw), lambda b, ph, i: (ph, 0, 0))
    state = pl.BlockSpec((None, None, HEAD_DIM, qkw), lambda b, ph, i: (b, ph, 0, 0))
    return pl.pallas_call(
        functools.partial(_gla_kernel, nblk=nblk, tl=tl),
        out_shape=(jax.ShapeDtypeStruct((bsz * seq, MIX_W), bf16),
                   jax.ShapeDtypeStruct((bsz, 2, HEAD_DIM, qkw), f32)),
        grid=(bsz, 2, nblk),
        in_specs=[tile(T_GQK), tile(T_GV), tile(T_GG), tile(T_MISC),
                  per_dir(LANES), per_dir(LANES), per_dir(1), const((1, MIX_W)), const((MIX_W, MIX_W)), state],
        out_specs=(pl.BlockSpec((tl, MIX_W),
                                lambda b, ph, i: (b * nblk + jnp.where(ph == 0, nblk - 1, nblk - 1 - i), 0)),
                   state),
        scratch_shapes=[pltpu.VMEM((seq, MIX_W), f32), pltpu.VMEM((MIX_W, qkw), f32), pltpu.VMEM((tl, qkw), f32)],
        compiler_params=_cparams("parallel", "arbitrary", "arbitrary"),
        name="gla_mixer",
    )(p, p, p, p, gw_hi, gw_lo, gb, ng, hsum, s0)


def _rope(x, cos, sin_signed):
    lane = lax.broadcasted_iota(jnp.int32, (1, LANES), 1)
    half = DIFF_DK // 2
    rot = jnp.where((lane % DIFF_DK) < half, pltpu.roll(x, LANES - half, 1), pltpu.roll(x, half, 1))
    return x * cos + rot * sin_signed


def _diff_kernel(*refs, seq, ctx_len, rope, lam_init):
    if rope:
        (q_ref, k_ref, v_ref, ck_ref, cv_ref, cq_ref, sq_ref, ckk_ref, skk_ref, lam_ref, ng_ref,
         o_ref, kb_scr, vt_scr) = refs
    else:
        q_ref, k_ref, v_ref, lam_ref, ng_ref, o_ref, kb_scr, vt_scr = refs
    qi = pl.program_id(2)
    lane = lax.broadcasted_iota(jnp.int32, (1, LANES), 1)
    first = lane < HEAD_DIM

    def put_v(v, lo, hi):
        vt = v.T
        ones = jnp.ones((HEAD_DIM, hi - lo), f32)
        vt_scr[0, :, lo:hi] = jnp.concatenate([vt[0:HEAD_DIM], ones], axis=0).astype(bf16)
        vt_scr[1, :, lo:hi] = jnp.concatenate([vt[HEAD_DIM:], ones], axis=0).astype(bf16)

    @pl.when(qi == 0)
    def _():
        k = k_ref[...]
        if rope:
            k = _rope(k, ckk_ref[...], skk_ref[...])
        kb_scr[0:seq, :] = k.astype(bf16)
        put_v(v_ref[...], 0, seq)
        if ctx_len:
            kb_scr[seq:seq + ctx_len, :] = ck_ref[...].astype(bf16)
            put_v(cv_ref[...], seq, seq + ctx_len)

    q = q_ref[...]
    if rope:
        q = _rope(q, cq_ref[...], sq_ref[...])
    q = q * (DIFF_DK ** -0.5 * LOG2E)
    n_keys = seq + ctx_len
    ck = max(w for w in range(LANES, DIFF_KEY_CHUNK + 1, LANES) if n_keys % w == 0)
    nck = n_keys // ck
    tq = q.shape[0]
    sub = min(DIFF_Q_SUB, tq)
    jobs = [(qs, g) for qs in range(tq // sub) for g in range(4)]
    lp = lam_ref[...]
    lam = (jnp.exp(jnp.sum(lp[0:1] * lp[1:2], keepdims=True)) - jnp.exp(jnp.sum(lp[2:3] * lp[3:4], keepdims=True))
           + lam_init)

    def finish(qs, res):
        o = jnp.concatenate([res[0] - lam * res[1], res[2] - lam * res[3]], axis=0).T
        o2 = o * o
        ss_a = jnp.sum(jnp.where(first, o2, 0.0), axis=-1, keepdims=True)
        ss_b = jnp.sum(jnp.where(first, 0.0, o2), axis=-1, keepdims=True)
        inv = jnp.where(first, lax.rsqrt(ss_a * (1.0 / HEAD_DIM) + NORM_EPS),
                        lax.rsqrt(ss_b * (1.0 / HEAD_DIM) + NORM_EPS))
        o_ref[qs * sub:(qs + 1) * sub, :] = (o * inv * ng_ref[...] * (1.0 - lam_init)).astype(bf16)

    res = []
    s_prev = m_prev = None
    for j in range(len(jobs) + 1):
        s_cur, mvec, acc = [], None, None
        if j < len(jobs):
            qs, g = jobs[j]
            qm = jnp.where(lane // DIFF_DK == g, q[qs * sub:(qs + 1) * sub], 0.0).astype(bf16)
        for c in range(nck):
            if j < len(jobs):
                s = _dot(qm, kb_scr[c * ck:(c + 1) * ck, :], NT_DIMS)
                s_cur.append(s)
                for k in range(ck // LANES):
                    blk = s[:, k * LANES:(k + 1) * LANES]
                    mvec = blk if mvec is None else jnp.maximum(mvec, blk)
            if j > 0:
                p = jnp.exp2(s_prev[c] - m_prev).astype(bf16)
                part = _dot(vt_scr[jobs[j - 1][1] // 2, :, c * ck:(c + 1) * ck], p, NT_DIMS)
                acc = part if acc is None else acc + part
        if j > 0:
            res.append(acc[0:HEAD_DIM] / acc[HEAD_DIM:HEAD_DIM + 1])
            if len(res) == 4:
                finish(jobs[j - 1][0], res)
                res = []
        if j < len(jobs):
            s_prev, m_prev = s_cur, jnp.max(mvec, axis=-1, keepdims=True)


def diff_mixer(p, lam_p, ng, bsz, seq, lam_init, ctx=None, rope_tabs=None, tq=512):
    tq = min(tq, seq)
    nq = seq // tq
    npair = MIX_W // LANES
    rope = rope_tabs is not None
    ctx_len = ctx[0].shape[1] if ctx is not None else 0
    assert rope == (ctx is not None)
    q_spec = pl.BlockSpec((None, tq, LANES), lambda b, hp, i: (T_DQ, b * nq + i, hp))
    k_spec = pl.BlockSpec((None, seq, LANES), lambda b, hp, i: (T_DK, b, hp))
    v_spec = pl.BlockSpec((None, seq, LANES), lambda b, hp, i: (T_DV, b, hp))
    const = lambda shape: pl.BlockSpec(shape, lambda b, hp, i: (0,) * len(shape))
    args = [p, p, p]
    specs = [q_spec, k_spec, v_spec]
    if rope:
        c_spec = pl.BlockSpec((None, ctx_len, LANES), lambda b, hp, i: (b, 0, hp))
        tab_q = pl.BlockSpec((tq, LANES), lambda b, hp, i: (i, 0))
        args += [ctx[0], ctx[1], rope_tabs[0], rope_tabs[1], rope_tabs[0], rope_tabs[1]]
        specs += [c_spec, c_spec, tab_q, tab_q, const((seq, LANES)), const((seq, LANES))]
    args += [lam_p, ng]
    specs += [const((8, LANES)), const((1, LANES))]
    return pl.pallas_call(
        functools.partial(_diff_kernel, seq=seq, ctx_len=ctx_len, rope=rope, lam_init=lam_init),
        out_shape=jax.ShapeDtypeStruct((bsz * seq, MIX_W), bf16),
        grid=(bsz, npair, nq),
        in_specs=specs,
        out_specs=pl.BlockSpec((tq, LANES), lambda b, hp, i: (b * nq + i, hp)),
        scratch_shapes=[pltpu.VMEM((seq + ctx_len, LANES), bf16), pltpu.VMEM((2, LANES, seq + ctx_len), bf16)],
        compiler_params=_cparams("parallel", "parallel", "arbitrary"),
        name="diff_mixer",
    )(*args)


def _ssd_kernel(sx_ref, bcr_ref, sxp_ref, bcp_ref, sxn_ref, bcn_ref, cw_ref, cb_ref, z_ref, dt_ref, dtb_ref, alog_ref,
                dexp_ref, ng_ref, exp_ref, selb_ref, selc_ref, s0_ref, o_ref, sfin_ref,
                yf_scr, st_scr, dt_scr, dta_scr, xs_ref, bc_ref, *, nblk, tl):
    ph = pl.program_id(1)
    i = pl.program_id(2)
    fwd = ph == 0
    c = SSD_CHUNK
    nch = tl // c
    rep = NH // SSD_G
    blk = jnp.where(fwd, i, nblk - 1 - i)

    @pl.when(i == 0)
    def _():
        st_scr[...] = _expand_state(s0_ref[...], SSD_N, HEAD_DIM)

    brow = pl.multiple_of(blk * tl, tl)

    @pl.when(fwd)
    def _():
        row = lax.broadcasted_iota(jnp.int32, (tl, 1), 0)
        for raw_ref, prev_ref, next_ref, dst_ref, lo in ((sx_ref, sxp_ref, sxn_ref, xs_ref, 0),
                                                        (bcr_ref, bcp_ref, bcn_ref, bc_ref, MIX_W)):
            x = raw_ref[...]
            w = cw_ref[:, lo:lo + x.shape[1]]
            prev_row = jnp.where(blk == 0, 0.0, prev_ref[7:8, :])
            next_row = jnp.where(blk == nblk - 1, 0.0, next_ref[0:1, :])
            xm = jnp.where(row == 0, prev_row, pltpu.roll(x, 1, 0))
            xp = jnp.where(row == tl - 1, next_row, pltpu.roll(x, tl - 1, 0))
            dst_ref[pl.ds(brow, tl), :] = _silu(xm * w[0:1] + x * w[1:2] + xp * w[2:3]
                                                + cb_ref[:, lo:lo + x.shape[1]])

    dtv = _softplus(dt_ref[...] + dtb_ref[...])
    dtv = jnp.where(fwd, dtv, pltpu.roll(dtv, LANES - NH, 1))
    av = -jnp.exp(alog_ref[...])
    av = jnp.where(fwd, av, pltpu.roll(av, LANES - NH, 1))
    dt_scr[...] = dtv
    dta_scr[...] = dtv * av

    tri = _tri_mask(c, c, jnp.where(fwd, 1, -1))
    trib = jnp.where(tri, 1.0, 0.0).astype(bf16)
    head = lax.broadcasted_iota(jnp.int32, (1, MIX_W), 1) // HEAD_DIM
    bd = (lax.broadcasted_iota(jnp.int32, (MIX_W, MIX_W), 0) // SSD_N
          == lax.broadcasted_iota(jnp.int32, (MIX_W, MIX_W), 1) // HEAD_DIM)
    def chunk(j, carry):
        cj = jnp.where(fwd, j, nch - 1 - j)
        off = pl.multiple_of(cj * c, c)
        srow = pl.multiple_of(brow + off, c)
        dtc = dt_scr[pl.ds(off, c), :]
        cum = _sel_dot(trib, dta_scr[pl.ds(off, c), :])
        cum_t = cum.T
        dt_t = dtc.T
        earg = _sel_dot(cum, exp_ref[...])
        elast = jnp.where(fwd, earg[c - 1:c, :], earg[0:1, :])
        dtx = _sel_dot(dtc, exp_ref[...])
        bcb = bc_ref[pl.ds(srow, c), :].astype(bf16)
        bt = _dot(bcb, selb_ref[...])
        ct = _dot(bcb, selc_ref[...])
        x = xs_ref[pl.ds(srow, c), :]
        xb = x.astype(bf16)
        st = st_scr[...]
        y = _dot((ct * jnp.exp(earg)).astype(bf16), st.astype(bf16))
        btail = (bt * jnp.exp(elast - earg) * dtx).astype(bf16)
        new = _dot(btail, xb, TN_DIMS)
        st_scr[...] = st * jnp.exp(elast) + jnp.where(bd, new, 0.0)
        btb = bt.astype(bf16)
        cb = [_dot(jnp.where(head == g * rep, ct, 0.0).astype(bf16), btb, NT_DIMS) for g in range(SSD_G)]
        ws = []
        for h in range(NH):
            seg = cum[:, h:h + 1] - cum_t[h:h + 1, :]
            lm = jnp.exp(jnp.where(tri, seg, -jnp.inf))
            ws.append((cb[h // rep] * lm * dt_t[h:h + 1, :]).astype(bf16))
        r = _dot(jnp.concatenate(ws, axis=0), xb)
        for h in range(NH):
            y += jnp.where(head == h, r[h * c:(h + 1) * c], 0.0)
        yf_scr[pl.ds(pl.multiple_of(brow + off, c), c), :] += y
        return carry

    @pl.when(fwd)
    def _():
        yf_scr[pl.ds(brow, tl), :] = jnp.zeros((tl, MIX_W), f32)

    lax.fori_loop(0, nch, chunk, 0, unroll=True)

    @pl.when(ph == 1)
    def _():
        yt = yf_scr[pl.ds(brow, tl), :] + xs_ref[pl.ds(brow, tl), :] * dexp_ref[...]
        t = yt * _silu(z_ref[...])
        ms = jnp.mean(t * t, axis=-1, keepdims=True)
        o_ref[...] = (t * lax.rsqrt(ms + NORM_EPS) * ng_ref[...]).astype(bf16)

    @pl.when(i == nblk - 1)
    def _():
        sfin_ref[...] = _compact_state(st_scr[...])


def ssd_mixer(p, cw, cb, dtb, alog, dexp, ng, expand, selb, selc, s0, bsz, seq, tl=1024):
    tl = min(tl, seq)
    nblk = seq // tl
    n8 = bsz * seq // 8
    per8 = tl // 8
    blk = lambda ph, i: jnp.where(ph == 0, i, nblk - 1 - i)
    rows = lambda b, ph, i: b * nblk + blk(ph, i)
    const = lambda shape: pl.BlockSpec(shape, lambda b, ph, i: (0,) * len(shape))
    state = pl.BlockSpec((None, None, SSD_N, MIX_W), lambda b, ph, i: (b, ph, 0, 0))
    body = lambda tile, w: pl.BlockSpec((None, tl, w), lambda b, ph, i: (tile, rows(b, ph, i), 0))
    crow = lambda b, ph, i: jnp.where(ph == 0, rows(b, ph, i), b * nblk + nblk - 1)
    cbody = lambda tile, w: pl.BlockSpec((None, tl, w), lambda b, ph, i: (tile, crow(b, ph, i), 0))
    prev8 = lambda tile, w: pl.BlockSpec(
        (None, 8, w), lambda b, ph, i: (tile, jnp.maximum(crow(b, ph, i) * per8 - 1, 0), 0))
    next8 = lambda tile, w: pl.BlockSpec(
        (None, 8, w), lambda b, ph, i: (tile, jnp.minimum((crow(b, ph, i) + 1) * per8, n8 - 1), 0))
    bcw = 2 * SSD_G * SSD_N
    return pl.pallas_call(
        functools.partial(_ssd_kernel, nblk=nblk, tl=tl),
        out_shape=(jax.ShapeDtypeStruct((bsz * seq, MIX_W), bf16),
                   jax.ShapeDtypeStruct((bsz, 2, SSD_N, MIX_W), f32)),
        grid=(bsz, 2, nblk),
        in_specs=[cbody(T_SX, PT), cbody(T_MISC, bcw), prev8(T_SX, PT), prev8(T_MISC, bcw),
                  next8(T_SX, PT), next8(T_MISC, bcw), const((3, MIX_W + bcw)), const((1, MIX_W + bcw)),
                  body(T_SZ, PT),
                  pl.BlockSpec((None, tl, LANES), lambda b, ph, i: (T_MISC, rows(b, ph, i), MISC_DT // LANES)),
                  const((1, LANES)), const((1, LANES)), const((1, MIX_W)), const((1, MIX_W)),
                  const((LANES, MIX_W)), const((2 * LANES, MIX_W)), const((2 * LANES, MIX_W)), state],
        out_specs=(pl.BlockSpec((tl, MIX_W),
                                lambda b, ph, i: (b * nblk + jnp.where(ph == 0, nblk - 1, nblk - 1 - i), 0)),
                   state),
        scratch_shapes=[pltpu.VMEM((seq, MIX_W), f32), pltpu.VMEM((MIX_W, MIX_W), f32),
                        pltpu.VMEM((tl, LANES), f32), pltpu.VMEM((tl, LANES), f32),
                        pltpu.VMEM((seq, MIX_W), f32), pltpu.VMEM((seq, bcw), f32)],
        compiler_params=_cparams("parallel", "arbitrary", "arbitrary"),
        name="ssd_mixer",
    )(p, p, p, p, p, p, cw, cb.reshape(1, -1), p, p, dtb, alog, dexp, ng, expand, selb, selc, s0)


def _hy_filter_kernel(zf_ref, zb_ref, tf_ref, tb_ref, w1_ref, b1_ref, sw_ref, w2_ref, b2_ref,
                      w3f_ref, w3b_ref, b3f_ref, b3b_ref, df_ref, db_ref, skip_ref, r_ref, hf_scr, hb_scr, *, seq):
    first = jnp.logical_and(pl.program_id(0) == 0, pl.program_id(1) == 0)

    @pl.when(first)
    def _():
        for z_ref, h_scr in ((zf_ref, hf_scr), (zb_ref, hb_scr)):
            h = jnp.sin(sw_ref[...] * (_dot(w1_ref[...], z_ref[...], precision=HI) + b1_ref[...]))
            h_scr[...] = jnp.sin(sw_ref[...] * (_dot(w2_ref[...], h, precision=HI) + b2_ref[...]))

    hf = (_dot(w3f_ref[...], hf_scr[...], precision=HI) + b3f_ref[...]) * jnp.exp(-tf_ref[...] * jnp.abs(df_ref[...]))
    hb = (_dot(w3b_ref[...], hb_scr[...], precision=HI) + b3b_ref[...]) * jnp.exp(-tb_ref[...] * jnp.abs(db_ref[...]))
    den = (jnp.sum(jnp.abs(hf), axis=-1, keepdims=True) + jnp.sum(jnp.abs(hb), axis=-1, keepdims=True)) + NORM_EPS
    col = lax.broadcasted_iota(jnp.int32, (1, seq), 1)
    r_ref[:, 0:seq] = jnp.where(col == 0, 0.0, hb / den)
    r_ref[:, seq:2 * seq] = hf / den + jnp.where(col == 0, skip_ref[...], 0.0)


def hy_filters(fp, seq, cbf=128):
    t = jnp.arange(seq, dtype=f32) / seq
    t_rev = jnp.concatenate([t[:1], t[:0:-1]])

    def feats(tt):
        tc = tt[:, None]
        ang = 2.0 * math.pi * tc * jnp.arange(1, HY_BANDS + 1, dtype=f32)
        z = jnp.concatenate([tc, jnp.cos(ang), jnp.sin(ang)], axis=-1)
        return jnp.pad(z, ((0, 0), (0, HY_EMB_PAD - HY_EMB))).T

    ch = MIX_W
    nb = ch // cbf
    col = lambda a: a.reshape(-1, 1)
    w3t = fp['w3'].T.reshape(2, 2, ch, HY_HID)
    b3 = fp['b3'].reshape(2, 2, ch, 1)
    dec = fp['decay'].reshape(2, 2, ch, 1)
    const = lambda shape: pl.BlockSpec(shape, lambda o, j: (0,) * len(shape))
    sel = lambda d, last: pl.BlockSpec((None, None, cbf, last), lambda o, j: (o, d, j, 0))
    return pl.pallas_call(
        functools.partial(_hy_filter_kernel, seq=seq),
        out_shape=jax.ShapeDtypeStruct((2, ch, 2 * seq), f32),
        grid=(2, nb),
        in_specs=[const((HY_EMB_PAD, seq)), const((HY_EMB_PAD, seq)), const((1, seq)), const((1, seq)),
                  const((HY_HID, HY_EMB_PAD)), const((HY_HID, 1)), const((HY_HID, 1)),
                  const((HY_HID, HY_HID)), const((HY_HID, 1)),
                  sel(0, HY_HID), sel(1, HY_HID), sel(0, 1), sel(1, 1), sel(0, 1), sel(1, 1),
                  pl.BlockSpec((None, cbf, 1), lambda o, j: (o, j, 0))],
        out_specs=pl.BlockSpec((None, cbf, 2 * seq), lambda o, j: (o, j, 0)),
        scratch_shapes=[pltpu.VMEM((HY_HID, seq), f32), pltpu.VMEM((HY_HID, seq), f32)],
        compiler_params=_cparams("arbitrary", "arbitrary"),
        name="hy_filters",
    )(feats(t), feats(t_rev), t.reshape(1, seq), t_rev.reshape(1, seq),
      jnp.pad(fp['w1'], ((0, HY_EMB_PAD - HY_EMB), (0, 0))).T, col(fp['b1']), col(fp['sin_w']),
      fp['w2'].T, col(fp['b2']), w3t, w3t, b3, b3, dec, dec, fp['skip'].reshape(2, ch, 1))


def _hyena_kernel(cw_ref, cb_ref, hv_ref, h1_ref, h2_ref, r_ref, o_ref, acc_scr, *, bsz, n_i, cb_n):
    tb = HY_TB
    gr = tb // 2
    sub = tb // LANES
    seq = n_i * tb
    rows = n_i * bsz
    base = pl.program_id(0) * cb_n
    n_ch = MIX_W
    lane = lax.broadcasted_iota(jnp.int32, (1, tb), 1)
    zblk = jnp.zeros((bsz, tb), f32)

    def load(ref, ch):
        return jnp.concatenate(
            [jnp.concatenate([ref[ch, pl.ds(sub * ib + k, bsz, stride=sub * n_i), :] for k in range(sub)], axis=1)
             for ib in range(n_i)], axis=0)

    def short_conv(a, stream, ch):
        idx = stream * n_ch + base + ch
        prev = jnp.concatenate([zblk, a[:rows - bsz]], axis=0) if n_i > 1 else zblk
        nxt = jnp.concatenate([a[bsz:], zblk], axis=0) if n_i > 1 else zblk
        am = pltpu.roll(jnp.where(lane == tb - 1, prev, a), 1, 1)
        ap = pltpu.roll(jnp.where(lane == 0, nxt, a), tb - 1, 1)
        return (am * cw_ref[idx] + a * cw_ref[3 * n_ch + idx] + ap * cw_ref[6 * n_ch + idx]) + cb_ref[idx]

    def long_conv(u, order, ch):
        rrow = r_ref[order, pl.ds(ch, 1), :]
        bits = lambda a: lax.bitcast_convert_type(a.astype(bf16).astype(f32), jnp.int32)
        word = (bits(pltpu.roll(rrow, 1, 1)) & jnp.int32(-65536)) | lax.shift_right_logical(bits(rrow), 16)
        g = pltpu.bitcast(
            pltpu.roll(jnp.broadcast_to(word, (gr // 2, 2 * seq)), 0, 1, stride=2, stride_axis=0), bf16)
        for d in [0] + [e for e in range(-(n_i - 1), n_i) if e != 0]:
            n = n_i - abs(d)
            src = max(0, -d) * bsz
            dst = max(0, d) * bsz
            c0 = seq + d * tb
            m = jnp.concatenate([g[:, c0:c0 + tb], g[:, c0 - gr:c0 - gr + tb]], axis=0)
            part = _dot(u[src:src + n * bsz].astype(bf16), m)
            if d == 0:
                acc_scr[order] = part
            else:
                acc_scr[order, dst:dst + n * bsz, :] += part
        return acc_scr[order]

    def body(ch, carry):
        hv = short_conv(load(hv_ref, ch), 0, ch)
        hx1 = short_conv(load(h1_ref, ch), 1, ch)
        hx2 = short_conv(load(h2_ref, ch), 2, ch)
        zz = hx1 * long_conv(hv, 0, ch)
        out = hx2 * long_conv(zz, 1, ch)
        for ib in range(n_i):
            for k in range(sub):
                o_ref[ch, pl.ds(sub * ib + k, bsz, stride=sub * n_i), :] = (
                    out[ib * bsz:(ib + 1) * bsz, k * LANES:(k + 1) * LANES])
        return carry

    lax.fori_loop(0, cb_n, body, 0, unroll=2)


def hyena_mixer(hy_t, r, cw, cb, bsz, seq, cb_n=8):
    n_i = seq // HY_TB
    ch = MIX_W
    nblk = ch // cb_n
    n_rows = bsz * seq // LANES
    x3 = hy_t.reshape(3 * ch, n_rows, LANES)
    stream = lambda s: pl.BlockSpec((cb_n, n_rows, LANES), lambda j: (s * nblk + j, 0, 0))
    smem = pl.BlockSpec(memory_space=pltpu.SMEM)
    out = pl.pallas_call(
        functools.partial(_hyena_kernel, bsz=bsz, n_i=n_i, cb_n=cb_n),
        out_shape=jax.ShapeDtypeStruct((ch, n_rows, LANES), f32),
        grid=(nblk,),
        in_specs=[smem, smem, stream(0), stream(1), stream(2),
                  pl.BlockSpec((2, cb_n, 2 * seq), lambda j: (0, j, 0))],
        out_specs=pl.BlockSpec((cb_n, n_rows, LANES), lambda j: (j, 0, 0)),
        scratch_shapes=[pltpu.VMEM((2, bsz * n_i, HY_TB), f32)],
        compiler_params=_cparams("parallel"),
        name="hyena_mixer",
    )(cw.reshape(-1), cb, x3, x3, x3, r)
    return out.reshape(ch, bsz * seq)


_IN_SIZES = (NH * GLA_DK, NH * GLA_DK, MIX_W, MIX_W, 2 * GLA_RANK, MIX_W, MIX_W, MIX_W,
             MIX_W, MIX_W + 2 * SSD_G * SSD_N, 2 * NH, 3 * MIX_W)
_IN_OFFS = np.concatenate([[0], np.cumsum(_IN_SIZES)]).tolist()


def _prep_w_in(w_in):
    (gq, gk, gv, gg, gr, dq, dk, dv, sz, sxbc, sdt, hy) = [w_in[:, _IN_OFFS[i]:_IN_OFFS[i + 1]]
                                                          for i in range(len(_IN_SIZES))]
    d = w_in.shape[0]
    zeros = lambda n: jnp.zeros((d, n), w_in.dtype)
    misc = jnp.concatenate([sxbc[:, MIX_W:], gr, zeros(LANES - 2 * GLA_RANK), sdt, zeros(LANES - 2 * NH)], axis=1)
    w_tok = jnp.concatenate([gq, gk, gv, gg, dq, dk, dv, sz, sxbc[:, :MIX_W], misc], axis=1)
    w_tok = w_tok.astype(bf16).reshape(d, N_PT, PT).transpose(1, 0, 2)
    return w_tok, hy.T.astype(bf16)


def _prep_gate_w(gate_w):
    w = jnp.zeros((2, LANES, gate_w.shape[2]), f32)
    for d in range(2):
        w = w.at[d, d * GLA_RANK:(d + 1) * GLA_RANK, :].set(gate_w[d])
    hi = w.astype(bf16)
    return hi, (w - hi.astype(f32)).astype(bf16)


def _rope_tables(seq):
    rows = seq // GRID_W
    r, col = jnp.meshgrid(jnp.arange(rows), jnp.arange(GRID_W), indexing='ij')
    r = r.reshape(-1).astype(f32)
    col = col.reshape(-1).astype(f32)
    nf = DIFF_DK // 4
    inv = ROPE_BASE ** (-jnp.arange(nf, dtype=f32) / nf)
    ang = jnp.concatenate([r[:, None] * inv, col[:, None] * inv], axis=-1)
    cos, sin = jnp.cos(ang), jnp.sin(ang)
    reps = LANES // DIFF_DK
    return jnp.tile(jnp.concatenate([cos, cos], axis=-1), (1, reps)), jnp.tile(jnp.concatenate([-sin, sin], axis=-1), (1, reps))


def _const_tables():
    hsum = np.kron(np.eye(NH, dtype=np.float32), np.ones((HEAD_DIM, HEAD_DIM), np.float32))
    expand = np.zeros((LANES, MIX_W), np.float32)
    selb = np.zeros((2 * LANES, MIX_W), np.float32)
    selc = np.zeros((2 * LANES, MIX_W), np.float32)
    rep = NH // SSD_G
    for h in range(NH):
        expand[h, h * HEAD_DIM:(h + 1) * HEAD_DIM] = 1.0
        g = h // rep
        for n in range(SSD_N):
            selb[g * SSD_N + n, h * SSD_N + n] = 1.0
            selc[SSD_G * SSD_N + g * SSD_N + n, h * SSD_N + n] = 1.0
    return tuple(jnp.asarray(a, dtype=bf16) for a in (hsum, expand, selb, selc))


def _pack_state(s):
    bsz, _, h, a, b = s.shape
    return s.transpose(0, 1, 4, 2, 3).reshape(bsz, 2, b, h * a)


def _unpack_state(st, a):
    bsz, _, b, _ = st.shape
    return st.reshape(bsz, 2, b, NH, a).transpose(0, 1, 3, 4, 2)


def _layer(x, mod, rows_per_mod, lw, layer, consts, bsz, seq, lam_init, r_filt, ctx, rope_tabs, final_g,
           final_norm):
    hsum, expand, selb, selc = consts
    p, hy_t = in_proj(x, mod, lw['norm1_g'], lw['w_tok'], lw['w_hyT'], rows_per_mod)
    if ctx is None:
        gla_s0 = jnp.zeros((bsz, 2, HEAD_DIM, NH * GLA_DK), f32)
        ssd_s0 = jnp.zeros((bsz, 2, SSD_N, MIX_W), f32)
        dctx = None
    else:
        ctx_k, ctx_v, gla_s, ssd_s = ctx
        gla_s0 = _pack_state(gla_s)
        ssd_s0 = _pack_state(ssd_s)
        dctx = (ctx_k.reshape(bsz, -1, MIX_W), ctx_v.reshape(bsz, -1, MIX_W))
    o_gla, gla_fin = gla_mixer(p, lw['gla_gw_hi'], lw['gla_gw_lo'], lw['gla_gb'], lw['gla_ng'], hsum, gla_s0,
                               bsz, seq)
    o_diff = diff_mixer(p, lw['lam_p'], lw['diff_ng'], bsz, seq, lam_init, dctx, rope_tabs)
    o_ssd, ssd_fin = ssd_mixer(p, lw['ssd_conv_w'], lw['ssd_conv_b'], lw['ssd_dtb'], lw['ssd_alog'], lw['ssd_dexp'], lw['ssd_ng'],
                               expand, selb, selc, ssd_s0, bsz, seq)
    o_hy = hyena_mixer(hy_t, r_filt, lw['hy_cw'], lw['hy_cb'], bsz, seq)
    x = out_proj(x, mod, o_gla, o_diff, o_ssd, o_hy, lw['w_out'], layer, rows_per_mod)
    x = mlp(x, mod, lw['norm2_g'], lw['mlp_w1'], lw['mlp_w2'], layer, final_g, rows_per_mod, final_norm)
    return x, p, gla_fin, ssd_fin


def kernel(x_prompt, x_sample, cache_diff_k, cache_diff_v, state_gla, state_ssd, c, c_ctx, ada_w, ada_b, norm1_g, norm2_g, w_in, w_out, gla_gate_w, gla_gate_b, gla_norm_g, diff_lambda, diff_norm_g, ssd_conv_w, ssd_conv_b, ssd_dt_bias, ssd_a_log, ssd_d, ssd_norm_g, hy_conv_w, hy_conv_b, hy_f_w1, hy_f_b1, hy_f_w2, hy_f_b2, hy_f_w3, hy_f_b3, hy_sin_w, hy_decay, hy_skip, mlp_w1, mlp_w2, final_g):
    bp, lp, d = x_prompt.shape
    bs, ls, _ = x_sample.shape
    depth = w_in.shape[0]
    consts = _const_tables()
    rope_tabs = _rope_tables(ls)
    n_c = 16
    cvec = jnp.concatenate([c_ctx[None], c, jnp.zeros((n_c - 1 - bs, d), f32)], axis=0)
    hp = x_prompt.reshape(bp * lp, d)
    hs = x_sample.reshape(bs * ls, d)
    ks_, vs_, gs_, ss_ = [], [], [], []
    w_out_b, mlp_w1_b, mlp_w2_b = w_out.astype(bf16), mlp_w1.astype(bf16), mlp_w2.astype(bf16)
    for l in range(depth):
        w_tok, w_hyT = _prep_w_in(w_in[l])
        pad_l = lambda a: jnp.pad(a, (0, LANES - a.shape[0])).reshape(1, LANES)
        gw_hi, gw_lo = _prep_gate_w(gla_gate_w[l])
        lw = dict(
            norm1_g=norm1_g[l], norm2_g=norm2_g[l], w_tok=w_tok, w_hyT=w_hyT, w_out=w_out_b,
            mlp_w1=mlp_w1_b, mlp_w2=mlp_w2_b,
            gla_gw_hi=gw_hi, gla_gw_lo=gw_lo, gla_gb=gla_gate_b[l].reshape(2, 1, -1), gla_ng=jnp.tile(gla_norm_g[l], NH).reshape(1, MIX_W),
            lam_p=jnp.pad(diff_lambda[l], ((0, 4), (0, LANES - DIFF_DK))),
            diff_ng=jnp.tile(diff_norm_g[l], LANES // HEAD_DIM).reshape(1, LANES),
            ssd_conv_w=ssd_conv_w[l], ssd_conv_b=ssd_conv_b[l],
            ssd_dtb=pad_l(ssd_dt_bias[l].reshape(-1)), ssd_alog=pad_l(ssd_a_log[l].reshape(-1)),
            ssd_dexp=jnp.repeat(ssd_d[l], HEAD_DIM).reshape(1, MIX_W), ssd_ng=ssd_norm_g[l].reshape(1, MIX_W),
            hy_cw=hy_conv_w[l], hy_cb=hy_conv_b[l],
        )
        fp = dict(w1=hy_f_w1[l], b1=hy_f_b1[l], w2=hy_f_w2[l], b2=hy_f_b2[l], w3=hy_f_w3[l], b3=hy_f_b3[l],
                  sin_w=hy_sin_w[l], decay=hy_decay[l], skip=hy_skip[l])
        mod = ada_mod(cvec, ada_w, ada_b, l).reshape(n_c, 6, d)
        lam_init = 0.8 - 0.6 * math.exp(-0.3 * l)
        last = l == depth - 1
        hp, p_p, g_l, s_l = _layer(hp, mod[0:1], bp * lp, lw, l, consts, bp, lp, lam_init, hy_filters(fp, lp),
                                   None, None, final_g, last)
        ks_.append(p_p[T_DK].reshape(bp, lp, NH, 2 * DIFF_DK))
        vs_.append(p_p[T_DV].reshape(bp, lp, NH, HEAD_DIM))
        gs_.append(_unpack_state(g_l, GLA_DK))
        ss_.append(_unpack_state(s_l, HEAD_DIM))
        hs, _, _, _ = _layer(hs, mod[1:1 + bs], ls, lw, l, consts, bs, ls, lam_init, hy_filters(fp, ls),
                             (cache_diff_k[:, l], cache_diff_v[:, l], state_gla[:, l], state_ssd[:, l]),
                             rope_tabs, final_g, last)
    return (hp.reshape(bp, lp, d), hs.reshape(bs, ls, d), jnp.stack(ks_, axis=1), jnp.stack(vs_, axis=1),
            jnp.stack(gs_, axis=1), jnp.stack(ss_, axis=1))
```

```python
import functools
import math

import numpy as np
import jax
import jax.numpy as jnp
from jax import lax
from jax.experimental import pallas as pl
from jax.experimental.pallas import tpu as pltpu

f32 = jnp.float32
bf16 = jnp.bfloat16
HI = lax.Precision.HIGHEST

LANES = 128
VMEM_LIMIT = 56 * 2**20

D_MODEL = 2048
GRID_W = 64
MIX_W = D_MODEL // 4
HEAD_DIM = 64
NH = MIX_W // HEAD_DIM
D_FF = 4 * D_MODEL
NORM_EPS = 1e-6
LOG2E = 1.0 / math.log(2.0)
GLA_DK = HEAD_DIM // 2
GLA_RANK = 16
GLA_TAU = 16.0
GLA_CHUNK = 64
DIFF_DK = HEAD_DIM // 2
ROPE_BASE = 10000.0
SSD_N = 64
SSD_G = 2
SSD_CHUNK = 128
HY_BANDS = 16
HY_EMB = 2 * HY_BANDS + 1
HY_EMB_PAD = 40
HY_HID = 64
DIFF_KEY_CHUNK = 512
DIFF_Q_SUB = 512
MLP_TF = 512
HY_TB = 256

PT = 512
(T_GQK, T_GV, T_GG, T_DQ, T_DK, T_DV, T_SZ, T_SX, T_MISC) = range(9)
N_PT = 9
MISC_GR = 256
MISC_DT = 384
SH1, SC1, G1, SH2, SC2, G2 = range(6)

NT_DIMS = (((1,), (1,)), ((), ()))
TN_DIMS = (((0,), (0,)), ((), ()))


def _cparams(*sem):
    return pltpu.CompilerParams(dimension_semantics=sem, vmem_limit_bytes=VMEM_LIMIT)


def _sigmoid(x):
    return 1.0 / (1.0 + jnp.exp(-x))


def _silu(x):
    return x * _sigmoid(x)


def _softplus(x):
    return jnp.maximum(x, 0.0) + jnp.log1p(jnp.exp(-jnp.abs(x)))


def _log_sigmoid(x):
    return jnp.minimum(x, 0.0) - jnp.log1p(jnp.exp(-jnp.abs(x)))


def _tri_mask(rows, c, sgn):
    r_i = lax.broadcasted_iota(jnp.int32, (rows, c), 0) & (c - 1)
    c_i = lax.broadcasted_iota(jnp.int32, (rows, c), 1)
    return (r_i - c_i) * sgn >= 0


def _expand_state(s, row_w, col_w):
    rows, cols = NH * row_w, NH * col_w
    bd = (lax.broadcasted_iota(jnp.int32, (rows, cols), 0) // row_w
          == lax.broadcasted_iota(jnp.int32, (rows, cols), 1) // col_w)
    return jnp.where(bd, jnp.concatenate([s] * NH, axis=0), 0.0)


def _compact_state(st):
    row_w = st.shape[0] // NH
    out = st[0:row_w]
    for h in range(1, NH):
        out += st[h * row_w:(h + 1) * row_w]
    return out


def _split(x, n):
    parts = []
    for _ in range(n - 1):
        hi = x.astype(bf16)
        parts.append(hi)
        x = x - hi.astype(f32)
    parts.append(x.astype(bf16))
    return parts


def _sel_dot(a, b, dims=None, terms=3):
    if a.dtype == bf16:
        prods = [_dot(a, p, dims) for p in _split(b, terms)]
    else:
        prods = [_dot(p, b, dims) for p in _split(a, terms)]
    return functools.reduce(lambda x, y: x + y, prods)


def _per_head_groups(g2):
    assert SSD_G == 2 and SSD_N == HEAD_DIM and LANES == 2 * SSD_N
    first = lax.broadcasted_iota(jnp.int32, (1, LANES), 1) < SSD_N
    swapped = pltpu.roll(g2, SSD_N, 1)
    g0 = jnp.where(first, g2, swapped)
    g1 = jnp.where(first, swapped, g2)
    rep = NH * HEAD_DIM // (SSD_G * LANES)
    return jnp.concatenate([g0] * rep + [g1] * rep, axis=1)


def _dot(a, b, dims=None, precision=None):
    if dims is None:
        return jnp.dot(a, b, preferred_element_type=f32, precision=precision)
    return lax.dot_general(a, b, dims, preferred_element_type=f32, precision=precision)


def _ada_kernel(c_ref, w_ref, b_ref, o_ref):
    s = _silu(c_ref[...]).astype(bf16)
    o_ref[...] = _dot(s, w_ref[...].astype(bf16)) + b_ref[...]


def ada_mod(cvec, w, b, layer, tn=1024):
    m, d = cvec.shape
    n = w.shape[2]
    return pl.pallas_call(
        _ada_kernel,
        out_shape=jax.ShapeDtypeStruct((m, n), f32),
        grid=(n // tn,),
        in_specs=[pl.BlockSpec((m, d), lambda j: (0, 0)),
                  pl.BlockSpec((None, d, tn), lambda j: (layer, 0, j)),
                  pl.BlockSpec((None, 1, tn), lambda j: (layer, 0, j))],
        out_specs=pl.BlockSpec((m, tn), lambda j: (0, j)),
        compiler_params=_cparams("arbitrary"),
        name="ada_mod",
    )(cvec, w, b.reshape(b.shape[0], 1, n))


def _norm_mod(x, g, sc, sh):
    ms = jnp.mean(x * x, axis=-1, keepdims=True)
    return (x * lax.rsqrt(ms + NORM_EPS) * g) * (1.0 + sc) + sh


def _inproj_kernel(x_ref, mod_ref, g_ref, wt_ref, wh_ref, p_ref, h_ref, u_scr):
    j = pl.program_id(1)

    @pl.when(j == 0)
    def _():
        u = _norm_mod(x_ref[...], g_ref[...], mod_ref[SC1:SC1 + 1, :], mod_ref[SH1:SH1 + 1, :])
        u_scr[...] = u.astype(bf16)

    @pl.when(j < N_PT)
    def _():
        p_ref[...] = _dot(u_scr[...], wt_ref[...])

    @pl.when(j >= N_PT)
    def _():
        h_ref[...] = _dot(wh_ref[...], u_scr[...], NT_DIMS)


def in_proj(x, mod, g, w_tok, w_hyT, rows_per_mod, tm=1024):
    t, d = x.shape
    assert t % tm == 0 and rows_per_mod % tm == 0, (t, rows_per_mod, tm)
    n_hy = w_hyT.shape[0] // PT
    return pl.pallas_call(
        _inproj_kernel,
        out_shape=(jax.ShapeDtypeStruct((N_PT, t, PT), f32), jax.ShapeDtypeStruct((n_hy * PT, t), f32)),
        grid=(t // tm, N_PT + n_hy),
        in_specs=[pl.BlockSpec((tm, d), lambda i, j: (i, 0)),
                  pl.BlockSpec((None, 6, d), lambda i, j: ((i * tm) // rows_per_mod, 0, 0)),
                  pl.BlockSpec((1, d), lambda i, j: (0, 0)),
                  pl.BlockSpec((None, d, PT), lambda i, j: (jnp.minimum(j, N_PT - 1), 0, 0)),
                  pl.BlockSpec((PT, d), lambda i, j: (jnp.maximum(j - N_PT, 0), 0))],
        out_specs=(pl.BlockSpec((None, tm, PT), lambda i, j: (jnp.minimum(j, N_PT - 1), i, 0)),
                   pl.BlockSpec((PT, tm), lambda i, j: (jnp.maximum(j - N_PT, 0), i))),
        scratch_shapes=[pltpu.VMEM((tm, d), bf16)],
        compiler_params=_cparams("parallel", "arbitrary"),
        name="in_proj",
    )(x, mod, g.reshape(1, d), w_tok, w_hyT)


def _outproj_kernel(x_ref, mod_ref, a_ref, b_ref, c_ref, dT_ref, w_ref, o_ref):
    acc = _dot(a_ref[...], w_ref[0:MIX_W, :])
    acc += _dot(b_ref[...], w_ref[MIX_W:2 * MIX_W, :])
    acc += _dot(c_ref[...], w_ref[2 * MIX_W:3 * MIX_W, :])
    acc += _dot(dT_ref[...].astype(bf16), w_ref[3 * MIX_W:4 * MIX_W, :], TN_DIMS)
    o_ref[...] = x_ref[...] + mod_ref[G1:G1 + 1, :] * acc


def out_proj(x, mod, o_a, o_b, o_c, o_dT, w_out, layer, rows_per_mod, tm=512):
    t, d = x.shape
    assert t % tm == 0 and rows_per_mod % tm == 0, (t, rows_per_mod, tm)
    tok = lambda i: (i, 0)
    return pl.pallas_call(
        _outproj_kernel,
        out_shape=jax.ShapeDtypeStruct((t, d), f32),
        grid=(t // tm,),
        in_specs=[pl.BlockSpec((tm, d), tok),
                  pl.BlockSpec((None, 6, d), lambda i: ((i * tm) // rows_per_mod, 0, 0)),
                  pl.BlockSpec((tm, MIX_W), tok), pl.BlockSpec((tm, MIX_W), tok), pl.BlockSpec((tm, MIX_W), tok),
                  pl.BlockSpec((MIX_W, tm), lambda i: (0, i)),
                  pl.BlockSpec((None, d, d), lambda i: (layer, 0, 0))],
        out_specs=pl.BlockSpec((tm, d), tok),
        compiler_params=_cparams("parallel"),
        name="out_proj",
    )(x, mod, o_a, o_b, o_c, o_dT, w_out)


def _mlp_kernel(x_ref, mod_ref, g_ref, w1_ref, w2_ref, fg_ref, o_ref, u_scr, *, final_norm):
    j = pl.program_id(1)

    @pl.when(j == 0)
    def _():
        u = _norm_mod(x_ref[...], g_ref[...], mod_ref[SC2:SC2 + 1, :], mod_ref[SH2:SH2 + 1, :])
        u_scr[...] = u.astype(bf16)
        o_ref[...] = jnp.zeros_like(o_ref)

    h = jnp.maximum(_dot(u_scr[...], w1_ref[...]), 0.0)
    hb = (h * h).astype(bf16)
    d = o_ref.shape[1]
    for n in range(d // PT):
        o_ref[:, n * PT:(n + 1) * PT] += _dot(hb, w2_ref[:, n * PT:(n + 1) * PT])

    @pl.when(j == pl.num_programs(1) - 1)
    def _():
        y = x_ref[...] + mod_ref[G2:G2 + 1, :] * o_ref[...]
        if final_norm:
            ms = jnp.mean(y * y, axis=-1, keepdims=True)
            y = y * lax.rsqrt(ms + NORM_EPS) * fg_ref[...]
        o_ref[...] = y


def mlp(x, mod, g, w1, w2, layer, final_g, rows_per_mod, final_norm, tm=1024, tf=MLP_TF):
    t, d = x.shape
    assert t % tm == 0 and rows_per_mod % tm == 0, (t, rows_per_mod, tm)
    n_f = w1.shape[2] // tf
    return pl.pallas_call(
        functools.partial(_mlp_kernel, final_norm=final_norm),
        out_shape=jax.ShapeDtypeStruct((t, d), f32),
        grid=(t // tm, n_f),
        in_specs=[pl.BlockSpec((tm, d), lambda i, j: (i, 0)),
                  pl.BlockSpec((None, 6, d), lambda i, j: ((i * tm) // rows_per_mod, 0, 0)),
                  pl.BlockSpec((1, d), lambda i, j: (0, 0)),
                  pl.BlockSpec((None, d, tf), lambda i, j: (layer, 0, j)),
                  pl.BlockSpec((None, tf, d), lambda i, j: (layer, j, 0)),
                  pl.BlockSpec((1, d), lambda i, j: (0, 0))],
        out_specs=pl.BlockSpec((tm, d), lambda i, j: (i, 0)),
        scratch_shapes=[pltpu.VMEM((tm, d), bf16)],
        compiler_params=_cparams("parallel", "arbitrary"),
        name="mlp",
    )(x, mod, g.reshape(1, d), w1, w2, final_g.reshape(1, d))


def _gla_kernel(qkf_ref, vf_ref, miscf_ref, qkb_ref, vb_ref, miscb_ref, gg_ref, gwh_ref, gwl_ref, gb_ref, ng_ref,
                hsum_ref, s0_ref, o_ref, sfin_ref, of_scr, st_scr, la_scr, *, nblk, tl):
    ph = pl.program_id(1)
    i = pl.program_id(2)
    c = GLA_CHUNK
    nch = tl // c
    qkw = NH * GLA_DK

    @pl.when(ph == 0)
    def _scan():
        chains = ((0, qkf_ref, vf_ref, miscf_ref, i), (1, qkb_ref, vb_ref, miscb_ref, nblk - 1 - i))

        @pl.when(i == 0)
        def _():
            for d in range(2):
                st_scr[d] = _expand_state(s0_ref[d], HEAD_DIM, GLA_DK)

        qk_head = lax.broadcasted_iota(jnp.int32, (1, qkw), 1) // GLA_DK
        v_head = lax.broadcasted_iota(jnp.int32, (1, MIX_W), 1) // HEAD_DIM
        bd = (lax.broadcasted_iota(jnp.int32, (MIX_W, qkw), 0) // HEAD_DIM
              == lax.broadcasted_iota(jnp.int32, (MIX_W, qkw), 1) // GLA_DK)
        for d, _, _, misc_ref, blk in chains:
            gr_hi, gr_lo = _split(misc_ref[:, MISC_GR:MISC_GR + LANES], 2)
            logits = (_dot(gr_hi, gwh_ref[d]) + _dot(gr_lo, gwh_ref[d]) + _dot(gr_hi, gwl_ref[d])) + gb_ref[d]
            la_scr[d] = _log_sigmoid(logits) * (1.0 / GLA_TAU)

            @pl.when((blk <= nblk - 1 - blk) if d == 0 else (blk > nblk - 1 - blk))
            def _():
                of_scr[pl.ds(pl.multiple_of(blk * tl, tl), tl), :] = jnp.zeros((tl, MIX_W), f32)

        def chunk(d, qk_ref, v_ref, blk, cj):
            sgn = 1 if d == 0 else -1
            off = cj * c
            q = qk_ref[off:off + c, 0:qkw] * (GLA_DK ** -0.5)
            k = qk_ref[off:off + c, qkw:2 * qkw]
            vb = v_ref[off:off + c, :].astype(bf16)
            trib = jnp.where(_tri_mask(c, c, sgn), 1.0, 0.0).astype(bf16)
            b = _sel_dot(trib, la_scr[d, off:off + c, :])
            btot = b[c - 1:c, :] if d == 0 else b[0:1, :]
            qd = q * jnp.exp(b)
            kd = (k * jnp.exp(-b)).astype(bf16)
            kt = (k * jnp.exp(btot - b)).astype(bf16)
            qst = jnp.concatenate([jnp.where(qk_head == h, qd, 0.0) for h in range(NH)], axis=0).astype(bf16)
            att = jnp.where(_tri_mask(NH * c, c, sgn), _dot(qst, kd, NT_DIMS), 0.0)
            r = _dot(att.astype(bf16), vb)
            o = jnp.where(v_head == 0, r[0:c], 0.0)
            for h in range(1, NH):
                o += jnp.where(v_head == h, r[h * c:(h + 1) * c], 0.0)
            st = st_scr[d]
            o += _dot(qd.astype(bf16), st.astype(bf16), NT_DIMS)
            kv = _dot(vb, kt, TN_DIMS)
            st_scr[d] = st * jnp.exp(btot) + jnp.where(bd, kv, 0.0)
            of_scr[pl.ds(pl.multiple_of(blk * tl + off, c), c), :] += o

        for j in range(nch):
            for d, qk_ref, v_ref, _, blk in chains:
                chunk(d, qk_ref, v_ref, blk, j if d == 0 else nch - 1 - j)

        @pl.when(i == nblk - 1)
        def _():
            for d in range(2):
                sfin_ref[d] = _compact_state(st_scr[d])

    @pl.when(ph == 1)
    def _finish():
        ot = of_scr[pl.ds(pl.multiple_of(i * tl, tl), tl), :]
        ss = _sel_dot(ot * ot, hsum_ref[...], terms=2) * (1.0 / HEAD_DIM)
        o_ref[...] = (ot * lax.rsqrt(ss + NORM_EPS) * ng_ref[...] * _silu(gg_ref[...])).astype(bf16)


def gla_mixer(p, gw_hi, gw_lo, gb, ng, hsum, s0, bsz, seq, tl=512):
    tl = min(tl, seq)
    nblk = seq // tl
    qkw = NH * GLA_DK
    fwd_blk = lambda ph, i: jnp.where(ph == 0, i, nblk - 1)
    bwd_blk = lambda ph, i: jnp.where(ph == 0, nblk - 1 - i, 0)
    tile = lambda col, blk: pl.BlockSpec((None, tl, PT), lambda b, ph, i: (col, b * nblk + blk(ph, i), 0))
    const = lambda shape: pl.BlockSpec(shape, lambda b, ph, i: (0,) * len(shape))
    state = pl.BlockSpec((None, 2, HEAD_DIM, qkw), lambda b, ph, i: (b, 0, 0, 0))
    out_blk = lambda ph, i: jnp.where(ph == 0, 0, i)
    return pl.pallas_call(
        functools.partial(_gla_kernel, nblk=nblk, tl=tl),
        out_shape=(jax.ShapeDtypeStruct((bsz * seq, MIX_W), bf16),
                   jax.ShapeDtypeStruct((bsz, 2, HEAD_DIM, qkw), f32)),
        grid=(bsz, 2, nblk),
        in_specs=[tile(T_GQK, fwd_blk), tile(T_GV, fwd_blk), tile(T_MISC, fwd_blk),
                  tile(T_GQK, bwd_blk), tile(T_GV, bwd_blk), tile(T_MISC, bwd_blk), tile(T_GG, out_blk),
                  const((2, LANES, qkw)), const((2, LANES, qkw)), const((2, 1, qkw)),
                  const((1, MIX_W)), const((MIX_W, MIX_W)), state],
        out_specs=(pl.BlockSpec((tl, MIX_W), lambda b, ph, i: (b * nblk + out_blk(ph, i), 0)), state),
        scratch_shapes=[pltpu.VMEM((seq, MIX_W), f32), pltpu.VMEM((2, MIX_W, qkw), f32),
                        pltpu.VMEM((2, tl, qkw), f32)],
        compiler_params=_cparams("parallel", "arbitrary", "arbitrary"),
        name="gla_mixer",
    )(p, p, p, p, p, p, p, gw_hi, gw_lo, gb, ng, hsum, s0)


def _rope(x, cos, sin_signed):
    lane = lax.broadcasted_iota(jnp.int32, (1, LANES), 1)
    half = DIFF_DK // 2
    rot = jnp.where((lane % DIFF_DK) < half, pltpu.roll(x, LANES - half, 1), pltpu.roll(x, half, 1))
    return x * cos + rot * sin_signed


def _diff_kernel(*refs, seq, ctx_len, rope, lam_init):
    if rope:
        (q_ref, k_ref, v_ref, ck_ref, cv_ref, cq_ref, sq_ref, ckk_ref, skk_ref, lam_ref, ng_ref,
         o_ref, kb_scr, vt_scr) = refs
    else:
        q_ref, k_ref, v_ref, lam_ref, ng_ref, o_ref, kb_scr, vt_scr = refs
    qi = pl.program_id(2)
    lane = lax.broadcasted_iota(jnp.int32, (1, LANES), 1)
    first = lane < HEAD_DIM

    def put_v(v, lo, hi):
        vt = v.T
        ones = jnp.ones((HEAD_DIM, hi - lo), f32)
        vt_scr[0, :, lo:hi] = jnp.concatenate([vt[0:HEAD_DIM], ones], axis=0).astype(bf16)
        vt_scr[1, :, lo:hi] = jnp.concatenate([vt[HEAD_DIM:], ones], axis=0).astype(bf16)

    @pl.when(qi == 0)
    def _():
        k = k_ref[...]
        if rope:
            k = _rope(k, ckk_ref[...], skk_ref[...])
        kb_scr[0:seq, :] = k.astype(bf16)
        put_v(v_ref[...], 0, seq)
        if ctx_len:
            kb_scr[seq:seq + ctx_len, :] = ck_ref[...].astype(bf16)
            put_v(cv_ref[...], seq, seq + ctx_len)

    q = q_ref[...]
    if rope:
        q = _rope(q, cq_ref[...], sq_ref[...])
    q = q * (DIFF_DK ** -0.5 * LOG2E)
    n_keys = seq + ctx_len
    ck = max(w for w in range(LANES, DIFF_KEY_CHUNK + 1, LANES) if n_keys % w == 0)
    nck = n_keys // ck
    tq = q.shape[0]
    sub = min(DIFF_Q_SUB, tq)
    jobs = [(qs, g) for qs in range(tq // sub) for g in range(4)]
    lp = lam_ref[...]
    lam = (jnp.exp(jnp.sum(lp[0:1] * lp[1:2], keepdims=True)) - jnp.exp(jnp.sum(lp[2:3] * lp[3:4], keepdims=True))
           + lam_init)

    def finish(qs, res):
        o = jnp.concatenate([res[0] - lam * res[1], res[2] - lam * res[3]], axis=0).T
        o2 = o * o
        ss_a = jnp.sum(jnp.where(first, o2, 0.0), axis=-1, keepdims=True)
        ss_b = jnp.sum(jnp.where(first, 0.0, o2), axis=-1, keepdims=True)
        inv = jnp.where(first, lax.rsqrt(ss_a * (1.0 / HEAD_DIM) + NORM_EPS),
                        lax.rsqrt(ss_b * (1.0 / HEAD_DIM) + NORM_EPS))
        o_ref[qs * sub:(qs + 1) * sub, :] = (o * inv * ng_ref[...] * (1.0 - lam_init)).astype(bf16)

    res = []
    s_prev = m_prev = None
    for j in range(len(jobs) + 1):
        s_cur, mvec, acc = [], None, None
        if j < len(jobs):
            qs, g = jobs[j]
            qm = jnp.where(lane // DIFF_DK == g, q[qs * sub:(qs + 1) * sub], 0.0).astype(bf16)
        for c in range(nck):
            if j < len(jobs):
                s = _dot(qm, kb_scr[c * ck:(c + 1) * ck, :], NT_DIMS)
                s_cur.append(s)
                for k in range(ck // LANES):
                    blk = s[:, k * LANES:(k + 1) * LANES]
                    mvec = blk if mvec is None else jnp.maximum(mvec, blk)
            if j > 0:
                p = jnp.exp2(s_prev[c] - m_prev).astype(bf16)
                part = _dot(vt_scr[jobs[j - 1][1] // 2, :, c * ck:(c + 1) * ck], p, NT_DIMS)
                acc = part if acc is None else acc + part
        if j > 0:
            res.append(acc[0:HEAD_DIM] / acc[HEAD_DIM:HEAD_DIM + 1])
            if len(res) == 4:
                finish(jobs[j - 1][0], res)
                res = []
        if j < len(jobs):
            s_prev, m_prev = s_cur, jnp.max(mvec, axis=-1, keepdims=True)


def diff_mixer(p, lam_p, ng, bsz, seq, lam_init, ctx=None, rope_tabs=None, tq=1024):
    tq = min(tq, seq)
    nq = seq // tq
    npair = MIX_W // LANES
    rope = rope_tabs is not None
    ctx_len = ctx[0].shape[1] if ctx is not None else 0
    assert rope == (ctx is not None)
    q_spec = pl.BlockSpec((None, tq, LANES), lambda b, hp, i: (T_DQ, b * nq + i, hp))
    k_spec = pl.BlockSpec((None, seq, LANES), lambda b, hp, i: (T_DK, b, hp))
    v_spec = pl.BlockSpec((None, seq, LANES), lambda b, hp, i: (T_DV, b, hp))
    const = lambda shape: pl.BlockSpec(shape, lambda b, hp, i: (0,) * len(shape))
    args = [p, p, p]
    specs = [q_spec, k_spec, v_spec]
    if rope:
        c_spec = pl.BlockSpec((None, ctx_len, LANES), lambda b, hp, i: (b, 0, hp))
        tab_q = pl.BlockSpec((tq, LANES), lambda b, hp, i: (i, 0))
        args += [ctx[0], ctx[1], rope_tabs[0], rope_tabs[1], rope_tabs[0], rope_tabs[1]]
        specs += [c_spec, c_spec, tab_q, tab_q, const((seq, LANES)), const((seq, LANES))]
    args += [lam_p, ng]
    specs += [const((8, LANES)), const((1, LANES))]
    return pl.pallas_call(
        functools.partial(_diff_kernel, seq=seq, ctx_len=ctx_len, rope=rope, lam_init=lam_init),
        out_shape=jax.ShapeDtypeStruct((bsz * seq, MIX_W), bf16),
        grid=(bsz, npair, nq),
        in_specs=specs,
        out_specs=pl.BlockSpec((tq, LANES), lambda b, hp, i: (b * nq + i, hp)),
        scratch_shapes=[pltpu.VMEM((seq + ctx_len, LANES), bf16), pltpu.VMEM((2, LANES, seq + ctx_len), bf16)],
        compiler_params=_cparams("parallel", "parallel", "arbitrary"),
        name="diff_mixer",
    )(*args)


def _ssd_kernel(sx_ref, bcr_ref, sxp_ref, bcp_ref, sxn_ref, bcn_ref, cw_ref, cb_ref, z_ref, dt_ref, dtb_ref, alog_ref,
                dexp_ref, ng_ref, exp_ref, s0_ref, o_ref, sfin_ref,
                yf_scr, st_scr, dt_scr, dta_scr, xs_ref, bc_ref, *, nblk, tl):
    ph = pl.program_id(1)
    i = pl.program_id(2)
    fwd = ph == 0
    c = SSD_CHUNK
    nch = tl // c
    rep = NH // SSD_G
    blk = jnp.where(fwd, i, nblk - 1 - i)

    @pl.when(i == 0)
    def _():
        st_scr[...] = _expand_state(s0_ref[...], SSD_N, HEAD_DIM)

    brow = pl.multiple_of(blk * tl, tl)

    @pl.when(fwd)
    def _():
        row = lax.broadcasted_iota(jnp.int32, (tl, 1), 0)
        for raw_ref, prev_ref, next_ref, dst_ref, lo in ((sx_ref, sxp_ref, sxn_ref, xs_ref, 0),
                                                        (bcr_ref, bcp_ref, bcn_ref, bc_ref, MIX_W)):
            x = raw_ref[...]
            w = cw_ref[:, lo:lo + x.shape[1]]
            prev_row = jnp.where(blk == 0, 0.0, prev_ref[7:8, :])
            next_row = jnp.where(blk == nblk - 1, 0.0, next_ref[0:1, :])
            xm = jnp.where(row == 0, prev_row, pltpu.roll(x, 1, 0))
            xp = jnp.where(row == tl - 1, next_row, pltpu.roll(x, tl - 1, 0))
            dst_ref[pl.ds(brow, tl), :] = _silu(xm * w[0:1] + x * w[1:2] + xp * w[2:3]
                                                + cb_ref[:, lo:lo + x.shape[1]])

    dtv = _softplus(dt_ref[...] + dtb_ref[...])
    dtv = jnp.where(fwd, dtv, pltpu.roll(dtv, LANES - NH, 1))
    av = -jnp.exp(alog_ref[...])
    av = jnp.where(fwd, av, pltpu.roll(av, LANES - NH, 1))
    dt_scr[...] = dtv
    dta_scr[...] = dtv * av

    tri = _tri_mask(c, c, jnp.where(fwd, 1, -1))
    trib = jnp.where(tri, 1.0, 0.0).astype(bf16)
    head = lax.broadcasted_iota(jnp.int32, (1, MIX_W), 1) // HEAD_DIM
    bd = (lax.broadcasted_iota(jnp.int32, (MIX_W, MIX_W), 0) // SSD_N
          == lax.broadcasted_iota(jnp.int32, (MIX_W, MIX_W), 1) // HEAD_DIM)
    def chunk(j, carry):
        cj = jnp.where(fwd, j, nch - 1 - j)
        off = pl.multiple_of(cj * c, c)
        srow = pl.multiple_of(brow + off, c)
        dtc = dt_scr[pl.ds(off, c), :]
        cum = _sel_dot(trib, dta_scr[pl.ds(off, c), :])
        cum_t = cum.T
        dt_t = dtc.T
        earg = _sel_dot(cum, exp_ref[...])
        elast = jnp.where(fwd, earg[c - 1:c, :], earg[0:1, :])
        dtx = _sel_dot(dtc, exp_ref[...])
        bt = _per_head_groups(bc_ref[pl.ds(srow, c), 0:LANES])
        ct = _per_head_groups(bc_ref[pl.ds(srow, c), LANES:2 * LANES])
        x = xs_ref[pl.ds(srow, c), :]
        xb = x.astype(bf16)
        st = st_scr[...]
        y = _dot((ct * jnp.exp(earg)).astype(bf16), st.astype(bf16))
        btail = (bt * jnp.exp(elast - earg) * dtx).astype(bf16)
        new = _dot(btail, xb, TN_DIMS)
        st_scr[...] = st * jnp.exp(elast) + jnp.where(bd, new, 0.0)
        btb = bt.astype(bf16)
        cb = [_dot(jnp.where(head == g * rep, ct, 0.0).astype(bf16), btb, NT_DIMS) for g in range(SSD_G)]
        ws = []
        for h in range(NH):
            seg = cum[:, h:h + 1] - cum_t[h:h + 1, :]
            lm = jnp.exp(jnp.where(tri, seg, -jnp.inf))
            ws.append((cb[h // rep] * lm * dt_t[h:h + 1, :]).astype(bf16))
        r = _dot(jnp.concatenate(ws, axis=0), xb)
        for h in range(NH):
            y += jnp.where(head == h, r[h * c:(h + 1) * c], 0.0)
        yf_scr[pl.ds(pl.multiple_of(brow + off, c), c), :] += y
        return carry

    @pl.when(fwd)
    def _():
        yf_scr[pl.ds(brow, tl), :] = jnp.zeros((tl, MIX_W), f32)

    lax.fori_loop(0, nch, chunk, 0, unroll=True)

    @pl.when(ph == 1)
    def _():
        yt = yf_scr[pl.ds(brow, tl), :] + xs_ref[pl.ds(brow, tl), :] * dexp_ref[...]
        t = yt * _silu(z_ref[...])
        ms = jnp.mean(t * t, axis=-1, keepdims=True)
        o_ref[...] = (t * lax.rsqrt(ms + NORM_EPS) * ng_ref[...]).astype(bf16)

    @pl.when(i == nblk - 1)
    def _():
        sfin_ref[...] = _compact_state(st_scr[...])


def ssd_mixer(p, cw, cb, dtb, alog, dexp, ng, expand, s0, bsz, seq, tl=1024):
    tl = min(tl, seq)
    nblk = seq // tl
    n8 = bsz * seq // 8
    per8 = tl // 8
    blk = lambda ph, i: jnp.where(ph == 0, i, nblk - 1 - i)
    rows = lambda b, ph, i: b * nblk + blk(ph, i)
    const = lambda shape: pl.BlockSpec(shape, lambda b, ph, i: (0,) * len(shape))
    state = pl.BlockSpec((None, None, SSD_N, MIX_W), lambda b, ph, i: (b, ph, 0, 0))
    body = lambda tile, w: pl.BlockSpec((None, tl, w), lambda b, ph, i: (tile, rows(b, ph, i), 0))
    crow = lambda b, ph, i: jnp.where(ph == 0, rows(b, ph, i), b * nblk + nblk - 1)
    cbody = lambda tile, w: pl.BlockSpec((None, tl, w), lambda b, ph, i: (tile, crow(b, ph, i), 0))
    prev8 = lambda tile, w: pl.BlockSpec(
        (None, 8, w), lambda b, ph, i: (tile, jnp.maximum(crow(b, ph, i) * per8 - 1, 0), 0))
    next8 = lambda tile, w: pl.BlockSpec(
        (None, 8, w), lambda b, ph, i: (tile, jnp.minimum((crow(b, ph, i) + 1) * per8, n8 - 1), 0))
    bcw = 2 * SSD_G * SSD_N
    return pl.pallas_call(
        functools.partial(_ssd_kernel, nblk=nblk, tl=tl),
        out_shape=(jax.ShapeDtypeStruct((bsz * seq, MIX_W), bf16),
                   jax.ShapeDtypeStruct((bsz, 2, SSD_N, MIX_W), f32)),
        grid=(bsz, 2, nblk),
        in_specs=[cbody(T_SX, PT), cbody(T_MISC, bcw), prev8(T_SX, PT), prev8(T_MISC, bcw),
                  next8(T_SX, PT), next8(T_MISC, bcw), const((3, MIX_W + bcw)), const((1, MIX_W + bcw)),
                  body(T_SZ, PT),
                  pl.BlockSpec((None, tl, LANES), lambda b, ph, i: (T_MISC, rows(b, ph, i), MISC_DT // LANES)),
                  const((1, LANES)), const((1, LANES)), const((1, MIX_W)), const((1, MIX_W)),
                  const((LANES, MIX_W)), state],
        out_specs=(pl.BlockSpec((tl, MIX_W),
                                lambda b, ph, i: (b * nblk + jnp.where(ph == 0, nblk - 1, nblk - 1 - i), 0)),
                   state),
        scratch_shapes=[pltpu.VMEM((seq, MIX_W), f32), pltpu.VMEM((MIX_W, MIX_W), f32),
                        pltpu.VMEM((tl, LANES), f32), pltpu.VMEM((tl, LANES), f32),
                        pltpu.VMEM((seq, MIX_W), f32), pltpu.VMEM((seq, bcw), f32)],
        compiler_params=_cparams("parallel", "arbitrary", "arbitrary"),
        name="ssd_mixer",
    )(p, p, p, p, p, p, cw, cb.reshape(1, -1), p, p, dtb, alog, dexp, ng, expand, s0)


def _hy_filter_kernel(zf_ref, zb_ref, tf_ref, tb_ref, w1_ref, b1_ref, sw_ref, w2_ref, b2_ref,
                      w3f_ref, w3b_ref, b3f_ref, b3b_ref, df_ref, db_ref, skip_ref, r_ref, hf_scr, hb_scr, *, seq):
    first = jnp.logical_and(pl.program_id(0) == 0, pl.program_id(1) == 0)

    @pl.when(first)
    def _():
        for z_ref, h_scr in ((zf_ref, hf_scr), (zb_ref, hb_scr)):
            h = jnp.sin(sw_ref[...] * (_dot(w1_ref[...], z_ref[...], precision=HI) + b1_ref[...]))
            h_scr[...] = jnp.sin(sw_ref[...] * (_dot(w2_ref[...], h, precision=HI) + b2_ref[...]))

    hf = (_dot(w3f_ref[...], hf_scr[...], precision=HI) + b3f_ref[...]) * jnp.exp(-tf_ref[...] * jnp.abs(df_ref[...]))
    hb = (_dot(w3b_ref[...], hb_scr[...], precision=HI) + b3b_ref[...]) * jnp.exp(-tb_ref[...] * jnp.abs(db_ref[...]))
    den = (jnp.sum(jnp.abs(hf), axis=-1, keepdims=True) + jnp.sum(jnp.abs(hb), axis=-1, keepdims=True)) + NORM_EPS
    col = lax.broadcasted_iota(jnp.int32, (1, seq), 1)
    r_ref[:, 0:seq] = jnp.where(col == 0, 0.0, hb / den)
    r_ref[:, seq:2 * seq] = hf / den + jnp.where(col == 0, skip_ref[...], 0.0)


def hy_filters(fp, seq, cbf=128):
    t = jnp.arange(seq, dtype=f32) / seq
    t_rev = jnp.concatenate([t[:1], t[:0:-1]])

    def feats(tt):
        tc = tt[:, None]
        ang = 2.0 * math.pi * tc * jnp.arange(1, HY_BANDS + 1, dtype=f32)
        z = jnp.concatenate([tc, jnp.cos(ang), jnp.sin(ang)], axis=-1)
        return jnp.pad(z, ((0, 0), (0, HY_EMB_PAD - HY_EMB))).T

    ch = MIX_W
    nb = ch // cbf
    col = lambda a: a.reshape(-1, 1)
    w3t = fp['w3'].T.reshape(2, 2, ch, HY_HID)
    b3 = fp['b3'].reshape(2, 2, ch, 1)
    dec = fp['decay'].reshape(2, 2, ch, 1)
    const = lambda shape: pl.BlockSpec(shape, lambda o, j: (0,) * len(shape))
    sel = lambda d, last: pl.BlockSpec((None, None, cbf, last), lambda o, j: (o, d, j, 0))
    return pl.pallas_call(
        functools.partial(_hy_filter_kernel, seq=seq),
        out_shape=jax.ShapeDtypeStruct((2, ch, 2 * seq), f32),
        grid=(2, nb),
        in_specs=[const((HY_EMB_PAD, seq)), const((HY_EMB_PAD, seq)), const((1, seq)), const((1, seq)),
                  const((HY_HID, HY_EMB_PAD)), const((HY_HID, 1)), const((HY_HID, 1)),
                  const((HY_HID, HY_HID)), const((HY_HID, 1)),
                  sel(0, HY_HID), sel(1, HY_HID), sel(0, 1), sel(1, 1), sel(0, 1), sel(1, 1),
                  pl.BlockSpec((None, cbf, 1), lambda o, j: (o, j, 0))],
        out_specs=pl.BlockSpec((None, cbf, 2 * seq), lambda o, j: (o, j, 0)),
        scratch_shapes=[pltpu.VMEM((HY_HID, seq), f32), pltpu.VMEM((HY_HID, seq), f32)],
        compiler_params=_cparams("arbitrary", "arbitrary"),
        name="hy_filters",
    )(feats(t), feats(t_rev), t.reshape(1, seq), t_rev.reshape(1, seq),
      jnp.pad(fp['w1'], ((0, HY_EMB_PAD - HY_EMB), (0, 0))).T, col(fp['b1']), col(fp['sin_w']),
      fp['w2'].T, col(fp['b2']), w3t, w3t, b3, b3, dec, dec, fp['skip'].reshape(2, ch, 1))


def _hyena_kernel(cw_ref, cb_ref, hv_ref, h1_ref, h2_ref, r_ref, o_ref, acc_scr, *, bsz, n_i, cb_n):
    tb = HY_TB
    gr = tb // 2
    sub = tb // LANES
    seq = n_i * tb
    rows = n_i * bsz
    base = pl.program_id(0) * cb_n
    n_ch = MIX_W
    lane = lax.broadcasted_iota(jnp.int32, (1, tb), 1)
    zblk = jnp.zeros((bsz, tb), f32)

    def load(ref, ch):
        return jnp.concatenate(
            [jnp.concatenate([ref[ch, pl.ds(sub * ib + k, bsz, stride=sub * n_i), :] for k in range(sub)], axis=1)
             for ib in range(n_i)], axis=0)

    def short_conv(a, stream, ch):
        idx = stream * n_ch + base + ch
        prev = jnp.concatenate([zblk, a[:rows - bsz]], axis=0) if n_i > 1 else zblk
        nxt = jnp.concatenate([a[bsz:], zblk], axis=0) if n_i > 1 else zblk
        am = pltpu.roll(jnp.where(lane == tb - 1, prev, a), 1, 1)
        ap = pltpu.roll(jnp.where(lane == 0, nxt, a), tb - 1, 1)
        return (am * cw_ref[idx] + a * cw_ref[3 * n_ch + idx] + ap * cw_ref[6 * n_ch + idx]) + cb_ref[idx]

    def long_conv(u, order, ch):
        rrow = r_ref[order, pl.ds(ch, 1), :]
        bits = lambda a: lax.bitcast_convert_type(a.astype(bf16).astype(f32), jnp.int32)
        word = (bits(pltpu.roll(rrow, 1, 1)) & jnp.int32(-65536)) | lax.shift_right_logical(bits(rrow), 16)
        g = pltpu.bitcast(
            pltpu.roll(jnp.broadcast_to(word, (gr // 2, 2 * seq)), 0, 1, stride=2, stride_axis=0), bf16)
        for d in [0] + [e for e in range(-(n_i - 1), n_i) if e != 0]:
            n = n_i - abs(d)
            src = max(0, -d) * bsz
            dst = max(0, d) * bsz
            c0 = seq + d * tb
            m = jnp.concatenate([g[:, c0:c0 + tb], g[:, c0 - gr:c0 - gr + tb]], axis=0)
            part = _dot(u[src:src + n * bsz].astype(bf16), m)
            if d == 0:
                acc_scr[order] = part
            else:
                acc_scr[order, dst:dst + n * bsz, :] += part
        return acc_scr[order]

    def body(ch, carry):
        hv = short_conv(load(hv_ref, ch), 0, ch)
        hx1 = short_conv(load(h1_ref, ch), 1, ch)
        hx2 = short_conv(load(h2_ref, ch), 2, ch)
        zz = hx1 * long_conv(hv, 0, ch)
        out = hx2 * long_conv(zz, 1, ch)
        for ib in range(n_i):
            for k in range(sub):
                o_ref[ch, pl.ds(sub * ib + k, bsz, stride=sub * n_i), :] = (
                    out[ib * bsz:(ib + 1) * bsz, k * LANES:(k + 1) * LANES])
        return carry

    lax.fori_loop(0, cb_n, body, 0, unroll=4)


def hyena_mixer(hy_t, r, cw, cb, bsz, seq, cb_n=8):
    n_i = seq // HY_TB
    ch = MIX_W
    nblk = ch // cb_n
    n_rows = bsz * seq // LANES
    x3 = hy_t.reshape(3 * ch, n_rows, LANES)
    stream = lambda s: pl.BlockSpec((cb_n, n_rows, LANES), lambda j: (s * nblk + j, 0, 0))
    smem = pl.BlockSpec(memory_space=pltpu.SMEM)
    out = pl.pallas_call(
        functools.partial(_hyena_kernel, bsz=bsz, n_i=n_i, cb_n=cb_n),
        out_shape=jax.ShapeDtypeStruct((ch, n_rows, LANES), f32),
        grid=(nblk,),
        in_specs=[smem, smem, stream(0), stream(1), stream(2),
                  pl.BlockSpec((2, cb_n, 2 * seq), lambda j: (0, j, 0))],
        out_specs=pl.BlockSpec((cb_n, n_rows, LANES), lambda j: (j, 0, 0)),
        scratch_shapes=[pltpu.VMEM((2, bsz * n_i, HY_TB), f32)],
        compiler_params=_cparams("parallel"),
        name="hyena_mixer",
    )(cw.reshape(-1), cb, x3, x3, x3, r)
    return out.reshape(ch, bsz * seq)


_IN_SIZES = (NH * GLA_DK, NH * GLA_DK, MIX_W, MIX_W, 2 * GLA_RANK, MIX_W, MIX_W, MIX_W,
             MIX_W, MIX_W + 2 * SSD_G * SSD_N, 2 * NH, 3 * MIX_W)
_IN_OFFS = np.concatenate([[0], np.cumsum(_IN_SIZES)]).tolist()


def _prep_w_in(w_in):
    (gq, gk, gv, gg, gr, dq, dk, dv, sz, sxbc, sdt, hy) = [w_in[:, _IN_OFFS[i]:_IN_OFFS[i + 1]]
                                                          for i in range(len(_IN_SIZES))]
    d = w_in.shape[0]
    zeros = lambda n: jnp.zeros((d, n), w_in.dtype)
    misc = jnp.concatenate([sxbc[:, MIX_W:], gr, zeros(LANES - 2 * GLA_RANK), sdt, zeros(LANES - 2 * NH)], axis=1)
    w_tok = jnp.concatenate([gq, gk, gv, gg, dq, dk, dv, sz, sxbc[:, :MIX_W], misc], axis=1)
    w_tok = w_tok.astype(bf16).reshape(d, N_PT, PT).transpose(1, 0, 2)
    return w_tok, hy.T.astype(bf16)


def _prep_gate_w(gate_w):
    w = jnp.zeros((2, LANES, gate_w.shape[2]), f32)
    for d in range(2):
        w = w.at[d, d * GLA_RANK:(d + 1) * GLA_RANK, :].set(gate_w[d])
    hi = w.astype(bf16)
    return hi, (w - hi.astype(f32)).astype(bf16)


def _rope_tables(seq):
    rows = seq // GRID_W
    r, col = jnp.meshgrid(jnp.arange(rows), jnp.arange(GRID_W), indexing='ij')
    r = r.reshape(-1).astype(f32)
    col = col.reshape(-1).astype(f32)
    nf = DIFF_DK // 4
    inv = ROPE_BASE ** (-jnp.arange(nf, dtype=f32) / nf)
    ang = jnp.concatenate([r[:, None] * inv, col[:, None] * inv], axis=-1)
    cos, sin = jnp.cos(ang), jnp.sin(ang)
    reps = LANES // DIFF_DK
    return jnp.tile(jnp.concatenate([cos, cos], axis=-1), (1, reps)), jnp.tile(jnp.concatenate([-sin, sin], axis=-1), (1, reps))


def _const_tables():
    hsum = np.kron(np.eye(NH, dtype=np.float32), np.ones((HEAD_DIM, HEAD_DIM), np.float32))
    expand = np.zeros((LANES, MIX_W), np.float32)
    for h in range(NH):
        expand[h, h * HEAD_DIM:(h + 1) * HEAD_DIM] = 1.0
    return tuple(jnp.asarray(a, dtype=bf16) for a in (hsum, expand))


def _pack_state(s):
    bsz, _, h, a, b = s.shape
    return s.transpose(0, 1, 4, 2, 3).reshape(bsz, 2, b, h * a)


def _unpack_state(st, a):
    bsz, _, b, _ = st.shape
    return st.reshape(bsz, 2, b, NH, a).transpose(0, 1, 3, 4, 2)


def _layer(x, mod, rows_per_mod, lw, layer, consts, bsz, seq, lam_init, r_filt, ctx, rope_tabs, final_g,
           final_norm):
    hsum, expand = consts
    p, hy_t = in_proj(x, mod, lw['norm1_g'], lw['w_tok'], lw['w_hyT'], rows_per_mod)
    if ctx is None:
        gla_s0 = jnp.zeros((bsz, 2, HEAD_DIM, NH * GLA_DK), f32)
        ssd_s0 = jnp.zeros((bsz, 2, SSD_N, MIX_W), f32)
        dctx = None
    else:
        ctx_k, ctx_v, gla_s, ssd_s = ctx
        gla_s0 = _pack_state(gla_s)
        ssd_s0 = _pack_state(ssd_s)
        dctx = (ctx_k.reshape(bsz, -1, MIX_W), ctx_v.reshape(bsz, -1, MIX_W))
    o_gla, gla_fin = gla_mixer(p, lw['gla_gw_hi'], lw['gla_gw_lo'], lw['gla_gb'], lw['gla_ng'], hsum, gla_s0,
                               bsz, seq)
    o_diff = diff_mixer(p, lw['lam_p'], lw['diff_ng'], bsz, seq, lam_init, dctx, rope_tabs)
    o_ssd, ssd_fin = ssd_mixer(p, lw['ssd_conv_w'], lw['ssd_conv_b'], lw['ssd_dtb'], lw['ssd_alog'], lw['ssd_dexp'], lw['ssd_ng'],
                               expand, ssd_s0, bsz, seq)
    o_hy = hyena_mixer(hy_t, r_filt, lw['hy_cw'], lw['hy_cb'], bsz, seq)
    x = out_proj(x, mod, o_gla, o_diff, o_ssd, o_hy, lw['w_out'], layer, rows_per_mod)
    x = mlp(x, mod, lw['norm2_g'], lw['mlp_w1'], lw['mlp_w2'], layer, final_g, rows_per_mod, final_norm)
    return x, p, gla_fin, ssd_fin


def kernel(x_prompt, x_sample, cache_diff_k, cache_diff_v, state_gla, state_ssd, c, c_ctx, ada_w, ada_b, norm1_g, norm2_g, w_in, w_out, gla_gate_w, gla_gate_b, gla_norm_g, diff_lambda, diff_norm_g, ssd_conv_w, ssd_conv_b, ssd_dt_bias, ssd_a_log, ssd_d, ssd_norm_g, hy_conv_w, hy_conv_b, hy_f_w1, hy_f_b1, hy_f_w2, hy_f_b2, hy_f_w3, hy_f_b3, hy_sin_w, hy_decay, hy_skip, mlp_w1, mlp_w2, final_g):
    bp, lp, d = x_prompt.shape
    bs, ls, _ = x_sample.shape
    depth = w_in.shape[0]
    consts = _const_tables()
    rope_tabs = _rope_tables(ls)
    n_c = 16
    cvec = jnp.concatenate([c_ctx[None], c, jnp.zeros((n_c - 1 - bs, d), f32)], axis=0)
    hp = x_prompt.reshape(bp * lp, d)
    hs = x_sample.reshape(bs * ls, d)
    ks_, vs_, gs_, ss_ = [], [], [], []
    w_out_b, mlp_w1_b, mlp_w2_b = w_out.astype(bf16), mlp_w1.astype(bf16), mlp_w2.astype(bf16)
    for l in range(depth):
        w_tok, w_hyT = _prep_w_in(w_in[l])
        pad_l = lambda a: jnp.pad(a, (0, LANES - a.shape[0])).reshape(1, LANES)
        gw_hi, gw_lo = _prep_gate_w(gla_gate_w[l])
        lw = dict(
            norm1_g=norm1_g[l], norm2_g=norm2_g[l], w_tok=w_tok, w_hyT=w_hyT, w_out=w_out_b,
            mlp_w1=mlp_w1_b, mlp_w2=mlp_w2_b,
            gla_gw_hi=gw_hi, gla_gw_lo=gw_lo, gla_gb=gla_gate_b[l].reshape(2, 1, -1), gla_ng=jnp.tile(gla_norm_g[l], NH).reshape(1, MIX_W),
            lam_p=jnp.pad(diff_lambda[l], ((0, 4), (0, LANES - DIFF_DK))),
            diff_ng=jnp.tile(diff_norm_g[l], LANES // HEAD_DIM).reshape(1, LANES),
            ssd_conv_w=ssd_conv_w[l], ssd_conv_b=ssd_conv_b[l],
            ssd_dtb=pad_l(ssd_dt_bias[l].reshape(-1)), ssd_alog=pad_l(ssd_a_log[l].reshape(-1)),
            ssd_dexp=jnp.repeat(ssd_d[l], HEAD_DIM).reshape(1, MIX_W), ssd_ng=ssd_norm_g[l].reshape(1, MIX_W),
            hy_cw=hy_conv_w[l], hy_cb=hy_conv_b[l],
        )
        fp = dict(w1=hy_f_w1[l], b1=hy_f_b1[l], w2=hy_f_w2[l], b2=hy_f_b2[l], w3=hy_f_w3[l], b3=hy_f_b3[l],
                  sin_w=hy_sin_w[l], decay=hy_decay[l], skip=hy_skip[l])
        mod = ada_mod(cvec, ada_w, ada_b, l).reshape(n_c, 6, d)
        lam_init = 0.8 - 0.6 * math.exp(-0.3 * l)
        last = l == depth - 1
        hp, p_p, g_l, s_l = _layer(hp, mod[0:1], bp * lp, lw, l, consts, bp, lp, lam_init, hy_filters(fp, lp),
                                   None, None, final_g, last)
        ks_.append(p_p[T_DK].reshape(bp, lp, NH, 2 * DIFF_DK))
        vs_.append(p_p[T_DV].reshape(bp, lp, NH, HEAD_DIM))
        gs_.append(_unpack_state(g_l, GLA_DK))
        ss_.append(_unpack_state(s_l, HEAD_DIM))
        hs, _, _, _ = _layer(hs, mod[1:1 + bs], ls, lw, l, consts, bs, ls, lam_init, hy_filters(fp, ls),
                             (cache_diff_k[:, l], cache_diff_v[:, l], state_gla[:, l], state_ssd[:, l]),
                             rope_tabs, final_g, last)
    return (hp.reshape(bp, lp, d), hs.reshape(bs, ls, d), jnp.stack(ks_, axis=1), jnp.stack(vs_, axis=1),
            jnp.stack(gs_, axis=1), jnp.stack(ss_, axis=1))
```

```python
import functools
import math

import numpy as np
import jax
import jax.numpy as jnp
from jax import lax
from jax.experimental import pallas as pl
from jax.experimental.pallas import tpu as pltpu

f32 = jnp.float32
bf16 = jnp.bfloat16
HI = lax.Precision.HIGHEST

LANES = 128
VMEM_LIMIT = 56 * 2**20

D_MODEL = 2048
GRID_W = 64
MIX_W = D_MODEL // 4
HEAD_DIM = 64
NH = MIX_W // HEAD_DIM
D_FF = 4 * D_MODEL
NORM_EPS = 1e-6
LOG2E = 1.0 / math.log(2.0)
GLA_DK = HEAD_DIM // 2
GLA_RANK = 16
GLA_TAU = 16.0
GLA_CHUNK = 128
DIFF_DK = HEAD_DIM // 2
ROPE_BASE = 10000.0
SSD_N = 64
SSD_G = 2
SSD_CHUNK = 128
HY_BANDS = 16
HY_EMB = 2 * HY_BANDS + 1
HY_EMB_PAD = 40
HY_HID = 64
DIFF_KEY_CHUNK = 512
DIFF_Q_SUB = 512
MLP_TF = 512
HY_TB = 256

PT = 512
(T_GQK, T_GV, T_GG, T_DQ, T_DK, T_DV, T_SZ, T_SX, T_MISC) = range(9)
N_PT = 9
MISC_GR = 256
MISC_DT = 384
SH1, SC1, G1, SH2, SC2, G2 = range(6)

NT_DIMS = (((1,), (1,)), ((), ()))
TN_DIMS = (((0,), (0,)), ((), ()))


def _cparams(*sem):
    return pltpu.CompilerParams(dimension_semantics=sem, vmem_limit_bytes=VMEM_LIMIT)


def _sigmoid(x):
    return 1.0 / (1.0 + jnp.exp(-x))


def _silu(x):
    return x * _sigmoid(x)


def _softplus(x):
    return jnp.maximum(x, 0.0) + jnp.log1p(jnp.exp(-jnp.abs(x)))


def _log_sigmoid(x):
    return jnp.minimum(x, 0.0) - jnp.log1p(jnp.exp(-jnp.abs(x)))


def _tri_mask(rows, c, sgn):
    r_i = lax.broadcasted_iota(jnp.int32, (rows, c), 0) & (c - 1)
    c_i = lax.broadcasted_iota(jnp.int32, (rows, c), 1)
    return (r_i - c_i) * sgn >= 0


def _expand_state(s, row_w, col_w):
    rows, cols = NH * row_w, NH * col_w
    bd = (lax.broadcasted_iota(jnp.int32, (rows, cols), 0) // row_w
          == lax.broadcasted_iota(jnp.int32, (rows, cols), 1) // col_w)
    return jnp.where(bd, jnp.concatenate([s] * NH, axis=0), 0.0)


def _compact_state(st):
    row_w = st.shape[0] // NH
    out = st[0:row_w]
    for h in range(1, NH):
        out += st[h * row_w:(h + 1) * row_w]
    return out


def _split(x, n):
    parts = []
    for _ in range(n - 1):
        hi = x.astype(bf16)
        parts.append(hi)
        x = x - hi.astype(f32)
    parts.append(x.astype(bf16))
    return parts


def _sel_dot(a, b, dims=None, terms=3):
    if a.dtype == bf16:
        prods = [_dot(a, p, dims) for p in _split(b, terms)]
    else:
        prods = [_dot(p, b, dims) for p in _split(a, terms)]
    return functools.reduce(lambda x, y: x + y, prods)


def _per_head_groups(g2):
    assert SSD_G == 2 and SSD_N == HEAD_DIM and LANES == 2 * SSD_N
    first = lax.broadcasted_iota(jnp.int32, (1, LANES), 1) < SSD_N
    swapped = pltpu.roll(g2, SSD_N, 1)
    g0 = jnp.where(first, g2, swapped)
    g1 = jnp.where(first, swapped, g2)
    rep = NH * HEAD_DIM // (SSD_G * LANES)
    return jnp.concatenate([g0] * rep + [g1] * rep, axis=1)


def _dot(a, b, dims=None, precision=None):
    if dims is None:
        return jnp.dot(a, b, preferred_element_type=f32, precision=precision)
    return lax.dot_general(a, b, dims, preferred_element_type=f32, precision=precision)


def _ada_kernel(c_ref, w_ref, b_ref, o_ref):
    s = _silu(c_ref[...]).astype(bf16)
    o_ref[...] = _dot(s, w_ref[...].astype(bf16)) + b_ref[...]


def ada_mod(cvec, w, b, layer, tn=1024):
    m, d = cvec.shape
    n = w.shape[2]
    return pl.pallas_call(
        _ada_kernel,
        out_shape=jax.ShapeDtypeStruct((m, n), f32),
        grid=(n // tn,),
        in_specs=[pl.BlockSpec((m, d), lambda j: (0, 0)),
                  pl.BlockSpec((None, d, tn), lambda j: (layer, 0, j)),
                  pl.BlockSpec((None, 1, tn), lambda j: (layer, 0, j))],
        out_specs=pl.BlockSpec((m, tn), lambda j: (0, j)),
        compiler_params=_cparams("arbitrary"),
        name="ada_mod",
    )(cvec, w, b.reshape(b.shape[0], 1, n))


def _norm_mod(x, g, sc, sh):
    ms = jnp.mean(x * x, axis=-1, keepdims=True)
    return (x * lax.rsqrt(ms + NORM_EPS) * g) * (1.0 + sc) + sh


def _inproj_kernel(x_ref, mod_ref, g_ref, wt_ref, wh_ref, p_ref, h_ref, u_scr):
    j = pl.program_id(1)

    @pl.when(j == 0)
    def _():
        u = _norm_mod(x_ref[...], g_ref[...], mod_ref[SC1:SC1 + 1, :], mod_ref[SH1:SH1 + 1, :])
        u_scr[...] = u.astype(bf16)

    @pl.when(j < N_PT)
    def _():
        p_ref[...] = _dot(u_scr[...], wt_ref[...])

    @pl.when(j >= N_PT)
    def _():
        h_ref[...] = _dot(wh_ref[...], u_scr[...], NT_DIMS)


def in_proj(x, mod, g, w_tok, w_hyT, rows_per_mod, tm=1024):
    t, d = x.shape
    assert t % tm == 0 and rows_per_mod % tm == 0, (t, rows_per_mod, tm)
    n_hy = w_hyT.shape[0] // PT
    return pl.pallas_call(
        _inproj_kernel,
        out_shape=(jax.ShapeDtypeStruct((N_PT, t, PT), f32), jax.ShapeDtypeStruct((n_hy * PT, t), f32)),
        grid=(t // tm, N_PT + n_hy),
        in_specs=[pl.BlockSpec((tm, d), lambda i, j: (i, 0)),
                  pl.BlockSpec((None, 6, d), lambda i, j: ((i * tm) // rows_per_mod, 0, 0)),
                  pl.BlockSpec((1, d), lambda i, j: (0, 0)),
                  pl.BlockSpec((None, d, PT), lambda i, j: (jnp.minimum(j, N_PT - 1), 0, 0)),
                  pl.BlockSpec((PT, d), lambda i, j: (jnp.maximum(j - N_PT, 0), 0))],
        out_specs=(pl.BlockSpec((None, tm, PT), lambda i, j: (jnp.minimum(j, N_PT - 1), i, 0)),
                   pl.BlockSpec((PT, tm), lambda i, j: (jnp.maximum(j - N_PT, 0), i))),
        scratch_shapes=[pltpu.VMEM((tm, d), bf16)],
        compiler_params=_cparams("parallel", "arbitrary"),
        name="in_proj",
    )(x, mod, g.reshape(1, d), w_tok, w_hyT)


def _outproj_kernel(x_ref, mod_ref, a_ref, b_ref, c_ref, dT_ref, w_ref, o_ref):
    acc = _dot(a_ref[...], w_ref[0:MIX_W, :])
    acc += _dot(b_ref[...], w_ref[MIX_W:2 * MIX_W, :])
    acc += _dot(c_ref[...], w_ref[2 * MIX_W:3 * MIX_W, :])
    acc += _dot(dT_ref[...].astype(bf16), w_ref[3 * MIX_W:4 * MIX_W, :], TN_DIMS)
    o_ref[...] = x_ref[...] + mod_ref[G1:G1 + 1, :] * acc


def out_proj(x, mod, o_a, o_b, o_c, o_dT, w_out, layer, rows_per_mod, tm=512):
    t, d = x.shape
    assert t % tm == 0 and rows_per_mod % tm == 0, (t, rows_per_mod, tm)
    tok = lambda i: (i, 0)
    return pl.pallas_call(
        _outproj_kernel,
        out_shape=jax.ShapeDtypeStruct((t, d), f32),
        grid=(t // tm,),
        in_specs=[pl.BlockSpec((tm, d), tok),
                  pl.BlockSpec((None, 6, d), lambda i: ((i * tm) // rows_per_mod, 0, 0)),
                  pl.BlockSpec((tm, MIX_W), tok), pl.BlockSpec((tm, MIX_W), tok), pl.BlockSpec((tm, MIX_W), tok),
                  pl.BlockSpec((MIX_W, tm), lambda i: (0, i)),
                  pl.BlockSpec((None, d, d), lambda i: (layer, 0, 0))],
        out_specs=pl.BlockSpec((tm, d), tok),
        compiler_params=_cparams("parallel"),
        name="out_proj",
    )(x, mod, o_a, o_b, o_c, o_dT, w_out)


def _mlp_kernel(x_ref, mod_ref, g_ref, w1_ref, w2_ref, fg_ref, o_ref, u_scr, *, final_norm):
    j = pl.program_id(1)

    @pl.when(j == 0)
    def _():
        u = _norm_mod(x_ref[...], g_ref[...], mod_ref[SC2:SC2 + 1, :], mod_ref[SH2:SH2 + 1, :])
        u_scr[...] = u.astype(bf16)
        o_ref[...] = jnp.zeros_like(o_ref)

    h = jnp.maximum(_dot(u_scr[...], w1_ref[...]), 0.0)
    hb = (h * h).astype(bf16)
    d = o_ref.shape[1]
    for n in range(d // PT):
        o_ref[:, n * PT:(n + 1) * PT] += _dot(hb, w2_ref[:, n * PT:(n + 1) * PT])

    @pl.when(j == pl.num_programs(1) - 1)
    def _():
        y = x_ref[...] + mod_ref[G2:G2 + 1, :] * o_ref[...]
        if final_norm:
            ms = jnp.mean(y * y, axis=-1, keepdims=True)
            y = y * lax.rsqrt(ms + NORM_EPS) * fg_ref[...]
        o_ref[...] = y


def mlp(x, mod, g, w1, w2, layer, final_g, rows_per_mod, final_norm, tm=1024, tf=MLP_TF):
    t, d = x.shape
    assert t % tm == 0 and rows_per_mod % tm == 0, (t, rows_per_mod, tm)
    n_f = w1.shape[2] // tf
    return pl.pallas_call(
        functools.partial(_mlp_kernel, final_norm=final_norm),
        out_shape=jax.ShapeDtypeStruct((t, d), f32),
        grid=(t // tm, n_f),
        in_specs=[pl.BlockSpec((tm, d), lambda i, j: (i, 0)),
                  pl.BlockSpec((None, 6, d), lambda i, j: ((i * tm) // rows_per_mod, 0, 0)),
                  pl.BlockSpec((1, d), lambda i, j: (0, 0)),
                  pl.BlockSpec((None, d, tf), lambda i, j: (layer, 0, j)),
                  pl.BlockSpec((None, tf, d), lambda i, j: (layer, j, 0)),
                  pl.BlockSpec((1, d), lambda i, j: (0, 0))],
        out_specs=pl.BlockSpec((tm, d), lambda i, j: (i, 0)),
        scratch_shapes=[pltpu.VMEM((tm, d), bf16)],
        compiler_params=_cparams("parallel", "arbitrary"),
        name="mlp",
    )(x, mod, g.reshape(1, d), w1, w2, final_g.reshape(1, d))


def _gla_kernel(qk_ref, v_ref, gg_ref, misc_ref, gwh_ref, gwl_ref, gb_ref, ng_ref, hsum_ref, s0_ref,
                o_ref, sfin_ref, of_scr, st_scr, la_scr, *, nblk, tl):
    ph = pl.program_id(1)
    i = pl.program_id(2)
    fwd = ph == 0
    c = GLA_CHUNK
    nch = tl // c
    qkw = NH * GLA_DK

    @pl.when(i == 0)
    def _():
        st_scr[...] = _expand_state(s0_ref[...], HEAD_DIM, GLA_DK)

    gr_hi, gr_lo = _split(misc_ref[:, MISC_GR:MISC_GR + LANES], 2)
    logits = (_dot(gr_hi, gwh_ref[...]) + _dot(gr_lo, gwh_ref[...]) + _dot(gr_hi, gwl_ref[...])) + gb_ref[...]
    la_scr[...] = _log_sigmoid(logits) * (1.0 / GLA_TAU)

    sgn = jnp.where(fwd, 1, -1)
    trib = jnp.where(_tri_mask(c, c, sgn), 1.0, 0.0).astype(bf16)
    tri8 = _tri_mask(NH * c, c, sgn)
    qk_head = lax.broadcasted_iota(jnp.int32, (1, qkw), 1) // GLA_DK
    v_head = lax.broadcasted_iota(jnp.int32, (1, MIX_W), 1) // HEAD_DIM
    bd = (lax.broadcasted_iota(jnp.int32, (MIX_W, qkw), 0) // HEAD_DIM
          == lax.broadcasted_iota(jnp.int32, (MIX_W, qkw), 1) // GLA_DK)
    brow = pl.multiple_of(jnp.where(fwd, i, nblk - 1 - i) * tl, tl)

    def chunk(j, carry):
        cj = jnp.where(fwd, j, nch - 1 - j)
        off = pl.multiple_of(cj * c, c)
        q = qk_ref[pl.ds(off, c), 0:qkw] * (GLA_DK ** -0.5)
        k = qk_ref[pl.ds(off, c), qkw:2 * qkw]
        vb = v_ref[pl.ds(off, c), :].astype(bf16)
        b = _sel_dot(trib, la_scr[pl.ds(off, c), :])
        btot = jnp.where(fwd, b[c - 1:c, :], b[0:1, :])
        bmid = b[c // 2 - 1:c // 2, :]
        qd = q * jnp.exp(b)
        qi = q * jnp.exp(b - bmid)
        kd = (k * jnp.exp(bmid - b)).astype(bf16)
        kt = (k * jnp.exp(btot - b)).astype(bf16)
        qst = jnp.concatenate([jnp.where(qk_head == h, qi, 0.0) for h in range(NH)], axis=0).astype(bf16)
        att = jnp.where(tri8, _dot(qst, kd, NT_DIMS), 0.0)
        r = _dot(att.astype(bf16), vb)
        o = jnp.where(v_head == 0, r[0:c], 0.0)
        for h in range(1, NH):
            o += jnp.where(v_head == h, r[h * c:(h + 1) * c], 0.0)
        st = st_scr[...]
        o += _dot(qd.astype(bf16), st.astype(bf16), NT_DIMS)
        kv = _dot(vb, kt, TN_DIMS)
        st_scr[...] = st * jnp.exp(btot) + jnp.where(bd, kv, 0.0)
        of_scr[pl.ds(pl.multiple_of(brow + off, c), c), :] += o
        return carry

    @pl.when(fwd)
    def _():
        of_scr[pl.ds(brow, tl), :] = jnp.zeros((tl, MIX_W), f32)

    lax.fori_loop(0, nch, chunk, 0, unroll=True)

    @pl.when(ph == 1)
    def _():
        ot = of_scr[pl.ds(brow, tl), :]
        ss = _sel_dot(ot * ot, hsum_ref[...], terms=2) * (1.0 / HEAD_DIM)
        o_ref[...] = (ot * lax.rsqrt(ss + NORM_EPS) * ng_ref[...] * _silu(gg_ref[...])).astype(bf16)

    @pl.when(i == nblk - 1)
    def _():
        sfin_ref[...] = _compact_state(st_scr[...])


def gla_mixer(p, gw_hi, gw_lo, gb, ng, hsum, s0, bsz, seq, tl=1024):
    tl = min(tl, seq)
    nblk = seq // tl
    qkw = NH * GLA_DK
    blk = lambda ph, i: jnp.where(ph == 0, i, nblk - 1 - i)
    tile = lambda col: pl.BlockSpec((None, tl, PT), lambda b, ph, i: (col, b * nblk + blk(ph, i), 0))
    const = lambda shape: pl.BlockSpec(shape, lambda b, ph, i: (0,) * len(shape))
    per_dir = lambda rows: pl.BlockSpec((None, rows, qkw), lambda b, ph, i: (ph, 0, 0))
    state = pl.BlockSpec((None, None, HEAD_DIM, qkw), lambda b, ph, i: (b, ph, 0, 0))
    return pl.pallas_call(
        functools.partial(_gla_kernel, nblk=nblk, tl=tl),
        out_shape=(jax.ShapeDtypeStruct((bsz * seq, MIX_W), bf16),
                   jax.ShapeDtypeStruct((bsz, 2, HEAD_DIM, qkw), f32)),
        grid=(bsz, 2, nblk),
        in_specs=[tile(T_GQK), tile(T_GV), tile(T_GG), tile(T_MISC),
                  per_dir(LANES), per_dir(LANES), per_dir(1), const((1, MIX_W)), const((MIX_W, MIX_W)), state],
        out_specs=(pl.BlockSpec((tl, MIX_W),
                                lambda b, ph, i: (b * nblk + jnp.where(ph == 0, nblk - 1, nblk - 1 - i), 0)),
                   state),
        scratch_shapes=[pltpu.VMEM((seq, MIX_W), f32), pltpu.VMEM((MIX_W, qkw), f32), pltpu.VMEM((tl, qkw), f32)],
        compiler_params=_cparams("parallel", "arbitrary", "arbitrary"),
        name="gla_mixer",
    )(p, p, p, p, gw_hi, gw_lo, gb, ng, hsum, s0)


def _rope(x, cos, sin_signed):
    lane = lax.broadcasted_iota(jnp.int32, (1, LANES), 1)
    half = DIFF_DK // 2
    rot = jnp.where((lane % DIFF_DK) < half, pltpu.roll(x, LANES - half, 1), pltpu.roll(x, half, 1))
    return x * cos + rot * sin_signed


def _diff_kernel(*refs, seq, ctx_len, rope, lam_init):
    if rope:
        (q_ref, k_ref, v_ref, ck_ref, cv_ref, cq_ref, sq_ref, ckk_ref, skk_ref, lam_ref, ng_ref,
         o_ref, kb_scr, vt_scr) = refs
    else:
        q_ref, k_ref, v_ref, lam_ref, ng_ref, o_ref, kb_scr, vt_scr = refs
    qi = pl.program_id(2)
    lane = lax.broadcasted_iota(jnp.int32, (1, LANES), 1)
    first = lane < HEAD_DIM

    def put_v(v, lo, hi):
        vt = v.T
        ones = jnp.ones((HEAD_DIM, hi - lo), f32)
        vt_scr[0, :, lo:hi] = jnp.concatenate([vt[0:HEAD_DIM], ones], axis=0).astype(bf16)
        vt_scr[1, :, lo:hi] = jnp.concatenate([vt[HEAD_DIM:], ones], axis=0).astype(bf16)

    @pl.when(qi == 0)
    def _():
        k = k_ref[...]
        if rope:
            k = _rope(k, ckk_ref[...], skk_ref[...])
        kb_scr[0:seq, :] = k.astype(bf16)
        put_v(v_ref[...], 0, seq)
        if ctx_len:
            kb_scr[seq:seq + ctx_len, :] = ck_ref[...].astype(bf16)
            put_v(cv_ref[...], seq, seq + ctx_len)

    q = q_ref[...]
    if rope:
        q = _rope(q, cq_ref[...], sq_ref[...])
    q = q * (DIFF_DK ** -0.5 * LOG2E)
    n_keys = seq + ctx_len
    ck = max(w for w in range(LANES, DIFF_KEY_CHUNK + 1, LANES) if n_keys % w == 0)
    nck = n_keys // ck
    tq = q.shape[0]
    sub = min(DIFF_Q_SUB, tq)
    jobs = [(qs, g) for qs in range(tq // sub) for g in range(4)]
    lp = lam_ref[...]
    lam = (jnp.exp(jnp.sum(lp[0:1] * lp[1:2], keepdims=True)) - jnp.exp(jnp.sum(lp[2:3] * lp[3:4], keepdims=True))
           + lam_init)

    def finish(qs, res):
        o = jnp.concatenate([res[0] - lam * res[1], res[2] - lam * res[3]], axis=0).T
        o2 = o * o
        ss_a = jnp.sum(jnp.where(first, o2, 0.0), axis=-1, keepdims=True)
        ss_b = jnp.sum(jnp.where(first, 0.0, o2), axis=-1, keepdims=True)
        inv = jnp.where(first, lax.rsqrt(ss_a * (1.0 / HEAD_DIM) + NORM_EPS),
                        lax.rsqrt(ss_b * (1.0 / HEAD_DIM) + NORM_EPS))
        o_ref[qs * sub:(qs + 1) * sub, :] = (o * inv * ng_ref[...] * (1.0 - lam_init)).astype(bf16)

    res = []
    s_prev = m_prev = None
    for j in range(len(jobs) + 1):
        s_cur, mvec, acc = [], None, None
        if j < len(jobs):
            qs, g = jobs[j]
            qm = jnp.where(lane // DIFF_DK == g, q[qs * sub:(qs + 1) * sub], 0.0).astype(bf16)
        for c in range(nck):
            if j < len(jobs):
                s = _dot(qm, kb_scr[c * ck:(c + 1) * ck, :], NT_DIMS)
                s_cur.append(s)
                for k in range(ck // LANES):
                    blk = s[:, k * LANES:(k + 1) * LANES]
                    mvec = blk if mvec is None else jnp.maximum(mvec, blk)
            if j > 0:
                p = jnp.exp2(s_prev[c] - m_prev).astype(bf16)
                part = _dot(vt_scr[jobs[j - 1][1] // 2, :, c * ck:(c + 1) * ck], p, NT_DIMS)
                acc = part if acc is None else acc + part
        if j > 0:
            res.append(acc[0:HEAD_DIM] / acc[HEAD_DIM:HEAD_DIM + 1])
            if len(res) == 4:
                finish(jobs[j - 1][0], res)
                res = []
        if j < len(jobs):
            s_prev, m_prev = s_cur, jnp.max(mvec, axis=-1, keepdims=True)


def diff_mixer(p, lam_p, ng, bsz, seq, lam_init, ctx=None, rope_tabs=None, tq=1024):
    tq = min(tq, seq)
    nq = seq // tq
    npair = MIX_W // LANES
    rope = rope_tabs is not None
    ctx_len = ctx[0].shape[1] if ctx is not None else 0
    assert rope == (ctx is not None)
    q_spec = pl.BlockSpec((None, tq, LANES), lambda b, hp, i: (T_DQ, b * nq + i, hp))
    k_spec = pl.BlockSpec((None, seq, LANES), lambda b, hp, i: (T_DK, b, hp))
    v_spec = pl.BlockSpec((None, seq, LANES), lambda b, hp, i: (T_DV, b, hp))
    const = lambda shape: pl.BlockSpec(shape, lambda b, hp, i: (0,) * len(shape))
    args = [p, p, p]
    specs = [q_spec, k_spec, v_spec]
    if rope:
        c_spec = pl.BlockSpec((None, ctx_len, LANES), lambda b, hp, i: (b, 0, hp))
        tab_q = pl.BlockSpec((tq, LANES), lambda b, hp, i: (i, 0))
        args += [ctx[0], ctx[1], rope_tabs[0], rope_tabs[1], rope_tabs[0], rope_tabs[1]]
        specs += [c_spec, c_spec, tab_q, tab_q, const((seq, LANES)), const((seq, LANES))]
    args += [lam_p, ng]
    specs += [const((8, LANES)), const((1, LANES))]
    return pl.pallas_call(
        functools.partial(_diff_kernel, seq=seq, ctx_len=ctx_len, rope=rope, lam_init=lam_init),
        out_shape=jax.ShapeDtypeStruct((bsz * seq, MIX_W), bf16),
        grid=(bsz, npair, nq),
        in_specs=specs,
        out_specs=pl.BlockSpec((tq, LANES), lambda b, hp, i: (b * nq + i, hp)),
        scratch_shapes=[pltpu.VMEM((seq + ctx_len, LANES), bf16), pltpu.VMEM((2, LANES, seq + ctx_len), bf16)],
        compiler_params=_cparams("parallel", "parallel", "arbitrary"),
        name="diff_mixer",
    )(*args)


def _ssd_kernel(sx_ref, bcr_ref, sxp_ref, bcp_ref, sxn_ref, bcn_ref, cw_ref, cb_ref, z_ref, dt_ref, dtb_ref, alog_ref,
                dexp_ref, ng_ref, exp_ref, s0_ref, o_ref, sfin_ref,
                yf_scr, st_scr, dt_scr, dta_scr, xs_ref, bc_ref, *, nblk, tl):
    ph = pl.program_id(1)
    i = pl.program_id(2)
    fwd = ph == 0
    c = SSD_CHUNK
    nch = tl // c
    rep = NH // SSD_G
    blk = jnp.where(fwd, i, nblk - 1 - i)

    @pl.when(i == 0)
    def _():
        st_scr[...] = _expand_state(s0_ref[...], SSD_N, HEAD_DIM)

    brow = pl.multiple_of(blk * tl, tl)

    @pl.when(fwd)
    def _():
        row = lax.broadcasted_iota(jnp.int32, (tl, 1), 0)
        for raw_ref, prev_ref, next_ref, dst_ref, lo in ((sx_ref, sxp_ref, sxn_ref, xs_ref, 0),
                                                        (bcr_ref, bcp_ref, bcn_ref, bc_ref, MIX_W)):
            x = raw_ref[...]
            w = cw_ref[:, lo:lo + x.shape[1]]
            prev_row = jnp.where(blk == 0, 0.0, prev_ref[7:8, :])
            next_row = jnp.where(blk == nblk - 1, 0.0, next_ref[0:1, :])
            xm = jnp.where(row == 0, prev_row, pltpu.roll(x, 1, 0))
            xp = jnp.where(row == tl - 1, next_row, pltpu.roll(x, tl - 1, 0))
            dst_ref[pl.ds(brow, tl), :] = _silu(xm * w[0:1] + x * w[1:2] + xp * w[2:3]
                                                + cb_ref[:, lo:lo + x.shape[1]])

    dtv = _softplus(dt_ref[...] + dtb_ref[...])
    dtv = jnp.where(fwd, dtv, pltpu.roll(dtv, LANES - NH, 1))
    av = -jnp.exp(alog_ref[...])
    av = jnp.where(fwd, av, pltpu.roll(av, LANES - NH, 1))
    dt_scr[...] = dtv
    dta_scr[...] = dtv * av

    tri = _tri_mask(c, c, jnp.where(fwd, 1, -1))
    trib = jnp.where(tri, 1.0, 0.0).astype(bf16)
    head = lax.broadcasted_iota(jnp.int32, (1, MIX_W), 1) // HEAD_DIM
    bd = (lax.broadcasted_iota(jnp.int32, (MIX_W, MIX_W), 0) // SSD_N
          == lax.broadcasted_iota(jnp.int32, (MIX_W, MIX_W), 1) // HEAD_DIM)
    def chunk(j, carry):
        cj = jnp.where(fwd, j, nch - 1 - j)
        off = pl.multiple_of(cj * c, c)
        srow = pl.multiple_of(brow + off, c)
        dtc = dt_scr[pl.ds(off, c), :]
        cum = _sel_dot(trib, dta_scr[pl.ds(off, c), :])
        cum_t = cum.T
        dt_t = dtc.T
        earg = _sel_dot(cum, exp_ref[...])
        elast = jnp.where(fwd, earg[c - 1:c, :], earg[0:1, :])
        dtx = _sel_dot(dtc, exp_ref[...])
        bt = _per_head_groups(bc_ref[pl.ds(srow, c), 0:LANES])
        ct = _per_head_groups(bc_ref[pl.ds(srow, c), LANES:2 * LANES])
        x = xs_ref[pl.ds(srow, c), :]
        xb = x.astype(bf16)
        st = st_scr[...]
        y = _dot((ct * jnp.exp(earg)).astype(bf16), st.astype(bf16))
        btail = (bt * jnp.exp(elast - earg) * dtx).astype(bf16)
        new = _dot(btail, xb, TN_DIMS)
        st_scr[...] = st * jnp.exp(elast) + jnp.where(bd, new, 0.0)
        btb = bt.astype(bf16)
        cb = [_dot(jnp.where(head == g * rep, ct, 0.0).astype(bf16), btb, NT_DIMS) for g in range(SSD_G)]
        ws = []
        for h in range(NH):
            seg = cum[:, h:h + 1] - cum_t[h:h + 1, :]
            lm = jnp.exp(jnp.where(tri, seg, -jnp.inf))
            ws.append((cb[h // rep] * lm * dt_t[h:h + 1, :]).astype(bf16))
        r = _dot(jnp.concatenate(ws, axis=0), xb)
        for h in range(NH):
            y += jnp.where(head == h, r[h * c:(h + 1) * c], 0.0)
        yf_scr[pl.ds(pl.multiple_of(brow + off, c), c), :] += y
        return carry

    @pl.when(fwd)
    def _():
        yf_scr[pl.ds(brow, tl), :] = jnp.zeros((tl, MIX_W), f32)

    lax.fori_loop(0, nch, chunk, 0, unroll=True)

    @pl.when(ph == 1)
    def _():
        yt = yf_scr[pl.ds(brow, tl), :] + xs_ref[pl.ds(brow, tl), :] * dexp_ref[...]
        t = yt * _silu(z_ref[...])
        ms = jnp.mean(t * t, axis=-1, keepdims=True)
        o_ref[...] = (t * lax.rsqrt(ms + NORM_EPS) * ng_ref[...]).astype(bf16)

    @pl.when(i == nblk - 1)
    def _():
        sfin_ref[...] = _compact_state(st_scr[...])


def ssd_mixer(p, cw, cb, dtb, alog, dexp, ng, expand, s0, bsz, seq, tl=1024):
    tl = min(tl, seq)
    nblk = seq // tl
    n8 = bsz * seq // 8
    per8 = tl // 8
    blk = lambda ph, i: jnp.where(ph == 0, i, nblk - 1 - i)
    rows = lambda b, ph, i: b * nblk + blk(ph, i)
    const = lambda shape: pl.BlockSpec(shape, lambda b, ph, i: (0,) * len(shape))
    state = pl.BlockSpec((None, None, SSD_N, MIX_W), lambda b, ph, i: (b, ph, 0, 0))
    body = lambda tile, w: pl.BlockSpec((None, tl, w), lambda b, ph, i: (tile, rows(b, ph, i), 0))
    crow = lambda b, ph, i: jnp.where(ph == 0, rows(b, ph, i), b * nblk + nblk - 1)
    cbody = lambda tile, w: pl.BlockSpec((None, tl, w), lambda b, ph, i: (tile, crow(b, ph, i), 0))
    prev8 = lambda tile, w: pl.BlockSpec(
        (None, 8, w), lambda b, ph, i: (tile, jnp.maximum(crow(b, ph, i) * per8 - 1, 0), 0))
    next8 = lambda tile, w: pl.BlockSpec(
        (None, 8, w), lambda b, ph, i: (tile, jnp.minimum((crow(b, ph, i) + 1) * per8, n8 - 1), 0))
    bcw = 2 * SSD_G * SSD_N
    return pl.pallas_call(
        functools.partial(_ssd_kernel, nblk=nblk, tl=tl),
        out_shape=(jax.ShapeDtypeStruct((bsz * seq, MIX_W), bf16),
                   jax.ShapeDtypeStruct((bsz, 2, SSD_N, MIX_W), f32)),
        grid=(bsz, 2, nblk),
        in_specs=[cbody(T_SX, PT), cbody(T_MISC, bcw), prev8(T_SX, PT), prev8(T_MISC, bcw),
                  next8(T_SX, PT), next8(T_MISC, bcw), const((3, MIX_W + bcw)), const((1, MIX_W + bcw)),
                  body(T_SZ, PT),
                  pl.BlockSpec((None, tl, LANES), lambda b, ph, i: (T_MISC, rows(b, ph, i), MISC_DT // LANES)),
                  const((1, LANES)), const((1, LANES)), const((1, MIX_W)), const((1, MIX_W)),
                  const((LANES, MIX_W)), state],
        out_specs=(pl.BlockSpec((tl, MIX_W),
                                lambda b, ph, i: (b * nblk + jnp.where(ph == 0, nblk - 1, nblk - 1 - i), 0)),
                   state),
        scratch_shapes=[pltpu.VMEM((seq, MIX_W), f32), pltpu.VMEM((MIX_W, MIX_W), f32),
                        pltpu.VMEM((tl, LANES), f32), pltpu.VMEM((tl, LANES), f32),
                        pltpu.VMEM((seq, MIX_W), f32), pltpu.VMEM((seq, bcw), f32)],
        compiler_params=_cparams("parallel", "arbitrary", "arbitrary"),
        name="ssd_mixer",
    )(p, p, p, p, p, p, cw, cb.reshape(1, -1), p, p, dtb, alog, dexp, ng, expand, s0)


def _hy_filter_kernel(zf_ref, zb_ref, tf_ref, tb_ref, w1_ref, b1_ref, sw_ref, w2_ref, b2_ref,
                      w3f_ref, w3b_ref, b3f_ref, b3b_ref, df_ref, db_ref, skip_ref, r_ref, hf_scr, hb_scr, *, seq):
    first = jnp.logical_and(pl.program_id(0) == 0, pl.program_id(1) == 0)

    @pl.when(first)
    def _():
        for z_ref, h_scr in ((zf_ref, hf_scr), (zb_ref, hb_scr)):
            h = jnp.sin(sw_ref[...] * (_dot(w1_ref[...], z_ref[...], precision=HI) + b1_ref[...]))
            h_scr[...] = jnp.sin(sw_ref[...] * (_dot(w2_ref[...], h, precision=HI) + b2_ref[...]))

    hf = (_dot(w3f_ref[...], hf_scr[...], precision=HI) + b3f_ref[...]) * jnp.exp(-tf_ref[...] * jnp.abs(df_ref[...]))
    hb = (_dot(w3b_ref[...], hb_scr[...], precision=HI) + b3b_ref[...]) * jnp.exp(-tb_ref[...] * jnp.abs(db_ref[...]))
    den = (jnp.sum(jnp.abs(hf), axis=-1, keepdims=True) + jnp.sum(jnp.abs(hb), axis=-1, keepdims=True)) + NORM_EPS
    col = lax.broadcasted_iota(jnp.int32, (1, seq), 1)
    r_ref[:, 0:seq] = jnp.where(col == 0, 0.0, hb / den)
    r_ref[:, seq:2 * seq] = hf / den + jnp.where(col == 0, skip_ref[...], 0.0)


def hy_filters(fp, seq, cbf=128):
    t = jnp.arange(seq, dtype=f32) / seq
    t_rev = jnp.concatenate([t[:1], t[:0:-1]])

    def feats(tt):
        tc = tt[:, None]
        ang = 2.0 * math.pi * tc * jnp.arange(1, HY_BANDS + 1, dtype=f32)
        z = jnp.concatenate([tc, jnp.cos(ang), jnp.sin(ang)], axis=-1)
        return jnp.pad(z, ((0, 0), (0, HY_EMB_PAD - HY_EMB))).T

    ch = MIX_W
    nb = ch // cbf
    col = lambda a: a.reshape(-1, 1)
    w3t = fp['w3'].T.reshape(2, 2, ch, HY_HID)
    b3 = fp['b3'].reshape(2, 2, ch, 1)
    dec = fp['decay'].reshape(2, 2, ch, 1)
    const = lambda shape: pl.BlockSpec(shape, lambda o, j: (0,) * len(shape))
    sel = lambda d, last: pl.BlockSpec((None, None, cbf, last), lambda o, j: (o, d, j, 0))
    return pl.pallas_call(
        functools.partial(_hy_filter_kernel, seq=seq),
        out_shape=jax.ShapeDtypeStruct((2, ch, 2 * seq), f32),
        grid=(2, nb),
        in_specs=[const((HY_EMB_PAD, seq)), const((HY_EMB_PAD, seq)), const((1, seq)), const((1, seq)),
                  const((HY_HID, HY_EMB_PAD)), const((HY_HID, 1)), const((HY_HID, 1)),
                  const((HY_HID, HY_HID)), const((HY_HID, 1)),
                  sel(0, HY_HID), sel(1, HY_HID), sel(0, 1), sel(1, 1), sel(0, 1), sel(1, 1),
                  pl.BlockSpec((None, cbf, 1), lambda o, j: (o, j, 0))],
        out_specs=pl.BlockSpec((None, cbf, 2 * seq), lambda o, j: (o, j, 0)),
        scratch_shapes=[pltpu.VMEM((HY_HID, seq), f32), pltpu.VMEM((HY_HID, seq), f32)],
        compiler_params=_cparams("arbitrary", "arbitrary"),
        name="hy_filters",
    )(feats(t), feats(t_rev), t.reshape(1, seq), t_rev.reshape(1, seq),
      jnp.pad(fp['w1'], ((0, HY_EMB_PAD - HY_EMB), (0, 0))).T, col(fp['b1']), col(fp['sin_w']),
      fp['w2'].T, col(fp['b2']), w3t, w3t, b3, b3, dec, dec, fp['skip'].reshape(2, ch, 1))


def _hyena_kernel(cw_ref, cb_ref, hv_ref, h1_ref, h2_ref, r_ref, o_ref, acc_scr, *, bsz, n_i, cb_n):
    tb = HY_TB
    gr = tb // 2
    sub = tb // LANES
    seq = n_i * tb
    rows = n_i * bsz
    base = pl.program_id(0) * cb_n
    n_ch = MIX_W
    lane = lax.broadcasted_iota(jnp.int32, (1, tb), 1)
    zblk = jnp.zeros((bsz, tb), f32)

    def load(ref, ch):
        return jnp.concatenate(
            [jnp.concatenate([ref[ch, pl.ds(sub * ib + k, bsz, stride=sub * n_i), :] for k in range(sub)], axis=1)
             for ib in range(n_i)], axis=0)

    def short_conv(a, stream, ch):
        idx = stream * n_ch + base + ch
        prev = jnp.concatenate([zblk, a[:rows - bsz]], axis=0) if n_i > 1 else zblk
        nxt = jnp.concatenate([a[bsz:], zblk], axis=0) if n_i > 1 else zblk
        am = pltpu.roll(jnp.where(lane == tb - 1, prev, a), 1, 1)
        ap = pltpu.roll(jnp.where(lane == 0, nxt, a), tb - 1, 1)
        return (am * cw_ref[idx] + a * cw_ref[3 * n_ch + idx] + ap * cw_ref[6 * n_ch + idx]) + cb_ref[idx]

    def long_conv(u, order, ch):
        rrow = r_ref[order, pl.ds(ch, 1), :]
        bits = lambda a: lax.bitcast_convert_type(a.astype(bf16).astype(f32), jnp.int32)
        word = (bits(pltpu.roll(rrow, 1, 1)) & jnp.int32(-65536)) | lax.shift_right_logical(bits(rrow), 16)
        g = pltpu.bitcast(
            pltpu.roll(jnp.broadcast_to(word, (gr // 2, 2 * seq)), 0, 1, stride=2, stride_axis=0), bf16)
        for d in [0] + [e for e in range(-(n_i - 1), n_i) if e != 0]:
            n = n_i - abs(d)
            src = max(0, -d) * bsz
            dst = max(0, d) * bsz
            c0 = seq + d * tb
            m = jnp.concatenate([g[:, c0:c0 + tb], g[:, c0 - gr:c0 - gr + tb]], axis=0)
            part = _dot(u[src:src + n * bsz].astype(bf16), m)
            if d == 0:
                acc_scr[order] = part
            else:
                acc_scr[order, dst:dst + n * bsz, :] += part
        return acc_scr[order]

    def body(ch, carry):
        hv = short_conv(load(hv_ref, ch), 0, ch)
        hx1 = short_conv(load(h1_ref, ch), 1, ch)
        hx2 = short_conv(load(h2_ref, ch), 2, ch)
        zz = hx1 * long_conv(hv, 0, ch)
        out = hx2 * long_conv(zz, 1, ch)
        for ib in range(n_i):
            for k in range(sub):
                o_ref[ch, pl.ds(sub * ib + k, bsz, stride=sub * n_i), :] = (
                    out[ib * bsz:(ib + 1) * bsz, k * LANES:(k + 1) * LANES])
        return carry

    lax.fori_loop(0, cb_n, body, 0, unroll=4)


def hyena_mixer(hy_t, r, cw, cb, bsz, seq, cb_n=8):
    n_i = seq // HY_TB
    ch = MIX_W
    nblk = ch // cb_n
    n_rows = bsz * seq // LANES
    x3 = hy_t.reshape(3 * ch, n_rows, LANES)
    stream = lambda s: pl.BlockSpec((cb_n, n_rows, LANES), lambda j: (s * nblk + j, 0, 0))
    smem = pl.BlockSpec(memory_space=pltpu.SMEM)
    out = pl.pallas_call(
        functools.partial(_hyena_kernel, bsz=bsz, n_i=n_i, cb_n=cb_n),
        out_shape=jax.ShapeDtypeStruct((ch, n_rows, LANES), f32),
        grid=(nblk,),
        in_specs=[smem, smem, stream(0), stream(1), stream(2),
                  pl.BlockSpec((2, cb_n, 2 * seq), lambda j: (0, j, 0))],
        out_specs=pl.BlockSpec((cb_n, n_rows, LANES), lambda j: (j, 0, 0)),
        scratch_shapes=[pltpu.VMEM((2, bsz * n_i, HY_TB), f32)],
        compiler_params=_cparams("parallel"),
        name="hyena_mixer",
    )(cw.reshape(-1), cb, x3, x3, x3, r)
    return out.reshape(ch, bsz * seq)


_IN_SIZES = (NH * GLA_DK, NH * GLA_DK, MIX_W, MIX_W, 2 * GLA_RANK, MIX_W, MIX_W, MIX_W,
             MIX_W, MIX_W + 2 * SSD_G * SSD_N, 2 * NH, 3 * MIX_W)
_IN_OFFS = np.concatenate([[0], np.cumsum(_IN_SIZES)]).tolist()


def _prep_w_in(w_in):
    (gq, gk, gv, gg, gr, dq, dk, dv, sz, sxbc, sdt, hy) = [w_in[:, _IN_OFFS[i]:_IN_OFFS[i + 1]]
                                                          for i in range(len(_IN_SIZES))]
    d = w_in.shape[0]
    zeros = lambda n: jnp.zeros((d, n), w_in.dtype)
    misc = jnp.concatenate([sxbc[:, MIX_W:], gr, zeros(LANES - 2 * GLA_RANK), sdt, zeros(LANES - 2 * NH)], axis=1)
    w_tok = jnp.concatenate([gq, gk, gv, gg, dq, dk, dv, sz, sxbc[:, :MIX_W], misc], axis=1)
    w_tok = w_tok.astype(bf16).reshape(d, N_PT, PT).transpose(1, 0, 2)
    return w_tok, hy.T.astype(bf16)


def _prep_gate_w(gate_w):
    w = jnp.zeros((2, LANES, gate_w.shape[2]), f32)
    for d in range(2):
        w = w.at[d, d * GLA_RANK:(d + 1) * GLA_RANK, :].set(gate_w[d])
    hi = w.astype(bf16)
    return hi, (w - hi.astype(f32)).astype(bf16)


def _rope_tables(seq):
    rows = seq // GRID_W
    r, col = jnp.meshgrid(jnp.arange(rows), jnp.arange(GRID_W), indexing='ij')
    r = r.reshape(-1).astype(f32)
    col = col.reshape(-1).astype(f32)
    nf = DIFF_DK // 4
    inv = ROPE_BASE ** (-jnp.arange(nf, dtype=f32) / nf)
    ang = jnp.concatenate([r[:, None] * inv, col[:, None] * inv], axis=-1)
    cos, sin = jnp.cos(ang), jnp.sin(ang)
    reps = LANES // DIFF_DK
    return jnp.tile(jnp.concatenate([cos, cos], axis=-1), (1, reps)), jnp.tile(jnp.concatenate([-sin, sin], axis=-1), (1, reps))


def _const_tables():
    hsum = np.kron(np.eye(NH, dtype=np.float32), np.ones((HEAD_DIM, HEAD_DIM), np.float32))
    expand = np.zeros((LANES, MIX_W), np.float32)
    for h in range(NH):
        expand[h, h * HEAD_DIM:(h + 1) * HEAD_DIM] = 1.0
    return tuple(jnp.asarray(a, dtype=bf16) for a in (hsum, expand))


def _pack_state(s):
    bsz, _, h, a, b = s.shape
    return s.transpose(0, 1, 4, 2, 3).reshape(bsz, 2, b, h * a)


def _unpack_state(st, a):
    bsz, _, b, _ = st.shape
    return st.reshape(bsz, 2, b, NH, a).transpose(0, 1, 3, 4, 2)


def _layer(x, mod, rows_per_mod, lw, layer, consts, bsz, seq, lam_init, r_filt, ctx, rope_tabs, final_g,
           final_norm):
    hsum, expand = consts
    p, hy_t = in_proj(x, mod, lw['norm1_g'], lw['w_tok'], lw['w_hyT'], rows_per_mod)
    if ctx is None:
        gla_s0 = jnp.zeros((bsz, 2, HEAD_DIM, NH * GLA_DK), f32)
        ssd_s0 = jnp.zeros((bsz, 2, SSD_N, MIX_W), f32)
        dctx = None
    else:
        ctx_k, ctx_v, gla_s, ssd_s = ctx
        gla_s0 = _pack_state(gla_s)
        ssd_s0 = _pack_state(ssd_s)
        dctx = (ctx_k.reshape(bsz, -1, MIX_W), ctx_v.reshape(bsz, -1, MIX_W))
    o_gla, gla_fin = gla_mixer(p, lw['gla_gw_hi'], lw['gla_gw_lo'], lw['gla_gb'], lw['gla_ng'], hsum, gla_s0,
                               bsz, seq)
    o_diff = diff_mixer(p, lw['lam_p'], lw['diff_ng'], bsz, seq, lam_init, dctx, rope_tabs)
    o_ssd, ssd_fin = ssd_mixer(p, lw['ssd_conv_w'], lw['ssd_conv_b'], lw['ssd_dtb'], lw['ssd_alog'], lw['ssd_dexp'], lw['ssd_ng'],
                               expand, ssd_s0, bsz, seq)
    o_hy = hyena_mixer(hy_t, r_filt, lw['hy_cw'], lw['hy_cb'], bsz, seq)
    x = out_proj(x, mod, o_gla, o_diff, o_ssd, o_hy, lw['w_out'], layer, rows_per_mod)
    x = mlp(x, mod, lw['norm2_g'], lw['mlp_w1'], lw['mlp_w2'], layer, final_g, rows_per_mod, final_norm)
    return x, p, gla_fin, ssd_fin


def kernel(x_prompt, x_sample, cache_diff_k, cache_diff_v, state_gla, state_ssd, c, c_ctx, ada_w, ada_b, norm1_g, norm2_g, w_in, w_out, gla_gate_w, gla_gate_b, gla_norm_g, diff_lambda, diff_norm_g, ssd_conv_w, ssd_conv_b, ssd_dt_bias, ssd_a_log, ssd_d, ssd_norm_g, hy_conv_w, hy_conv_b, hy_f_w1, hy_f_b1, hy_f_w2, hy_f_b2, hy_f_w3, hy_f_b3, hy_sin_w, hy_decay, hy_skip, mlp_w1, mlp_w2, final_g):
    bp, lp, d = x_prompt.shape
    bs, ls, _ = x_sample.shape
    depth = w_in.shape[0]
    consts = _const_tables()
    rope_tabs = _rope_tables(ls)
    n_c = 16
    cvec = jnp.concatenate([c_ctx[None], c, jnp.zeros((n_c - 1 - bs, d), f32)], axis=0)
    hp = x_prompt.reshape(bp * lp, d)
    hs = x_sample.reshape(bs * ls, d)
    ks_, vs_, gs_, ss_ = [], [], [], []
    w_out_b, mlp_w1_b, mlp_w2_b = w_out.astype(bf16), mlp_w1.astype(bf16), mlp_w2.astype(bf16)
    for l in range(depth):
        w_tok, w_hyT = _prep_w_in(w_in[l])
        pad_l = lambda a: jnp.pad(a, (0, LANES - a.shape[0])).reshape(1, LANES)
        gw_hi, gw_lo = _prep_gate_w(gla_gate_w[l])
        lw = dict(
            norm1_g=norm1_g[l], norm2_g=norm2_g[l], w_tok=w_tok, w_hyT=w_hyT, w_out=w_out_b,
            mlp_w1=mlp_w1_b, mlp_w2=mlp_w2_b,
            gla_gw_hi=gw_hi, gla_gw_lo=gw_lo, gla_gb=gla_gate_b[l].reshape(2, 1, -1), gla_ng=jnp.tile(gla_norm_g[l], NH).reshape(1, MIX_W),
            lam_p=jnp.pad(diff_lambda[l], ((0, 4), (0, LANES - DIFF_DK))),
            diff_ng=jnp.tile(diff_norm_g[l], LANES // HEAD_DIM).reshape(1, LANES),
            ssd_conv_w=ssd_conv_w[l], ssd_conv_b=ssd_conv_b[l],
            ssd_dtb=pad_l(ssd_dt_bias[l].reshape(-1)), ssd_alog=pad_l(ssd_a_log[l].reshape(-1)),
            ssd_dexp=jnp.repeat(ssd_d[l], HEAD_DIM).reshape(1, MIX_W), ssd_ng=ssd_norm_g[l].reshape(1, MIX_W),
            hy_cw=hy_conv_w[l], hy_cb=hy_conv_b[l],
        )
        fp = dict(w1=hy_f_w1[l], b1=hy_f_b1[l], w2=hy_f_w2[l], b2=hy_f_b2[l], w3=hy_f_w3[l], b3=hy_f_b3[l],
                  sin_w=hy_sin_w[l], decay=hy_decay[l], skip=hy_skip[l])
        mod = ada_mod(cvec, ada_w, ada_b, l).reshape(n_c, 6, d)
        lam_init = 0.8 - 0.6 * math.exp(-0.3 * l)
        last = l == depth - 1
        hp, p_p, g_l, s_l = _layer(hp, mod[0:1], bp * lp, lw, l, consts, bp, lp, lam_init, hy_filters(fp, lp),
                                   None, None, final_g, last)
        ks_.append(p_p[T_DK].reshape(bp, lp, NH, 2 * DIFF_DK))
        vs_.append(p_p[T_DV].reshape(bp, lp, NH, HEAD_DIM))
        gs_.append(_unpack_state(g_l, GLA_DK))
        ss_.append(_unpack_state(s_l, HEAD_DIM))
        hs, _, _, _ = _layer(hs, mod[1:1 + bs], ls, lw, l, consts, bs, ls, lam_init, hy_filters(fp, ls),
                             (cache_diff_k[:, l], cache_diff_v[:, l], state_gla[:, l], state_ssd[:, l]),
                             rope_tabs, final_g, last)
    return (hp.reshape(bp, lp, d), hs.reshape(bs, ls, d), jnp.stack(ks_, axis=1), jnp.stack(vs_, axis=1),
            jnp.stack(gs_, axis=1), jnp.stack(ss_, axis=1))
```

```python
import functools
import math

import numpy as np
import jax
import jax.numpy as jnp
from jax import lax
from jax.experimental import pallas as pl
from jax.experimental.pallas import tpu as pltpu

f32 = jnp.float32
bf16 = jnp.bfloat16
HI = lax.Precision.HIGHEST

LANES = 128
VMEM_LIMIT = 56 * 2**20

D_MODEL = 2048
GRID_W = 64
MIX_W = D_MODEL // 4
HEAD_DIM = 64
NH = MIX_W // HEAD_DIM
D_FF = 4 * D_MODEL
NORM_EPS = 1e-6
LOG2E = 1.0 / math.log(2.0)
GLA_DK = HEAD_DIM // 2
GLA_RANK = 16
GLA_TAU = 16.0
GLA_CHUNK = 128
DIFF_DK = HEAD_DIM // 2
ROPE_BASE = 10000.0
SSD_N = 64
SSD_G = 2
SSD_CHUNK = 128
HY_BANDS = 16
HY_EMB = 2 * HY_BANDS + 1
HY_EMB_PAD = 40
HY_HID = 64
DIFF_KEY_CHUNK = 512
DIFF_Q_SUB = 512
MLP_TF = 1024
HY_TB = 256

PT = 512
(T_GQK, T_GV, T_GG, T_DQ, T_DK, T_DV, T_SZ, T_SX, T_MISC) = range(9)
N_PT = 9
MISC_GR = 256
MISC_DT = 384
SH1, SC1, G1, SH2, SC2, G2 = range(6)

NT_DIMS = (((1,), (1,)), ((), ()))
TN_DIMS = (((0,), (0,)), ((), ()))


def _cparams(*sem):
    return pltpu.CompilerParams(dimension_semantics=sem, vmem_limit_bytes=VMEM_LIMIT)


def _sigmoid(x):
    return 1.0 / (1.0 + jnp.exp(-x))


def _silu(x):
    return x * _sigmoid(x)


def _softplus(x):
    return jnp.maximum(x, 0.0) + jnp.log1p(jnp.exp(-jnp.abs(x)))


def _log_sigmoid(x):
    return jnp.minimum(x, 0.0) - jnp.log1p(jnp.exp(-jnp.abs(x)))


def _tri_mask(rows, c, sgn):
    r_i = lax.broadcasted_iota(jnp.int32, (rows, c), 0) & (c - 1)
    c_i = lax.broadcasted_iota(jnp.int32, (rows, c), 1)
    return (r_i - c_i) * sgn >= 0


def _expand_state(s, row_w, col_w):
    rows, cols = NH * row_w, NH * col_w
    bd = (lax.broadcasted_iota(jnp.int32, (rows, cols), 0) // row_w
          == lax.broadcasted_iota(jnp.int32, (rows, cols), 1) // col_w)
    return jnp.where(bd, jnp.concatenate([s] * NH, axis=0), 0.0)


def _compact_state(st):
    row_w = st.shape[0] // NH
    out = st[0:row_w]
    for h in range(1, NH):
        out += st[h * row_w:(h + 1) * row_w]
    return out


def _split(x, n):
    parts = []
    for _ in range(n - 1):
        hi = x.astype(bf16)
        parts.append(hi)
        x = x - hi.astype(f32)
    parts.append(x.astype(bf16))
    return parts


def _sel_dot(a, b, dims=None, terms=3):
    if a.dtype == bf16:
        prods = [_dot(a, p, dims) for p in _split(b, terms)]
    else:
        prods = [_dot(p, b, dims) for p in _split(a, terms)]
    return functools.reduce(lambda x, y: x + y, prods)


def _per_head_groups(g2):
    assert SSD_G == 2 and SSD_N == HEAD_DIM and LANES == 2 * SSD_N
    first = lax.broadcasted_iota(jnp.int32, (1, LANES), 1) < SSD_N
    swapped = pltpu.roll(g2, SSD_N, 1)
    g0 = jnp.where(first, g2, swapped)
    g1 = jnp.where(first, swapped, g2)
    rep = NH * HEAD_DIM // (SSD_G * LANES)
    return jnp.concatenate([g0] * rep + [g1] * rep, axis=1)


def _dot(a, b, dims=None, precision=None):
    if dims is None:
        return jnp.dot(a, b, preferred_element_type=f32, precision=precision)
    return lax.dot_general(a, b, dims, preferred_element_type=f32, precision=precision)


def _ada_kernel(c_ref, w_ref, b_ref, o_ref):
    s = _silu(c_ref[...]).astype(bf16)
    o_ref[...] = _dot(s, w_ref[...].astype(bf16)) + b_ref[...]


def ada_mod(cvec, w, b, layer, tn=1024):
    m, d = cvec.shape
    n = w.shape[2]
    return pl.pallas_call(
        _ada_kernel,
        out_shape=jax.ShapeDtypeStruct((m, n), f32),
        grid=(n // tn,),
        in_specs=[pl.BlockSpec((m, d), lambda j: (0, 0)),
                  pl.BlockSpec((None, d, tn), lambda j: (layer, 0, j)),
                  pl.BlockSpec((None, 1, tn), lambda j: (layer, 0, j))],
        out_specs=pl.BlockSpec((m, tn), lambda j: (0, j)),
        compiler_params=_cparams("arbitrary"),
        name="ada_mod",
    )(cvec, w, b.reshape(b.shape[0], 1, n))


def _norm_mod(x, g, sc, sh):
    ms = jnp.mean(x * x, axis=-1, keepdims=True)
    return (x * lax.rsqrt(ms + NORM_EPS) * g) * (1.0 + sc) + sh


def _norm_next_chunk(xn_ref, modn_ref, g_ref, sc, sh, u_scr, slot, row0):
    xn = xn_ref[...]
    u = _norm_mod(xn, g_ref[...], modn_ref[sc:sc + 1, :], modn_ref[sh:sh + 1, :])
    u_scr[slot, pl.ds(row0, xn.shape[0]), :] = u.astype(bf16)


def _inproj_kernel(x_ref, mod_ref, g_ref, wt_ref, wh_ref, p_ref, h_ref, u_scr):
    j = pl.program_id(1)

    @pl.when(j == 0)
    def _():
        u = _norm_mod(x_ref[...], g_ref[...], mod_ref[SC1:SC1 + 1, :], mod_ref[SH1:SH1 + 1, :])
        u_scr[...] = u.astype(bf16)

    @pl.when(j < N_PT)
    def _():
        p_ref[...] = _dot(u_scr[...], wt_ref[...])

    @pl.when(j >= N_PT)
    def _():
        h_ref[...] = _dot(wh_ref[...], u_scr[...], NT_DIMS)


def in_proj(x, mod, g, w_tok, w_hyT, rows_per_mod, tm=1024):
    t, d = x.shape
    assert t % tm == 0 and rows_per_mod % tm == 0, (t, rows_per_mod, tm)
    n_hy = w_hyT.shape[0] // PT
    return pl.pallas_call(
        _inproj_kernel,
        out_shape=(jax.ShapeDtypeStruct((N_PT, t, PT), f32), jax.ShapeDtypeStruct((n_hy * PT, t), f32)),
        grid=(t // tm, N_PT + n_hy),
        in_specs=[pl.BlockSpec((tm, d), lambda i, j: (i, 0)),
                  pl.BlockSpec((None, 6, d), lambda i, j: ((i * tm) // rows_per_mod, 0, 0)),
                  pl.BlockSpec((1, d), lambda i, j: (0, 0)),
                  pl.BlockSpec((None, d, PT), lambda i, j: (jnp.minimum(j, N_PT - 1), 0, 0)),
                  pl.BlockSpec((PT, d), lambda i, j: (jnp.maximum(j - N_PT, 0), 0))],
        out_specs=(pl.BlockSpec((None, tm, PT), lambda i, j: (jnp.minimum(j, N_PT - 1), i, 0)),
                   pl.BlockSpec((PT, tm), lambda i, j: (jnp.maximum(j - N_PT, 0), i))),
        scratch_shapes=[pltpu.VMEM((tm, d), bf16)],
        compiler_params=_cparams("parallel", "arbitrary"),
        name="in_proj",
    )(x, mod, g.reshape(1, d), w_tok, w_hyT)


def _outproj_kernel(x_ref, mod_ref, a_ref, b_ref, c_ref, dT_ref, w_ref, o_ref):
    acc = _dot(a_ref[...], w_ref[0:MIX_W, :])
    acc += _dot(b_ref[...], w_ref[MIX_W:2 * MIX_W, :])
    acc += _dot(c_ref[...], w_ref[2 * MIX_W:3 * MIX_W, :])
    acc += _dot(dT_ref[...].astype(bf16), w_ref[3 * MIX_W:4 * MIX_W, :], TN_DIMS)
    o_ref[...] = x_ref[...] + mod_ref[G1:G1 + 1, :] * acc


def out_proj(x, mod, o_a, o_b, o_c, o_dT, w_out, layer, rows_per_mod, tm=512):
    t, d = x.shape
    assert t % tm == 0 and rows_per_mod % tm == 0, (t, rows_per_mod, tm)
    tok = lambda i: (i, 0)
    return pl.pallas_call(
        _outproj_kernel,
        out_shape=jax.ShapeDtypeStruct((t, d), f32),
        grid=(t // tm,),
        in_specs=[pl.BlockSpec((tm, d), tok),
                  pl.BlockSpec((None, 6, d), lambda i: ((i * tm) // rows_per_mod, 0, 0)),
                  pl.BlockSpec((tm, MIX_W), tok), pl.BlockSpec((tm, MIX_W), tok), pl.BlockSpec((tm, MIX_W), tok),
                  pl.BlockSpec((MIX_W, tm), lambda i: (0, i)),
                  pl.BlockSpec((None, d, d), lambda i: (layer, 0, 0))],
        out_specs=pl.BlockSpec((tm, d), tok),
        compiler_params=_cparams("parallel"),
        name="out_proj",
    )(x, mod, o_a, o_b, o_c, o_dT, w_out)


def _mlp_kernel(xc_ref, mod_ref, xn_ref, modn_ref, g_ref, w1_ref, w2_ref, fg_ref, o_ref, u_scr, *, final_norm):
    io = pl.program_id(0)
    j = pl.program_id(1)
    rows = xn_ref.shape[0]
    row0 = pl.multiple_of(j * rows, rows)

    @pl.when(io == 0)
    def _():
        _norm_next_chunk(xn_ref, modn_ref, g_ref, SC2, SH2, u_scr, 0, row0)

    @pl.when(jnp.logical_and(io > 0, j == 0))
    def _():
        o_ref[...] = jnp.zeros_like(o_ref)

    @pl.when(io > 0)
    def _():
        _norm_next_chunk(xn_ref, modn_ref, g_ref, SC2, SH2, u_scr, io % 2, row0)
        h = jnp.maximum(_dot(u_scr[1 - io % 2], w1_ref[...]), 0.0)
        hb = (h * h).astype(bf16)
        d = o_ref.shape[1]
        for n in range(d // PT):
            cols = slice(n * PT, (n + 1) * PT)
            o_ref[:, cols] += mod_ref[G2:G2 + 1, cols] * _dot(hb, w2_ref[:, cols])
        o_ref[pl.ds(row0, rows), :] += xc_ref[...]

    if final_norm:
        @pl.when(jnp.logical_and(io > 0, j == pl.num_programs(1) - 1))
        def _():
            y = o_ref[...]
            ms = jnp.mean(y * y, axis=-1, keepdims=True)
            o_ref[...] = y * lax.rsqrt(ms + NORM_EPS) * fg_ref[...]


def mlp(x, mod, g, w1, w2, layer, final_g, rows_per_mod, final_norm, tm=1024, tf=MLP_TF):
    t, d = x.shape
    assert t % tm == 0 and rows_per_mod % tm == 0, (t, rows_per_mod, tm)
    n_f = w1.shape[2] // tf
    rows = tm // n_f
    n_tiles = t // tm
    tile = lambda io: jnp.maximum(io - 1, 0)
    chunk_of = lambda ti, j: jnp.minimum(ti * n_f + j, t // rows - 1)
    mod_of = lambda ti: jnp.minimum((ti * tm) // rows_per_mod, mod.shape[0] - 1)
    return pl.pallas_call(
        functools.partial(_mlp_kernel, final_norm=final_norm),
        out_shape=jax.ShapeDtypeStruct((t, d), f32),
        grid=(n_tiles + 1, n_f),
        in_specs=[pl.BlockSpec((rows, d), lambda io, j: (chunk_of(tile(io), j), 0)),
                  pl.BlockSpec((None, 6, d), lambda io, j: (mod_of(tile(io)), 0, 0)),
                  pl.BlockSpec((rows, d), lambda io, j: (chunk_of(jnp.minimum(io, n_tiles - 1), j), 0)),
                  pl.BlockSpec((None, 6, d), lambda io, j: (mod_of(jnp.minimum(io, n_tiles - 1)), 0, 0)),
                  pl.BlockSpec((1, d), lambda io, j: (0, 0)),
                  pl.BlockSpec((None, d, tf), lambda io, j: (layer, 0, j)),
                  pl.BlockSpec((None, tf, d), lambda io, j: (layer, j, 0)),
                  pl.BlockSpec((1, d), lambda io, j: (0, 0))],
        out_specs=pl.BlockSpec((tm, d), lambda io, j: (tile(io), 0)),
        scratch_shapes=[pltpu.VMEM((2, tm, d), bf16)],
        compiler_params=_cparams("arbitrary", "arbitrary"),
        name="mlp",
    )(x, mod, x, mod, g.reshape(1, d), w1, w2, final_g.reshape(1, d))


def _gla_kernel(qk_ref, v_ref, gg_ref, misc_ref, gwh_ref, gwl_ref, gb_ref, ng_ref, hsum_ref, s0_ref,
                o_ref, sfin_ref, of_scr, st_scr, la_scr, *, nblk, tl):
    ph = pl.program_id(1)
    i = pl.program_id(2)
    fwd = ph == 0
    c = GLA_CHUNK
    nch = tl // c
    qkw = NH * GLA_DK

    @pl.when(i == 0)
    def _():
        st_scr[...] = _expand_state(s0_ref[...], HEAD_DIM, GLA_DK)

    gr_hi, gr_lo = _split(misc_ref[:, MISC_GR:MISC_GR + LANES], 2)
    logits = (_dot(gr_hi, gwh_ref[...]) + _dot(gr_lo, gwh_ref[...]) + _dot(gr_hi, gwl_ref[...])) + gb_ref[...]
    la_scr[...] = _log_sigmoid(logits) * (1.0 / GLA_TAU)

    sgn = jnp.where(fwd, 1, -1)
    trib = jnp.where(_tri_mask(c, c, sgn), 1.0, 0.0).astype(bf16)
    tri8 = _tri_mask(NH * c, c, sgn)
    qk_head = lax.broadcasted_iota(jnp.int32, (1, qkw), 1) // GLA_DK
    v_head = lax.broadcasted_iota(jnp.int32, (1, MIX_W), 1) // HEAD_DIM
    bd = (lax.broadcasted_iota(jnp.int32, (MIX_W, qkw), 0) // HEAD_DIM
          == lax.broadcasted_iota(jnp.int32, (MIX_W, qkw), 1) // GLA_DK)
    brow = pl.multiple_of(jnp.where(fwd, i, nblk - 1 - i) * tl, tl)

    def chunk(j, carry):
        cj = jnp.where(fwd, j, nch - 1 - j)
        off = pl.multiple_of(cj * c, c)
        q = qk_ref[pl.ds(off, c), 0:qkw] * (GLA_DK ** -0.5)
        k = qk_ref[pl.ds(off, c), qkw:2 * qkw]
        vb = v_ref[pl.ds(off, c), :].astype(bf16)
        b = _sel_dot(trib, la_scr[pl.ds(off, c), :])
        btot = jnp.where(fwd, b[c - 1:c, :], b[0:1, :])
        bmid = b[c // 2 - 1:c // 2, :]
        qd = q * jnp.exp(b)
        qi = q * jnp.exp(b - bmid)
        kd = (k * jnp.exp(bmid - b)).astype(bf16)
        kt = (k * jnp.exp(btot - b)).astype(bf16)
        qst = jnp.concatenate([jnp.where(qk_head == h, qi, 0.0) for h in range(NH)], axis=0).astype(bf16)
        att = jnp.where(tri8, _dot(qst, kd, NT_DIMS), 0.0)
        r = _dot(att.astype(bf16), vb)
        o = jnp.where(v_head == 0, r[0:c], 0.0)
        for h in range(1, NH):
            o += jnp.where(v_head == h, r[h * c:(h + 1) * c], 0.0)
        st = st_scr[...]
        o += _dot(qd.astype(bf16), st.astype(bf16), NT_DIMS)
        kv = _dot(vb, kt, TN_DIMS)
        st_scr[...] = st * jnp.exp(btot) + jnp.where(bd, kv, 0.0)
        of_scr[pl.ds(pl.multiple_of(brow + off, c), c), :] += o
        return carry

    @pl.when(fwd)
    def _():
        of_scr[pl.ds(brow, tl), :] = jnp.zeros((tl, MIX_W), f32)

    lax.fori_loop(0, nch, chunk, 0, unroll=True)

    @pl.when(ph == 1)
    def _():
        ot = of_scr[pl.ds(brow, tl), :]
        ss = _sel_dot(ot * ot, hsum_ref[...], terms=2) * (1.0 / HEAD_DIM)
        o_ref[...] = (ot * lax.rsqrt(ss + NORM_EPS) * ng_ref[...] * _silu(gg_ref[...])).astype(bf16)

    @pl.when(i == nblk - 1)
    def _():
        sfin_ref[...] = _compact_state(st_scr[...])


def gla_mixer(p, gw_hi, gw_lo, gb, ng, hsum, s0, bsz, seq, tl=1024):
    tl = min(tl, seq)
    nblk = seq // tl
    qkw = NH * GLA_DK
    blk = lambda ph, i: jnp.where(ph == 0, i, nblk - 1 - i)
    tile = lambda col: pl.BlockSpec((None, tl, PT), lambda b, ph, i: (col, b * nblk + blk(ph, i), 0))
    const = lambda shape: pl.BlockSpec(shape, lambda b, ph, i: (0,) * len(shape))
    per_dir = lambda rows: pl.BlockSpec((None, rows, qkw), lambda b, ph, i: (ph, 0, 0))
    state = pl.BlockSpec((None, None, HEAD_DIM, qkw), lambda b, ph, i: (b, ph, 0, 0))
    return pl.pallas_call(
        functools.partial(_gla_kernel, nblk=nblk, tl=tl),
        out_shape=(jax.ShapeDtypeStruct((bsz * seq, MIX_W), bf16),
                   jax.ShapeDtypeStruct((bsz, 2, HEAD_DIM, qkw), f32)),
        grid=(bsz, 2, nblk),
        in_specs=[tile(T_GQK), tile(T_GV), tile(T_GG), tile(T_MISC),
                  per_dir(LANES), per_dir(LANES), per_dir(1), const((1, MIX_W)), const((MIX_W, MIX_W)), state],
        out_specs=(pl.BlockSpec((tl, MIX_W),
                                lambda b, ph, i: (b * nblk + jnp.where(ph == 0, nblk - 1, nblk - 1 - i), 0)),
                   state),
        scratch_shapes=[pltpu.VMEM((seq, MIX_W), f32), pltpu.VMEM((MIX_W, qkw), f32), pltpu.VMEM((tl, qkw), f32)],
        compiler_params=_cparams("parallel", "arbitrary", "arbitrary"),
        name="gla_mixer",
    )(p, p, p, p, gw_hi, gw_lo, gb, ng, hsum, s0)


def _rope(x, cos, sin_signed):
    lane = lax.broadcasted_iota(jnp.int32, (1, LANES), 1)
    half = DIFF_DK // 2
    rot = jnp.where((lane % DIFF_DK) < half, pltpu.roll(x, LANES - half, 1), pltpu.roll(x, half, 1))
    return x * cos + rot * sin_signed


def _diff_kernel(*refs, seq, ctx_len, rope, lam_init):
    if rope:
        (q_ref, k_ref, v_ref, ck_ref, cv_ref, cq_ref, sq_ref, ckk_ref, skk_ref, lam_ref, ng_ref,
         o_ref, kb_scr, vt_scr) = refs
    else:
        q_ref, k_ref, v_ref, lam_ref, ng_ref, o_ref, kb_scr, vt_scr = refs
    qi = pl.program_id(2)
    lane = lax.broadcasted_iota(jnp.int32, (1, LANES), 1)
    first = lane < HEAD_DIM

    def put_v(v, lo, hi):
        vt = v.T
        ones = jnp.ones((HEAD_DIM, hi - lo), f32)
        vt_scr[0, :, lo:hi] = jnp.concatenate([vt[0:HEAD_DIM], ones], axis=0).astype(bf16)
        vt_scr[1, :, lo:hi] = jnp.concatenate([vt[HEAD_DIM:], ones], axis=0).astype(bf16)

    @pl.when(qi == 0)
    def _():
        k = k_ref[...]
        if rope:
            k = _rope(k, ckk_ref[...], skk_ref[...])
        kb_scr[0:seq, :] = k.astype(bf16)
        put_v(v_ref[...], 0, seq)
        if ctx_len:
            kb_scr[seq:seq + ctx_len, :] = ck_ref[...].astype(bf16)
            put_v(cv_ref[...], seq, seq + ctx_len)

    q = q_ref[...]
    if rope:
        q = _rope(q, cq_ref[...], sq_ref[...])
    q = q * (DIFF_DK ** -0.5 * LOG2E)
    n_keys = seq + ctx_len
    ck = max(w for w in range(LANES, DIFF_KEY_CHUNK + 1, LANES) if n_keys % w == 0)
    nck = n_keys // ck
    tq = q.shape[0]
    sub = min(DIFF_Q_SUB, tq)
    jobs = [(qs, g) for qs in range(tq // sub) for g in range(4)]
    lp = lam_ref[...]
    lam = (jnp.exp(jnp.sum(lp[0:1] * lp[1:2], keepdims=True)) - jnp.exp(jnp.sum(lp[2:3] * lp[3:4], keepdims=True))
           + lam_init)

    def finish(qs, res):
        o = jnp.concatenate([res[0] - lam * res[1], res[2] - lam * res[3]], axis=0).T
        o2 = o * o
        ss_a = jnp.sum(jnp.where(first, o2, 0.0), axis=-1, keepdims=True)
        ss_b = jnp.sum(jnp.where(first, 0.0, o2), axis=-1, keepdims=True)
        inv = jnp.where(first, lax.rsqrt(ss_a * (1.0 / HEAD_DIM) + NORM_EPS),
                        lax.rsqrt(ss_b * (1.0 / HEAD_DIM) + NORM_EPS))
        o_ref[qs * sub:(qs + 1) * sub, :] = (o * inv * ng_ref[...] * (1.0 - lam_init)).astype(bf16)

    res = []
    s_prev = m_prev = None
    for j in range(len(jobs) + 1):
        s_cur, mvec, acc = [], None, None
        if j < len(jobs):
            qs, g = jobs[j]
            qm = jnp.where(lane // DIFF_DK == g, q[qs * sub:(qs + 1) * sub], 0.0).astype(bf16)
        for c in range(nck):
            if j < len(jobs):
                s = _dot(qm, kb_scr[c * ck:(c + 1) * ck, :], NT_DIMS)
                s_cur.append(s)
                for k in range(ck // LANES):
                    blk = s[:, k * LANES:(k + 1) * LANES]
                    mvec = blk if mvec is None else jnp.maximum(mvec, blk)
            if j > 0:
                p = jnp.exp2(s_prev[c] - m_prev).astype(bf16)
                part = _dot(vt_scr[jobs[j - 1][1] // 2, :, c * ck:(c + 1) * ck], p, NT_DIMS)
                acc = part if acc is None else acc + part
        if j > 0:
            res.append(acc[0:HEAD_DIM] / acc[HEAD_DIM:HEAD_DIM + 1])
            if len(res) == 4:
                finish(jobs[j - 1][0], res)
                res = []
        if j < len(jobs):
            s_prev, m_prev = s_cur, jnp.max(mvec, axis=-1, keepdims=True)


def diff_mixer(p, lam_p, ng, bsz, seq, lam_init, ctx=None, rope_tabs=None, tq=1024):
    tq = min(tq, seq)
    nq = seq // tq
    npair = MIX_W // LANES
    rope = rope_tabs is not None
    ctx_len = ctx[0].shape[1] if ctx is not None else 0
    assert rope == (ctx is not None)
    q_spec = pl.BlockSpec((None, tq, LANES), lambda b, hp, i: (T_DQ, b * nq + i, hp))
    k_spec = pl.BlockSpec((None, seq, LANES), lambda b, hp, i: (T_DK, b, hp))
    v_spec = pl.BlockSpec((None, seq, LANES), lambda b, hp, i: (T_DV, b, hp))
    const = lambda shape: pl.BlockSpec(shape, lambda b, hp, i: (0,) * len(shape))
    args = [p, p, p]
    specs = [q_spec, k_spec, v_spec]
    if rope:
        c_spec = pl.BlockSpec((None, ctx_len, LANES), lambda b, hp, i: (b, 0, hp))
        tab_q = pl.BlockSpec((tq, LANES), lambda b, hp, i: (i, 0))
        args += [ctx[0], ctx[1], rope_tabs[0], rope_tabs[1], rope_tabs[0], rope_tabs[1]]
        specs += [c_spec, c_spec, tab_q, tab_q, const((seq, LANES)), const((seq, LANES))]
    args += [lam_p, ng]
    specs += [const((8, LANES)), const((1, LANES))]
    return pl.pallas_call(
        functools.partial(_diff_kernel, seq=seq, ctx_len=ctx_len, rope=rope, lam_init=lam_init),
        out_shape=jax.ShapeDtypeStruct((bsz * seq, MIX_W), bf16),
        grid=(bsz, npair, nq),
        in_specs=specs,
        out_specs=pl.BlockSpec((tq, LANES), lambda b, hp, i: (b * nq + i, hp)),
        scratch_shapes=[pltpu.VMEM((seq + ctx_len, LANES), bf16), pltpu.VMEM((2, LANES, seq + ctx_len), bf16)],
        compiler_params=_cparams("parallel", "parallel", "arbitrary"),
        name="diff_mixer",
    )(*args)


def _ssd_kernel(sx_ref, bcr_ref, sxp_ref, bcp_ref, sxn_ref, bcn_ref, cw_ref, cb_ref, z_ref, dt_ref, dtb_ref, alog_ref,
                dexp_ref, ng_ref, exp_ref, s0_ref, o_ref, sfin_ref,
                yf_scr, st_scr, dt_scr, dta_scr, xs_ref, bc_ref, *, nblk, tl):
    ph = pl.program_id(1)
    i = pl.program_id(2)
    fwd = ph == 0
    c = SSD_CHUNK
    nch = tl // c
    rep = NH // SSD_G
    blk = jnp.where(fwd, i, nblk - 1 - i)

    @pl.when(i == 0)
    def _():
        st_scr[...] = _expand_state(s0_ref[...], SSD_N, HEAD_DIM)

    brow = pl.multiple_of(blk * tl, tl)

    @pl.when(fwd)
    def _():
        row = lax.broadcasted_iota(jnp.int32, (tl, 1), 0)
        for raw_ref, prev_ref, next_ref, dst_ref, lo in ((sx_ref, sxp_ref, sxn_ref, xs_ref, 0),
                                                        (bcr_ref, bcp_ref, bcn_ref, bc_ref, MIX_W)):
            x = raw_ref[...]
            w = cw_ref[:, lo:lo + x.shape[1]]
            prev_row = jnp.where(blk == 0, 0.0, prev_ref[7:8, :])
            next_row = jnp.where(blk == nblk - 1, 0.0, next_ref[0:1, :])
            xm = jnp.where(row == 0, prev_row, pltpu.roll(x, 1, 0))
            xp = jnp.where(row == tl - 1, next_row, pltpu.roll(x, tl - 1, 0))
            dst_ref[pl.ds(brow, tl), :] = _silu(xm * w[0:1] + x * w[1:2] + xp * w[2:3]
                                                + cb_ref[:, lo:lo + x.shape[1]])

    dtv = _softplus(dt_ref[...] + dtb_ref[...])
    dtv = jnp.where(fwd, dtv, pltpu.roll(dtv, LANES - NH, 1))
    av = -jnp.exp(alog_ref[...])
    av = jnp.where(fwd, av, pltpu.roll(av, LANES - NH, 1))
    dt_scr[...] = dtv
    dta_scr[...] = dtv * av

    tri = _tri_mask(c, c, jnp.where(fwd, 1, -1))
    trib = jnp.where(tri, 1.0, 0.0).astype(bf16)
    head = lax.broadcasted_iota(jnp.int32, (1, MIX_W), 1) // HEAD_DIM
    bd = (lax.broadcasted_iota(jnp.int32, (MIX_W, MIX_W), 0) // SSD_N
          == lax.broadcasted_iota(jnp.int32, (MIX_W, MIX_W), 1) // HEAD_DIM)
    def chunk(j, carry):
        cj = jnp.where(fwd, j, nch - 1 - j)
        off = pl.multiple_of(cj * c, c)
        srow = pl.multiple_of(brow + off, c)
        dtc = dt_scr[pl.ds(off, c), :]
        cum = _sel_dot(trib, dta_scr[pl.ds(off, c), :])
        cum_t = cum.T
        dt_t = dtc.T
        earg = _sel_dot(cum, exp_ref[...])
        elast = jnp.where(fwd, earg[c - 1:c, :], earg[0:1, :])
        dtx = _sel_dot(dtc, exp_ref[...])
        bt = _per_head_groups(bc_ref[pl.ds(srow, c), 0:LANES])
        ct = _per_head_groups(bc_ref[pl.ds(srow, c), LANES:2 * LANES])
        x = xs_ref[pl.ds(srow, c), :]
        xb = x.astype(bf16)
        st = st_scr[...]
        y = _dot((ct * jnp.exp(earg)).astype(bf16), st.astype(bf16))
        btail = (bt * jnp.exp(elast - earg) * dtx).astype(bf16)
        new = _dot(btail, xb, TN_DIMS)
        st_scr[...] = st * jnp.exp(elast) + jnp.where(bd, new, 0.0)
        btb = bt.astype(bf16)
        cb = [_dot(jnp.where(head == g * rep, ct, 0.0).astype(bf16), btb, NT_DIMS) for g in range(SSD_G)]
        ws = []
        for h in range(NH):
            seg = cum[:, h:h + 1] - cum_t[h:h + 1, :]
            lm = jnp.exp(jnp.where(tri, seg, -jnp.inf))
            ws.append((cb[h // rep] * lm * dt_t[h:h + 1, :]).astype(bf16))
        r = _dot(jnp.concatenate(ws, axis=0), xb)
        for h in range(NH):
            y += jnp.where(head == h, r[h * c:(h + 1) * c], 0.0)
        yf_scr[pl.ds(pl.multiple_of(brow + off, c), c), :] += y
        return carry

    @pl.when(fwd)
    def _():
        yf_scr[pl.ds(brow, tl), :] = jnp.zeros((tl, MIX_W), f32)

    lax.fori_loop(0, nch, chunk, 0, unroll=True)

    @pl.when(ph == 1)
    def _():
        yt = yf_scr[pl.ds(brow, tl), :] + xs_ref[pl.ds(brow, tl), :] * dexp_ref[...]
        t = yt * _silu(z_ref[...])
        ms = jnp.mean(t * t, axis=-1, keepdims=True)
        o_ref[...] = (t * lax.rsqrt(ms + NORM_EPS) * ng_ref[...]).astype(bf16)

    @pl.when(i == nblk - 1)
    def _():
        sfin_ref[...] = _compact_state(st_scr[...])


def ssd_mixer(p, cw, cb, dtb, alog, dexp, ng, expand, s0, bsz, seq, tl=1024):
    tl = min(tl, seq)
    nblk = seq // tl
    n8 = bsz * seq // 8
    per8 = tl // 8
    blk = lambda ph, i: jnp.where(ph == 0, i, nblk - 1 - i)
    rows = lambda b, ph, i: b * nblk + blk(ph, i)
    const = lambda shape: pl.BlockSpec(shape, lambda b, ph, i: (0,) * len(shape))
    state = pl.BlockSpec((None, None, SSD_N, MIX_W), lambda b, ph, i: (b, ph, 0, 0))
    body = lambda tile, w: pl.BlockSpec((None, tl, w), lambda b, ph, i: (tile, rows(b, ph, i), 0))
    crow = lambda b, ph, i: jnp.where(ph == 0, rows(b, ph, i), b * nblk + nblk - 1)
    cbody = lambda tile, w: pl.BlockSpec((None, tl, w), lambda b, ph, i: (tile, crow(b, ph, i), 0))
    prev8 = lambda tile, w: pl.BlockSpec(
        (None, 8, w), lambda b, ph, i: (tile, jnp.maximum(crow(b, ph, i) * per8 - 1, 0), 0))
    next8 = lambda tile, w: pl.BlockSpec(
        (None, 8, w), lambda b, ph, i: (tile, jnp.minimum((crow(b, ph, i) + 1) * per8, n8 - 1), 0))
    bcw = 2 * SSD_G * SSD_N
    return pl.pallas_call(
        functools.partial(_ssd_kernel, nblk=nblk, tl=tl),
        out_shape=(jax.ShapeDtypeStruct((bsz * seq, MIX_W), bf16),
                   jax.ShapeDtypeStruct((bsz, 2, SSD_N, MIX_W), f32)),
        grid=(bsz, 2, nblk),
        in_specs=[cbody(T_SX, PT), cbody(T_MISC, bcw), prev8(T_SX, PT), prev8(T_MISC, bcw),
                  next8(T_SX, PT), next8(T_MISC, bcw), const((3, MIX_W + bcw)), const((1, MIX_W + bcw)),
                  body(T_SZ, PT),
                  pl.BlockSpec((None, tl, LANES), lambda b, ph, i: (T_MISC, rows(b, ph, i), MISC_DT // LANES)),
                  const((1, LANES)), const((1, LANES)), const((1, MIX_W)), const((1, MIX_W)),
                  const((LANES, MIX_W)), state],
        out_specs=(pl.BlockSpec((tl, MIX_W),
                                lambda b, ph, i: (b * nblk + jnp.where(ph == 0, nblk - 1, nblk - 1 - i), 0)),
                   state),
        scratch_shapes=[pltpu.VMEM((seq, MIX_W), f32), pltpu.VMEM((MIX_W, MIX_W), f32),
                        pltpu.VMEM((tl, LANES), f32), pltpu.VMEM((tl, LANES), f32),
                        pltpu.VMEM((seq, MIX_W), f32), pltpu.VMEM((seq, bcw), f32)],
        compiler_params=_cparams("parallel", "arbitrary", "arbitrary"),
        name="ssd_mixer",
    )(p, p, p, p, p, p, cw, cb.reshape(1, -1), p, p, dtb, alog, dexp, ng, expand, s0)


def _hy_filter_kernel(zf_ref, zb_ref, tf_ref, tb_ref, w1_ref, b1_ref, sw_ref, w2_ref, b2_ref,
                      w3f_ref, w3b_ref, b3f_ref, b3b_ref, df_ref, db_ref, skip_ref, r_ref, hf_scr, hb_scr, *, seq):
    first = jnp.logical_and(pl.program_id(0) == 0, pl.program_id(1) == 0)

    @pl.when(first)
    def _():
        for z_ref, h_scr in ((zf_ref, hf_scr), (zb_ref, hb_scr)):
            h = jnp.sin(sw_ref[...] * (_dot(w1_ref[...], z_ref[...], precision=HI) + b1_ref[...]))
            h_scr[...] = jnp.sin(sw_ref[...] * (_dot(w2_ref[...], h, precision=HI) + b2_ref[...]))

    hf = (_dot(w3f_ref[...], hf_scr[...], precision=HI) + b3f_ref[...]) * jnp.exp(-tf_ref[...] * jnp.abs(df_ref[...]))
    hb = (_dot(w3b_ref[...], hb_scr[...], precision=HI) + b3b_ref[...]) * jnp.exp(-tb_ref[...] * jnp.abs(db_ref[...]))
    den = (jnp.sum(jnp.abs(hf), axis=-1, keepdims=True) + jnp.sum(jnp.abs(hb), axis=-1, keepdims=True)) + NORM_EPS
    col = lax.broadcasted_iota(jnp.int32, (1, seq), 1)
    r_ref[:, 0:seq] = jnp.where(col == 0, 0.0, hb / den)
    r_ref[:, seq:2 * seq] = hf / den + jnp.where(col == 0, skip_ref[...], 0.0)


def hy_filters(fp, seq, cbf=128):
    t = jnp.arange(seq, dtype=f32) / seq
    t_rev = jnp.concatenate([t[:1], t[:0:-1]])

    def feats(tt):
        tc = tt[:, None]
        ang = 2.0 * math.pi * tc * jnp.arange(1, HY_BANDS + 1, dtype=f32)
        z = jnp.concatenate([tc, jnp.cos(ang), jnp.sin(ang)], axis=-1)
        return jnp.pad(z, ((0, 0), (0, HY_EMB_PAD - HY_EMB))).T

    ch = MIX_W
    nb = ch // cbf
    col = lambda a: a.reshape(-1, 1)
    w3t = fp['w3'].T.reshape(2, 2, ch, HY_HID)
    b3 = fp['b3'].reshape(2, 2, ch, 1)
    dec = fp['decay'].reshape(2, 2, ch, 1)
    const = lambda shape: pl.BlockSpec(shape, lambda o, j: (0,) * len(shape))
    sel = lambda d, last: pl.BlockSpec((None, None, cbf, last), lambda o, j: (o, d, j, 0))
    return pl.pallas_call(
        functools.partial(_hy_filter_kernel, seq=seq),
        out_shape=jax.ShapeDtypeStruct((2, ch, 2 * seq), f32),
        grid=(2, nb),
        in_specs=[const((HY_EMB_PAD, seq)), const((HY_EMB_PAD, seq)), const((1, seq)), const((1, seq)),
                  const((HY_HID, HY_EMB_PAD)), const((HY_HID, 1)), const((HY_HID, 1)),
                  const((HY_HID, HY_HID)), const((HY_HID, 1)),
                  sel(0, HY_HID), sel(1, HY_HID), sel(0, 1), sel(1, 1), sel(0, 1), sel(1, 1),
                  pl.BlockSpec((None, cbf, 1), lambda o, j: (o, j, 0))],
        out_specs=pl.BlockSpec((None, cbf, 2 * seq), lambda o, j: (o, j, 0)),
        scratch_shapes=[pltpu.VMEM((HY_HID, seq), f32), pltpu.VMEM((HY_HID, seq), f32)],
        compiler_params=_cparams("arbitrary", "arbitrary"),
        name="hy_filters",
    )(feats(t), feats(t_rev), t.reshape(1, seq), t_rev.reshape(1, seq),
      jnp.pad(fp['w1'], ((0, HY_EMB_PAD - HY_EMB), (0, 0))).T, col(fp['b1']), col(fp['sin_w']),
      fp['w2'].T, col(fp['b2']), w3t, w3t, b3, b3, dec, dec, fp['skip'].reshape(2, ch, 1))


def _hyena_kernel(cw_ref, cb_ref, hv_ref, h1_ref, h2_ref, r_ref, o_ref, acc_scr, *, bsz, n_i, cb_n):
    tb = HY_TB
    gr = tb // 2
    sub = tb // LANES
    seq = n_i * tb
    rows = n_i * bsz
    base = pl.program_id(0) * cb_n
    n_ch = MIX_W
    lane = lax.broadcasted_iota(jnp.int32, (1, tb), 1)
    zblk = jnp.zeros((bsz, tb), f32)

    def load(ref, ch):
        return jnp.concatenate(
            [jnp.concatenate([ref[ch, pl.ds(sub * ib + k, bsz, stride=sub * n_i), :] for k in range(sub)], axis=1)
             for ib in range(n_i)], axis=0)

    def short_conv(a, stream, ch):
        idx = stream * n_ch + base + ch
        prev = jnp.concatenate([zblk, a[:rows - bsz]], axis=0) if n_i > 1 else zblk
        nxt = jnp.concatenate([a[bsz:], zblk], axis=0) if n_i > 1 else zblk
        am = pltpu.roll(jnp.where(lane == tb - 1, prev, a), 1, 1)
        ap = pltpu.roll(jnp.where(lane == 0, nxt, a), tb - 1, 1)
        return (am * cw_ref[idx] + a * cw_ref[3 * n_ch + idx] + ap * cw_ref[6 * n_ch + idx]) + cb_ref[idx]

    def long_conv(u, order, ch):
        rrow = r_ref[order, pl.ds(ch, 1), :]
        bits = lambda a: lax.bitcast_convert_type(a.astype(bf16).astype(f32), jnp.int32)
        word = (bits(pltpu.roll(rrow, 1, 1)) & jnp.int32(-65536)) | lax.shift_right_logical(bits(rrow), 16)
        g = pltpu.bitcast(
            pltpu.roll(jnp.broadcast_to(word, (gr // 2, 2 * seq)), 0, 1, stride=2, stride_axis=0), bf16)
        for d in [0] + [e for e in range(-(n_i - 1), n_i) if e != 0]:
            n = n_i - abs(d)
            src = max(0, -d) * bsz
            dst = max(0, d) * bsz
            c0 = seq + d * tb
            m = jnp.concatenate([g[:, c0:c0 + tb], g[:, c0 - gr:c0 - gr + tb]], axis=0)
            part = _dot(u[src:src + n * bsz].astype(bf16), m)
            if d == 0:
                acc_scr[order] = part
            else:
                acc_scr[order, dst:dst + n * bsz, :] += part
        return acc_scr[order]

    def body(ch, carry):
        hv = short_conv(load(hv_ref, ch), 0, ch)
        hx1 = short_conv(load(h1_ref, ch), 1, ch)
        hx2 = short_conv(load(h2_ref, ch), 2, ch)
        zz = hx1 * long_conv(hv, 0, ch)
        out = hx2 * long_conv(zz, 1, ch)
        for ib in range(n_i):
            for k in range(sub):
                o_ref[ch, pl.ds(sub * ib + k, bsz, stride=sub * n_i), :] = (
                    out[ib * bsz:(ib + 1) * bsz, k * LANES:(k + 1) * LANES])
        return carry

    lax.fori_loop(0, cb_n, body, 0, unroll=4)


def hyena_mixer(hy_t, r, cw, cb, bsz, seq, cb_n=8):
    n_i = seq // HY_TB
    ch = MIX_W
    nblk = ch // cb_n
    n_rows = bsz * seq // LANES
    x3 = hy_t.reshape(3 * ch, n_rows, LANES)
    stream = lambda s: pl.BlockSpec((cb_n, n_rows, LANES), lambda j: (s * nblk + j, 0, 0))
    smem = pl.BlockSpec(memory_space=pltpu.SMEM)
    out = pl.pallas_call(
        functools.partial(_hyena_kernel, bsz=bsz, n_i=n_i, cb_n=cb_n),
        out_shape=jax.ShapeDtypeStruct((ch, n_rows, LANES), f32),
        grid=(nblk,),
        in_specs=[smem, smem, stream(0), stream(1), stream(2),
                  pl.BlockSpec((2, cb_n, 2 * seq), lambda j: (0, j, 0))],
        out_specs=pl.BlockSpec((cb_n, n_rows, LANES), lambda j: (j, 0, 0)),
        scratch_shapes=[pltpu.VMEM((2, bsz * n_i, HY_TB), f32)],
        compiler_params=_cparams("parallel"),
        name="hyena_mixer",
    )(cw.reshape(-1), cb, x3, x3, x3, r)
    return out.reshape(ch, bsz * seq)


_IN_SIZES = (NH * GLA_DK, NH * GLA_DK, MIX_W, MIX_W, 2 * GLA_RANK, MIX_W, MIX_W, MIX_W,
             MIX_W, MIX_W + 2 * SSD_G * SSD_N, 2 * NH, 3 * MIX_W)
_IN_OFFS = np.concatenate([[0], np.cumsum(_IN_SIZES)]).tolist()


def _prep_w_in(w_in):
    (gq, gk, gv, gg, gr, dq, dk, dv, sz, sxbc, sdt, hy) = [w_in[:, _IN_OFFS[i]:_IN_OFFS[i + 1]]
                                                          for i in range(len(_IN_SIZES))]
    d = w_in.shape[0]
    zeros = lambda n: jnp.zeros((d, n), w_in.dtype)
    misc = jnp.concatenate([sxbc[:, MIX_W:], gr, zeros(LANES - 2 * GLA_RANK), sdt, zeros(LANES - 2 * NH)], axis=1)
    w_tok = jnp.concatenate([gq, gk, gv, gg, dq, dk, dv, sz, sxbc[:, :MIX_W], misc], axis=1)
    w_tok = w_tok.astype(bf16).reshape(d, N_PT, PT).transpose(1, 0, 2)
    return w_tok, hy.T.astype(bf16)


def _prep_gate_w(gate_w):
    w = jnp.zeros((2, LANES, gate_w.shape[2]), f32)
    for d in range(2):
        w = w.at[d, d * GLA_RANK:(d + 1) * GLA_RANK, :].set(gate_w[d])
    hi = w.astype(bf16)
    return hi, (w - hi.astype(f32)).astype(bf16)


def _rope_tables(seq):
    rows = seq // GRID_W
    r, col = jnp.meshgrid(jnp.arange(rows), jnp.arange(GRID_W), indexing='ij')
    r = r.reshape(-1).astype(f32)
    col = col.reshape(-1).astype(f32)
    nf = DIFF_DK // 4
    inv = ROPE_BASE ** (-jnp.arange(nf, dtype=f32) / nf)
    ang = jnp.concatenate([r[:, None] * inv, col[:, None] * inv], axis=-1)
    cos, sin = jnp.cos(ang), jnp.sin(ang)
    reps = LANES // DIFF_DK
    return jnp.tile(jnp.concatenate([cos, cos], axis=-1), (1, reps)), jnp.tile(jnp.concatenate([-sin, sin], axis=-1), (1, reps))


def _const_tables():
    hsum = np.kron(np.eye(NH, dtype=np.float32), np.ones((HEAD_DIM, HEAD_DIM), np.float32))
    expand = np.zeros((LANES, MIX_W), np.float32)
    for h in range(NH):
        expand[h, h * HEAD_DIM:(h + 1) * HEAD_DIM] = 1.0
    return tuple(jnp.asarray(a, dtype=bf16) for a in (hsum, expand))


def _pack_state(s):
    bsz, _, h, a, b = s.shape
    return s.transpose(0, 1, 4, 2, 3).reshape(bsz, 2, b, h * a)


def _unpack_state(st, a):
    bsz, _, b, _ = st.shape
    return st.reshape(bsz, 2, b, NH, a).transpose(0, 1, 3, 4, 2)


def _layer(x, mod, rows_per_mod, lw, layer, consts, bsz, seq, lam_init, r_filt, ctx, rope_tabs, final_g,
           final_norm):
    hsum, expand = consts
    p, hy_t = in_proj(x, mod, lw['norm1_g'], lw['w_tok'], lw['w_hyT'], rows_per_mod)
    if ctx is None:
        gla_s0 = jnp.zeros((bsz, 2, HEAD_DIM, NH * GLA_DK), f32)
        ssd_s0 = jnp.zeros((bsz, 2, SSD_N, MIX_W), f32)
        dctx = None
    else:
        ctx_k, ctx_v, gla_s, ssd_s = ctx
        gla_s0 = _pack_state(gla_s)
        ssd_s0 = _pack_state(ssd_s)
        dctx = (ctx_k.reshape(bsz, -1, MIX_W), ctx_v.reshape(bsz, -1, MIX_W))
    o_gla, gla_fin = gla_mixer(p, lw['gla_gw_hi'], lw['gla_gw_lo'], lw['gla_gb'], lw['gla_ng'], hsum, gla_s0,
                               bsz, seq)
    o_diff = diff_mixer(p, lw['lam_p'], lw['diff_ng'], bsz, seq, lam_init, dctx, rope_tabs)
    o_ssd, ssd_fin = ssd_mixer(p, lw['ssd_conv_w'], lw['ssd_conv_b'], lw['ssd_dtb'], lw['ssd_alog'], lw['ssd_dexp'], lw['ssd_ng'],
                               expand, ssd_s0, bsz, seq)
    o_hy = hyena_mixer(hy_t, r_filt, lw['hy_cw'], lw['hy_cb'], bsz, seq)
    x = out_proj(x, mod, o_gla, o_diff, o_ssd, o_hy, lw['w_out'], layer, rows_per_mod)
    x = mlp(x, mod, lw['norm2_g'], lw['mlp_w1'], lw['mlp_w2'], layer, final_g, rows_per_mod, final_norm)
    return x, p, gla_fin, ssd_fin


def kernel(x_prompt, x_sample, cache_diff_k, cache_diff_v, state_gla, state_ssd, c, c_ctx, ada_w, ada_b, norm1_g, norm2_g, w_in, w_out, gla_gate_w, gla_gate_b, gla_norm_g, diff_lambda, diff_norm_g, ssd_conv_w, ssd_conv_b, ssd_dt_bias, ssd_a_log, ssd_d, ssd_norm_g, hy_conv_w, hy_conv_b, hy_f_w1, hy_f_b1, hy_f_w2, hy_f_b2, hy_f_w3, hy_f_b3, hy_sin_w, hy_decay, hy_skip, mlp_w1, mlp_w2, final_g):
    bp, lp, d = x_prompt.shape
    bs, ls, _ = x_sample.shape
    depth = w_in.shape[0]
    consts = _const_tables()
    rope_tabs = _rope_tables(ls)
    n_c = 16
    cvec = jnp.concatenate([c_ctx[None], c, jnp.zeros((n_c - 1 - bs, d), f32)], axis=0)
    hp = x_prompt.reshape(bp * lp, d)
    hs = x_sample.reshape(bs * ls, d)
    ks_, vs_, gs_, ss_ = [], [], [], []
    w_out_b, mlp_w1_b, mlp_w2_b = w_out.astype(bf16), mlp_w1.astype(bf16), mlp_w2.astype(bf16)
    for l in range(depth):
        w_tok, w_hyT = _prep_w_in(w_in[l])
        pad_l = lambda a: jnp.pad(a, (0, LANES - a.shape[0])).reshape(1, LANES)
        gw_hi, gw_lo = _prep_gate_w(gla_gate_w[l])
        lw = dict(
            norm1_g=norm1_g[l], norm2_g=norm2_g[l], w_tok=w_tok, w_hyT=w_hyT, w_out=w_out_b,
            mlp_w1=mlp_w1_b, mlp_w2=mlp_w2_b,
            gla_gw_hi=gw_hi, gla_gw_lo=gw_lo, gla_gb=gla_gate_b[l].reshape(2, 1, -1), gla_ng=jnp.tile(gla_norm_g[l], NH).reshape(1, MIX_W),
            lam_p=jnp.pad(diff_lambda[l], ((0, 4), (0, LANES - DIFF_DK))),
            diff_ng=jnp.tile(diff_norm_g[l], LANES // HEAD_DIM).reshape(1, LANES),
            ssd_conv_w=ssd_conv_w[l], ssd_conv_b=ssd_conv_b[l],
            ssd_dtb=pad_l(ssd_dt_bias[l].reshape(-1)), ssd_alog=pad_l(ssd_a_log[l].reshape(-1)),
            ssd_dexp=jnp.repeat(ssd_d[l], HEAD_DIM).reshape(1, MIX_W), ssd_ng=ssd_norm_g[l].reshape(1, MIX_W),
            hy_cw=hy_conv_w[l], hy_cb=hy_conv_b[l],
        )
        fp = dict(w1=hy_f_w1[l], b1=hy_f_b1[l], w2=hy_f_w2[l], b2=hy_f_b2[l], w3=hy_f_w3[l], b3=hy_f_b3[l],
                  sin_w=hy_sin_w[l], decay=hy_decay[l], skip=hy_skip[l])
        mod = ada_mod(cvec, ada_w, ada_b, l).reshape(n_c, 6, d)
        lam_init = 0.8 - 0.6 * math.exp(-0.3 * l)
        last = l == depth - 1
        hp, p_p, g_l, s_l = _layer(hp, mod[0:1], bp * lp, lw, l, consts, bp, lp, lam_init, hy_filters(fp, lp),
                                   None, None, final_g, last)
        ks_.append(p_p[T_DK].reshape(bp, lp, NH, 2 * DIFF_DK))
        vs_.append(p_p[T_DV].reshape(bp, lp, NH, HEAD_DIM))
        gs_.append(_unpack_state(g_l, GLA_DK))
        ss_.append(_unpack_state(s_l, HEAD_DIM))
        hs, _, _, _ = _layer(hs, mod[1:1 + bs], ls, lw, l, consts, bs, ls, lam_init, hy_filters(fp, ls),
                             (cache_diff_k[:, l], cache_diff_v[:, l], state_gla[:, l], state_ssd[:, l]),
                             rope_tabs, final_g, last)
    return (hp.reshape(bp, lp, d), hs.reshape(bs, ls, d), jnp.stack(ks_, axis=1), jnp.stack(vs_, axis=1),
            jnp.stack(gs_, axis=1), jnp.stack(ss_, axis=1))
```

```python
import functools
import math

import numpy as np
import jax
import jax.numpy as jnp
from jax import lax
from jax.experimental import pallas as pl
from jax.experimental.pallas import tpu as pltpu

f32 = jnp.float32
bf16 = jnp.bfloat16
HI = lax.Precision.HIGHEST

LANES = 128
VMEM_LIMIT = 56 * 2**20

D_MODEL = 2048
GRID_W = 64
MIX_W = D_MODEL // 4
HEAD_DIM = 64
NH = MIX_W // HEAD_DIM
D_FF = 4 * D_MODEL
NORM_EPS = 1e-6
LOG2E = 1.0 / math.log(2.0)
GLA_DK = HEAD_DIM // 2
GLA_RANK = 16
GLA_TAU = 16.0
GLA_CHUNK = 128
DIFF_DK = HEAD_DIM // 2
ROPE_BASE = 10000.0
SSD_N = 64
SSD_G = 2
SSD_CHUNK = 128
HY_BANDS = 16
HY_EMB = 2 * HY_BANDS + 1
HY_EMB_PAD = 40
HY_HID = 64
DIFF_KEY_CHUNK = 512
DIFF_Q_SUB = 512
HALO = 16
MLP_TF = 1024
HY_TB = 256

PT = 512
(T_GQK, T_GV, T_GG, T_DQ, T_DK, T_DV, T_SZ, T_SX, T_MISC) = range(9)
N_PT = 9
MISC_GR = 256
MISC_DT = 384
SH1, SC1, G1, SH2, SC2, G2 = range(6)

NT_DIMS = (((1,), (1,)), ((), ()))
TN_DIMS = (((0,), (0,)), ((), ()))


def _cparams(*sem):
    return pltpu.CompilerParams(dimension_semantics=sem, vmem_limit_bytes=VMEM_LIMIT)


def _sigmoid(x):
    return 1.0 / (1.0 + jnp.exp(-x))


def _silu(x):
    return x * _sigmoid(x)


def _softplus(x):
    return jnp.maximum(x, 0.0) + jnp.log1p(jnp.exp(-jnp.abs(x)))


def _log_sigmoid(x):
    return jnp.minimum(x, 0.0) - jnp.log1p(jnp.exp(-jnp.abs(x)))


def _tri_mask(rows, c, sgn):
    r_i = lax.broadcasted_iota(jnp.int32, (rows, c), 0) & (c - 1)
    c_i = lax.broadcasted_iota(jnp.int32, (rows, c), 1)
    return (r_i - c_i) * sgn >= 0


def _expand_state(s, row_w, col_w):
    rows, cols = NH * row_w, NH * col_w
    bd = (lax.broadcasted_iota(jnp.int32, (rows, cols), 0) // row_w
          == lax.broadcasted_iota(jnp.int32, (rows, cols), 1) // col_w)
    return jnp.where(bd, jnp.concatenate([s] * NH, axis=0), 0.0)


def _compact_state(st):
    row_w = st.shape[0] // NH
    out = st[0:row_w]
    for h in range(1, NH):
        out += st[h * row_w:(h + 1) * row_w]
    return out


def _split(x, n):
    parts = []
    for _ in range(n - 1):
        hi = x.astype(bf16)
        parts.append(hi)
        x = x - hi.astype(f32)
    parts.append(x.astype(bf16))
    return parts


def _sel_dot(a, b, dims=None, terms=3):
    if a.dtype == bf16:
        prods = [_dot(a, p, dims) for p in _split(b, terms)]
    else:
        prods = [_dot(p, b, dims) for p in _split(a, terms)]
    return functools.reduce(lambda x, y: x + y, prods)


def _per_head_groups(g2):
    assert SSD_G == 2 and SSD_N == HEAD_DIM and LANES == 2 * SSD_N
    first = lax.broadcasted_iota(jnp.int32, (1, LANES), 1) < SSD_N
    swapped = pltpu.roll(g2, SSD_N, 1)
    g0 = jnp.where(first, g2, swapped)
    g1 = jnp.where(first, swapped, g2)
    rep = NH * HEAD_DIM // (SSD_G * LANES)
    return jnp.concatenate([g0] * rep + [g1] * rep, axis=1)


def _dot(a, b, dims=None, precision=None):
    if dims is None:
        return jnp.dot(a, b, preferred_element_type=f32, precision=precision)
    return lax.dot_general(a, b, dims, preferred_element_type=f32, precision=precision)


def _ada_kernel(c_ref, w_ref, b_ref, o_ref):
    s = _silu(c_ref[...]).astype(bf16)
    o_ref[...] = _dot(s, w_ref[...].astype(bf16)) + b_ref[...]


def ada_mod(cvec, w, b, layer, tn=1024):
    m, d = cvec.shape
    n = w.shape[2]
    return pl.pallas_call(
        _ada_kernel,
        out_shape=jax.ShapeDtypeStruct((m, n), f32),
        grid=(n // tn,),
        in_specs=[pl.BlockSpec((m, d), lambda j: (0, 0)),
                  pl.BlockSpec((None, d, tn), lambda j: (layer, 0, j)),
                  pl.BlockSpec((None, 1, tn), lambda j: (layer, 0, j))],
        out_specs=pl.BlockSpec((m, tn), lambda j: (0, j)),
        compiler_params=_cparams("arbitrary"),
        name="ada_mod",
    )(cvec, w, b.reshape(b.shape[0], 1, n))


def _norm_mod(x, g, sc, sh):
    ms = jnp.mean(x * x, axis=-1, keepdims=True)
    return (x * lax.rsqrt(ms + NORM_EPS) * g) * (1.0 + sc) + sh


def _norm_next_chunk(xn_ref, modn_ref, g_ref, sc, sh, u_scr, slot, row0):
    xn = xn_ref[...]
    u = _norm_mod(xn, g_ref[...], modn_ref[sc:sc + 1, :], modn_ref[sh:sh + 1, :])
    u_scr[slot, pl.ds(row0, xn.shape[0]), :] = u.astype(bf16)


def _inproj_kernel(x_ref, mod_ref, g_ref, wt_ref, wh_ref, p_ref, h_ref, u_scr):
    j = pl.program_id(1)

    @pl.when(j == 0)
    def _():
        u = _norm_mod(x_ref[...], g_ref[...], mod_ref[SC1:SC1 + 1, :], mod_ref[SH1:SH1 + 1, :])
        u_scr[...] = u.astype(bf16)

    @pl.when(j < N_PT)
    def _():
        p_ref[...] = _dot(u_scr[...], wt_ref[...]).astype(p_ref.dtype)

    @pl.when(j >= N_PT)
    def _():
        h_ref[...] = _dot(wh_ref[...], u_scr[...], NT_DIMS)


def in_proj(x, mod, g, w_tok, w_hyT, rows_per_mod, tm=1024):
    t, d = x.shape
    assert t % tm == 0 and rows_per_mod % tm == 0, (t, rows_per_mod, tm)
    n_hy = w_hyT.shape[0] // PT
    return pl.pallas_call(
        _inproj_kernel,
        out_shape=(jax.ShapeDtypeStruct((N_PT, t, PT), bf16), jax.ShapeDtypeStruct((n_hy * PT, t), f32)),
        grid=(t // tm, N_PT + n_hy),
        in_specs=[pl.BlockSpec((tm, d), lambda i, j: (i, 0)),
                  pl.BlockSpec((None, 6, d), lambda i, j: ((i * tm) // rows_per_mod, 0, 0)),
                  pl.BlockSpec((1, d), lambda i, j: (0, 0)),
                  pl.BlockSpec((None, d, PT), lambda i, j: (jnp.minimum(j, N_PT - 1), 0, 0)),
                  pl.BlockSpec((PT, d), lambda i, j: (jnp.maximum(j - N_PT, 0), 0))],
        out_specs=(pl.BlockSpec((None, tm, PT), lambda i, j: (jnp.minimum(j, N_PT - 1), i, 0)),
                   pl.BlockSpec((PT, tm), lambda i, j: (jnp.maximum(j - N_PT, 0), i))),
        scratch_shapes=[pltpu.VMEM((tm, d), bf16)],
        compiler_params=_cparams("parallel", "arbitrary"),
        name="in_proj",
    )(x, mod, g.reshape(1, d), w_tok, w_hyT)


def _outproj_kernel(x_ref, mod_ref, a_ref, b_ref, c_ref, dT_ref, w_ref, o_ref):
    acc = _dot(a_ref[...], w_ref[0:MIX_W, :])
    acc += _dot(b_ref[...], w_ref[MIX_W:2 * MIX_W, :])
    acc += _dot(c_ref[...], w_ref[2 * MIX_W:3 * MIX_W, :])
    acc += _dot(dT_ref[...].astype(bf16), w_ref[3 * MIX_W:4 * MIX_W, :], TN_DIMS)
    o_ref[...] = x_ref[...] + mod_ref[G1:G1 + 1, :] * acc


def out_proj(x, mod, o_a, o_b, o_c, o_dT, w_out, layer, rows_per_mod, tm=512):
    t, d = x.shape
    assert t % tm == 0 and rows_per_mod % tm == 0, (t, rows_per_mod, tm)
    tok = lambda i: (i, 0)
    return pl.pallas_call(
        _outproj_kernel,
        out_shape=jax.ShapeDtypeStruct((t, d), f32),
        grid=(t // tm,),
        in_specs=[pl.BlockSpec((tm, d), tok),
                  pl.BlockSpec((None, 6, d), lambda i: ((i * tm) // rows_per_mod, 0, 0)),
                  pl.BlockSpec((tm, MIX_W), tok), pl.BlockSpec((tm, MIX_W), tok), pl.BlockSpec((tm, MIX_W), tok),
                  pl.BlockSpec((MIX_W, tm), lambda i: (0, i)),
                  pl.BlockSpec((None, d, d), lambda i: (layer, 0, 0))],
        out_specs=pl.BlockSpec((tm, d), tok),
        compiler_params=_cparams("parallel"),
        name="out_proj",
    )(x, mod, o_a, o_b, o_c, o_dT, w_out)


def _mlp_kernel(xc_ref, mod_ref, xn_ref, modn_ref, g_ref, w1_ref, w2_ref, fg_ref, o_ref, u_scr, *, final_norm):
    io = pl.program_id(0)
    j = pl.program_id(1)
    rows = xn_ref.shape[0]
    row0 = pl.multiple_of(j * rows, rows)

    @pl.when(io == 0)
    def _():
        _norm_next_chunk(xn_ref, modn_ref, g_ref, SC2, SH2, u_scr, 0, row0)

    @pl.when(jnp.logical_and(io > 0, j == 0))
    def _():
        o_ref[...] = jnp.zeros_like(o_ref)

    @pl.when(io > 0)
    def _():
        _norm_next_chunk(xn_ref, modn_ref, g_ref, SC2, SH2, u_scr, io % 2, row0)
        h = jnp.maximum(_dot(u_scr[1 - io % 2], w1_ref[...]), 0.0)
        hb = (h * h).astype(bf16)
        d = o_ref.shape[1]
        for n in range(d // PT):
            cols = slice(n * PT, (n + 1) * PT)
            o_ref[:, cols] += mod_ref[G2:G2 + 1, cols] * _dot(hb, w2_ref[:, cols])
        o_ref[pl.ds(row0, rows), :] += xc_ref[...]

    if final_norm:
        @pl.when(jnp.logical_and(io > 0, j == pl.num_programs(1) - 1))
        def _():
            y = o_ref[...]
            ms = jnp.mean(y * y, axis=-1, keepdims=True)
            o_ref[...] = y * lax.rsqrt(ms + NORM_EPS) * fg_ref[...]


def mlp(x, mod, g, w1, w2, layer, final_g, rows_per_mod, final_norm, tm=1024, tf=MLP_TF):
    t, d = x.shape
    assert t % tm == 0 and rows_per_mod % tm == 0, (t, rows_per_mod, tm)
    n_f = w1.shape[2] // tf
    rows = tm // n_f
    n_tiles = t // tm
    tile = lambda io: jnp.maximum(io - 1, 0)
    chunk_of = lambda ti, j: jnp.minimum(ti * n_f + j, t // rows - 1)
    mod_of = lambda ti: jnp.minimum((ti * tm) // rows_per_mod, mod.shape[0] - 1)
    return pl.pallas_call(
        functools.partial(_mlp_kernel, final_norm=final_norm),
        out_shape=jax.ShapeDtypeStruct((t, d), f32),
        grid=(n_tiles + 1, n_f),
        in_specs=[pl.BlockSpec((rows, d), lambda io, j: (chunk_of(tile(io), j), 0)),
                  pl.BlockSpec((None, 6, d), lambda io, j: (mod_of(tile(io)), 0, 0)),
                  pl.BlockSpec((rows, d), lambda io, j: (chunk_of(jnp.minimum(io, n_tiles - 1), j), 0)),
                  pl.BlockSpec((None, 6, d), lambda io, j: (mod_of(jnp.minimum(io, n_tiles - 1)), 0, 0)),
                  pl.BlockSpec((1, d), lambda io, j: (0, 0)),
                  pl.BlockSpec((None, d, tf), lambda io, j: (layer, 0, j)),
                  pl.BlockSpec((None, tf, d), lambda io, j: (layer, j, 0)),
                  pl.BlockSpec((1, d), lambda io, j: (0, 0))],
        out_specs=pl.BlockSpec((tm, d), lambda io, j: (tile(io), 0)),
        scratch_shapes=[pltpu.VMEM((2, tm, d), bf16)],
        compiler_params=_cparams("arbitrary", "arbitrary"),
        name="mlp",
    )(x, mod, x, mod, g.reshape(1, d), w1, w2, final_g.reshape(1, d))


def _gla_kernel(qk_ref, v_ref, gg_ref, misc_ref, gwh_ref, gwl_ref, gb_ref, ng_ref, hsum_ref, s0_ref,
                o_ref, sfin_ref, of_scr, st_scr, la_scr, *, nblk, tl):
    ph = pl.program_id(1)
    i = pl.program_id(2)
    fwd = ph == 0
    c = GLA_CHUNK
    nch = tl // c
    qkw = NH * GLA_DK

    @pl.when(i == 0)
    def _():
        st_scr[...] = _expand_state(s0_ref[...], HEAD_DIM, GLA_DK)

    gr = misc_ref[:, MISC_GR:MISC_GR + LANES]
    logits = (_dot(gr, gwh_ref[...]) + _dot(gr, gwl_ref[...])) + gb_ref[...]
    la_scr[...] = _log_sigmoid(logits) * (1.0 / GLA_TAU)

    sgn = jnp.where(fwd, 1, -1)
    trib = jnp.where(_tri_mask(c, c, sgn), 1.0, 0.0).astype(bf16)
    tri8 = _tri_mask(NH * c, c, sgn)
    qk_head = lax.broadcasted_iota(jnp.int32, (1, qkw), 1) // GLA_DK
    v_head = lax.broadcasted_iota(jnp.int32, (1, MIX_W), 1) // HEAD_DIM
    bd = (lax.broadcasted_iota(jnp.int32, (MIX_W, qkw), 0) // HEAD_DIM
          == lax.broadcasted_iota(jnp.int32, (MIX_W, qkw), 1) // GLA_DK)
    brow = pl.multiple_of(jnp.where(fwd, i, nblk - 1 - i) * tl, tl)

    def chunk(j, carry):
        cj = jnp.where(fwd, j, nch - 1 - j)
        off = pl.multiple_of(cj * c, c)
        q = qk_ref[pl.ds(off, c), 0:qkw].astype(f32) * (GLA_DK ** -0.5)
        k = qk_ref[pl.ds(off, c), qkw:2 * qkw].astype(f32)
        vb = v_ref[pl.ds(off, c), :]
        b = _sel_dot(trib, la_scr[pl.ds(off, c), :])
        btot = jnp.where(fwd, b[c - 1:c, :], b[0:1, :])
        bmid = b[c // 2 - 1:c // 2, :]
        qd = q * jnp.exp(b)
        qi = q * jnp.exp(b - bmid)
        kd = (k * jnp.exp(bmid - b)).astype(bf16)
        kt = (k * jnp.exp(btot - b)).astype(bf16)
        qst = jnp.concatenate([jnp.where(qk_head == h, qi, 0.0) for h in range(NH)], axis=0).astype(bf16)
        att = jnp.where(tri8, _dot(qst, kd, NT_DIMS), 0.0)
        r = _dot(att.astype(bf16), vb)
        o = jnp.where(v_head == 0, r[0:c], 0.0)
        for h in range(1, NH):
            o += jnp.where(v_head == h, r[h * c:(h + 1) * c], 0.0)
        st = st_scr[...]
        o += _dot(qd.astype(bf16), st.astype(bf16), NT_DIMS)
        kv = _dot(vb, kt, TN_DIMS)
        st_scr[...] = st * jnp.exp(btot) + jnp.where(bd, kv, 0.0)
        of_scr[pl.ds(pl.multiple_of(brow + off, c), c), :] += o
        return carry

    @pl.when(fwd)
    def _():
        of_scr[pl.ds(brow, tl), :] = jnp.zeros((tl, MIX_W), f32)

    lax.fori_loop(0, nch, chunk, 0, unroll=True)

    @pl.when(ph == 1)
    def _():
        ot = of_scr[pl.ds(brow, tl), :]
        ss = _sel_dot(ot * ot, hsum_ref[...], terms=2) * (1.0 / HEAD_DIM)
        o_ref[...] = (ot * lax.rsqrt(ss + NORM_EPS) * ng_ref[...] * _silu(gg_ref[...].astype(f32))).astype(bf16)

    @pl.when(i == nblk - 1)
    def _():
        sfin_ref[...] = _compact_state(st_scr[...])


def gla_mixer(p, gw_hi, gw_lo, gb, ng, hsum, s0, bsz, seq, tl=1024):
    tl = min(tl, seq)
    nblk = seq // tl
    qkw = NH * GLA_DK
    blk = lambda ph, i: jnp.where(ph == 0, i, nblk - 1 - i)
    tile = lambda col: pl.BlockSpec((None, tl, PT), lambda b, ph, i: (col, b * nblk + blk(ph, i), 0))
    const = lambda shape: pl.BlockSpec(shape, lambda b, ph, i: (0,) * len(shape))
    per_dir = lambda rows: pl.BlockSpec((None, rows, qkw), lambda b, ph, i: (ph, 0, 0))
    state = pl.BlockSpec((None, None, HEAD_DIM, qkw), lambda b, ph, i: (b, ph, 0, 0))
    return pl.pallas_call(
        functools.partial(_gla_kernel, nblk=nblk, tl=tl),
        out_shape=(jax.ShapeDtypeStruct((bsz * seq, MIX_W), bf16),
                   jax.ShapeDtypeStruct((bsz, 2, HEAD_DIM, qkw), f32)),
        grid=(bsz, 2, nblk),
        in_specs=[tile(T_GQK), tile(T_GV), tile(T_GG), tile(T_MISC),
                  per_dir(LANES), per_dir(LANES), per_dir(1), const((1, MIX_W)), const((MIX_W, MIX_W)), state],
        out_specs=(pl.BlockSpec((tl, MIX_W),
                                lambda b, ph, i: (b * nblk + jnp.where(ph == 0, nblk - 1, nblk - 1 - i), 0)),
                   state),
        scratch_shapes=[pltpu.VMEM((seq, MIX_W), f32), pltpu.VMEM((MIX_W, qkw), f32), pltpu.VMEM((tl, qkw), f32)],
        compiler_params=_cparams("parallel", "arbitrary", "arbitrary"),
        name="gla_mixer",
    )(p, p, p, p, gw_hi, gw_lo, gb, ng, hsum, s0)


def _rope(x, cos, sin_signed):
    lane = lax.broadcasted_iota(jnp.int32, (1, LANES), 1)
    half = DIFF_DK // 2
    rot = jnp.where((lane % DIFF_DK) < half, pltpu.roll(x, LANES - half, 1), pltpu.roll(x, half, 1))
    return x * cos + rot * sin_signed


def _diff_kernel(*refs, seq, ctx_len, rope, lam_init):
    if rope:
        (q_ref, k_ref, v_ref, ck_ref, cv_ref, cq_ref, sq_ref, ckk_ref, skk_ref, lam_ref, ng_ref,
         o_ref, kb_scr, vt_scr) = refs
    else:
        q_ref, k_ref, v_ref, lam_ref, ng_ref, o_ref, kb_scr, vt_scr = refs
    qi = pl.program_id(2)
    lane = lax.broadcasted_iota(jnp.int32, (1, LANES), 1)
    first = lane < HEAD_DIM

    def put_v(v, lo, hi):
        vt = v.T
        ones = jnp.ones((HEAD_DIM, hi - lo), f32)
        vt_scr[0, :, lo:hi] = jnp.concatenate([vt[0:HEAD_DIM], ones], axis=0).astype(bf16)
        vt_scr[1, :, lo:hi] = jnp.concatenate([vt[HEAD_DIM:], ones], axis=0).astype(bf16)

    @pl.when(qi == 0)
    def _():
        k = k_ref[...].astype(f32)
        if rope:
            k = _rope(k, ckk_ref[...], skk_ref[...])
        kb_scr[0:seq, :] = k.astype(bf16)
        put_v(v_ref[...].astype(f32), 0, seq)
        if ctx_len:
            kb_scr[seq:seq + ctx_len, :] = ck_ref[...].astype(bf16)
            put_v(cv_ref[...], seq, seq + ctx_len)

    q = q_ref[...].astype(f32)
    if rope:
        q = _rope(q, cq_ref[...], sq_ref[...])
    q = q * (DIFF_DK ** -0.5 * LOG2E)
    n_keys = seq + ctx_len
    ck = max(w for w in range(LANES, DIFF_KEY_CHUNK + 1, LANES) if n_keys % w == 0)
    nck = n_keys // ck
    tq = q.shape[0]
    sub = min(DIFF_Q_SUB, tq)
    jobs = [(qs, g) for qs in range(tq // sub) for g in range(4)]
    lp = lam_ref[...]
    lam = (jnp.exp(jnp.sum(lp[0:1] * lp[1:2], keepdims=True)) - jnp.exp(jnp.sum(lp[2:3] * lp[3:4], keepdims=True))
           + lam_init)

    def finish(qs, res):
        o = jnp.concatenate([res[0] - lam * res[1], res[2] - lam * res[3]], axis=0).T
        o2 = o * o
        ss_a = jnp.sum(jnp.where(first, o2, 0.0), axis=-1, keepdims=True)
        ss_b = jnp.sum(jnp.where(first, 0.0, o2), axis=-1, keepdims=True)
        inv = jnp.where(first, lax.rsqrt(ss_a * (1.0 / HEAD_DIM) + NORM_EPS),
                        lax.rsqrt(ss_b * (1.0 / HEAD_DIM) + NORM_EPS))
        o_ref[qs * sub:(qs + 1) * sub, :] = (o * inv * ng_ref[...] * (1.0 - lam_init)).astype(bf16)

    res = []
    s_prev = m_prev = None
    for j in range(len(jobs) + 1):
        s_cur, mvec, acc = [], None, None
        if j < len(jobs):
            qs, g = jobs[j]
            qm = jnp.where(lane // DIFF_DK == g, q[qs * sub:(qs + 1) * sub], 0.0).astype(bf16)
        for c in range(nck):
            if j < len(jobs):
                s = _dot(qm, kb_scr[c * ck:(c + 1) * ck, :], NT_DIMS)
                s_cur.append(s)
                for k in range(ck // LANES):
                    blk = s[:, k * LANES:(k + 1) * LANES]
                    mvec = blk if mvec is None else jnp.maximum(mvec, blk)
            if j > 0:
                p = jnp.exp2(s_prev[c] - m_prev).astype(bf16)
                part = _dot(vt_scr[jobs[j - 1][1] // 2, :, c * ck:(c + 1) * ck], p, NT_DIMS)
                acc = part if acc is None else acc + part
        if j > 0:
            res.append(acc[0:HEAD_DIM] / acc[HEAD_DIM:HEAD_DIM + 1])
            if len(res) == 4:
                finish(jobs[j - 1][0], res)
                res = []
        if j < len(jobs):
            s_prev, m_prev = s_cur, jnp.max(mvec, axis=-1, keepdims=True)


def diff_mixer(p, lam_p, ng, bsz, seq, lam_init, ctx=None, rope_tabs=None, tq=1024):
    tq = min(tq, seq)
    nq = seq // tq
    npair = MIX_W // LANES
    rope = rope_tabs is not None
    ctx_len = ctx[0].shape[1] if ctx is not None else 0
    assert rope == (ctx is not None)
    q_spec = pl.BlockSpec((None, tq, LANES), lambda b, hp, i: (T_DQ, b * nq + i, hp))
    k_spec = pl.BlockSpec((None, seq, LANES), lambda b, hp, i: (T_DK, b, hp))
    v_spec = pl.BlockSpec((None, seq, LANES), lambda b, hp, i: (T_DV, b, hp))
    const = lambda shape: pl.BlockSpec(shape, lambda b, hp, i: (0,) * len(shape))
    args = [p, p, p]
    specs = [q_spec, k_spec, v_spec]
    if rope:
        c_spec = pl.BlockSpec((None, ctx_len, LANES), lambda b, hp, i: (b, 0, hp))
        tab_q = pl.BlockSpec((tq, LANES), lambda b, hp, i: (i, 0))
        args += [ctx[0], ctx[1], rope_tabs[0], rope_tabs[1], rope_tabs[0], rope_tabs[1]]
        specs += [c_spec, c_spec, tab_q, tab_q, const((seq, LANES)), const((seq, LANES))]
    args += [lam_p, ng]
    specs += [const((8, LANES)), const((1, LANES))]
    return pl.pallas_call(
        functools.partial(_diff_kernel, seq=seq, ctx_len=ctx_len, rope=rope, lam_init=lam_init),
        out_shape=jax.ShapeDtypeStruct((bsz * seq, MIX_W), bf16),
        grid=(bsz, npair, nq),
        in_specs=specs,
        out_specs=pl.BlockSpec((tq, LANES), lambda b, hp, i: (b * nq + i, hp)),
        scratch_shapes=[pltpu.VMEM((seq + ctx_len, LANES), bf16), pltpu.VMEM((2, LANES, seq + ctx_len), bf16)],
        compiler_params=_cparams("parallel", "parallel", "arbitrary"),
        name="diff_mixer",
    )(*args)


def _ssd_kernel(sx_ref, bcr_ref, sxp_ref, bcp_ref, sxn_ref, bcn_ref, cw_ref, cb_ref, z_ref, dt_ref, dtb_ref, alog_ref,
                dexp_ref, ng_ref, exp_ref, s0_ref, o_ref, sfin_ref,
                yf_scr, st_scr, dt_scr, dta_scr, xs_ref, bc_ref, *, nblk, tl):
    ph = pl.program_id(1)
    i = pl.program_id(2)
    fwd = ph == 0
    c = SSD_CHUNK
    nch = tl // c
    rep = NH // SSD_G
    blk = jnp.where(fwd, i, nblk - 1 - i)

    @pl.when(i == 0)
    def _():
        st_scr[...] = _expand_state(s0_ref[...], SSD_N, HEAD_DIM)

    brow = pl.multiple_of(blk * tl, tl)

    @pl.when(fwd)
    def _():
        row = lax.broadcasted_iota(jnp.int32, (tl, 1), 0)
        for raw_ref, prev_ref, next_ref, dst_ref, lo in ((sx_ref, sxp_ref, sxn_ref, xs_ref, 0),
                                                        (bcr_ref, bcp_ref, bcn_ref, bc_ref, MIX_W)):
            x = raw_ref[...].astype(f32)
            w = cw_ref[:, lo:lo + x.shape[1]]
            prev_row = jnp.where(blk == 0, 0.0, prev_ref[HALO - 1:HALO, :].astype(f32))
            next_row = jnp.where(blk == nblk - 1, 0.0, next_ref[0:1, :].astype(f32))
            xm = jnp.where(row == 0, prev_row, pltpu.roll(x, 1, 0))
            xp = jnp.where(row == tl - 1, next_row, pltpu.roll(x, tl - 1, 0))
            dst_ref[pl.ds(brow, tl), :] = _silu(xm * w[0:1] + x * w[1:2] + xp * w[2:3]
                                                + cb_ref[:, lo:lo + x.shape[1]])

    dtv = _softplus(dt_ref[...].astype(f32) + dtb_ref[...])
    dtv = jnp.where(fwd, dtv, pltpu.roll(dtv, LANES - NH, 1))
    av = -jnp.exp(alog_ref[...])
    av = jnp.where(fwd, av, pltpu.roll(av, LANES - NH, 1))
    dt_scr[...] = dtv
    dta_scr[...] = dtv * av

    tri = _tri_mask(c, c, jnp.where(fwd, 1, -1))
    trib = jnp.where(tri, 1.0, 0.0).astype(bf16)
    head = lax.broadcasted_iota(jnp.int32, (1, MIX_W), 1) // HEAD_DIM
    bd = (lax.broadcasted_iota(jnp.int32, (MIX_W, MIX_W), 0) // SSD_N
          == lax.broadcasted_iota(jnp.int32, (MIX_W, MIX_W), 1) // HEAD_DIM)
    def chunk(j, carry):
        cj = jnp.where(fwd, j, nch - 1 - j)
        off = pl.multiple_of(cj * c, c)
        srow = pl.multiple_of(brow + off, c)
        dtc = dt_scr[pl.ds(off, c), :]
        cum = _sel_dot(trib, dta_scr[pl.ds(off, c), :])
        cum_t = cum.T
        dt_t = dtc.T
        earg = _sel_dot(cum, exp_ref[...])
        elast = jnp.where(fwd, earg[c - 1:c, :], earg[0:1, :])
        dtx = _sel_dot(dtc, exp_ref[...])
        bt = _per_head_groups(bc_ref[pl.ds(srow, c), 0:LANES])
        ct = _per_head_groups(bc_ref[pl.ds(srow, c), LANES:2 * LANES])
        x = xs_ref[pl.ds(srow, c), :]
        xb = x.astype(bf16)
        st = st_scr[...]
        y = _dot((ct * jnp.exp(earg)).astype(bf16), st.astype(bf16))
        btail = (bt * jnp.exp(elast - earg) * dtx).astype(bf16)
        new = _dot(btail, xb, TN_DIMS)
        st_scr[...] = st * jnp.exp(elast) + jnp.where(bd, new, 0.0)
        btb = bt.astype(bf16)
        cb = [_dot(jnp.where(head == g * rep, ct, 0.0).astype(bf16), btb, NT_DIMS) for g in range(SSD_G)]
        ws = []
        for h in range(NH):
            seg = cum[:, h:h + 1] - cum_t[h:h + 1, :]
            lm = jnp.exp(jnp.where(tri, seg, -jnp.inf))
            ws.append((cb[h // rep] * lm * dt_t[h:h + 1, :]).astype(bf16))
        r = _dot(jnp.concatenate(ws, axis=0), xb)
        for h in range(NH):
            y += jnp.where(head == h, r[h * c:(h + 1) * c], 0.0)
        yf_scr[pl.ds(pl.multiple_of(brow + off, c), c), :] += y
        return carry

    @pl.when(fwd)
    def _():
        yf_scr[pl.ds(brow, tl), :] = jnp.zeros((tl, MIX_W), f32)

    lax.fori_loop(0, nch, chunk, 0, unroll=True)

    @pl.when(ph == 1)
    def _():
        yt = yf_scr[pl.ds(brow, tl), :] + xs_ref[pl.ds(brow, tl), :] * dexp_ref[...]
        t = yt * _silu(z_ref[...].astype(f32))
        ms = jnp.mean(t * t, axis=-1, keepdims=True)
        o_ref[...] = (t * lax.rsqrt(ms + NORM_EPS) * ng_ref[...]).astype(bf16)

    @pl.when(i == nblk - 1)
    def _():
        sfin_ref[...] = _compact_state(st_scr[...])


def ssd_mixer(p, cw, cb, dtb, alog, dexp, ng, expand, s0, bsz, seq, tl=1024):
    tl = min(tl, seq)
    nblk = seq // tl
    n8 = bsz * seq // HALO
    per8 = tl // HALO
    blk = lambda ph, i: jnp.where(ph == 0, i, nblk - 1 - i)
    rows = lambda b, ph, i: b * nblk + blk(ph, i)
    const = lambda shape: pl.BlockSpec(shape, lambda b, ph, i: (0,) * len(shape))
    state = pl.BlockSpec((None, None, SSD_N, MIX_W), lambda b, ph, i: (b, ph, 0, 0))
    body = lambda tile, w: pl.BlockSpec((None, tl, w), lambda b, ph, i: (tile, rows(b, ph, i), 0))
    crow = lambda b, ph, i: jnp.where(ph == 0, rows(b, ph, i), b * nblk + nblk - 1)
    cbody = lambda tile, w: pl.BlockSpec((None, tl, w), lambda b, ph, i: (tile, crow(b, ph, i), 0))
    prev8 = lambda tile, w: pl.BlockSpec(
        (None, HALO, w), lambda b, ph, i: (tile, jnp.maximum(crow(b, ph, i) * per8 - 1, 0), 0))
    next8 = lambda tile, w: pl.BlockSpec(
        (None, HALO, w), lambda b, ph, i: (tile, jnp.minimum((crow(b, ph, i) + 1) * per8, n8 - 1), 0))
    bcw = 2 * SSD_G * SSD_N
    return pl.pallas_call(
        functools.partial(_ssd_kernel, nblk=nblk, tl=tl),
        out_shape=(jax.ShapeDtypeStruct((bsz * seq, MIX_W), bf16),
                   jax.ShapeDtypeStruct((bsz, 2, SSD_N, MIX_W), f32)),
        grid=(bsz, 2, nblk),
        in_specs=[cbody(T_SX, PT), cbody(T_MISC, bcw), prev8(T_SX, PT), prev8(T_MISC, bcw),
                  next8(T_SX, PT), next8(T_MISC, bcw), const((3, MIX_W + bcw)), const((1, MIX_W + bcw)),
                  body(T_SZ, PT),
                  pl.BlockSpec((None, tl, LANES), lambda b, ph, i: (T_MISC, rows(b, ph, i), MISC_DT // LANES)),
                  const((1, LANES)), const((1, LANES)), const((1, MIX_W)), const((1, MIX_W)),
                  const((LANES, MIX_W)), state],
        out_specs=(pl.BlockSpec((tl, MIX_W),
                                lambda b, ph, i: (b * nblk + jnp.where(ph == 0, nblk - 1, nblk - 1 - i), 0)),
                   state),
        scratch_shapes=[pltpu.VMEM((seq, MIX_W), f32), pltpu.VMEM((MIX_W, MIX_W), f32),
                        pltpu.VMEM((tl, LANES), f32), pltpu.VMEM((tl, LANES), f32),
                        pltpu.VMEM((seq, MIX_W), f32), pltpu.VMEM((seq, bcw), f32)],
        compiler_params=_cparams("parallel", "arbitrary", "arbitrary"),
        name="ssd_mixer",
    )(p, p, p, p, p, p, cw, cb.reshape(1, -1), p, p, dtb, alog, dexp, ng, expand, s0)


def _hy_filter_kernel(zf_ref, zb_ref, tf_ref, tb_ref, w1_ref, b1_ref, sw_ref, w2_ref, b2_ref,
                      w3f_ref, w3b_ref, b3f_ref, b3b_ref, df_ref, db_ref, skip_ref, r_ref, hf_scr, hb_scr, *, seq):
    first = jnp.logical_and(pl.program_id(0) == 0, pl.program_id(1) == 0)

    @pl.when(first)
    def _():
        for z_ref, h_scr in ((zf_ref, hf_scr), (zb_ref, hb_scr)):
            h = jnp.sin(sw_ref[...] * (_dot(w1_ref[...], z_ref[...], precision=HI) + b1_ref[...]))
            h_scr[...] = jnp.sin(sw_ref[...] * (_dot(w2_ref[...], h, precision=HI) + b2_ref[...]))

    hf = (_dot(w3f_ref[...], hf_scr[...], precision=HI) + b3f_ref[...]) * jnp.exp(-tf_ref[...] * jnp.abs(df_ref[...]))
    hb = (_dot(w3b_ref[...], hb_scr[...], precision=HI) + b3b_ref[...]) * jnp.exp(-tb_ref[...] * jnp.abs(db_ref[...]))
    den = (jnp.sum(jnp.abs(hf), axis=-1, keepdims=True) + jnp.sum(jnp.abs(hb), axis=-1, keepdims=True)) + NORM_EPS
    col = lax.broadcasted_iota(jnp.int32, (1, seq), 1)
    r_ref[:, 0:seq] = jnp.where(col == 0, 0.0, hb / den)
    r_ref[:, seq:2 * seq] = hf / den + jnp.where(col == 0, skip_ref[...], 0.0)


def hy_filters(fp, seq, cbf=128):
    t = jnp.arange(seq, dtype=f32) / seq
    t_rev = jnp.concatenate([t[:1], t[:0:-1]])

    def feats(tt):
        tc = tt[:, None]
        ang = 2.0 * math.pi * tc * jnp.arange(1, HY_BANDS + 1, dtype=f32)
        z = jnp.concatenate([tc, jnp.cos(ang), jnp.sin(ang)], axis=-1)
        return jnp.pad(z, ((0, 0), (0, HY_EMB_PAD - HY_EMB))).T

    ch = MIX_W
    nb = ch // cbf
    col = lambda a: a.reshape(-1, 1)
    w3t = fp['w3'].T.reshape(2, 2, ch, HY_HID)
    b3 = fp['b3'].reshape(2, 2, ch, 1)
    dec = fp['decay'].reshape(2, 2, ch, 1)
    const = lambda shape: pl.BlockSpec(shape, lambda o, j: (0,) * len(shape))
    sel = lambda d, last: pl.BlockSpec((None, None, cbf, last), lambda o, j: (o, d, j, 0))
    return pl.pallas_call(
        functools.partial(_hy_filter_kernel, seq=seq),
        out_shape=jax.ShapeDtypeStruct((2, ch, 2 * seq), f32),
        grid=(2, nb),
        in_specs=[const((HY_EMB_PAD, seq)), const((HY_EMB_PAD, seq)), const((1, seq)), const((1, seq)),
                  const((HY_HID, HY_EMB_PAD)), const((HY_HID, 1)), const((HY_HID, 1)),
                  const((HY_HID, HY_HID)), const((HY_HID, 1)),
                  sel(0, HY_HID), sel(1, HY_HID), sel(0, 1), sel(1, 1), sel(0, 1), sel(1, 1),
                  pl.BlockSpec((None, cbf, 1), lambda o, j: (o, j, 0))],
        out_specs=pl.BlockSpec((None, cbf, 2 * seq), lambda o, j: (o, j, 0)),
        scratch_shapes=[pltpu.VMEM((HY_HID, seq), f32), pltpu.VMEM((HY_HID, seq), f32)],
        compiler_params=_cparams("arbitrary", "arbitrary"),
        name="hy_filters",
    )(feats(t), feats(t_rev), t.reshape(1, seq), t_rev.reshape(1, seq),
      jnp.pad(fp['w1'], ((0, HY_EMB_PAD - HY_EMB), (0, 0))).T, col(fp['b1']), col(fp['sin_w']),
      fp['w2'].T, col(fp['b2']), w3t, w3t, b3, b3, dec, dec, fp['skip'].reshape(2, ch, 1))


def _hyena_kernel(cw_ref, cb_ref, hv_ref, h1_ref, h2_ref, r_ref, o_ref, acc_scr, *, bsz, n_i, cb_n):
    tb = HY_TB
    gr = tb // 2
    sub = tb // LANES
    seq = n_i * tb
    rows = n_i * bsz
    base = pl.program_id(0) * cb_n
    n_ch = MIX_W
    lane = lax.broadcasted_iota(jnp.int32, (1, tb), 1)
    zblk = jnp.zeros((bsz, tb), f32)

    def load(ref, ch):
        return jnp.concatenate(
            [jnp.concatenate([ref[ch, pl.ds(sub * ib + k, bsz, stride=sub * n_i), :] for k in range(sub)], axis=1)
             for ib in range(n_i)], axis=0)

    def short_conv(a, stream, ch):
        idx = stream * n_ch + base + ch
        prev = jnp.concatenate([zblk, a[:rows - bsz]], axis=0) if n_i > 1 else zblk
        nxt = jnp.concatenate([a[bsz:], zblk], axis=0) if n_i > 1 else zblk
        am = pltpu.roll(jnp.where(lane == tb - 1, prev, a), 1, 1)
        ap = pltpu.roll(jnp.where(lane == 0, nxt, a), tb - 1, 1)
        return (am * cw_ref[idx] + a * cw_ref[3 * n_ch + idx] + ap * cw_ref[6 * n_ch + idx]) + cb_ref[idx]

    def long_conv(u, order, ch):
        rrow = r_ref[order, pl.ds(ch, 1), :]
        bits = lambda a: lax.bitcast_convert_type(a.astype(bf16).astype(f32), jnp.int32)
        word = (bits(pltpu.roll(rrow, 1, 1)) & jnp.int32(-65536)) | lax.shift_right_logical(bits(rrow), 16)
        g = pltpu.bitcast(
            pltpu.roll(jnp.broadcast_to(word, (gr // 2, 2 * seq)), 0, 1, stride=2, stride_axis=0), bf16)
        for d in [0] + [e for e in range(-(n_i - 1), n_i) if e != 0]:
            n = n_i - abs(d)
            src = max(0, -d) * bsz
            dst = max(0, d) * bsz
            c0 = seq + d * tb
            m = jnp.concatenate([g[:, c0:c0 + tb], g[:, c0 - gr:c0 - gr + tb]], axis=0)
            part = _dot(u[src:src + n * bsz].astype(bf16), m)
            if d == 0:
                acc_scr[order] = part
            else:
                acc_scr[order, dst:dst + n * bsz, :] += part
        return acc_scr[order]

    def body(ch, carry):
        hv = short_conv(load(hv_ref, ch), 0, ch)
        hx1 = short_conv(load(h1_ref, ch), 1, ch)
        hx2 = short_conv(load(h2_ref, ch), 2, ch)
        zz = hx1 * long_conv(hv, 0, ch)
        out = hx2 * long_conv(zz, 1, ch)
        for ib in range(n_i):
            for k in range(sub):
                o_ref[ch, pl.ds(sub * ib + k, bsz, stride=sub * n_i), :] = (
                    out[ib * bsz:(ib + 1) * bsz, k * LANES:(k + 1) * LANES])
        return carry

    lax.fori_loop(0, cb_n, body, 0, unroll=4)


def hyena_mixer(hy_t, r, cw, cb, bsz, seq, cb_n=8):
    n_i = seq // HY_TB
    ch = MIX_W
    nblk = ch // cb_n
    n_rows = bsz * seq // LANES
    x3 = hy_t.reshape(3 * ch, n_rows, LANES)
    stream = lambda s: pl.BlockSpec((cb_n, n_rows, LANES), lambda j: (s * nblk + j, 0, 0))
    smem = pl.BlockSpec(memory_space=pltpu.SMEM)
    out = pl.pallas_call(
        functools.partial(_hyena_kernel, bsz=bsz, n_i=n_i, cb_n=cb_n),
        out_shape=jax.ShapeDtypeStruct((ch, n_rows, LANES), f32),
        grid=(nblk,),
        in_specs=[smem, smem, stream(0), stream(1), stream(2),
                  pl.BlockSpec((2, cb_n, 2 * seq), lambda j: (0, j, 0))],
        out_specs=pl.BlockSpec((cb_n, n_rows, LANES), lambda j: (j, 0, 0)),
        scratch_shapes=[pltpu.VMEM((2, bsz * n_i, HY_TB), f32)],
        compiler_params=_cparams("parallel"),
        name="hyena_mixer",
    )(cw.reshape(-1), cb, x3, x3, x3, r)
    return out.reshape(ch, bsz * seq)


_IN_SIZES = (NH * GLA_DK, NH * GLA_DK, MIX_W, MIX_W, 2 * GLA_RANK, MIX_W, MIX_W, MIX_W,
             MIX_W, MIX_W + 2 * SSD_G * SSD_N, 2 * NH, 3 * MIX_W)
_IN_OFFS = np.concatenate([[0], np.cumsum(_IN_SIZES)]).tolist()


def _prep_w_in(w_in):
    (gq, gk, gv, gg, gr, dq, dk, dv, sz, sxbc, sdt, hy) = [w_in[:, _IN_OFFS[i]:_IN_OFFS[i + 1]]
                                                          for i in range(len(_IN_SIZES))]
    d = w_in.shape[0]
    zeros = lambda n: jnp.zeros((d, n), w_in.dtype)
    misc = jnp.concatenate([sxbc[:, MIX_W:], gr, zeros(LANES - 2 * GLA_RANK), sdt, zeros(LANES - 2 * NH)], axis=1)
    w_tok = jnp.concatenate([gq, gk, gv, gg, dq, dk, dv, sz, sxbc[:, :MIX_W], misc], axis=1)
    w_tok = w_tok.astype(bf16).reshape(d, N_PT, PT).transpose(1, 0, 2)
    return w_tok, hy.T.astype(bf16)


def _prep_gate_w(gate_w):
    w = jnp.zeros((2, LANES, gate_w.shape[2]), f32)
    for d in range(2):
        w = w.at[d, d * GLA_RANK:(d + 1) * GLA_RANK, :].set(gate_w[d])
    hi = w.astype(bf16)
    return hi, (w - hi.astype(f32)).astype(bf16)


def _rope_tables(seq):
    rows = seq // GRID_W
    r, col = jnp.meshgrid(jnp.arange(rows), jnp.arange(GRID_W), indexing='ij')
    r = r.reshape(-1).astype(f32)
    col = col.reshape(-1).astype(f32)
    nf = DIFF_DK // 4
    inv = ROPE_BASE ** (-jnp.arange(nf, dtype=f32) / nf)
    ang = jnp.concatenate([r[:, None] * inv, col[:, None] * inv], axis=-1)
    cos, sin = jnp.cos(ang), jnp.sin(ang)
    reps = LANES // DIFF_DK
    return jnp.tile(jnp.concatenate([cos, cos], axis=-1), (1, reps)), jnp.tile(jnp.concatenate([-sin, sin], axis=-1), (1, reps))


def _const_tables():
    hsum = np.kron(np.eye(NH, dtype=np.float32), np.ones((HEAD_DIM, HEAD_DIM), np.float32))
    expand = np.zeros((LANES, MIX_W), np.float32)
    for h in range(NH):
        expand[h, h * HEAD_DIM:(h + 1) * HEAD_DIM] = 1.0
    return tuple(jnp.asarray(a, dtype=bf16) for a in (hsum, expand))


def _pack_state(s):
    bsz, _, h, a, b = s.shape
    return s.transpose(0, 1, 4, 2, 3).reshape(bsz, 2, b, h * a)


def _unpack_state(st, a):
    bsz, _, b, _ = st.shape
    return st.reshape(bsz, 2, b, NH, a).transpose(0, 1, 3, 4, 2)


def _layer(x, mod, rows_per_mod, lw, layer, consts, bsz, seq, lam_init, r_filt, ctx, rope_tabs, final_g,
           final_norm):
    hsum, expand = consts
    p, hy_t = in_proj(x, mod, lw['norm1_g'], lw['w_tok'], lw['w_hyT'], rows_per_mod)
    if ctx is None:
        gla_s0 = jnp.zeros((bsz, 2, HEAD_DIM, NH * GLA_DK), f32)
        ssd_s0 = jnp.zeros((bsz, 2, SSD_N, MIX_W), f32)
        dctx = None
    else:
        ctx_k, ctx_v, gla_s, ssd_s = ctx
        gla_s0 = _pack_state(gla_s)
        ssd_s0 = _pack_state(ssd_s)
        dctx = (ctx_k.reshape(bsz, -1, MIX_W), ctx_v.reshape(bsz, -1, MIX_W))
    o_gla, gla_fin = gla_mixer(p, lw['gla_gw_hi'], lw['gla_gw_lo'], lw['gla_gb'], lw['gla_ng'], hsum, gla_s0,
                               bsz, seq)
    o_diff = diff_mixer(p, lw['lam_p'], lw['diff_ng'], bsz, seq, lam_init, dctx, rope_tabs)
    o_ssd, ssd_fin = ssd_mixer(p, lw['ssd_conv_w'], lw['ssd_conv_b'], lw['ssd_dtb'], lw['ssd_alog'], lw['ssd_dexp'], lw['ssd_ng'],
                               expand, ssd_s0, bsz, seq)
    o_hy = hyena_mixer(hy_t, r_filt, lw['hy_cw'], lw['hy_cb'], bsz, seq)
    x = out_proj(x, mod, o_gla, o_diff, o_ssd, o_hy, lw['w_out'], layer, rows_per_mod)
    x = mlp(x, mod, lw['norm2_g'], lw['mlp_w1'], lw['mlp_w2'], layer, final_g, rows_per_mod, final_norm)
    return x, p, gla_fin, ssd_fin


def kernel(x_prompt, x_sample, cache_diff_k, cache_diff_v, state_gla, state_ssd, c, c_ctx, ada_w, ada_b, norm1_g, norm2_g, w_in, w_out, gla_gate_w, gla_gate_b, gla_norm_g, diff_lambda, diff_norm_g, ssd_conv_w, ssd_conv_b, ssd_dt_bias, ssd_a_log, ssd_d, ssd_norm_g, hy_conv_w, hy_conv_b, hy_f_w1, hy_f_b1, hy_f_w2, hy_f_b2, hy_f_w3, hy_f_b3, hy_sin_w, hy_decay, hy_skip, mlp_w1, mlp_w2, final_g):
    bp, lp, d = x_prompt.shape
    bs, ls, _ = x_sample.shape
    depth = w_in.shape[0]
    consts = _const_tables()
    rope_tabs = _rope_tables(ls)
    n_c = 16
    cvec = jnp.concatenate([c_ctx[None], c, jnp.zeros((n_c - 1 - bs, d), f32)], axis=0)
    hp = x_prompt.reshape(bp * lp, d)
    hs = x_sample.reshape(bs * ls, d)
    ks_, vs_, gs_, ss_ = [], [], [], []
    w_out_b, mlp_w1_b, mlp_w2_b = w_out.astype(bf16), mlp_w1.astype(bf16), mlp_w2.astype(bf16)
    for l in range(depth):
        w_tok, w_hyT = _prep_w_in(w_in[l])
        pad_l = lambda a: jnp.pad(a, (0, LANES - a.shape[0])).reshape(1, LANES)
        gw_hi, gw_lo = _prep_gate_w(gla_gate_w[l])
        lw = dict(
            norm1_g=norm1_g[l], norm2_g=norm2_g[l], w_tok=w_tok, w_hyT=w_hyT, w_out=w_out_b,
            mlp_w1=mlp_w1_b, mlp_w2=mlp_w2_b,
            gla_gw_hi=gw_hi, gla_gw_lo=gw_lo, gla_gb=gla_gate_b[l].reshape(2, 1, -1), gla_ng=jnp.tile(gla_norm_g[l], NH).reshape(1, MIX_W),
            lam_p=jnp.pad(diff_lambda[l], ((0, 4), (0, LANES - DIFF_DK))),
            diff_ng=jnp.tile(diff_norm_g[l], LANES // HEAD_DIM).reshape(1, LANES),
            ssd_conv_w=ssd_conv_w[l], ssd_conv_b=ssd_conv_b[l],
            ssd_dtb=pad_l(ssd_dt_bias[l].reshape(-1)), ssd_alog=pad_l(ssd_a_log[l].reshape(-1)),
            ssd_dexp=jnp.repeat(ssd_d[l], HEAD_DIM).reshape(1, MIX_W), ssd_ng=ssd_norm_g[l].reshape(1, MIX_W),
            hy_cw=hy_conv_w[l], hy_cb=hy_conv_b[l],
        )
        fp = dict(w1=hy_f_w1[l], b1=hy_f_b1[l], w2=hy_f_w2[l], b2=hy_f_b2[l], w3=hy_f_w3[l], b3=hy_f_b3[l],
                  sin_w=hy_sin_w[l], decay=hy_decay[l], skip=hy_skip[l])
        mod = ada_mod(cvec, ada_w, ada_b, l).reshape(n_c, 6, d)
        lam_init = 0.8 - 0.6 * math.exp(-0.3 * l)
        last = l == depth - 1
        hp, p_p, g_l, s_l = _layer(hp, mod[0:1], bp * lp, lw, l, consts, bp, lp, lam_init, hy_filters(fp, lp),
                                   None, None, final_g, last)
        ks_.append(p_p[T_DK].astype(f32).reshape(bp, lp, NH, 2 * DIFF_DK))
        vs_.append(p_p[T_DV].astype(f32).reshape(bp, lp, NH, HEAD_DIM))
        gs_.append(_unpack_state(g_l, GLA_DK))
        ss_.append(_unpack_state(s_l, HEAD_DIM))
        hs, _, _, _ = _layer(hs, mod[1:1 + bs], ls, lw, l, consts, bs, ls, lam_init, hy_filters(fp, ls),
                             (cache_diff_k[:, l], cache_diff_v[:, l], state_gla[:, l], state_ssd[:, l]),
                             rope_tabs, final_g, last)
    return (hp.reshape(bp, lp, d), hs.reshape(bs, ls, d), jnp.stack(ks_, axis=1), jnp.stack(vs_, axis=1),
            jnp.stack(gs_, axis=1), jnp.stack(ss_, axis=1))
```

```python
import functools
import math

import numpy as np
import jax
import jax.numpy as jnp
from jax import lax
from jax.experimental import pallas as pl
from jax.experimental.pallas import tpu as pltpu

f32 = jnp.float32
bf16 = jnp.bfloat16
HI = lax.Precision.HIGHEST

LANES = 128
VMEM_LIMIT = 56 * 2**20

D_MODEL = 2048
GRID_W = 64
MIX_W = D_MODEL // 4
HEAD_DIM = 64
NH = MIX_W // HEAD_DIM
D_FF = 4 * D_MODEL
NORM_EPS = 1e-6
LOG2E = 1.0 / math.log(2.0)
GLA_DK = HEAD_DIM // 2
GLA_RANK = 16
GLA_TAU = 16.0
GLA_CHUNK = 128
DIFF_DK = HEAD_DIM // 2
ROPE_BASE = 10000.0
SSD_N = 64
SSD_G = 2
SSD_CHUNK = 128
HY_BANDS = 16
HY_EMB = 2 * HY_BANDS + 1
HY_EMB_PAD = 40
HY_HID = 64
DIFF_KEY_CHUNK = 512
DIFF_Q_SUB = 512
HALO = 16
MLP_TF = 1024
HY_TB = 256

PT = 512
(T_GQK, T_GV, T_GG, T_DQ, T_DK, T_DV, T_SZ, T_SX, T_MISC) = range(9)
N_PT = 9
PT_GROUP = 3
MISC_GR = 256
MISC_DT = 384
SH1, SC1, G1, SH2, SC2, G2 = range(6)

NT_DIMS = (((1,), (1,)), ((), ()))
TN_DIMS = (((0,), (0,)), ((), ()))


def _cparams(*sem):
    return pltpu.CompilerParams(dimension_semantics=sem, vmem_limit_bytes=VMEM_LIMIT)


def _sigmoid(x):
    return 1.0 / (1.0 + jnp.exp(-x))


def _silu(x):
    return x * _sigmoid(x)


def _softplus(x):
    return jnp.maximum(x, 0.0) + jnp.log1p(jnp.exp(-jnp.abs(x)))


def _log_sigmoid(x):
    return jnp.minimum(x, 0.0) - jnp.log1p(jnp.exp(-jnp.abs(x)))


def _tri_mask(rows, c, sgn):
    r_i = lax.broadcasted_iota(jnp.int32, (rows, c), 0) & (c - 1)
    c_i = lax.broadcasted_iota(jnp.int32, (rows, c), 1)
    return (r_i - c_i) * sgn >= 0


def _expand_state(s, row_w, col_w):
    rows, cols = NH * row_w, NH * col_w
    bd = (lax.broadcasted_iota(jnp.int32, (rows, cols), 0) // row_w
          == lax.broadcasted_iota(jnp.int32, (rows, cols), 1) // col_w)
    return jnp.where(bd, jnp.concatenate([s] * NH, axis=0), 0.0)


def _compact_state(st):
    row_w = st.shape[0] // NH
    out = st[0:row_w]
    for h in range(1, NH):
        out += st[h * row_w:(h + 1) * row_w]
    return out


def _split(x, n):
    parts = []
    for _ in range(n - 1):
        hi = x.astype(bf16)
        parts.append(hi)
        x = x - hi.astype(f32)
    parts.append(x.astype(bf16))
    return parts


def _sel_dot(a, b, dims=None, terms=3):
    if a.dtype == bf16:
        prods = [_dot(a, p, dims) for p in _split(b, terms)]
    else:
        prods = [_dot(p, b, dims) for p in _split(a, terms)]
    return functools.reduce(lambda x, y: x + y, prods)


def _per_head_groups(g2):
    assert SSD_G == 2 and SSD_N == HEAD_DIM and LANES == 2 * SSD_N
    first = lax.broadcasted_iota(jnp.int32, (1, LANES), 1) < SSD_N
    swapped = pltpu.roll(g2, SSD_N, 1)
    g0 = jnp.where(first, g2, swapped)
    g1 = jnp.where(first, swapped, g2)
    rep = NH * HEAD_DIM // (SSD_G * LANES)
    return jnp.concatenate([g0] * rep + [g1] * rep, axis=1)


def _dot(a, b, dims=None, precision=None):
    if dims is None:
        return jnp.dot(a, b, preferred_element_type=f32, precision=precision)
    return lax.dot_general(a, b, dims, preferred_element_type=f32, precision=precision)


def _ada_kernel(c_ref, w_ref, b_ref, o_ref):
    s = _silu(c_ref[...]).astype(bf16)
    o_ref[...] = _dot(s, w_ref[...].astype(bf16)) + b_ref[...]


def ada_mod(cvec, w, b, layer, tn=1024):
    m, d = cvec.shape
    n = w.shape[2]
    return pl.pallas_call(
        _ada_kernel,
        out_shape=jax.ShapeDtypeStruct((m, n), f32),
        grid=(n // tn,),
        in_specs=[pl.BlockSpec((m, d), lambda j: (0, 0)),
                  pl.BlockSpec((None, d, tn), lambda j: (layer, 0, j)),
                  pl.BlockSpec((None, 1, tn), lambda j: (layer, 0, j))],
        out_specs=pl.BlockSpec((m, tn), lambda j: (0, j)),
        compiler_params=_cparams("arbitrary"),
        name="ada_mod",
    )(cvec, w, b.reshape(b.shape[0], 1, n))


def _norm_mod(x, g, sc, sh):
    ms = jnp.mean(x * x, axis=-1, keepdims=True)
    return (x * lax.rsqrt(ms + NORM_EPS) * g) * (1.0 + sc) + sh


def _norm_next_chunk(xn_ref, modn_ref, g_ref, sc, sh, u_scr, slot, row0):
    xn = xn_ref[...]
    u = _norm_mod(xn, g_ref[...], modn_ref[sc:sc + 1, :], modn_ref[sh:sh + 1, :])
    u_scr[slot, pl.ds(row0, xn.shape[0]), :] = u.astype(bf16)


def _inproj_kernel(x_ref, mod_ref, g_ref, wt_ref, wh_ref, p_ref, h_ref, u_scr):
    j = pl.program_id(1)

    @pl.when(j == 0)
    def _():
        u = _norm_mod(x_ref[...], g_ref[...], mod_ref[SC1:SC1 + 1, :], mod_ref[SH1:SH1 + 1, :])
        u_scr[...] = u.astype(bf16)

    n_grp = N_PT // PT_GROUP

    @pl.when(j < n_grp)
    def _():
        for k in range(PT_GROUP):
            p_ref[k] = _dot(u_scr[...], wt_ref[k]).astype(p_ref.dtype)

    @pl.when(j >= n_grp)
    def _():
        h_ref[...] = _dot(wh_ref[...], u_scr[...], NT_DIMS)


def in_proj(x, mod, g, w_tok, w_hyT, rows_per_mod, tm=1024):
    t, d = x.shape
    assert t % tm == 0 and rows_per_mod % tm == 0, (t, rows_per_mod, tm)
    n_hy = w_hyT.shape[0] // PT
    n_grp = N_PT // PT_GROUP
    return pl.pallas_call(
        _inproj_kernel,
        out_shape=(jax.ShapeDtypeStruct((N_PT, t, PT), bf16), jax.ShapeDtypeStruct((n_hy * PT, t), f32)),
        grid=(t // tm, n_grp + n_hy),
        in_specs=[pl.BlockSpec((tm, d), lambda i, j: (i, 0)),
                  pl.BlockSpec((None, 6, d), lambda i, j: ((i * tm) // rows_per_mod, 0, 0)),
                  pl.BlockSpec((1, d), lambda i, j: (0, 0)),
                  pl.BlockSpec((PT_GROUP, d, PT), lambda i, j: (jnp.minimum(j, n_grp - 1), 0, 0)),
                  pl.BlockSpec((PT, d), lambda i, j: (jnp.maximum(j - n_grp, 0), 0))],
        out_specs=(pl.BlockSpec((PT_GROUP, tm, PT), lambda i, j: (jnp.minimum(j, n_grp - 1), i, 0)),
                   pl.BlockSpec((PT, tm), lambda i, j: (jnp.maximum(j - n_grp, 0), i))),
        scratch_shapes=[pltpu.VMEM((tm, d), bf16)],
        compiler_params=_cparams("parallel", "arbitrary"),
        name="in_proj",
    )(x, mod, g.reshape(1, d), w_tok, w_hyT)


def _outproj_kernel(x_ref, mod_ref, a_ref, b_ref, c_ref, dT_ref, w_ref, o_ref):
    acc = _dot(a_ref[...], w_ref[0:MIX_W, :])
    acc += _dot(b_ref[...], w_ref[MIX_W:2 * MIX_W, :])
    acc += _dot(c_ref[...], w_ref[2 * MIX_W:3 * MIX_W, :])
    acc += _dot(dT_ref[...].astype(bf16), w_ref[3 * MIX_W:4 * MIX_W, :], TN_DIMS)
    o_ref[...] = x_ref[...] + mod_ref[G1:G1 + 1, :] * acc


def out_proj(x, mod, o_a, o_b, o_c, o_dT, w_out, layer, rows_per_mod, tm=512):
    t, d = x.shape
    assert t % tm == 0 and rows_per_mod % tm == 0, (t, rows_per_mod, tm)
    tok = lambda i: (i, 0)
    return pl.pallas_call(
        _outproj_kernel,
        out_shape=jax.ShapeDtypeStruct((t, d), f32),
        grid=(t // tm,),
        in_specs=[pl.BlockSpec((tm, d), tok),
                  pl.BlockSpec((None, 6, d), lambda i: ((i * tm) // rows_per_mod, 0, 0)),
                  pl.BlockSpec((tm, MIX_W), tok), pl.BlockSpec((tm, MIX_W), tok), pl.BlockSpec((tm, MIX_W), tok),
                  pl.BlockSpec((MIX_W, tm), lambda i: (0, i)),
                  pl.BlockSpec((None, d, d), lambda i: (layer, 0, 0))],
        out_specs=pl.BlockSpec((tm, d), tok),
        compiler_params=_cparams("parallel"),
        name="out_proj",
    )(x, mod, o_a, o_b, o_c, o_dT, w_out)


def _mlp_kernel(xc_ref, mod_ref, xn_ref, modn_ref, g_ref, w1_ref, w2_ref, fg_ref, o_ref, u_scr, *, final_norm):
    io = pl.program_id(0)
    j = pl.program_id(1)
    rows = xn_ref.shape[0]
    row0 = pl.multiple_of(j * rows, rows)

    @pl.when(io == 0)
    def _():
        _norm_next_chunk(xn_ref, modn_ref, g_ref, SC2, SH2, u_scr, 0, row0)

    @pl.when(jnp.logical_and(io > 0, j == 0))
    def _():
        o_ref[...] = jnp.zeros_like(o_ref)

    @pl.when(io > 0)
    def _():
        _norm_next_chunk(xn_ref, modn_ref, g_ref, SC2, SH2, u_scr, io % 2, row0)
        h = jnp.maximum(_dot(u_scr[1 - io % 2], w1_ref[...]), 0.0)
        hb = (h * h).astype(bf16)
        d = o_ref.shape[1]
        for n in range(d // PT):
            cols = slice(n * PT, (n + 1) * PT)
            o_ref[:, cols] += mod_ref[G2:G2 + 1, cols] * _dot(hb, w2_ref[:, cols])
        o_ref[pl.ds(row0, rows), :] += xc_ref[...]

    if final_norm:
        @pl.when(jnp.logical_and(io > 0, j == pl.num_programs(1) - 1))
        def _():
            y = o_ref[...]
            ms = jnp.mean(y * y, axis=-1, keepdims=True)
            o_ref[...] = y * lax.rsqrt(ms + NORM_EPS) * fg_ref[...]


def mlp(x, mod, g, w1, w2, layer, final_g, rows_per_mod, final_norm, tm=1024, tf=MLP_TF):
    t, d = x.shape
    assert t % tm == 0 and rows_per_mod % tm == 0, (t, rows_per_mod, tm)
    n_f = w1.shape[2] // tf
    rows = tm // n_f
    n_tiles = t // tm
    tile = lambda io: jnp.maximum(io - 1, 0)
    chunk_of = lambda ti, j: jnp.minimum(ti * n_f + j, t // rows - 1)
    mod_of = lambda ti: jnp.minimum((ti * tm) // rows_per_mod, mod.shape[0] - 1)
    return pl.pallas_call(
        functools.partial(_mlp_kernel, final_norm=final_norm),
        out_shape=jax.ShapeDtypeStruct((t, d), f32),
        grid=(n_tiles + 1, n_f),
        in_specs=[pl.BlockSpec((rows, d), lambda io, j: (chunk_of(tile(io), j), 0)),
                  pl.BlockSpec((None, 6, d), lambda io, j: (mod_of(tile(io)), 0, 0)),
                  pl.BlockSpec((rows, d), lambda io, j: (chunk_of(jnp.minimum(io, n_tiles - 1), j), 0)),
                  pl.BlockSpec((None, 6, d), lambda io, j: (mod_of(jnp.minimum(io, n_tiles - 1)), 0, 0)),
                  pl.BlockSpec((1, d), lambda io, j: (0, 0)),
                  pl.BlockSpec((None, d, tf), lambda io, j: (layer, 0, j)),
                  pl.BlockSpec((None, tf, d), lambda io, j: (layer, j, 0)),
                  pl.BlockSpec((1, d), lambda io, j: (0, 0))],
        out_specs=pl.BlockSpec((tm, d), lambda io, j: (tile(io), 0)),
        scratch_shapes=[pltpu.VMEM((2, tm, d), bf16)],
        compiler_params=_cparams("arbitrary", "arbitrary"),
        name="mlp",
    )(x, mod, x, mod, g.reshape(1, d), w1, w2, final_g.reshape(1, d))


def _gla_kernel(qk_ref, v_ref, gg_ref, misc_ref, gwh_ref, gwl_ref, gb_ref, ng_ref, hsum_ref, s0_ref,
                o_ref, sfin_ref, of_scr, st_scr, la_scr, *, nblk, tl):
    ph = pl.program_id(1)
    i = pl.program_id(2)
    fwd = ph == 0
    c = GLA_CHUNK
    nch = tl // c
    qkw = NH * GLA_DK

    @pl.when(i == 0)
    def _():
        st_scr[...] = _expand_state(s0_ref[...], HEAD_DIM, GLA_DK)

    gr = misc_ref[:, MISC_GR:MISC_GR + LANES]
    logits = (_dot(gr, gwh_ref[...]) + _dot(gr, gwl_ref[...])) + gb_ref[...]
    la_scr[...] = _log_sigmoid(logits) * (1.0 / GLA_TAU)

    sgn = jnp.where(fwd, 1, -1)
    trib = jnp.where(_tri_mask(c, c, sgn), 1.0, 0.0).astype(bf16)
    tri8 = _tri_mask(NH * c, c, sgn)
    qk_head = lax.broadcasted_iota(jnp.int32, (1, qkw), 1) // GLA_DK
    v_head = lax.broadcasted_iota(jnp.int32, (1, MIX_W), 1) // HEAD_DIM
    bd = (lax.broadcasted_iota(jnp.int32, (MIX_W, qkw), 0) // HEAD_DIM
          == lax.broadcasted_iota(jnp.int32, (MIX_W, qkw), 1) // GLA_DK)
    brow = pl.multiple_of(jnp.where(fwd, i, nblk - 1 - i) * tl, tl)

    def chunk(j, carry):
        cj = jnp.where(fwd, j, nch - 1 - j)
        off = pl.multiple_of(cj * c, c)
        q = qk_ref[pl.ds(off, c), 0:qkw].astype(f32) * (GLA_DK ** -0.5)
        k = qk_ref[pl.ds(off, c), qkw:2 * qkw].astype(f32)
        vb = v_ref[pl.ds(off, c), :]
        b = _sel_dot(trib, la_scr[pl.ds(off, c), :])
        btot = jnp.where(fwd, b[c - 1:c, :], b[0:1, :])
        bmid = b[c // 2 - 1:c // 2, :]
        qd = q * jnp.exp(b)
        qi = q * jnp.exp(b - bmid)
        kd = (k * jnp.exp(bmid - b)).astype(bf16)
        kt = (k * jnp.exp(btot - b)).astype(bf16)
        qst = jnp.concatenate([jnp.where(qk_head == h, qi, 0.0) for h in range(NH)], axis=0).astype(bf16)
        att = jnp.where(tri8, _dot(qst, kd, NT_DIMS), 0.0)
        r = _dot(att.astype(bf16), vb)
        o = jnp.where(v_head == 0, r[0:c], 0.0)
        for h in range(1, NH):
            o += jnp.where(v_head == h, r[h * c:(h + 1) * c], 0.0)
        st = st_scr[...]
        o += _dot(qd.astype(bf16), st.astype(bf16), NT_DIMS)
        kv = _dot(vb, kt, TN_DIMS)
        st_scr[...] = st * jnp.exp(btot) + jnp.where(bd, kv, 0.0)
        of_scr[pl.ds(pl.multiple_of(brow + off, c), c), :] += o
        return carry

    @pl.when(fwd)
    def _():
        of_scr[pl.ds(brow, tl), :] = jnp.zeros((tl, MIX_W), f32)

    lax.fori_loop(0, nch, chunk, 0, unroll=True)

    @pl.when(ph == 1)
    def _():
        ot = of_scr[pl.ds(brow, tl), :]
        ss = _sel_dot(ot * ot, hsum_ref[...], terms=2) * (1.0 / HEAD_DIM)
        o_ref[...] = (ot * lax.rsqrt(ss + NORM_EPS) * ng_ref[...] * _silu(gg_ref[...].astype(f32))).astype(bf16)

    @pl.when(i == nblk - 1)
    def _():
        sfin_ref[...] = _compact_state(st_scr[...])


def gla_mixer(p, gw_hi, gw_lo, gb, ng, hsum, s0, bsz, seq, tl=1024):
    tl = min(tl, seq)
    nblk = seq // tl
    qkw = NH * GLA_DK
    blk = lambda ph, i: jnp.where(ph == 0, i, nblk - 1 - i)
    tile = lambda col: pl.BlockSpec((None, tl, PT), lambda b, ph, i: (col, b * nblk + blk(ph, i), 0))
    const = lambda shape: pl.BlockSpec(shape, lambda b, ph, i: (0,) * len(shape))
    per_dir = lambda rows: pl.BlockSpec((None, rows, qkw), lambda b, ph, i: (ph, 0, 0))
    state = pl.BlockSpec((None, None, HEAD_DIM, qkw), lambda b, ph, i: (b, ph, 0, 0))
    return pl.pallas_call(
        functools.partial(_gla_kernel, nblk=nblk, tl=tl),
        out_shape=(jax.ShapeDtypeStruct((bsz * seq, MIX_W), bf16),
                   jax.ShapeDtypeStruct((bsz, 2, HEAD_DIM, qkw), f32)),
        grid=(bsz, 2, nblk),
        in_specs=[tile(T_GQK), tile(T_GV), tile(T_GG), tile(T_MISC),
                  per_dir(LANES), per_dir(LANES), per_dir(1), const((1, MIX_W)), const((MIX_W, MIX_W)), state],
        out_specs=(pl.BlockSpec((tl, MIX_W),
                                lambda b, ph, i: (b * nblk + jnp.where(ph == 0, nblk - 1, nblk - 1 - i), 0)),
                   state),
        scratch_shapes=[pltpu.VMEM((seq, MIX_W), f32), pltpu.VMEM((MIX_W, qkw), f32), pltpu.VMEM((tl, qkw), f32)],
        compiler_params=_cparams("parallel", "arbitrary", "arbitrary"),
        name="gla_mixer",
    )(p, p, p, p, gw_hi, gw_lo, gb, ng, hsum, s0)


def _rope(x, cos, sin_signed):
    lane = lax.broadcasted_iota(jnp.int32, (1, LANES), 1)
    half = DIFF_DK // 2
    rot = jnp.where((lane % DIFF_DK) < half, pltpu.roll(x, LANES - half, 1), pltpu.roll(x, half, 1))
    return x * cos + rot * sin_signed


def _diff_kernel(*refs, seq, ctx_len, rope, lam_init):
    if rope:
        (q_ref, k_ref, v_ref, ck_ref, cv_ref, cq_ref, sq_ref, ckk_ref, skk_ref, lam_ref, ng_ref,
         o_ref, kb_scr, vt_scr) = refs
    else:
        q_ref, k_ref, v_ref, lam_ref, ng_ref, o_ref, kb_scr, vt_scr = refs
    qi = pl.program_id(2)
    lane = lax.broadcasted_iota(jnp.int32, (1, LANES), 1)
    first = lane < HEAD_DIM

    def put_v(v, lo, hi):
        vt = v.T
        ones = jnp.ones((HEAD_DIM, hi - lo), f32)
        vt_scr[0, :, lo:hi] = jnp.concatenate([vt[0:HEAD_DIM], ones], axis=0).astype(bf16)
        vt_scr[1, :, lo:hi] = jnp.concatenate([vt[HEAD_DIM:], ones], axis=0).astype(bf16)

    @pl.when(qi == 0)
    def _():
        k = k_ref[...].astype(f32)
        if rope:
            k = _rope(k, ckk_ref[...], skk_ref[...])
        kb_scr[0:seq, :] = k.astype(bf16)
        put_v(v_ref[...].astype(f32), 0, seq)
        if ctx_len:
            kb_scr[seq:seq + ctx_len, :] = ck_ref[...].astype(bf16)
            put_v(cv_ref[...], seq, seq + ctx_len)

    q = q_ref[...].astype(f32)
    if rope:
        q = _rope(q, cq_ref[...], sq_ref[...])
    q = q * (DIFF_DK ** -0.5 * LOG2E)
    n_keys = seq + ctx_len
    ck = max(w for w in range(LANES, DIFF_KEY_CHUNK + 1, LANES) if n_keys % w == 0)
    nck = n_keys // ck
    tq = q.shape[0]
    sub = min(DIFF_Q_SUB, tq)
    jobs = [(qs, g) for qs in range(tq // sub) for g in range(4)]
    lp = lam_ref[...]
    lam = (jnp.exp(jnp.sum(lp[0:1] * lp[1:2], keepdims=True)) - jnp.exp(jnp.sum(lp[2:3] * lp[3:4], keepdims=True))
           + lam_init)

    def finish(qs, res):
        o = jnp.concatenate([res[0] - lam * res[1], res[2] - lam * res[3]], axis=0).T
        o2 = o * o
        ss_a = jnp.sum(jnp.where(first, o2, 0.0), axis=-1, keepdims=True)
        ss_b = jnp.sum(jnp.where(first, 0.0, o2), axis=-1, keepdims=True)
        inv = jnp.where(first, lax.rsqrt(ss_a * (1.0 / HEAD_DIM) + NORM_EPS),
                        lax.rsqrt(ss_b * (1.0 / HEAD_DIM) + NORM_EPS))
        o_ref[qs * sub:(qs + 1) * sub, :] = (o * inv * ng_ref[...] * (1.0 - lam_init)).astype(bf16)

    res = []
    s_prev = m_prev = None
    for j in range(len(jobs) + 1):
        s_cur, mvec, acc = [], None, None
        if j < len(jobs):
            qs, g = jobs[j]
            qm = jnp.where(lane // DIFF_DK == g, q[qs * sub:(qs + 1) * sub], 0.0).astype(bf16)
        for c in range(nck):
            if j < len(jobs):
                s = _dot(qm, kb_scr[c * ck:(c + 1) * ck, :], NT_DIMS)
                s_cur.append(s)
                for k in range(ck // LANES):
                    blk = s[:, k * LANES:(k + 1) * LANES]
                    mvec = blk if mvec is None else jnp.maximum(mvec, blk)
            if j > 0:
                p = jnp.exp2(s_prev[c] - m_prev).astype(bf16)
                part = _dot(vt_scr[jobs[j - 1][1] // 2, :, c * ck:(c + 1) * ck], p, NT_DIMS)
                acc = part if acc is None else acc + part
        if j > 0:
            res.append(acc[0:HEAD_DIM] / acc[HEAD_DIM:HEAD_DIM + 1])
            if len(res) == 4:
                finish(jobs[j - 1][0], res)
                res = []
        if j < len(jobs):
            s_prev, m_prev = s_cur, jnp.max(mvec, axis=-1, keepdims=True)


def diff_mixer(p, lam_p, ng, bsz, seq, lam_init, ctx=None, rope_tabs=None, tq=1024):
    tq = min(tq, seq)
    nq = seq // tq
    npair = MIX_W // LANES
    rope = rope_tabs is not None
    ctx_len = ctx[0].shape[1] if ctx is not None else 0
    assert rope == (ctx is not None)
    q_spec = pl.BlockSpec((None, tq, LANES), lambda b, hp, i: (T_DQ, b * nq + i, hp))
    k_spec = pl.BlockSpec((None, seq, LANES), lambda b, hp, i: (T_DK, b, hp))
    v_spec = pl.BlockSpec((None, seq, LANES), lambda b, hp, i: (T_DV, b, hp))
    const = lambda shape: pl.BlockSpec(shape, lambda b, hp, i: (0,) * len(shape))
    args = [p, p, p]
    specs = [q_spec, k_spec, v_spec]
    if rope:
        c_spec = pl.BlockSpec((None, ctx_len, LANES), lambda b, hp, i: (b, 0, hp))
        tab_q = pl.BlockSpec((tq, LANES), lambda b, hp, i: (i, 0))
        args += [ctx[0], ctx[1], rope_tabs[0], rope_tabs[1], rope_tabs[0], rope_tabs[1]]
        specs += [c_spec, c_spec, tab_q, tab_q, const((seq, LANES)), const((seq, LANES))]
    args += [lam_p, ng]
    specs += [const((8, LANES)), const((1, LANES))]
    return pl.pallas_call(
        functools.partial(_diff_kernel, seq=seq, ctx_len=ctx_len, rope=rope, lam_init=lam_init),
        out_shape=jax.ShapeDtypeStruct((bsz * seq, MIX_W), bf16),
        grid=(bsz, npair, nq),
        in_specs=specs,
        out_specs=pl.BlockSpec((tq, LANES), lambda b, hp, i: (b * nq + i, hp)),
        scratch_shapes=[pltpu.VMEM((seq + ctx_len, LANES), bf16), pltpu.VMEM((2, LANES, seq + ctx_len), bf16)],
        compiler_params=_cparams("parallel", "parallel", "arbitrary"),
        name="diff_mixer",
    )(*args)


def _ssd_kernel(sx_ref, bcr_ref, sxp_ref, bcp_ref, sxn_ref, bcn_ref, cw_ref, cb_ref, z_ref, dt_ref, dtb_ref, alog_ref,
                dexp_ref, ng_ref, exp_ref, s0_ref, o_ref, sfin_ref,
                yf_scr, st_scr, dt_scr, dta_scr, xs_ref, bc_ref, *, nblk, tl):
    ph = pl.program_id(1)
    i = pl.program_id(2)
    fwd = ph == 0
    c = SSD_CHUNK
    nch = tl // c
    rep = NH // SSD_G
    blk = jnp.where(fwd, i, nblk - 1 - i)

    @pl.when(i == 0)
    def _():
        st_scr[...] = _expand_state(s0_ref[...], SSD_N, HEAD_DIM)

    brow = pl.multiple_of(blk * tl, tl)

    @pl.when(fwd)
    def _():
        row = lax.broadcasted_iota(jnp.int32, (tl, 1), 0)
        for raw_ref, prev_ref, next_ref, dst_ref, lo in ((sx_ref, sxp_ref, sxn_ref, xs_ref, 0),
                                                        (bcr_ref, bcp_ref, bcn_ref, bc_ref, MIX_W)):
            x = raw_ref[...].astype(f32)
            w = cw_ref[:, lo:lo + x.shape[1]]
            prev_row = jnp.where(blk == 0, 0.0, prev_ref[HALO - 1:HALO, :].astype(f32))
            next_row = jnp.where(blk == nblk - 1, 0.0, next_ref[0:1, :].astype(f32))
            xm = jnp.where(row == 0, prev_row, pltpu.roll(x, 1, 0))
            xp = jnp.where(row == tl - 1, next_row, pltpu.roll(x, tl - 1, 0))
            dst_ref[pl.ds(brow, tl), :] = _silu(xm * w[0:1] + x * w[1:2] + xp * w[2:3]
                                                + cb_ref[:, lo:lo + x.shape[1]])

    dtv = _softplus(dt_ref[...].astype(f32) + dtb_ref[...])
    dtv = jnp.where(fwd, dtv, pltpu.roll(dtv, LANES - NH, 1))
    av = -jnp.exp(alog_ref[...])
    av = jnp.where(fwd, av, pltpu.roll(av, LANES - NH, 1))
    dt_scr[...] = dtv
    dta_scr[...] = dtv * av

    tri = _tri_mask(c, c, jnp.where(fwd, 1, -1))
    trib = jnp.where(tri, 1.0, 0.0).astype(bf16)
    head = lax.broadcasted_iota(jnp.int32, (1, MIX_W), 1) // HEAD_DIM
    bd = (lax.broadcasted_iota(jnp.int32, (MIX_W, MIX_W), 0) // SSD_N
          == lax.broadcasted_iota(jnp.int32, (MIX_W, MIX_W), 1) // HEAD_DIM)
    def chunk(j, carry):
        cj = jnp.where(fwd, j, nch - 1 - j)
        off = pl.multiple_of(cj * c, c)
        srow = pl.multiple_of(brow + off, c)
        dtc = dt_scr[pl.ds(off, c), :]
        cum = _sel_dot(trib, dta_scr[pl.ds(off, c), :])
        cum_t = cum.T
        dt_t = dtc.T
        earg = _sel_dot(cum, exp_ref[...])
        elast = jnp.where(fwd, earg[c - 1:c, :], earg[0:1, :])
        dtx = _sel_dot(dtc, exp_ref[...])
        bt = _per_head_groups(bc_ref[pl.ds(srow, c), 0:LANES])
        ct = _per_head_groups(bc_ref[pl.ds(srow, c), LANES:2 * LANES])
        x = xs_ref[pl.ds(srow, c), :]
        xb = x.astype(bf16)
        st = st_scr[...]
        y = _dot((ct * jnp.exp(earg)).astype(bf16), st.astype(bf16))
        btail = (bt * jnp.exp(elast - earg) * dtx).astype(bf16)
        new = _dot(btail, xb, TN_DIMS)
        st_scr[...] = st * jnp.exp(elast) + jnp.where(bd, new, 0.0)
        btb = bt.astype(bf16)
        cb = [_dot(jnp.where(head == g * rep, ct, 0.0).astype(bf16), btb, NT_DIMS) for g in range(SSD_G)]
        ws = []
        for h in range(NH):
            seg = cum[:, h:h + 1] - cum_t[h:h + 1, :]
            lm = jnp.exp(jnp.where(tri, seg, -jnp.inf))
            ws.append((cb[h // rep] * lm * dt_t[h:h + 1, :]).astype(bf16))
        r = _dot(jnp.concatenate(ws, axis=0), xb)
        for h in range(NH):
            y += jnp.where(head == h, r[h * c:(h + 1) * c], 0.0)
        yf_scr[pl.ds(pl.multiple_of(brow + off, c), c), :] += y
        return carry

    @pl.when(fwd)
    def _():
        yf_scr[pl.ds(brow, tl), :] = jnp.zeros((tl, MIX_W), f32)

    lax.fori_loop(0, nch, chunk, 0, unroll=True)

    @pl.when(ph == 1)
    def _():
        yt = yf_scr[pl.ds(brow, tl), :] + xs_ref[pl.ds(brow, tl), :] * dexp_ref[...]
        t = yt * _silu(z_ref[...].astype(f32))
        ms = jnp.mean(t * t, axis=-1, keepdims=True)
        o_ref[...] = (t * lax.rsqrt(ms + NORM_EPS) * ng_ref[...]).astype(bf16)

    @pl.when(i == nblk - 1)
    def _():
        sfin_ref[...] = _compact_state(st_scr[...])


def ssd_mixer(p, cw, cb, dtb, alog, dexp, ng, expand, s0, bsz, seq, tl=1024):
    tl = min(tl, seq)
    nblk = seq // tl
    n8 = bsz * seq // HALO
    per8 = tl // HALO
    blk = lambda ph, i: jnp.where(ph == 0, i, nblk - 1 - i)
    rows = lambda b, ph, i: b * nblk + blk(ph, i)
    const = lambda shape: pl.BlockSpec(shape, lambda b, ph, i: (0,) * len(shape))
    state = pl.BlockSpec((None, None, SSD_N, MIX_W), lambda b, ph, i: (b, ph, 0, 0))
    body = lambda tile, w: pl.BlockSpec((None, tl, w), lambda b, ph, i: (tile, rows(b, ph, i), 0))
    crow = lambda b, ph, i: jnp.where(ph == 0, rows(b, ph, i), b * nblk + nblk - 1)
    cbody = lambda tile, w: pl.BlockSpec((None, tl, w), lambda b, ph, i: (tile, crow(b, ph, i), 0))
    prev8 = lambda tile, w: pl.BlockSpec(
        (None, HALO, w), lambda b, ph, i: (tile, jnp.maximum(crow(b, ph, i) * per8 - 1, 0), 0))
    next8 = lambda tile, w: pl.BlockSpec(
        (None, HALO, w), lambda b, ph, i: (tile, jnp.minimum((crow(b, ph, i) + 1) * per8, n8 - 1), 0))
    bcw = 2 * SSD_G * SSD_N
    return pl.pallas_call(
        functools.partial(_ssd_kernel, nblk=nblk, tl=tl),
        out_shape=(jax.ShapeDtypeStruct((bsz * seq, MIX_W), bf16),
                   jax.ShapeDtypeStruct((bsz, 2, SSD_N, MIX_W), f32)),
        grid=(bsz, 2, nblk),
        in_specs=[cbody(T_SX, PT), cbody(T_MISC, bcw), prev8(T_SX, PT), prev8(T_MISC, bcw),
                  next8(T_SX, PT), next8(T_MISC, bcw), const((3, MIX_W + bcw)), const((1, MIX_W + bcw)),
                  body(T_SZ, PT),
                  pl.BlockSpec((None, tl, LANES), lambda b, ph, i: (T_MISC, rows(b, ph, i), MISC_DT // LANES)),
                  const((1, LANES)), const((1, LANES)), const((1, MIX_W)), const((1, MIX_W)),
                  const((LANES, MIX_W)), state],
        out_specs=(pl.BlockSpec((tl, MIX_W),
                                lambda b, ph, i: (b * nblk + jnp.where(ph == 0, nblk - 1, nblk - 1 - i), 0)),
                   state),
        scratch_shapes=[pltpu.VMEM((seq, MIX_W), f32), pltpu.VMEM((MIX_W, MIX_W), f32),
                        pltpu.VMEM((tl, LANES), f32), pltpu.VMEM((tl, LANES), f32),
                        pltpu.VMEM((seq, MIX_W), f32), pltpu.VMEM((seq, bcw), f32)],
        compiler_params=_cparams("parallel", "arbitrary", "arbitrary"),
        name="ssd_mixer",
    )(p, p, p, p, p, p, cw, cb.reshape(1, -1), p, p, dtb, alog, dexp, ng, expand, s0)


def _hy_filter_kernel(zf_ref, zb_ref, tf_ref, tb_ref, w1_ref, b1_ref, sw_ref, w2_ref, b2_ref,
                      w3f_ref, w3b_ref, b3f_ref, b3b_ref, df_ref, db_ref, skip_ref, r_ref, hf_scr, hb_scr, *, seq):
    first = jnp.logical_and(pl.program_id(0) == 0, pl.program_id(1) == 0)

    @pl.when(first)
    def _():
        for z_ref, h_scr in ((zf_ref, hf_scr), (zb_ref, hb_scr)):
            h = jnp.sin(sw_ref[...] * (_dot(w1_ref[...], z_ref[...], precision=HI) + b1_ref[...]))
            h_scr[...] = jnp.sin(sw_ref[...] * (_dot(w2_ref[...], h, precision=HI) + b2_ref[...]))

    hf = (_dot(w3f_ref[...], hf_scr[...], precision=HI) + b3f_ref[...]) * jnp.exp(-tf_ref[...] * jnp.abs(df_ref[...]))
    hb = (_dot(w3b_ref[...], hb_scr[...], precision=HI) + b3b_ref[...]) * jnp.exp(-tb_ref[...] * jnp.abs(db_ref[...]))
    den = (jnp.sum(jnp.abs(hf), axis=-1, keepdims=True) + jnp.sum(jnp.abs(hb), axis=-1, keepdims=True)) + NORM_EPS
    col = lax.broadcasted_iota(jnp.int32, (1, seq), 1)
    r_ref[:, 0:seq] = jnp.where(col == 0, 0.0, hb / den)
    r_ref[:, seq:2 * seq] = hf / den + jnp.where(col == 0, skip_ref[...], 0.0)


def hy_filters(fp, seq, cbf=128):
    t = jnp.arange(seq, dtype=f32) / seq
    t_rev = jnp.concatenate([t[:1], t[:0:-1]])

    def feats(tt):
        tc = tt[:, None]
        ang = 2.0 * math.pi * tc * jnp.arange(1, HY_BANDS + 1, dtype=f32)
        z = jnp.concatenate([tc, jnp.cos(ang), jnp.sin(ang)], axis=-1)
        return jnp.pad(z, ((0, 0), (0, HY_EMB_PAD - HY_EMB))).T

    ch = MIX_W
    nb = ch // cbf
    col = lambda a: a.reshape(-1, 1)
    w3t = fp['w3'].T.reshape(2, 2, ch, HY_HID)
    b3 = fp['b3'].reshape(2, 2, ch, 1)
    dec = fp['decay'].reshape(2, 2, ch, 1)
    const = lambda shape: pl.BlockSpec(shape, lambda o, j: (0,) * len(shape))
    sel = lambda d, last: pl.BlockSpec((None, None, cbf, last), lambda o, j: (o, d, j, 0))
    return pl.pallas_call(
        functools.partial(_hy_filter_kernel, seq=seq),
        out_shape=jax.ShapeDtypeStruct((2, ch, 2 * seq), f32),
        grid=(2, nb),
        in_specs=[const((HY_EMB_PAD, seq)), const((HY_EMB_PAD, seq)), const((1, seq)), const((1, seq)),
                  const((HY_HID, HY_EMB_PAD)), const((HY_HID, 1)), const((HY_HID, 1)),
                  const((HY_HID, HY_HID)), const((HY_HID, 1)),
                  sel(0, HY_HID), sel(1, HY_HID), sel(0, 1), sel(1, 1), sel(0, 1), sel(1, 1),
                  pl.BlockSpec((None, cbf, 1), lambda o, j: (o, j, 0))],
        out_specs=pl.BlockSpec((None, cbf, 2 * seq), lambda o, j: (o, j, 0)),
        scratch_shapes=[pltpu.VMEM((HY_HID, seq), f32), pltpu.VMEM((HY_HID, seq), f32)],
        compiler_params=_cparams("arbitrary", "arbitrary"),
        name="hy_filters",
    )(feats(t), feats(t_rev), t.reshape(1, seq), t_rev.reshape(1, seq),
      jnp.pad(fp['w1'], ((0, HY_EMB_PAD - HY_EMB), (0, 0))).T, col(fp['b1']), col(fp['sin_w']),
      fp['w2'].T, col(fp['b2']), w3t, w3t, b3, b3, dec, dec, fp['skip'].reshape(2, ch, 1))


def _hyena_kernel(cw_ref, cb_ref, hv_ref, h1_ref, h2_ref, r_ref, o_ref, acc_scr, *, bsz, n_i, cb_n):
    tb = HY_TB
    gr = tb // 2
    sub = tb // LANES
    seq = n_i * tb
    rows = n_i * bsz
    base = pl.program_id(0) * cb_n
    n_ch = MIX_W
    lane = lax.broadcasted_iota(jnp.int32, (1, tb), 1)
    zblk = jnp.zeros((bsz, tb), f32)

    def load(ref, ch):
        return jnp.concatenate(
            [jnp.concatenate([ref[ch, pl.ds(sub * ib + k, bsz, stride=sub * n_i), :] for k in range(sub)], axis=1)
             for ib in range(n_i)], axis=0)

    def short_conv(a, stream, ch):
        idx = stream * n_ch + base + ch
        prev = jnp.concatenate([zblk, a[:rows - bsz]], axis=0) if n_i > 1 else zblk
        nxt = jnp.concatenate([a[bsz:], zblk], axis=0) if n_i > 1 else zblk
        am = pltpu.roll(jnp.where(lane == tb - 1, prev, a), 1, 1)
        ap = pltpu.roll(jnp.where(lane == 0, nxt, a), tb - 1, 1)
        return (am * cw_ref[idx] + a * cw_ref[3 * n_ch + idx] + ap * cw_ref[6 * n_ch + idx]) + cb_ref[idx]

    def long_conv(u, order, ch):
        rrow = r_ref[order, pl.ds(ch, 1), :]
        bits = lambda a: lax.bitcast_convert_type(a.astype(bf16).astype(f32), jnp.int32)
        word = (bits(pltpu.roll(rrow, 1, 1)) & jnp.int32(-65536)) | lax.shift_right_logical(bits(rrow), 16)
        g = pltpu.bitcast(
            pltpu.roll(jnp.broadcast_to(word, (gr // 2, 2 * seq)), 0, 1, stride=2, stride_axis=0), bf16)
        for d in [0] + [e for e in range(-(n_i - 1), n_i) if e != 0]:
            n = n_i - abs(d)
            src = max(0, -d) * bsz
            dst = max(0, d) * bsz
            c0 = seq + d * tb
            m = jnp.concatenate([g[:, c0:c0 + tb], g[:, c0 - gr:c0 - gr + tb]], axis=0)
            part = _dot(u[src:src + n * bsz].astype(bf16), m)
            if d == 0:
                acc_scr[order] = part
            else:
                acc_scr[order, dst:dst + n * bsz, :] += part
        return acc_scr[order]

    def body(ch, carry):
        hv = short_conv(load(hv_ref, ch), 0, ch)
        hx1 = short_conv(load(h1_ref, ch), 1, ch)
        hx2 = short_conv(load(h2_ref, ch), 2, ch)
        zz = hx1 * long_conv(hv, 0, ch)
        out = hx2 * long_conv(zz, 1, ch)
        for ib in range(n_i):
            for k in range(sub):
                o_ref[ch, pl.ds(sub * ib + k, bsz, stride=sub * n_i), :] = (
                    out[ib * bsz:(ib + 1) * bsz, k * LANES:(k + 1) * LANES])
        return carry

    lax.fori_loop(0, cb_n, body, 0, unroll=8)


def hyena_mixer(hy_t, r, cw, cb, bsz, seq, cb_n=8):
    n_i = seq // HY_TB
    ch = MIX_W
    nblk = ch // cb_n
    n_rows = bsz * seq // LANES
    x3 = hy_t.reshape(3 * ch, n_rows, LANES)
    stream = lambda s: pl.BlockSpec((cb_n, n_rows, LANES), lambda j: (s * nblk + j, 0, 0))
    smem = pl.BlockSpec(memory_space=pltpu.SMEM)
    out = pl.pallas_call(
        functools.partial(_hyena_kernel, bsz=bsz, n_i=n_i, cb_n=cb_n),
        out_shape=jax.ShapeDtypeStruct((ch, n_rows, LANES), f32),
        grid=(nblk,),
        in_specs=[smem, smem, stream(0), stream(1), stream(2),
                  pl.BlockSpec((2, cb_n, 2 * seq), lambda j: (0, j, 0))],
        out_specs=pl.BlockSpec((cb_n, n_rows, LANES), lambda j: (j, 0, 0)),
        scratch_shapes=[pltpu.VMEM((2, bsz * n_i, HY_TB), f32)],
        compiler_params=_cparams("parallel"),
        name="hyena_mixer",
    )(cw.reshape(-1), cb, x3, x3, x3, r)
    return out.reshape(ch, bsz * seq)


_IN_SIZES = (NH * GLA_DK, NH * GLA_DK, MIX_W, MIX_W, 2 * GLA_RANK, MIX_W, MIX_W, MIX_W,
             MIX_W, MIX_W + 2 * SSD_G * SSD_N, 2 * NH, 3 * MIX_W)
_IN_OFFS = np.concatenate([[0], np.cumsum(_IN_SIZES)]).tolist()


def _prep_w_in(w_in):
    (gq, gk, gv, gg, gr, dq, dk, dv, sz, sxbc, sdt, hy) = [w_in[:, _IN_OFFS[i]:_IN_OFFS[i + 1]]
                                                          for i in range(len(_IN_SIZES))]
    d = w_in.shape[0]
    zeros = lambda n: jnp.zeros((d, n), w_in.dtype)
    misc = jnp.concatenate([sxbc[:, MIX_W:], gr, zeros(LANES - 2 * GLA_RANK), sdt, zeros(LANES - 2 * NH)], axis=1)
    w_tok = jnp.concatenate([gq, gk, gv, gg, dq, dk, dv, sz, sxbc[:, :MIX_W], misc], axis=1)
    w_tok = w_tok.astype(bf16).reshape(d, N_PT, PT).transpose(1, 0, 2)
    return w_tok, hy.T.astype(bf16)


def _prep_gate_w(gate_w):
    w = jnp.zeros((2, LANES, gate_w.shape[2]), f32)
    for d in range(2):
        w = w.at[d, d * GLA_RANK:(d + 1) * GLA_RANK, :].set(gate_w[d])
    hi = w.astype(bf16)
    return hi, (w - hi.astype(f32)).astype(bf16)


def _rope_tables(seq):
    rows = seq // GRID_W
    r, col = jnp.meshgrid(jnp.arange(rows), jnp.arange(GRID_W), indexing='ij')
    r = r.reshape(-1).astype(f32)
    col = col.reshape(-1).astype(f32)
    nf = DIFF_DK // 4
    inv = ROPE_BASE ** (-jnp.arange(nf, dtype=f32) / nf)
    ang = jnp.concatenate([r[:, None] * inv, col[:, None] * inv], axis=-1)
    cos, sin = jnp.cos(ang), jnp.sin(ang)
    reps = LANES // DIFF_DK
    return jnp.tile(jnp.concatenate([cos, cos], axis=-1), (1, reps)), jnp.tile(jnp.concatenate([-sin, sin], axis=-1), (1, reps))


def _const_tables():
    hsum = np.kron(np.eye(NH, dtype=np.float32), np.ones((HEAD_DIM, HEAD_DIM), np.float32))
    expand = np.zeros((LANES, MIX_W), np.float32)
    for h in range(NH):
        expand[h, h * HEAD_DIM:(h + 1) * HEAD_DIM] = 1.0
    return tuple(jnp.asarray(a, dtype=bf16) for a in (hsum, expand))


def _pack_state(s):
    bsz, _, h, a, b = s.shape
    return s.transpose(0, 1, 4, 2, 3).reshape(bsz, 2, b, h * a)


def _unpack_state(st, a):
    bsz, _, b, _ = st.shape
    return st.reshape(bsz, 2, b, NH, a).transpose(0, 1, 3, 4, 2)


def _layer(x, mod, rows_per_mod, lw, layer, consts, bsz, seq, lam_init, r_filt, ctx, rope_tabs, final_g,
           final_norm):
    hsum, expand = consts
    p, hy_t = in_proj(x, mod, lw['norm1_g'], lw['w_tok'], lw['w_hyT'], rows_per_mod)
    if ctx is None:
        gla_s0 = jnp.zeros((bsz, 2, HEAD_DIM, NH * GLA_DK), f32)
        ssd_s0 = jnp.zeros((bsz, 2, SSD_N, MIX_W), f32)
        dctx = None
    else:
        ctx_k, ctx_v, gla_s, ssd_s = ctx
        gla_s0 = _pack_state(gla_s)
        ssd_s0 = _pack_state(ssd_s)
        dctx = (ctx_k.reshape(bsz, -1, MIX_W), ctx_v.reshape(bsz, -1, MIX_W))
    o_gla, gla_fin = gla_mixer(p, lw['gla_gw_hi'], lw['gla_gw_lo'], lw['gla_gb'], lw['gla_ng'], hsum, gla_s0,
                               bsz, seq)
    o_diff = diff_mixer(p, lw['lam_p'], lw['diff_ng'], bsz, seq, lam_init, dctx, rope_tabs)
    o_ssd, ssd_fin = ssd_mixer(p, lw['ssd_conv_w'], lw['ssd_conv_b'], lw['ssd_dtb'], lw['ssd_alog'], lw['ssd_dexp'], lw['ssd_ng'],
                               expand, ssd_s0, bsz, seq)
    o_hy = hyena_mixer(hy_t, r_filt, lw['hy_cw'], lw['hy_cb'], bsz, seq)
    x = out_proj(x, mod, o_gla, o_diff, o_ssd, o_hy, lw['w_out'], layer, rows_per_mod)
    x = mlp(x, mod, lw['norm2_g'], lw['mlp_w1'], lw['mlp_w2'], layer, final_g, rows_per_mod, final_norm)
    return x, p, gla_fin, ssd_fin


def kernel(x_prompt, x_sample, cache_diff_k, cache_diff_v, state_gla, state_ssd, c, c_ctx, ada_w, ada_b, norm1_g, norm2_g, w_in, w_out, gla_gate_w, gla_gate_b, gla_norm_g, diff_lambda, diff_norm_g, ssd_conv_w, ssd_conv_b, ssd_dt_bias, ssd_a_log, ssd_d, ssd_norm_g, hy_conv_w, hy_conv_b, hy_f_w1, hy_f_b1, hy_f_w2, hy_f_b2, hy_f_w3, hy_f_b3, hy_sin_w, hy_decay, hy_skip, mlp_w1, mlp_w2, final_g):
    bp, lp, d = x_prompt.shape
    bs, ls, _ = x_sample.shape
    depth = w_in.shape[0]
    consts = _const_tables()
    rope_tabs = _rope_tables(ls)
    n_c = 16
    cvec = jnp.concatenate([c_ctx[None], c, jnp.zeros((n_c - 1 - bs, d), f32)], axis=0)
    hp = x_prompt.reshape(bp * lp, d)
    hs = x_sample.reshape(bs * ls, d)
    ks_, vs_, gs_, ss_ = [], [], [], []
    w_out_b, mlp_w1_b, mlp_w2_b = w_out.astype(bf16), mlp_w1.astype(bf16), mlp_w2.astype(bf16)
    for l in range(depth):
        w_tok, w_hyT = _prep_w_in(w_in[l])
        pad_l = lambda a: jnp.pad(a, (0, LANES - a.shape[0])).reshape(1, LANES)
        gw_hi, gw_lo = _prep_gate_w(gla_gate_w[l])
        lw = dict(
            norm1_g=norm1_g[l], norm2_g=norm2_g[l], w_tok=w_tok, w_hyT=w_hyT, w_out=w_out_b,
            mlp_w1=mlp_w1_b, mlp_w2=mlp_w2_b,
            gla_gw_hi=gw_hi, gla_gw_lo=gw_lo, gla_gb=gla_gate_b[l].reshape(2, 1, -1), gla_ng=jnp.tile(gla_norm_g[l], NH).reshape(1, MIX_W),
            lam_p=jnp.pad(diff_lambda[l], ((0, 4), (0, LANES - DIFF_DK))),
            diff_ng=jnp.tile(diff_norm_g[l], LANES // HEAD_DIM).reshape(1, LANES),
            ssd_conv_w=ssd_conv_w[l], ssd_conv_b=ssd_conv_b[l],
            ssd_dtb=pad_l(ssd_dt_bias[l].reshape(-1)), ssd_alog=pad_l(ssd_a_log[l].reshape(-1)),
            ssd_dexp=jnp.repeat(ssd_d[l], HEAD_DIM).reshape(1, MIX_W), ssd_ng=ssd_norm_g[l].reshape(1, MIX_W),
            hy_cw=hy_conv_w[l], hy_cb=hy_conv_b[l],
        )
        fp = dict(w1=hy_f_w1[l], b1=hy_f_b1[l], w2=hy_f_w2[l], b2=hy_f_b2[l], w3=hy_f_w3[l], b3=hy_f_b3[l],
                  sin_w=hy_sin_w[l], decay=hy_decay[l], skip=hy_skip[l])
        mod = ada_mod(cvec, ada_w, ada_b, l).reshape(n_c, 6, d)
        lam_init = 0.8 - 0.6 * math.exp(-0.3 * l)
        last = l == depth - 1
        hp, p_p, g_l, s_l = _layer(hp, mod[0:1], bp * lp, lw, l, consts, bp, lp, lam_init, hy_filters(fp, lp),
                                   None, None, final_g, last)
        ks_.append(p_p[T_DK].astype(f32).reshape(bp, lp, NH, 2 * DIFF_DK))
        vs_.append(p_p[T_DV].astype(f32).reshape(bp, lp, NH, HEAD_DIM))
        gs_.append(_unpack_state(g_l, GLA_DK))
        ss_.append(_unpack_state(s_l, HEAD_DIM))
        hs, _, _, _ = _layer(hs, mod[1:1 + bs], ls, lw, l, consts, bs, ls, lam_init, hy_filters(fp, ls),
                             (cache_diff_k[:, l], cache_diff_v[:, l], state_gla[:, l], state_ssd[:, l]),
                             rope_tabs, final_g, last)
    return (hp.reshape(bp, lp, d), hs.reshape(bs, ls, d), jnp.stack(ks_, axis=1), jnp.stack(vs_, axis=1),
            jnp.stack(gs_, axis=1), jnp.stack(ss_, axis=1))
```

```python
import functools
import math

import numpy as np
import jax
import jax.numpy as jnp
from jax import lax
from jax.experimental import pallas as pl
from jax.experimental.pallas import tpu as pltpu

f32 = jnp.float32
bf16 = jnp.bfloat16
HI = lax.Precision.HIGHEST

LANES = 128
VMEM_LIMIT = 56 * 2**20

D_MODEL = 2048
GRID_W = 64
MIX_W = D_MODEL // 4
HEAD_DIM = 64
NH = MIX_W // HEAD_DIM
D_FF = 4 * D_MODEL
NORM_EPS = 1e-6
LOG2E = 1.0 / math.log(2.0)
GLA_DK = HEAD_DIM // 2
GLA_RANK = 16
GLA_TAU = 16.0
GLA_CHUNK = 128
DIFF_DK = HEAD_DIM // 2
ROPE_BASE = 10000.0
SSD_N = 64
SSD_G = 2
SSD_CHUNK = 128
HY_BANDS = 16
HY_EMB = 2 * HY_BANDS + 1
HY_EMB_PAD = 40
HY_HID = 64
DIFF_KEY_CHUNK = 512
DIFF_Q_SUB = 512
HALO = 16
MLP_TF = 1024
HY_TB = 256

PT = 512
(T_GQK, T_GV, T_GG, T_DQ, T_DK, T_DV, T_SZ, T_SX, T_MISC) = range(9)
N_PT = 9
PT_GROUP = 3
MISC_GR = 256
MISC_DT = 384
SH1, SC1, G1, SH2, SC2, G2 = range(6)

NT_DIMS = (((1,), (1,)), ((), ()))
TN_DIMS = (((0,), (0,)), ((), ()))


def _cparams(*sem):
    return pltpu.CompilerParams(dimension_semantics=sem, vmem_limit_bytes=VMEM_LIMIT)


def _sigmoid(x):
    return 1.0 / (1.0 + jnp.exp(-x))


def _silu(x):
    return x * _sigmoid(x)


def _softplus(x):
    return jnp.maximum(x, 0.0) + jnp.log1p(jnp.exp(-jnp.abs(x)))


def _log_sigmoid(x):
    return jnp.minimum(x, 0.0) - jnp.log1p(jnp.exp(-jnp.abs(x)))


def _tri_mask(rows, c, sgn):
    r_i = lax.broadcasted_iota(jnp.int32, (rows, c), 0) & (c - 1)
    c_i = lax.broadcasted_iota(jnp.int32, (rows, c), 1)
    return (r_i - c_i) * sgn >= 0


def _expand_state(s, row_w, col_w):
    rows, cols = NH * row_w, NH * col_w
    bd = (lax.broadcasted_iota(jnp.int32, (rows, cols), 0) // row_w
          == lax.broadcasted_iota(jnp.int32, (rows, cols), 1) // col_w)
    return jnp.where(bd, jnp.concatenate([s] * NH, axis=0), 0.0)


def _compact_state(st):
    row_w = st.shape[0] // NH
    out = st[0:row_w]
    for h in range(1, NH):
        out += st[h * row_w:(h + 1) * row_w]
    return out


def _split(x, n):
    parts = []
    for _ in range(n - 1):
        hi = x.astype(bf16)
        parts.append(hi)
        x = x - hi.astype(f32)
    parts.append(x.astype(bf16))
    return parts


def _sel_dot(a, b, dims=None, terms=3):
    if a.dtype == bf16:
        prods = [_dot(a, p, dims) for p in _split(b, terms)]
    else:
        prods = [_dot(p, b, dims) for p in _split(a, terms)]
    return functools.reduce(lambda x, y: x + y, prods)


def _per_head_groups(g2):
    assert SSD_G == 2 and SSD_N == HEAD_DIM and LANES == 2 * SSD_N
    first = lax.broadcasted_iota(jnp.int32, (1, LANES), 1) < SSD_N
    swapped = pltpu.roll(g2, SSD_N, 1)
    g0 = jnp.where(first, g2, swapped)
    g1 = jnp.where(first, swapped, g2)
    rep = NH * HEAD_DIM // (SSD_G * LANES)
    return jnp.concatenate([g0] * rep + [g1] * rep, axis=1)


def _dot(a, b, dims=None, precision=None):
    if dims is None:
        return jnp.dot(a, b, preferred_element_type=f32, precision=precision)
    return lax.dot_general(a, b, dims, preferred_element_type=f32, precision=precision)


def _ada_kernel(c_ref, w_ref, b_ref, o_ref):
    s = _silu(c_ref[...]).astype(bf16)
    o_ref[...] = _dot(s, w_ref[...].astype(bf16)) + b_ref[...]


def ada_mod(cvec, w, b, layer, tn=1024):
    m, d = cvec.shape
    n = w.shape[2]
    return pl.pallas_call(
        _ada_kernel,
        out_shape=jax.ShapeDtypeStruct((m, n), f32),
        grid=(n // tn,),
        in_specs=[pl.BlockSpec((m, d), lambda j: (0, 0)),
                  pl.BlockSpec((None, d, tn), lambda j: (layer, 0, j)),
                  pl.BlockSpec((None, 1, tn), lambda j: (layer, 0, j))],
        out_specs=pl.BlockSpec((m, tn), lambda j: (0, j)),
        compiler_params=_cparams("arbitrary"),
        name="ada_mod",
    )(cvec, w, b.reshape(b.shape[0], 1, n))


def _norm_mod(x, g, sc, sh):
    ms = jnp.mean(x * x, axis=-1, keepdims=True)
    return (x * lax.rsqrt(ms + NORM_EPS) * g) * (1.0 + sc) + sh


def _norm_next_chunk(xn_ref, modn_ref, g_ref, sc, sh, u_scr, slot, row0):
    xn = xn_ref[...]
    u = _norm_mod(xn, g_ref[...], modn_ref[sc:sc + 1, :], modn_ref[sh:sh + 1, :])
    u_scr[slot, pl.ds(row0, xn.shape[0]), :] = u.astype(bf16)


def _inproj_kernel(x_ref, mod_ref, g_ref, wt_ref, wh_ref, p_ref, h_ref, u_scr):
    j = pl.program_id(1)

    @pl.when(j == 0)
    def _():
        u = _norm_mod(x_ref[...], g_ref[...], mod_ref[SC1:SC1 + 1, :], mod_ref[SH1:SH1 + 1, :])
        u_scr[...] = u.astype(bf16)

    n_grp = N_PT // PT_GROUP

    @pl.when(j < n_grp)
    def _():
        for k in range(PT_GROUP):
            p_ref[k] = _dot(u_scr[...], wt_ref[k]).astype(p_ref.dtype)

    @pl.when(j >= n_grp)
    def _():
        h_ref[...] = _dot(wh_ref[...], u_scr[...], NT_DIMS)


def in_proj(x, mod, g, w_tok, w_hyT, rows_per_mod, tm=1024):
    t, d = x.shape
    assert t % tm == 0 and rows_per_mod % tm == 0, (t, rows_per_mod, tm)
    n_hy = w_hyT.shape[0] // PT
    n_grp = N_PT // PT_GROUP
    return pl.pallas_call(
        _inproj_kernel,
        out_shape=(jax.ShapeDtypeStruct((N_PT, t, PT), bf16), jax.ShapeDtypeStruct((n_hy * PT, t), f32)),
        grid=(t // tm, n_grp + n_hy),
        in_specs=[pl.BlockSpec((tm, d), lambda i, j: (i, 0)),
                  pl.BlockSpec((None, 6, d), lambda i, j: ((i * tm) // rows_per_mod, 0, 0)),
                  pl.BlockSpec((1, d), lambda i, j: (0, 0)),
                  pl.BlockSpec((PT_GROUP, d, PT), lambda i, j: (jnp.minimum(j, n_grp - 1), 0, 0)),
                  pl.BlockSpec((PT, d), lambda i, j: (jnp.maximum(j - n_grp, 0), 0))],
        out_specs=(pl.BlockSpec((PT_GROUP, tm, PT), lambda i, j: (jnp.minimum(j, n_grp - 1), i, 0)),
                   pl.BlockSpec((PT, tm), lambda i, j: (jnp.maximum(j - n_grp, 0), i))),
        scratch_shapes=[pltpu.VMEM((tm, d), bf16)],
        compiler_params=_cparams("parallel", "arbitrary"),
        name="in_proj",
    )(x, mod, g.reshape(1, d), w_tok, w_hyT)


def _outproj_kernel(x_ref, mod_ref, a_ref, b_ref, c_ref, dT_ref, w_ref, o_ref):
    acc = _dot(a_ref[...], w_ref[0:MIX_W, :])
    acc += _dot(b_ref[...], w_ref[MIX_W:2 * MIX_W, :])
    acc += _dot(c_ref[...], w_ref[2 * MIX_W:3 * MIX_W, :])
    acc += _dot(dT_ref[...].astype(bf16), w_ref[3 * MIX_W:4 * MIX_W, :], TN_DIMS)
    o_ref[...] = x_ref[...] + mod_ref[G1:G1 + 1, :] * acc


def out_proj(x, mod, o_a, o_b, o_c, o_dT, w_out, layer, rows_per_mod, tm=512):
    t, d = x.shape
    assert t % tm == 0 and rows_per_mod % tm == 0, (t, rows_per_mod, tm)
    tok = lambda i: (i, 0)
    return pl.pallas_call(
        _outproj_kernel,
        out_shape=jax.ShapeDtypeStruct((t, d), f32),
        grid=(t // tm,),
        in_specs=[pl.BlockSpec((tm, d), tok),
                  pl.BlockSpec((None, 6, d), lambda i: ((i * tm) // rows_per_mod, 0, 0)),
                  pl.BlockSpec((tm, MIX_W), tok), pl.BlockSpec((tm, MIX_W), tok), pl.BlockSpec((tm, MIX_W), tok),
                  pl.BlockSpec((MIX_W, tm), lambda i: (0, i)),
                  pl.BlockSpec((None, d, d), lambda i: (layer, 0, 0))],
        out_specs=pl.BlockSpec((tm, d), tok),
        compiler_params=_cparams("parallel"),
        name="out_proj",
    )(x, mod, o_a, o_b, o_c, o_dT, w_out)


def _mlp_kernel(xc_ref, mod_ref, xn_ref, modn_ref, g_ref, w1_ref, w2_ref, fg_ref, o_ref, u_scr, *, final_norm):
    io = pl.program_id(0)
    j = pl.program_id(1)
    rows = xn_ref.shape[0]
    row0 = pl.multiple_of(j * rows, rows)

    @pl.when(io == 0)
    def _():
        _norm_next_chunk(xn_ref, modn_ref, g_ref, SC2, SH2, u_scr, 0, row0)

    @pl.when(jnp.logical_and(io > 0, j == 0))
    def _():
        o_ref[...] = jnp.zeros_like(o_ref)

    @pl.when(io > 0)
    def _():
        _norm_next_chunk(xn_ref, modn_ref, g_ref, SC2, SH2, u_scr, io % 2, row0)
        h = jnp.maximum(_dot(u_scr[1 - io % 2], w1_ref[...]), 0.0)
        hb = (h * h).astype(bf16)
        d = o_ref.shape[1]
        for n in range(d // PT):
            cols = slice(n * PT, (n + 1) * PT)
            o_ref[:, cols] += mod_ref[G2:G2 + 1, cols] * _dot(hb, w2_ref[:, cols])
        o_ref[pl.ds(row0, rows), :] += xc_ref[...]

    if final_norm:
        @pl.when(jnp.logical_and(io > 0, j == pl.num_programs(1) - 1))
        def _():
            y = o_ref[...]
            ms = jnp.mean(y * y, axis=-1, keepdims=True)
            o_ref[...] = y * lax.rsqrt(ms + NORM_EPS) * fg_ref[...]


def mlp(x, mod, g, w1, w2, layer, final_g, rows_per_mod, final_norm, tm=1024, tf=MLP_TF):
    t, d = x.shape
    assert t % tm == 0 and rows_per_mod % tm == 0, (t, rows_per_mod, tm)
    n_f = w1.shape[2] // tf
    rows = tm // n_f
    n_tiles = t // tm
    tile = lambda io: jnp.maximum(io - 1, 0)
    wj = lambda io, j: jnp.where(io == 0, 0, j)
    chunk_of = lambda ti, j: jnp.minimum(ti * n_f + j, t // rows - 1)
    mod_of = lambda ti: jnp.minimum((ti * tm) // rows_per_mod, mod.shape[0] - 1)
    return pl.pallas_call(
        functools.partial(_mlp_kernel, final_norm=final_norm),
        out_shape=jax.ShapeDtypeStruct((t, d), f32),
        grid=(n_tiles + 1, n_f),
        in_specs=[pl.BlockSpec((rows, d), lambda io, j: (chunk_of(tile(io), j), 0)),
                  pl.BlockSpec((None, 6, d), lambda io, j: (mod_of(tile(io)), 0, 0)),
                  pl.BlockSpec((rows, d), lambda io, j: (chunk_of(jnp.minimum(io, n_tiles - 1), j), 0)),
                  pl.BlockSpec((None, 6, d), lambda io, j: (mod_of(jnp.minimum(io, n_tiles - 1)), 0, 0)),
                  pl.BlockSpec((1, d), lambda io, j: (0, 0)),
                  pl.BlockSpec((None, d, tf), lambda io, j: (layer, 0, wj(io, j))),
                  pl.BlockSpec((None, tf, d), lambda io, j: (layer, wj(io, j), 0)),
                  pl.BlockSpec((1, d), lambda io, j: (0, 0))],
        out_specs=pl.BlockSpec((tm, d), lambda io, j: (tile(io), 0)),
        scratch_shapes=[pltpu.VMEM((2, tm, d), bf16)],
        compiler_params=_cparams("arbitrary", "arbitrary"),
        name="mlp",
    )(x, mod, x, mod, g.reshape(1, d), w1, w2, final_g.reshape(1, d))


def _gla_kernel(qk_ref, v_ref, gg_ref, misc_ref, gwh_ref, gwl_ref, gb_ref, ng_ref, hsum_ref, s0_ref,
                o_ref, sfin_ref, of_scr, st_scr, la_scr, *, nblk, tl):
    ph = pl.program_id(1)
    i = pl.program_id(2)
    fwd = ph == 0
    c = GLA_CHUNK
    nch = tl // c
    qkw = NH * GLA_DK

    @pl.when(i == 0)
    def _():
        st_scr[...] = _expand_state(s0_ref[...], HEAD_DIM, GLA_DK)

    gr = misc_ref[:, MISC_GR:MISC_GR + LANES]
    logits = (_dot(gr, gwh_ref[...]) + _dot(gr, gwl_ref[...])) + gb_ref[...]
    la_scr[...] = _log_sigmoid(logits) * (1.0 / GLA_TAU)

    sgn = jnp.where(fwd, 1, -1)
    trib = jnp.where(_tri_mask(c, c, sgn), 1.0, 0.0).astype(bf16)
    tri8 = _tri_mask(NH * c, c, sgn)
    qk_head = lax.broadcasted_iota(jnp.int32, (1, qkw), 1) // GLA_DK
    v_head = lax.broadcasted_iota(jnp.int32, (1, MIX_W), 1) // HEAD_DIM
    bd = (lax.broadcasted_iota(jnp.int32, (MIX_W, qkw), 0) // HEAD_DIM
          == lax.broadcasted_iota(jnp.int32, (MIX_W, qkw), 1) // GLA_DK)
    brow = pl.multiple_of(jnp.where(fwd, i, nblk - 1 - i) * tl, tl)

    def chunk(j, carry):
        cj = jnp.where(fwd, j, nch - 1 - j)
        off = pl.multiple_of(cj * c, c)
        q = qk_ref[pl.ds(off, c), 0:qkw].astype(f32) * (GLA_DK ** -0.5)
        k = qk_ref[pl.ds(off, c), qkw:2 * qkw].astype(f32)
        vb = v_ref[pl.ds(off, c), :]
        b = _sel_dot(trib, la_scr[pl.ds(off, c), :])
        btot = jnp.where(fwd, b[c - 1:c, :], b[0:1, :])
        bmid = b[c // 2 - 1:c // 2, :]
        qd = q * jnp.exp(b)
        qi = q * jnp.exp(b - bmid)
        kd = (k * jnp.exp(bmid - b)).astype(bf16)
        kt = (k * jnp.exp(btot - b)).astype(bf16)
        qst = jnp.concatenate([jnp.where(qk_head == h, qi, 0.0) for h in range(NH)], axis=0).astype(bf16)
        att = jnp.where(tri8, _dot(qst, kd, NT_DIMS), 0.0)
        r = _dot(att.astype(bf16), vb)
        o = jnp.where(v_head == 0, r[0:c], 0.0)
        for h in range(1, NH):
            o += jnp.where(v_head == h, r[h * c:(h + 1) * c], 0.0)
        st = st_scr[...]
        o += _dot(qd.astype(bf16), st.astype(bf16), NT_DIMS)
        kv = _dot(vb, kt, TN_DIMS)
        st_scr[...] = st * jnp.exp(btot) + jnp.where(bd, kv, 0.0)
        of_scr[pl.ds(pl.multiple_of(brow + off, c), c), :] += o
        return carry

    @pl.when(fwd)
    def _():
        of_scr[pl.ds(brow, tl), :] = jnp.zeros((tl, MIX_W), f32)

    lax.fori_loop(0, nch, chunk, 0, unroll=True)

    @pl.when(ph == 1)
    def _():
        ot = of_scr[pl.ds(brow, tl), :]
        ss = _sel_dot(ot * ot, hsum_ref[...], terms=2) * (1.0 / HEAD_DIM)
        o_ref[...] = (ot * lax.rsqrt(ss + NORM_EPS) * ng_ref[...] * _silu(gg_ref[...].astype(f32))).astype(bf16)

    @pl.when(i == nblk - 1)
    def _():
        sfin_ref[...] = _compact_state(st_scr[...])


def gla_mixer(p, gw_hi, gw_lo, gb, ng, hsum, s0, bsz, seq, tl=1024):
    tl = min(tl, seq)
    nblk = seq // tl
    qkw = NH * GLA_DK
    blk = lambda ph, i: jnp.where(ph == 0, i, nblk - 1 - i)
    tile = lambda col: pl.BlockSpec((None, tl, PT), lambda b, ph, i: (col, b * nblk + blk(ph, i), 0))
    const = lambda shape: pl.BlockSpec(shape, lambda b, ph, i: (0,) * len(shape))
    per_dir = lambda rows: pl.BlockSpec((None, rows, qkw), lambda b, ph, i: (ph, 0, 0))
    state = pl.BlockSpec((None, None, HEAD_DIM, qkw), lambda b, ph, i: (b, ph, 0, 0))
    return pl.pallas_call(
        functools.partial(_gla_kernel, nblk=nblk, tl=tl),
        out_shape=(jax.ShapeDtypeStruct((bsz * seq, MIX_W), bf16),
                   jax.ShapeDtypeStruct((bsz, 2, HEAD_DIM, qkw), f32)),
        grid=(bsz, 2, nblk),
        in_specs=[tile(T_GQK), tile(T_GV), tile(T_GG), tile(T_MISC),
                  per_dir(LANES), per_dir(LANES), per_dir(1), const((1, MIX_W)), const((MIX_W, MIX_W)), state],
        out_specs=(pl.BlockSpec((tl, MIX_W),
                                lambda b, ph, i: (b * nblk + jnp.where(ph == 0, nblk - 1, nblk - 1 - i), 0)),
                   state),
        scratch_shapes=[pltpu.VMEM((seq, MIX_W), f32), pltpu.VMEM((MIX_W, qkw), f32), pltpu.VMEM((tl, qkw), f32)],
        compiler_params=_cparams("parallel", "arbitrary", "arbitrary"),
        name="gla_mixer",
    )(p, p, p, p, gw_hi, gw_lo, gb, ng, hsum, s0)


def _rope(x, cos, sin_signed):
    lane = lax.broadcasted_iota(jnp.int32, (1, LANES), 1)
    half = DIFF_DK // 2
    rot = jnp.where((lane % DIFF_DK) < half, pltpu.roll(x, LANES - half, 1), pltpu.roll(x, half, 1))
    return x * cos + rot * sin_signed


def _diff_kernel(*refs, seq, ctx_len, rope, lam_init):
    if rope:
        (q_ref, k_ref, v_ref, ck_ref, cv_ref, cq_ref, sq_ref, ckk_ref, skk_ref, lam_ref, ng_ref,
         o_ref, kb_scr, vt_scr) = refs
    else:
        q_ref, k_ref, v_ref, lam_ref, ng_ref, o_ref, kb_scr, vt_scr = refs
    qi = pl.program_id(2)
    lane = lax.broadcasted_iota(jnp.int32, (1, LANES), 1)
    first = lane < HEAD_DIM

    def put_v(v, lo, hi):
        vt = v.T
        ones = jnp.ones((HEAD_DIM, hi - lo), f32)
        vt_scr[0, :, lo:hi] = jnp.concatenate([vt[0:HEAD_DIM], ones], axis=0).astype(bf16)
        vt_scr[1, :, lo:hi] = jnp.concatenate([vt[HEAD_DIM:], ones], axis=0).astype(bf16)

    @pl.when(qi == 0)
    def _():
        k = k_ref[...].astype(f32)
        if rope:
            k = _rope(k, ckk_ref[...], skk_ref[...])
        kb_scr[0:seq, :] = k.astype(bf16)
        put_v(v_ref[...].astype(f32), 0, seq)
        if ctx_len:
            kb_scr[seq:seq + ctx_len, :] = ck_ref[...].astype(bf16)
            put_v(cv_ref[...], seq, seq + ctx_len)

    q = q_ref[...].astype(f32)
    if rope:
        q = _rope(q, cq_ref[...], sq_ref[...])
    q = q * (DIFF_DK ** -0.5 * LOG2E)
    n_keys = seq + ctx_len
    ck = max(w for w in range(LANES, DIFF_KEY_CHUNK + 1, LANES) if n_keys % w == 0)
    nck = n_keys // ck
    tq = q.shape[0]
    sub = min(DIFF_Q_SUB, tq)
    jobs = [(qs, g) for qs in range(tq // sub) for g in range(4)]
    lp = lam_ref[...]
    lam = (jnp.exp(jnp.sum(lp[0:1] * lp[1:2], keepdims=True)) - jnp.exp(jnp.sum(lp[2:3] * lp[3:4], keepdims=True))
           + lam_init)

    def finish(qs, res):
        o = jnp.concatenate([res[0] - lam * res[1], res[2] - lam * res[3]], axis=0).T
        o2 = o * o
        ss_a = jnp.sum(jnp.where(first, o2, 0.0), axis=-1, keepdims=True)
        ss_b = jnp.sum(jnp.where(first, 0.0, o2), axis=-1, keepdims=True)
        inv = jnp.where(first, lax.rsqrt(ss_a * (1.0 / HEAD_DIM) + NORM_EPS),
                        lax.rsqrt(ss_b * (1.0 / HEAD_DIM) + NORM_EPS))
        o_ref[qs * sub:(qs + 1) * sub, :] = (o * inv * ng_ref[...] * (1.0 - lam_init)).astype(bf16)

    res = []
    s_prev = m_prev = None
    for j in range(len(jobs) + 1):
        s_cur, mvec, acc = [], None, None
        if j < len(jobs):
            qs, g = jobs[j]
            qm = jnp.where(lane // DIFF_DK == g, q[qs * sub:(qs + 1) * sub], 0.0).astype(bf16)
        for c in range(nck):
            if j < len(jobs):
                s = _dot(qm, kb_scr[c * ck:(c + 1) * ck, :], NT_DIMS)
                s_cur.append(s)
                for k in range(ck // LANES):
                    blk = s[:, k * LANES:(k + 1) * LANES]
                    mvec = blk if mvec is None else jnp.maximum(mvec, blk)
            if j > 0:
                p = jnp.exp2(s_prev[c] - m_prev).astype(bf16)
                part = _dot(vt_scr[jobs[j - 1][1] // 2, :, c * ck:(c + 1) * ck], p, NT_DIMS)
                acc = part if acc is None else acc + part
        if j > 0:
            res.append(acc[0:HEAD_DIM] / acc[HEAD_DIM:HEAD_DIM + 1])
            if len(res) == 4:
                finish(jobs[j - 1][0], res)
                res = []
        if j < len(jobs):
            s_prev, m_prev = s_cur, jnp.max(mvec, axis=-1, keepdims=True)


def diff_mixer(p, lam_p, ng, bsz, seq, lam_init, ctx=None, rope_tabs=None, tq=1024):
    tq = min(tq, seq)
    nq = seq // tq
    npair = MIX_W // LANES
    rope = rope_tabs is not None
    ctx_len = ctx[0].shape[1] if ctx is not None else 0
    assert rope == (ctx is not None)
    q_spec = pl.BlockSpec((None, tq, LANES), lambda b, hp, i: (T_DQ, b * nq + i, hp))
    k_spec = pl.BlockSpec((None, seq, LANES), lambda b, hp, i: (T_DK, b, hp))
    v_spec = pl.BlockSpec((None, seq, LANES), lambda b, hp, i: (T_DV, b, hp))
    const = lambda shape: pl.BlockSpec(shape, lambda b, hp, i: (0,) * len(shape))
    args = [p, p, p]
    specs = [q_spec, k_spec, v_spec]
    if rope:
        c_spec = pl.BlockSpec((None, ctx_len, LANES), lambda b, hp, i: (b, 0, hp))
        tab_q = pl.BlockSpec((tq, LANES), lambda b, hp, i: (i, 0))
        args += [ctx[0], ctx[1], rope_tabs[0], rope_tabs[1], rope_tabs[0], rope_tabs[1]]
        specs += [c_spec, c_spec, tab_q, tab_q, const((seq, LANES)), const((seq, LANES))]
    args += [lam_p, ng]
    specs += [const((8, LANES)), const((1, LANES))]
    return pl.pallas_call(
        functools.partial(_diff_kernel, seq=seq, ctx_len=ctx_len, rope=rope, lam_init=lam_init),
        out_shape=jax.ShapeDtypeStruct((bsz * seq, MIX_W), bf16),
        grid=(bsz, npair, nq),
        in_specs=specs,
        out_specs=pl.BlockSpec((tq, LANES), lambda b, hp, i: (b * nq + i, hp)),
        scratch_shapes=[pltpu.VMEM((seq + ctx_len, LANES), bf16), pltpu.VMEM((2, LANES, seq + ctx_len), bf16)],
        compiler_params=_cparams("parallel", "parallel", "arbitrary"),
        name="diff_mixer",
    )(*args)


def _ssd_kernel(sx_ref, bcr_ref, sxp_ref, bcp_ref, sxn_ref, bcn_ref, cw_ref, cb_ref, z_ref, dt_ref, dtb_ref, alog_ref,
                dexp_ref, ng_ref, exp_ref, s0_ref, o_ref, sfin_ref,
                yf_scr, st_scr, dt_scr, dta_scr, xs_ref, bc_ref, *, nblk, tl):
    ph = pl.program_id(1)
    i = pl.program_id(2)
    fwd = ph == 0
    c = SSD_CHUNK
    nch = tl // c
    rep = NH // SSD_G
    blk = jnp.where(fwd, i, nblk - 1 - i)

    @pl.when(i == 0)
    def _():
        st_scr[...] = _expand_state(s0_ref[...], SSD_N, HEAD_DIM)

    brow = pl.multiple_of(blk * tl, tl)

    @pl.when(fwd)
    def _():
        row = lax.broadcasted_iota(jnp.int32, (tl, 1), 0)
        for raw_ref, prev_ref, next_ref, dst_ref, lo in ((sx_ref, sxp_ref, sxn_ref, xs_ref, 0),
                                                        (bcr_ref, bcp_ref, bcn_ref, bc_ref, MIX_W)):
            x = raw_ref[...].astype(f32)
            w = cw_ref[:, lo:lo + x.shape[1]]
            prev_row = jnp.where(blk == 0, 0.0, prev_ref[HALO - 1:HALO, :].astype(f32))
            next_row = jnp.where(blk == nblk - 1, 0.0, next_ref[0:1, :].astype(f32))
            xm = jnp.where(row == 0, prev_row, pltpu.roll(x, 1, 0))
            xp = jnp.where(row == tl - 1, next_row, pltpu.roll(x, tl - 1, 0))
            dst_ref[pl.ds(brow, tl), :] = _silu(xm * w[0:1] + x * w[1:2] + xp * w[2:3]
                                                + cb_ref[:, lo:lo + x.shape[1]])

    dtv = _softplus(dt_ref[...].astype(f32) + dtb_ref[...])
    dtv = jnp.where(fwd, dtv, pltpu.roll(dtv, LANES - NH, 1))
    av = -jnp.exp(alog_ref[...])
    av = jnp.where(fwd, av, pltpu.roll(av, LANES - NH, 1))
    dt_scr[...] = dtv
    dta_scr[...] = dtv * av

    tri = _tri_mask(c, c, jnp.where(fwd, 1, -1))
    trib = jnp.where(tri, 1.0, 0.0).astype(bf16)
    head = lax.broadcasted_iota(jnp.int32, (1, MIX_W), 1) // HEAD_DIM
    bd = (lax.broadcasted_iota(jnp.int32, (MIX_W, MIX_W), 0) // SSD_N
          == lax.broadcasted_iota(jnp.int32, (MIX_W, MIX_W), 1) // HEAD_DIM)
    def chunk(j, carry):
        cj = jnp.where(fwd, j, nch - 1 - j)
        off = pl.multiple_of(cj * c, c)
        srow = pl.multiple_of(brow + off, c)
        dtc = dt_scr[pl.ds(off, c), :]
        cum = _sel_dot(trib, dta_scr[pl.ds(off, c), :])
        cum_t = cum.T
        dt_t = dtc.T
        earg = _sel_dot(cum, exp_ref[...])
        elast = jnp.where(fwd, earg[c - 1:c, :], earg[0:1, :])
        dtx = _sel_dot(dtc, exp_ref[...])
        bt = _per_head_groups(bc_ref[pl.ds(srow, c), 0:LANES])
        ct = _per_head_groups(bc_ref[pl.ds(srow, c), LANES:2 * LANES])
        x = xs_ref[pl.ds(srow, c), :]
        xb = x.astype(bf16)
        st = st_scr[...]
        y = _dot((ct * jnp.exp(earg)).astype(bf16), st.astype(bf16))
        btail = (bt * jnp.exp(elast - earg) * dtx).astype(bf16)
        new = _dot(btail, xb, TN_DIMS)
        st_scr[...] = st * jnp.exp(elast) + jnp.where(bd, new, 0.0)
        btb = bt.astype(bf16)
        cb = [_dot(jnp.where(head == g * rep, ct, 0.0).astype(bf16), btb, NT_DIMS) for g in range(SSD_G)]
        ws = []
        for h in range(NH):
            seg = cum[:, h:h + 1] - cum_t[h:h + 1, :]
            lm = jnp.exp(jnp.where(tri, seg, -jnp.inf))
            ws.append((cb[h // rep] * lm * dt_t[h:h + 1, :]).astype(bf16))
        r = _dot(jnp.concatenate(ws, axis=0), xb)
        for h in range(NH):
            y += jnp.where(head == h, r[h * c:(h + 1) * c], 0.0)
        yf_scr[pl.ds(pl.multiple_of(brow + off, c), c), :] += y
        return carry

    @pl.when(fwd)
    def _():
        yf_scr[pl.ds(brow, tl), :] = jnp.zeros((tl, MIX_W), f32)

    lax.fori_loop(0, nch, chunk, 0, unroll=True)

    @pl.when(ph == 1)
    def _():
        yt = yf_scr[pl.ds(brow, tl), :] + xs_ref[pl.ds(brow, tl), :] * dexp_ref[...]
        t = yt * _silu(z_ref[...].astype(f32))
        ms = jnp.mean(t * t, axis=-1, keepdims=True)
        o_ref[...] = (t * lax.rsqrt(ms + NORM_EPS) * ng_ref[...]).astype(bf16)

    @pl.when(i == nblk - 1)
    def _():
        sfin_ref[...] = _compact_state(st_scr[...])


def ssd_mixer(p, cw, cb, dtb, alog, dexp, ng, expand, s0, bsz, seq, tl=1024):
    tl = min(tl, seq)
    nblk = seq // tl
    n8 = bsz * seq // HALO
    per8 = tl // HALO
    blk = lambda ph, i: jnp.where(ph == 0, i, nblk - 1 - i)
    rows = lambda b, ph, i: b * nblk + blk(ph, i)
    const = lambda shape: pl.BlockSpec(shape, lambda b, ph, i: (0,) * len(shape))
    state = pl.BlockSpec((None, None, SSD_N, MIX_W), lambda b, ph, i: (b, ph, 0, 0))
    body = lambda tile, w: pl.BlockSpec((None, tl, w), lambda b, ph, i: (tile, rows(b, ph, i), 0))
    crow = lambda b, ph, i: jnp.where(ph == 0, rows(b, ph, i), b * nblk + nblk - 1)
    cbody = lambda tile, w: pl.BlockSpec((None, tl, w), lambda b, ph, i: (tile, crow(b, ph, i), 0))
    prev8 = lambda tile, w: pl.BlockSpec(
        (None, HALO, w), lambda b, ph, i: (tile, jnp.maximum(crow(b, ph, i) * per8 - 1, 0), 0))
    next8 = lambda tile, w: pl.BlockSpec(
        (None, HALO, w), lambda b, ph, i: (tile, jnp.minimum((crow(b, ph, i) + 1) * per8, n8 - 1), 0))
    bcw = 2 * SSD_G * SSD_N
    return pl.pallas_call(
        functools.partial(_ssd_kernel, nblk=nblk, tl=tl),
        out_shape=(jax.ShapeDtypeStruct((bsz * seq, MIX_W), bf16),
                   jax.ShapeDtypeStruct((bsz, 2, SSD_N, MIX_W), f32)),
        grid=(bsz, 2, nblk),
        in_specs=[cbody(T_SX, PT), cbody(T_MISC, bcw), prev8(T_SX, PT), prev8(T_MISC, bcw),
                  next8(T_SX, PT), next8(T_MISC, bcw), const((3, MIX_W + bcw)), const((1, MIX_W + bcw)),
                  body(T_SZ, PT),
                  pl.BlockSpec((None, tl, LANES), lambda b, ph, i: (T_MISC, rows(b, ph, i), MISC_DT // LANES)),
                  const((1, LANES)), const((1, LANES)), const((1, MIX_W)), const((1, MIX_W)),
                  const((LANES, MIX_W)), state],
        out_specs=(pl.BlockSpec((tl, MIX_W),
                                lambda b, ph, i: (b * nblk + jnp.where(ph == 0, nblk - 1, nblk - 1 - i), 0)),
                   state),
        scratch_shapes=[pltpu.VMEM((seq, MIX_W), f32), pltpu.VMEM((MIX_W, MIX_W), f32),
                        pltpu.VMEM((tl, LANES), f32), pltpu.VMEM((tl, LANES), f32),
                        pltpu.VMEM((seq, MIX_W), f32), pltpu.VMEM((seq, bcw), f32)],
        compiler_params=_cparams("parallel", "arbitrary", "arbitrary"),
        name="ssd_mixer",
    )(p, p, p, p, p, p, cw, cb.reshape(1, -1), p, p, dtb, alog, dexp, ng, expand, s0)


def _hy_filter_kernel(zf_ref, zb_ref, tf_ref, tb_ref, w1_ref, b1_ref, sw_ref, w2_ref, b2_ref,
                      w3f_ref, w3b_ref, b3f_ref, b3b_ref, df_ref, db_ref, skip_ref, r_ref, hf_scr, hb_scr, *, seq):
    first = jnp.logical_and(pl.program_id(0) == 0, pl.program_id(1) == 0)

    @pl.when(first)
    def _():
        for z_ref, h_scr in ((zf_ref, hf_scr), (zb_ref, hb_scr)):
            h = jnp.sin(sw_ref[...] * (_dot(w1_ref[...], z_ref[...], precision=HI) + b1_ref[...]))
            h_scr[...] = jnp.sin(sw_ref[...] * (_dot(w2_ref[...], h, precision=HI) + b2_ref[...]))

    hf = (_dot(w3f_ref[...], hf_scr[...], precision=HI) + b3f_ref[...]) * jnp.exp(-tf_ref[...] * jnp.abs(df_ref[...]))
    hb = (_dot(w3b_ref[...], hb_scr[...], precision=HI) + b3b_ref[...]) * jnp.exp(-tb_ref[...] * jnp.abs(db_ref[...]))
    den = (jnp.sum(jnp.abs(hf), axis=-1, keepdims=True) + jnp.sum(jnp.abs(hb), axis=-1, keepdims=True)) + NORM_EPS
    col = lax.broadcasted_iota(jnp.int32, (1, seq), 1)
    r_ref[:, 0:seq] = jnp.where(col == 0, 0.0, hb / den)
    r_ref[:, seq:2 * seq] = hf / den + jnp.where(col == 0, skip_ref[...], 0.0)


def hy_filters(fp, seq, cbf=128):
    t = jnp.arange(seq, dtype=f32) / seq
    t_rev = jnp.concatenate([t[:1], t[:0:-1]])

    def feats(tt):
        tc = tt[:, None]
        ang = 2.0 * math.pi * tc * jnp.arange(1, HY_BANDS + 1, dtype=f32)
        z = jnp.concatenate([tc, jnp.cos(ang), jnp.sin(ang)], axis=-1)
        return jnp.pad(z, ((0, 0), (0, HY_EMB_PAD - HY_EMB))).T

    ch = MIX_W
    nb = ch // cbf
    col = lambda a: a.reshape(-1, 1)
    w3t = fp['w3'].T.reshape(2, 2, ch, HY_HID)
    b3 = fp['b3'].reshape(2, 2, ch, 1)
    dec = fp['decay'].reshape(2, 2, ch, 1)
    const = lambda shape: pl.BlockSpec(shape, lambda o, j: (0,) * len(shape))
    sel = lambda d, last: pl.BlockSpec((None, None, cbf, last), lambda o, j: (o, d, j, 0))
    return pl.pallas_call(
        functools.partial(_hy_filter_kernel, seq=seq),
        out_shape=jax.ShapeDtypeStruct((2, ch, 2 * seq), f32),
        grid=(2, nb),
        in_specs=[const((HY_EMB_PAD, seq)), const((HY_EMB_PAD, seq)), const((1, seq)), const((1, seq)),
                  const((HY_HID, HY_EMB_PAD)), const((HY_HID, 1)), const((HY_HID, 1)),
                  const((HY_HID, HY_HID)), const((HY_HID, 1)),
                  sel(0, HY_HID), sel(1, HY_HID), sel(0, 1), sel(1, 1), sel(0, 1), sel(1, 1),
                  pl.BlockSpec((None, cbf, 1), lambda o, j: (o, j, 0))],
        out_specs=pl.BlockSpec((None, cbf, 2 * seq), lambda o, j: (o, j, 0)),
        scratch_shapes=[pltpu.VMEM((HY_HID, seq), f32), pltpu.VMEM((HY_HID, seq), f32)],
        compiler_params=_cparams("arbitrary", "arbitrary"),
        name="hy_filters",
    )(feats(t), feats(t_rev), t.reshape(1, seq), t_rev.reshape(1, seq),
      jnp.pad(fp['w1'], ((0, HY_EMB_PAD - HY_EMB), (0, 0))).T, col(fp['b1']), col(fp['sin_w']),
      fp['w2'].T, col(fp['b2']), w3t, w3t, b3, b3, dec, dec, fp['skip'].reshape(2, ch, 1))


def _hyena_kernel(cw_ref, cb_ref, hv_ref, h1_ref, h2_ref, r_ref, o_ref, acc_scr, *, bsz, n_i, cb_n):
    tb = HY_TB
    gr = tb // 2
    sub = tb // LANES
    seq = n_i * tb
    rows = n_i * bsz
    base = pl.program_id(0) * cb_n
    n_ch = MIX_W
    lane = lax.broadcasted_iota(jnp.int32, (1, tb), 1)
    zblk = jnp.zeros((bsz, tb), f32)

    def load(ref, ch):
        return jnp.concatenate(
            [jnp.concatenate([ref[ch, pl.ds(sub * ib + k, bsz, stride=sub * n_i), :] for k in range(sub)], axis=1)
             for ib in range(n_i)], axis=0)

    def short_conv(a, stream, ch):
        idx = stream * n_ch + base + ch
        prev = jnp.concatenate([zblk, a[:rows - bsz]], axis=0) if n_i > 1 else zblk
        nxt = jnp.concatenate([a[bsz:], zblk], axis=0) if n_i > 1 else zblk
        am = pltpu.roll(jnp.where(lane == tb - 1, prev, a), 1, 1)
        ap = pltpu.roll(jnp.where(lane == 0, nxt, a), tb - 1, 1)
        return (am * cw_ref[idx] + a * cw_ref[3 * n_ch + idx] + ap * cw_ref[6 * n_ch + idx]) + cb_ref[idx]

    def long_conv(u, order, ch):
        rrow = r_ref[order, pl.ds(ch, 1), :]
        bits = lambda a: lax.bitcast_convert_type(a.astype(bf16).astype(f32), jnp.int32)
        word = (bits(pltpu.roll(rrow, 1, 1)) & jnp.int32(-65536)) | lax.shift_right_logical(bits(rrow), 16)
        g = pltpu.bitcast(
            pltpu.roll(jnp.broadcast_to(word, (gr // 2, 2 * seq)), 0, 1, stride=2, stride_axis=0), bf16)
        for d in [0] + [e for e in range(-(n_i - 1), n_i) if e != 0]:
            n = n_i - abs(d)
            src = max(0, -d) * bsz
            dst = max(0, d) * bsz
            c0 = seq + d * tb
            m = jnp.concatenate([g[:, c0:c0 + tb], g[:, c0 - gr:c0 - gr + tb]], axis=0)
            part = _dot(u[src:src + n * bsz].astype(bf16), m)
            if d == 0:
                acc_scr[order] = part
            else:
                acc_scr[order, dst:dst + n * bsz, :] += part
        return acc_scr[order]

    def body(ch, carry):
        hv = short_conv(load(hv_ref, ch), 0, ch)
        hx1 = short_conv(load(h1_ref, ch), 1, ch)
        hx2 = short_conv(load(h2_ref, ch), 2, ch)
        zz = hx1 * long_conv(hv, 0, ch)
        out = hx2 * long_conv(zz, 1, ch)
        for ib in range(n_i):
            for k in range(sub):
                o_ref[ch, pl.ds(sub * ib + k, bsz, stride=sub * n_i), :] = (
                    out[ib * bsz:(ib + 1) * bsz, k * LANES:(k + 1) * LANES])
        return carry

    lax.fori_loop(0, cb_n, body, 0, unroll=8)


def hyena_mixer(hy_t, r, cw, cb, bsz, seq, cb_n=8):
    n_i = seq // HY_TB
    ch = MIX_W
    nblk = ch // cb_n
    n_rows = bsz * seq // LANES
    x3 = hy_t.reshape(3 * ch, n_rows, LANES)
    stream = lambda s: pl.BlockSpec((cb_n, n_rows, LANES), lambda j: (s * nblk + j, 0, 0))
    smem = pl.BlockSpec(memory_space=pltpu.SMEM)
    out = pl.pallas_call(
        functools.partial(_hyena_kernel, bsz=bsz, n_i=n_i, cb_n=cb_n),
        out_shape=jax.ShapeDtypeStruct((ch, n_rows, LANES), f32),
        grid=(nblk,),
        in_specs=[smem, smem, stream(0), stream(1), stream(2),
                  pl.BlockSpec((2, cb_n, 2 * seq), lambda j: (0, j, 0))],
        out_specs=pl.BlockSpec((cb_n, n_rows, LANES), lambda j: (j, 0, 0)),
        scratch_shapes=[pltpu.VMEM((2, bsz * n_i, HY_TB), f32)],
        compiler_params=_cparams("parallel"),
        name="hyena_mixer",
    )(cw.reshape(-1), cb, x3, x3, x3, r)
    return out.reshape(ch, bsz * seq)


_IN_SIZES = (NH * GLA_DK, NH * GLA_DK, MIX_W, MIX_W, 2 * GLA_RANK, MIX_W, MIX_W, MIX_W,
             MIX_W, MIX_W + 2 * SSD_G * SSD_N, 2 * NH, 3 * MIX_W)
_IN_OFFS = np.concatenate([[0], np.cumsum(_IN_SIZES)]).tolist()


def _prep_w_in(w_in):
    (gq, gk, gv, gg, gr, dq, dk, dv, sz, sxbc, sdt, hy) = [w_in[:, _IN_OFFS[i]:_IN_OFFS[i + 1]]
                                                          for i in range(len(_IN_SIZES))]
    d = w_in.shape[0]
    zeros = lambda n: jnp.zeros((d, n), w_in.dtype)
    misc = jnp.concatenate([sxbc[:, MIX_W:], gr, zeros(LANES - 2 * GLA_RANK), sdt, zeros(LANES - 2 * NH)], axis=1)
    w_tok = jnp.concatenate([gq, gk, gv, gg, dq, dk, dv, sz, sxbc[:, :MIX_W], misc], axis=1)
    w_tok = w_tok.astype(bf16).reshape(d, N_PT, PT).transpose(1, 0, 2)
    return w_tok, hy.T.astype(bf16)


def _prep_gate_w(gate_w):
    w = jnp.zeros((2, LANES, gate_w.shape[2]), f32)
    for d in range(2):
        w = w.at[d, d * GLA_RANK:(d + 1) * GLA_RANK, :].set(gate_w[d])
    hi = w.astype(bf16)
    return hi, (w - hi.astype(f32)).astype(bf16)


def _rope_tables(seq):
    rows = seq // GRID_W
    r, col = jnp.meshgrid(jnp.arange(rows), jnp.arange(GRID_W), indexing='ij')
    r = r.reshape(-1).astype(f32)
    col = col.reshape(-1).astype(f32)
    nf = DIFF_DK // 4
    inv = ROPE_BASE ** (-jnp.arange(nf, dtype=f32) / nf)
    ang = jnp.concatenate([r[:, None] * inv, col[:, None] * inv], axis=-1)
    cos, sin = jnp.cos(ang), jnp.sin(ang)
    reps = LANES // DIFF_DK
    return jnp.tile(jnp.concatenate([cos, cos], axis=-1), (1, reps)), jnp.tile(jnp.concatenate([-sin, sin], axis=-1), (1, reps))


def _const_tables():
    hsum = np.kron(np.eye(NH, dtype=np.float32), np.ones((HEAD_DIM, HEAD_DIM), np.float32))
    expand = np.zeros((LANES, MIX_W), np.float32)
    for h in range(NH):
        expand[h, h * HEAD_DIM:(h + 1) * HEAD_DIM] = 1.0
    return tuple(jnp.asarray(a, dtype=bf16) for a in (hsum, expand))


def _pack_state(s):
    bsz, _, h, a, b = s.shape
    return s.transpose(0, 1, 4, 2, 3).reshape(bsz, 2, b, h * a)


def _unpack_state(st, a):
    bsz, _, b, _ = st.shape
    return st.reshape(bsz, 2, b, NH, a).transpose(0, 1, 3, 4, 2)


def _layer(x, mod, rows_per_mod, lw, layer, consts, bsz, seq, lam_init, r_filt, ctx, rope_tabs, final_g,
           final_norm):
    hsum, expand = consts
    p, hy_t = in_proj(x, mod, lw['norm1_g'], lw['w_tok'], lw['w_hyT'], rows_per_mod)
    if ctx is None:
        gla_s0 = jnp.zeros((bsz, 2, HEAD_DIM, NH * GLA_DK), f32)
        ssd_s0 = jnp.zeros((bsz, 2, SSD_N, MIX_W), f32)
        dctx = None
    else:
        ctx_k, ctx_v, gla_s, ssd_s = ctx
        gla_s0 = _pack_state(gla_s)
        ssd_s0 = _pack_state(ssd_s)
        dctx = (ctx_k.reshape(bsz, -1, MIX_W), ctx_v.reshape(bsz, -1, MIX_W))
    o_gla, gla_fin = gla_mixer(p, lw['gla_gw_hi'], lw['gla_gw_lo'], lw['gla_gb'], lw['gla_ng'], hsum, gla_s0,
                               bsz, seq)
    o_diff = diff_mixer(p, lw['lam_p'], lw['diff_ng'], bsz, seq, lam_init, dctx, rope_tabs)
    o_ssd, ssd_fin = ssd_mixer(p, lw['ssd_conv_w'], lw['ssd_conv_b'], lw['ssd_dtb'], lw['ssd_alog'], lw['ssd_dexp'], lw['ssd_ng'],
                               expand, ssd_s0, bsz, seq)
    o_hy = hyena_mixer(hy_t, r_filt, lw['hy_cw'], lw['hy_cb'], bsz, seq)
    x = out_proj(x, mod, o_gla, o_diff, o_ssd, o_hy, lw['w_out'], layer, rows_per_mod)
    x = mlp(x, mod, lw['norm2_g'], lw['mlp_w1'], lw['mlp_w2'], layer, final_g, rows_per_mod, final_norm)
    return x, p, gla_fin, ssd_fin


def kernel(x_prompt, x_sample, cache_diff_k, cache_diff_v, state_gla, state_ssd, c, c_ctx, ada_w, ada_b, norm1_g, norm2_g, w_in, w_out, gla_gate_w, gla_gate_b, gla_norm_g, diff_lambda, diff_norm_g, ssd_conv_w, ssd_conv_b, ssd_dt_bias, ssd_a_log, ssd_d, ssd_norm_g, hy_conv_w, hy_conv_b, hy_f_w1, hy_f_b1, hy_f_w2, hy_f_b2, hy_f_w3, hy_f_b3, hy_sin_w, hy_decay, hy_skip, mlp_w1, mlp_w2, final_g):
    bp, lp, d = x_prompt.shape
    bs, ls, _ = x_sample.shape
    depth = w_in.shape[0]
    consts = _const_tables()
    rope_tabs = _rope_tables(ls)
    n_c = 16
    cvec = jnp.concatenate([c_ctx[None], c, jnp.zeros((n_c - 1 - bs, d), f32)], axis=0)
    hp = x_prompt.reshape(bp * lp, d)
    hs = x_sample.reshape(bs * ls, d)
    ks_, vs_, gs_, ss_ = [], [], [], []
    w_out_b, mlp_w1_b, mlp_w2_b = w_out.astype(bf16), mlp_w1.astype(bf16), mlp_w2.astype(bf16)
    for l in range(depth):
        w_tok, w_hyT = _prep_w_in(w_in[l])
        pad_l = lambda a: jnp.pad(a, (0, LANES - a.shape[0])).reshape(1, LANES)
        gw_hi, gw_lo = _prep_gate_w(gla_gate_w[l])
        lw = dict(
            norm1_g=norm1_g[l], norm2_g=norm2_g[l], w_tok=w_tok, w_hyT=w_hyT, w_out=w_out_b,
            mlp_w1=mlp_w1_b, mlp_w2=mlp_w2_b,
            gla_gw_hi=gw_hi, gla_gw_lo=gw_lo, gla_gb=gla_gate_b[l].reshape(2, 1, -1), gla_ng=jnp.tile(gla_norm_g[l], NH).reshape(1, MIX_W),
            lam_p=jnp.pad(diff_lambda[l], ((0, 4), (0, LANES - DIFF_DK))),
            diff_ng=jnp.tile(diff_norm_g[l], LANES // HEAD_DIM).reshape(1, LANES),
            ssd_conv_w=ssd_conv_w[l], ssd_conv_b=ssd_conv_b[l],
            ssd_dtb=pad_l(ssd_dt_bias[l].reshape(-1)), ssd_alog=pad_l(ssd_a_log[l].reshape(-1)),
            ssd_dexp=jnp.repeat(ssd_d[l], HEAD_DIM).reshape(1, MIX_W), ssd_ng=ssd_norm_g[l].reshape(1, MIX_W),
            hy_cw=hy_conv_w[l], hy_cb=hy_conv_b[l],
        )
        fp = dict(w1=hy_f_w1[l], b1=hy_f_b1[l], w2=hy_f_w2[l], b2=hy_f_b2[l], w3=hy_f_w3[l], b3=hy_f_b3[l],
                  sin_w=hy_sin_w[l], decay=hy_decay[l], skip=hy_skip[l])
        mod = ada_mod(cvec, ada_w, ada_b, l).reshape(n_c, 6, d)
        lam_init = 0.8 - 0.6 * math.exp(-0.3 * l)
        last = l == depth - 1
        hp, p_p, g_l, s_l = _layer(hp, mod[0:1], bp * lp, lw, l, consts, bp, lp, lam_init, hy_filters(fp, lp),
                                   None, None, final_g, last)
        ks_.append(p_p[T_DK].astype(f32).reshape(bp, lp, NH, 2 * DIFF_DK))
        vs_.append(p_p[T_DV].astype(f32).reshape(bp, lp, NH, HEAD_DIM))
        gs_.append(_unpack_state(g_l, GLA_DK))
        ss_.append(_unpack_state(s_l, HEAD_DIM))
        hs, _, _, _ = _layer(hs, mod[1:1 + bs], ls, lw, l, consts, bs, ls, lam_init, hy_filters(fp, ls),
                             (cache_diff_k[:, l], cache_diff_v[:, l], state_gla[:, l], state_ssd[:, l]),
                             rope_tabs, final_g, last)
    return (hp.reshape(bp, lp, d), hs.reshape(bs, ls, d), jnp.stack(ks_, axis=1), jnp.stack(vs_, axis=1),
            jnp.stack(gs_, axis=1), jnp.stack(ss_, axis=1))
```

```python
import functools
import math

import numpy as np
import jax
import jax.numpy as jnp
from jax import lax
from jax.experimental import pallas as pl
from jax.experimental.pallas import tpu as pltpu

f32 = jnp.float32
bf16 = jnp.bfloat16
HI = lax.Precision.HIGHEST

LANES = 128
VMEM_LIMIT = 56 * 2**20

D_MODEL = 2048
GRID_W = 64
MIX_W = D_MODEL // 4
HEAD_DIM = 64
NH = MIX_W // HEAD_DIM
D_FF = 4 * D_MODEL
NORM_EPS = 1e-6
LOG2E = 1.0 / math.log(2.0)
GLA_DK = HEAD_DIM // 2
GLA_RANK = 16
GLA_TAU = 16.0
GLA_CHUNK = 128
DIFF_DK = HEAD_DIM // 2
ROPE_BASE = 10000.0
SSD_N = 64
SSD_G = 2
SSD_CHUNK = 128
HY_BANDS = 16
HY_EMB = 2 * HY_BANDS + 1
HY_EMB_PAD = 40
HY_HID = 64
DIFF_KEY_CHUNK = 512
DIFF_Q_SUB = 512
HALO = 16
MLP_TF = 1024
HY_TB = 256

PT = 512
(T_GQK, T_GV, T_GG, T_DQ, T_DK, T_DV, T_SZ, T_SX, T_MISC) = range(9)
N_PT = 9
PT_GROUP = 3
MISC_GR = 256
MISC_DT = 384
SH1, SC1, G1, SH2, SC2, G2 = range(6)

NT_DIMS = (((1,), (1,)), ((), ()))
TN_DIMS = (((0,), (0,)), ((), ()))


def _cparams(*sem):
    return pltpu.CompilerParams(dimension_semantics=sem, vmem_limit_bytes=VMEM_LIMIT)


def _sigmoid(x):
    return 1.0 / (1.0 + jnp.exp(-x))


def _silu(x):
    return x * _sigmoid(x)


def _softplus(x):
    return jnp.maximum(x, 0.0) + jnp.log1p(jnp.exp(-jnp.abs(x)))


def _log_sigmoid(x):
    return jnp.minimum(x, 0.0) - jnp.log1p(jnp.exp(-jnp.abs(x)))


def _tri_mask(rows, c, sgn):
    r_i = lax.broadcasted_iota(jnp.int32, (rows, c), 0) & (c - 1)
    c_i = lax.broadcasted_iota(jnp.int32, (rows, c), 1)
    return (r_i - c_i) * sgn >= 0


def _expand_state(s, row_w, col_w):
    rows, cols = NH * row_w, NH * col_w
    bd = (lax.broadcasted_iota(jnp.int32, (rows, cols), 0) // row_w
          == lax.broadcasted_iota(jnp.int32, (rows, cols), 1) // col_w)
    return jnp.where(bd, jnp.concatenate([s] * NH, axis=0), 0.0)


def _compact_state(st):
    row_w = st.shape[0] // NH
    out = st[0:row_w]
    for h in range(1, NH):
        out += st[h * row_w:(h + 1) * row_w]
    return out


def _split(x, n):
    parts = []
    for _ in range(n - 1):
        hi = x.astype(bf16)
        parts.append(hi)
        x = x - hi.astype(f32)
    parts.append(x.astype(bf16))
    return parts


def _sel_dot(a, b, dims=None, terms=3):
    if a.dtype == bf16:
        prods = [_dot(a, p, dims) for p in _split(b, terms)]
    else:
        prods = [_dot(p, b, dims) for p in _split(a, terms)]
    return functools.reduce(lambda x, y: x + y, prods)


def _block_diag_rows(x, rows, col_w):
    shape = (NH * rows, NH * col_w)
    keep = (lax.broadcasted_iota(jnp.int32, shape, 0) // rows == lax.broadcasted_iota(jnp.int32, shape, 1) // col_w)
    return jnp.where(keep, jnp.concatenate([x] * NH, axis=0), jnp.zeros((), x.dtype))


def _per_head_groups(g2):
    assert SSD_G == 2 and SSD_N == HEAD_DIM and LANES == 2 * SSD_N
    first = lax.broadcasted_iota(jnp.int32, (1, LANES), 1) < SSD_N
    swapped = pltpu.roll(g2, SSD_N, 1)
    g0 = jnp.where(first, g2, swapped)
    g1 = jnp.where(first, swapped, g2)
    rep = NH * HEAD_DIM // (SSD_G * LANES)
    return jnp.concatenate([g0] * rep + [g1] * rep, axis=1)


def _dot(a, b, dims=None, precision=None):
    if dims is None:
        return jnp.dot(a, b, preferred_element_type=f32, precision=precision)
    return lax.dot_general(a, b, dims, preferred_element_type=f32, precision=precision)


def _ada_kernel(c_ref, w_ref, b_ref, o_ref):
    s = _silu(c_ref[...]).astype(bf16)
    o_ref[...] = _dot(s, w_ref[...].astype(bf16)) + b_ref[...]


def ada_mod(cvec, w, b, layer, tn=1024):
    m, d = cvec.shape
    n = w.shape[2]
    return pl.pallas_call(
        _ada_kernel,
        out_shape=jax.ShapeDtypeStruct((m, n), f32),
        grid=(n // tn,),
        in_specs=[pl.BlockSpec((m, d), lambda j: (0, 0)),
                  pl.BlockSpec((None, d, tn), lambda j: (layer, 0, j)),
                  pl.BlockSpec((None, 1, tn), lambda j: (layer, 0, j))],
        out_specs=pl.BlockSpec((m, tn), lambda j: (0, j)),
        compiler_params=_cparams("arbitrary"),
        name="ada_mod",
    )(cvec, w, b.reshape(b.shape[0], 1, n))


def _norm_mod(x, g, sc, sh):
    ms = jnp.mean(x * x, axis=-1, keepdims=True)
    return (x * lax.rsqrt(ms + NORM_EPS) * g) * (1.0 + sc) + sh


def _norm_next_chunk(xn_ref, modn_ref, g_ref, sc, sh, u_scr, slot, row0):
    xn = xn_ref[...]
    u = _norm_mod(xn, g_ref[...], modn_ref[sc:sc + 1, :], modn_ref[sh:sh + 1, :])
    u_scr[slot, pl.ds(row0, xn.shape[0]), :] = u.astype(bf16)


def _inproj_kernel(x_ref, mod_ref, g_ref, wt_ref, wh_ref, p_ref, h_ref, u_scr):
    j = pl.program_id(1)

    @pl.when(j == 0)
    def _():
        u = _norm_mod(x_ref[...], g_ref[...], mod_ref[SC1:SC1 + 1, :], mod_ref[SH1:SH1 + 1, :])
        u_scr[...] = u.astype(bf16)

    n_grp = N_PT // PT_GROUP

    @pl.when(j < n_grp)
    def _():
        for k in range(PT_GROUP):
            p_ref[k] = _dot(u_scr[...], wt_ref[k]).astype(p_ref.dtype)

    @pl.when(j >= n_grp)
    def _():
        h_ref[...] = _dot(wh_ref[...], u_scr[...], NT_DIMS)


def in_proj(x, mod, g, w_tok, w_hyT, rows_per_mod, tm=1024):
    t, d = x.shape
    assert t % tm == 0 and rows_per_mod % tm == 0, (t, rows_per_mod, tm)
    n_hy = w_hyT.shape[0] // PT
    n_grp = N_PT // PT_GROUP
    return pl.pallas_call(
        _inproj_kernel,
        out_shape=(jax.ShapeDtypeStruct((N_PT, t, PT), bf16), jax.ShapeDtypeStruct((n_hy * PT, t), f32)),
        grid=(t // tm, n_grp + n_hy),
        in_specs=[pl.BlockSpec((tm, d), lambda i, j: (i, 0)),
                  pl.BlockSpec((None, 6, d), lambda i, j: ((i * tm) // rows_per_mod, 0, 0)),
                  pl.BlockSpec((1, d), lambda i, j: (0, 0)),
                  pl.BlockSpec((PT_GROUP, d, PT), lambda i, j: (jnp.minimum(j, n_grp - 1), 0, 0)),
                  pl.BlockSpec((PT, d), lambda i, j: (jnp.maximum(j - n_grp, 0), 0))],
        out_specs=(pl.BlockSpec((PT_GROUP, tm, PT), lambda i, j: (jnp.minimum(j, n_grp - 1), i, 0)),
                   pl.BlockSpec((PT, tm), lambda i, j: (jnp.maximum(j - n_grp, 0), i))),
        scratch_shapes=[pltpu.VMEM((tm, d), bf16)],
        compiler_params=_cparams("parallel", "arbitrary"),
        name="in_proj",
    )(x, mod, g.reshape(1, d), w_tok, w_hyT)


def _outproj_kernel(x_ref, mod_ref, a_ref, b_ref, c_ref, dT_ref, w_ref, o_ref):
    acc = _dot(a_ref[...], w_ref[0:MIX_W, :])
    acc += _dot(b_ref[...], w_ref[MIX_W:2 * MIX_W, :])
    acc += _dot(c_ref[...], w_ref[2 * MIX_W:3 * MIX_W, :])
    acc += _dot(dT_ref[...].astype(bf16), w_ref[3 * MIX_W:4 * MIX_W, :], TN_DIMS)
    o_ref[...] = x_ref[...] + mod_ref[G1:G1 + 1, :] * acc


def out_proj(x, mod, o_a, o_b, o_c, o_dT, w_out, layer, rows_per_mod, tm=512):
    t, d = x.shape
    assert t % tm == 0 and rows_per_mod % tm == 0, (t, rows_per_mod, tm)
    tok = lambda i: (i, 0)
    return pl.pallas_call(
        _outproj_kernel,
        out_shape=jax.ShapeDtypeStruct((t, d), f32),
        grid=(t // tm,),
        in_specs=[pl.BlockSpec((tm, d), tok),
                  pl.BlockSpec((None, 6, d), lambda i: ((i * tm) // rows_per_mod, 0, 0)),
                  pl.BlockSpec((tm, MIX_W), tok), pl.BlockSpec((tm, MIX_W), tok), pl.BlockSpec((tm, MIX_W), tok),
                  pl.BlockSpec((MIX_W, tm), lambda i: (0, i)),
                  pl.BlockSpec((None, d, d), lambda i: (layer, 0, 0))],
        out_specs=pl.BlockSpec((tm, d), tok),
        compiler_params=_cparams("parallel"),
        name="out_proj",
    )(x, mod, o_a, o_b, o_c, o_dT, w_out)


def _mlp_kernel(xc_ref, mod_ref, xn_ref, modn_ref, g_ref, w1_ref, w2_ref, fg_ref, o_ref, u_scr, *, final_norm):
    io = pl.program_id(0)
    j = pl.program_id(1)
    rows = xn_ref.shape[0]
    row0 = pl.multiple_of(j * rows, rows)

    @pl.when(io == 0)
    def _():
        _norm_next_chunk(xn_ref, modn_ref, g_ref, SC2, SH2, u_scr, 0, row0)

    @pl.when(jnp.logical_and(io > 0, j == 0))
    def _():
        o_ref[...] = jnp.zeros_like(o_ref)

    @pl.when(io > 0)
    def _():
        _norm_next_chunk(xn_ref, modn_ref, g_ref, SC2, SH2, u_scr, io % 2, row0)
        h = jnp.maximum(_dot(u_scr[1 - io % 2], w1_ref[...]), 0.0)
        hb = (h * h).astype(bf16)
        d = o_ref.shape[1]
        for n in range(d // PT):
            cols = slice(n * PT, (n + 1) * PT)
            o_ref[:, cols] += mod_ref[G2:G2 + 1, cols] * _dot(hb, w2_ref[:, cols])
        o_ref[pl.ds(row0, rows), :] += xc_ref[...]

    if final_norm:
        @pl.when(jnp.logical_and(io > 0, j == pl.num_programs(1) - 1))
        def _():
            y = o_ref[...]
            ms = jnp.mean(y * y, axis=-1, keepdims=True)
            o_ref[...] = y * lax.rsqrt(ms + NORM_EPS) * fg_ref[...]


def mlp(x, mod, g, w1, w2, layer, final_g, rows_per_mod, final_norm, tm=1024, tf=MLP_TF):
    t, d = x.shape
    assert t % tm == 0 and rows_per_mod % tm == 0, (t, rows_per_mod, tm)
    n_f = w1.shape[2] // tf
    rows = tm // n_f
    n_tiles = t // tm
    tile = lambda io: jnp.maximum(io - 1, 0)
    wj = lambda io, j: jnp.where(io == 0, 0, j)
    chunk_of = lambda ti, j: jnp.minimum(ti * n_f + j, t // rows - 1)
    mod_of = lambda ti: jnp.minimum((ti * tm) // rows_per_mod, mod.shape[0] - 1)
    return pl.pallas_call(
        functools.partial(_mlp_kernel, final_norm=final_norm),
        out_shape=jax.ShapeDtypeStruct((t, d), f32),
        grid=(n_tiles + 1, n_f),
        in_specs=[pl.BlockSpec((rows, d), lambda io, j: (chunk_of(tile(io), j), 0)),
                  pl.BlockSpec((None, 6, d), lambda io, j: (mod_of(tile(io)), 0, 0)),
                  pl.BlockSpec((rows, d), lambda io, j: (chunk_of(jnp.minimum(io, n_tiles - 1), j), 0)),
                  pl.BlockSpec((None, 6, d), lambda io, j: (mod_of(jnp.minimum(io, n_tiles - 1)), 0, 0)),
                  pl.BlockSpec((1, d), lambda io, j: (0, 0)),
                  pl.BlockSpec((None, d, tf), lambda io, j: (layer, 0, wj(io, j))),
                  pl.BlockSpec((None, tf, d), lambda io, j: (layer, wj(io, j), 0)),
                  pl.BlockSpec((1, d), lambda io, j: (0, 0))],
        out_specs=pl.BlockSpec((tm, d), lambda io, j: (tile(io), 0)),
        scratch_shapes=[pltpu.VMEM((2, tm, d), bf16)],
        compiler_params=_cparams("arbitrary", "arbitrary"),
        name="mlp",
    )(x, mod, x, mod, g.reshape(1, d), w1, w2, final_g.reshape(1, d))


def _gla_kernel(qk_ref, v_ref, gg_ref, misc_ref, gwh_ref, gwl_ref, gb_ref, ng_ref, hsum_ref, s0_ref,
                o_ref, sfin_ref, of_scr, st_scr, la_scr, *, nblk, tl):
    ph = pl.program_id(1)
    i = pl.program_id(2)
    fwd = ph == 0
    c = GLA_CHUNK
    nch = tl // c
    qkw = NH * GLA_DK

    @pl.when(i == 0)
    def _():
        st_scr[...] = _expand_state(s0_ref[...], HEAD_DIM, GLA_DK)

    gr = misc_ref[:, MISC_GR:MISC_GR + LANES]
    logits = (_dot(gr, gwh_ref[...]) + _dot(gr, gwl_ref[...])) + gb_ref[...]
    la_scr[...] = _log_sigmoid(logits) * (1.0 / GLA_TAU)

    sgn = jnp.where(fwd, 1, -1)
    trib = jnp.where(_tri_mask(c, c, sgn), 1.0, 0.0).astype(bf16)
    tri8 = _tri_mask(NH * c, c, sgn)
    qk_head = lax.broadcasted_iota(jnp.int32, (1, qkw), 1) // GLA_DK
    bd = (lax.broadcasted_iota(jnp.int32, (MIX_W, qkw), 0) // HEAD_DIM
          == lax.broadcasted_iota(jnp.int32, (MIX_W, qkw), 1) // GLA_DK)
    brow = pl.multiple_of(jnp.where(fwd, i, nblk - 1 - i) * tl, tl)

    def chunk(j, carry):
        cj = jnp.where(fwd, j, nch - 1 - j)
        off = pl.multiple_of(cj * c, c)
        q = qk_ref[pl.ds(off, c), 0:qkw].astype(f32) * (GLA_DK ** -0.5)
        k = qk_ref[pl.ds(off, c), qkw:2 * qkw].astype(f32)
        vb = v_ref[pl.ds(off, c), :]
        b = _sel_dot(trib, la_scr[pl.ds(off, c), :])
        btot = jnp.where(fwd, b[c - 1:c, :], b[0:1, :])
        bmid = b[c // 2 - 1:c // 2, :]
        qd = q * jnp.exp(b)
        qi = q * jnp.exp(b - bmid)
        kd = (k * jnp.exp(bmid - b)).astype(bf16)
        kt = (k * jnp.exp(btot - b)).astype(bf16)
        qst = jnp.concatenate([jnp.where(qk_head == h, qi, 0.0) for h in range(NH)], axis=0).astype(bf16)
        att = jnp.where(tri8, _dot(qst, kd, NT_DIMS), 0.0)
        attb = att.astype(bf16)
        att_cat = jnp.concatenate([attb[h * c:(h + 1) * c] for h in range(NH)], axis=1)
        o = _dot(att_cat, _block_diag_rows(vb, c, HEAD_DIM))
        st = st_scr[...]
        o += _dot(qd.astype(bf16), st.astype(bf16), NT_DIMS)
        kv = _dot(vb, kt, TN_DIMS)
        st_scr[...] = st * jnp.exp(btot) + jnp.where(bd, kv, 0.0)
        of_scr[pl.ds(pl.multiple_of(brow + off, c), c), :] += o
        return carry

    @pl.when(fwd)
    def _():
        of_scr[pl.ds(brow, tl), :] = jnp.zeros((tl, MIX_W), f32)

    lax.fori_loop(0, nch, chunk, 0, unroll=True)

    @pl.when(ph == 1)
    def _():
        ot = of_scr[pl.ds(brow, tl), :]
        ss = _sel_dot(ot * ot, hsum_ref[...], terms=2) * (1.0 / HEAD_DIM)
        o_ref[...] = (ot * lax.rsqrt(ss + NORM_EPS) * ng_ref[...] * _silu(gg_ref[...].astype(f32))).astype(bf16)

    @pl.when(i == nblk - 1)
    def _():
        sfin_ref[...] = _compact_state(st_scr[...])


def gla_mixer(p, gw_hi, gw_lo, gb, ng, hsum, s0, bsz, seq, tl=1024):
    tl = min(tl, seq)
    nblk = seq // tl
    qkw = NH * GLA_DK
    blk = lambda ph, i: jnp.where(ph == 0, i, nblk - 1 - i)
    tile = lambda col: pl.BlockSpec((None, tl, PT), lambda b, ph, i: (col, b * nblk + blk(ph, i), 0))
    const = lambda shape: pl.BlockSpec(shape, lambda b, ph, i: (0,) * len(shape))
    per_dir = lambda rows: pl.BlockSpec((None, rows, qkw), lambda b, ph, i: (ph, 0, 0))
    state = pl.BlockSpec((None, None, HEAD_DIM, qkw), lambda b, ph, i: (b, ph, 0, 0))
    return pl.pallas_call(
        functools.partial(_gla_kernel, nblk=nblk, tl=tl),
        out_shape=(jax.ShapeDtypeStruct((bsz * seq, MIX_W), bf16),
                   jax.ShapeDtypeStruct((bsz, 2, HEAD_DIM, qkw), f32)),
        grid=(bsz, 2, nblk),
        in_specs=[tile(T_GQK), tile(T_GV),
                  pl.BlockSpec((None, tl, PT),
                               lambda b, ph, i: (T_GG, b * nblk + jnp.where(ph == 0, nblk - 1, nblk - 1 - i), 0)),
                  tile(T_MISC),
                  per_dir(LANES), per_dir(LANES), per_dir(1), const((1, MIX_W)), const((MIX_W, MIX_W)), state],
        out_specs=(pl.BlockSpec((tl, MIX_W),
                                lambda b, ph, i: (b * nblk + jnp.where(ph == 0, nblk - 1, nblk - 1 - i), 0)),
                   state),
        scratch_shapes=[pltpu.VMEM((seq, MIX_W), f32), pltpu.VMEM((MIX_W, qkw), f32), pltpu.VMEM((tl, qkw), f32)],
        compiler_params=_cparams("parallel", "arbitrary", "arbitrary"),
        name="gla_mixer",
    )(p, p, p, p, gw_hi, gw_lo, gb, ng, hsum, s0)


def _rope(x, cos, sin_signed):
    lane = lax.broadcasted_iota(jnp.int32, (1, LANES), 1)
    half = DIFF_DK // 2
    rot = jnp.where((lane % DIFF_DK) < half, pltpu.roll(x, LANES - half, 1), pltpu.roll(x, half, 1))
    return x * cos + rot * sin_signed


def _diff_kernel(*refs, seq, ctx_len, rope, lam_init):
    if rope:
        (q_ref, k_ref, v_ref, ck_ref, cv_ref, cq_ref, sq_ref, ckk_ref, skk_ref, lam_ref, ng_ref,
         o_ref, kb_scr, vt_scr) = refs
    else:
        q_ref, k_ref, v_ref, lam_ref, ng_ref, o_ref, kb_scr, vt_scr = refs
    qi = pl.program_id(2)
    lane = lax.broadcasted_iota(jnp.int32, (1, LANES), 1)
    first = lane < HEAD_DIM

    def put_v(v, lo, hi):
        vt = v.T
        ones = jnp.ones((HEAD_DIM, hi - lo), f32)
        vt_scr[0, :, lo:hi] = jnp.concatenate([vt[0:HEAD_DIM], ones], axis=0).astype(bf16)
        vt_scr[1, :, lo:hi] = jnp.concatenate([vt[HEAD_DIM:], ones], axis=0).astype(bf16)

    @pl.when(qi == 0)
    def _():
        k = k_ref[...].astype(f32)
        if rope:
            k = _rope(k, ckk_ref[...], skk_ref[...])
        kb_scr[0:seq, :] = k.astype(bf16)
        put_v(v_ref[...].astype(f32), 0, seq)
        if ctx_len:
            kb_scr[seq:seq + ctx_len, :] = ck_ref[...].astype(bf16)
            put_v(cv_ref[...], seq, seq + ctx_len)

    q = q_ref[...].astype(f32)
    if rope:
        q = _rope(q, cq_ref[...], sq_ref[...])
    q = q * (DIFF_DK ** -0.5 * LOG2E)
    n_keys = seq + ctx_len
    ck = max(w for w in range(LANES, DIFF_KEY_CHUNK + 1, LANES) if n_keys % w == 0)
    nck = n_keys // ck
    tq = q.shape[0]
    sub = min(DIFF_Q_SUB, tq)
    jobs = [(qs, g) for qs in range(tq // sub) for g in range(4)]
    lp = lam_ref[...]
    lam = (jnp.exp(jnp.sum(lp[0:1] * lp[1:2], keepdims=True)) - jnp.exp(jnp.sum(lp[2:3] * lp[3:4], keepdims=True))
           + lam_init)

    def finish(qs, res):
        o = jnp.concatenate([res[0] - lam * res[1], res[2] - lam * res[3]], axis=0).T
        o2 = o * o
        ss_a = jnp.sum(jnp.where(first, o2, 0.0), axis=-1, keepdims=True)
        ss_b = jnp.sum(jnp.where(first, 0.0, o2), axis=-1, keepdims=True)
        inv = jnp.where(first, lax.rsqrt(ss_a * (1.0 / HEAD_DIM) + NORM_EPS),
                        lax.rsqrt(ss_b * (1.0 / HEAD_DIM) + NORM_EPS))
        o_ref[qs * sub:(qs + 1) * sub, :] = (o * inv * ng_ref[...] * (1.0 - lam_init)).astype(bf16)

    res = []
    s_prev = m_prev = None
    for j in range(len(jobs) + 1):
        s_cur, mvec, acc = [], None, None
        if j < len(jobs):
            qs, g = jobs[j]
            qm = jnp.where(lane // DIFF_DK == g, q[qs * sub:(qs + 1) * sub], 0.0).astype(bf16)
        for c in range(nck):
            if j < len(jobs):
                s = _dot(qm, kb_scr[c * ck:(c + 1) * ck, :], NT_DIMS)
                s_cur.append(s)
                for k in range(ck // LANES):
                    blk = s[:, k * LANES:(k + 1) * LANES]
                    mvec = blk if mvec is None else jnp.maximum(mvec, blk)
            if j > 0:
                p = jnp.exp2(s_prev[c] - m_prev).astype(bf16)
                part = _dot(vt_scr[jobs[j - 1][1] // 2, :, c * ck:(c + 1) * ck], p, NT_DIMS)
                acc = part if acc is None else acc + part
        if j > 0:
            res.append(acc[0:HEAD_DIM] / acc[HEAD_DIM:HEAD_DIM + 1])
            if len(res) == 4:
                finish(jobs[j - 1][0], res)
                res = []
        if j < len(jobs):
            s_prev, m_prev = s_cur, jnp.max(mvec, axis=-1, keepdims=True)


def diff_mixer(p, lam_p, ng, bsz, seq, lam_init, ctx=None, rope_tabs=None, tq=1024):
    tq = min(tq, seq)
    nq = seq // tq
    npair = MIX_W // LANES
    rope = rope_tabs is not None
    ctx_len = ctx[0].shape[1] if ctx is not None else 0
    assert rope == (ctx is not None)
    q_spec = pl.BlockSpec((None, tq, LANES), lambda b, hp, i: (T_DQ, b * nq + i, hp))
    k_spec = pl.BlockSpec((None, seq, LANES), lambda b, hp, i: (T_DK, b, hp))
    v_spec = pl.BlockSpec((None, seq, LANES), lambda b, hp, i: (T_DV, b, hp))
    const = lambda shape: pl.BlockSpec(shape, lambda b, hp, i: (0,) * len(shape))
    args = [p, p, p]
    specs = [q_spec, k_spec, v_spec]
    if rope:
        c_spec = pl.BlockSpec((None, ctx_len, LANES), lambda b, hp, i: (b, 0, hp))
        tab_q = pl.BlockSpec((tq, LANES), lambda b, hp, i: (i, 0))
        args += [ctx[0], ctx[1], rope_tabs[0], rope_tabs[1], rope_tabs[0], rope_tabs[1]]
        specs += [c_spec, c_spec, tab_q, tab_q, const((seq, LANES)), const((seq, LANES))]
    args += [lam_p, ng]
    specs += [const((8, LANES)), const((1, LANES))]
    return pl.pallas_call(
        functools.partial(_diff_kernel, seq=seq, ctx_len=ctx_len, rope=rope, lam_init=lam_init),
        out_shape=jax.ShapeDtypeStruct((bsz * seq, MIX_W), bf16),
        grid=(bsz, npair, nq),
        in_specs=specs,
        out_specs=pl.BlockSpec((tq, LANES), lambda b, hp, i: (b * nq + i, hp)),
        scratch_shapes=[pltpu.VMEM((seq + ctx_len, LANES), bf16), pltpu.VMEM((2, LANES, seq + ctx_len), bf16)],
        compiler_params=_cparams("parallel", "parallel", "arbitrary"),
        name="diff_mixer",
    )(*args)


def _ssd_kernel(sx_ref, bcr_ref, sxp_ref, bcp_ref, sxn_ref, bcn_ref, cw_ref, cb_ref, z_ref, dt_ref, dtb_ref, alog_ref,
                dexp_ref, ng_ref, exp_ref, s0_ref, o_ref, sfin_ref,
                yf_scr, st_scr, dt_scr, dta_scr, xs_ref, bc_ref, *, nblk, tl):
    ph = pl.program_id(1)
    i = pl.program_id(2)
    fwd = ph == 0
    c = SSD_CHUNK
    nch = tl // c
    rep = NH // SSD_G
    blk = jnp.where(fwd, i, nblk - 1 - i)

    @pl.when(i == 0)
    def _():
        st_scr[...] = _expand_state(s0_ref[...], SSD_N, HEAD_DIM)

    brow = pl.multiple_of(blk * tl, tl)

    @pl.when(fwd)
    def _():
        row = lax.broadcasted_iota(jnp.int32, (tl, 1), 0)
        for raw_ref, prev_ref, next_ref, dst_ref, lo in ((sx_ref, sxp_ref, sxn_ref, xs_ref, 0),
                                                        (bcr_ref, bcp_ref, bcn_ref, bc_ref, MIX_W)):
            x = raw_ref[...].astype(f32)
            w = cw_ref[:, lo:lo + x.shape[1]]
            prev_row = jnp.where(blk == 0, 0.0, prev_ref[HALO - 1:HALO, :].astype(f32))
            next_row = jnp.where(blk == nblk - 1, 0.0, next_ref[0:1, :].astype(f32))
            xm = jnp.where(row == 0, prev_row, pltpu.roll(x, 1, 0))
            xp = jnp.where(row == tl - 1, next_row, pltpu.roll(x, tl - 1, 0))
            dst_ref[pl.ds(brow, tl), :] = _silu(xm * w[0:1] + x * w[1:2] + xp * w[2:3]
                                                + cb_ref[:, lo:lo + x.shape[1]])

    dtv = _softplus(dt_ref[...].astype(f32) + dtb_ref[...])
    dtv = jnp.where(fwd, dtv, pltpu.roll(dtv, LANES - NH, 1))
    av = -jnp.exp(alog_ref[...])
    av = jnp.where(fwd, av, pltpu.roll(av, LANES - NH, 1))
    dt_scr[...] = dtv
    dta_scr[...] = dtv * av

    tri = _tri_mask(c, c, jnp.where(fwd, 1, -1))
    trib = jnp.where(tri, 1.0, 0.0).astype(bf16)
    head = lax.broadcasted_iota(jnp.int32, (1, MIX_W), 1) // HEAD_DIM
    bd = (lax.broadcasted_iota(jnp.int32, (MIX_W, MIX_W), 0) // SSD_N
          == lax.broadcasted_iota(jnp.int32, (MIX_W, MIX_W), 1) // HEAD_DIM)
    def chunk(j, carry):
        cj = jnp.where(fwd, j, nch - 1 - j)
        off = pl.multiple_of(cj * c, c)
        srow = pl.multiple_of(brow + off, c)
        dtc = dt_scr[pl.ds(off, c), :]
        cum = _sel_dot(trib, dta_scr[pl.ds(off, c), :])
        cum_t = cum.T
        dt_t = dtc.T
        earg = _sel_dot(cum, exp_ref[...])
        elast = jnp.where(fwd, earg[c - 1:c, :], earg[0:1, :])
        dtx = _sel_dot(dtc, exp_ref[...])
        bt = _per_head_groups(bc_ref[pl.ds(srow, c), 0:LANES])
        ct = _per_head_groups(bc_ref[pl.ds(srow, c), LANES:2 * LANES])
        x = xs_ref[pl.ds(srow, c), :]
        xb = x.astype(bf16)
        st = st_scr[...]
        y = _dot((ct * jnp.exp(earg)).astype(bf16), st.astype(bf16))
        btail = (bt * jnp.exp(elast - earg) * dtx).astype(bf16)
        new = _dot(btail, xb, TN_DIMS)
        st_scr[...] = st * jnp.exp(elast) + jnp.where(bd, new, 0.0)
        btb = bt.astype(bf16)
        cb = [_dot(jnp.where(head == g * rep, ct, 0.0).astype(bf16), btb, NT_DIMS) for g in range(SSD_G)]
        ws = []
        for h in range(NH):
            seg = cum[:, h:h + 1] - cum_t[h:h + 1, :]
            lm = jnp.exp(jnp.where(tri, seg, -jnp.inf))
            ws.append((cb[h // rep] * lm * dt_t[h:h + 1, :]).astype(bf16))
        y += _dot(jnp.concatenate(ws, axis=1), _block_diag_rows(xb, c, HEAD_DIM))
        yf_scr[pl.ds(pl.multiple_of(brow + off, c), c), :] += y
        return carry

    @pl.when(fwd)
    def _():
        yf_scr[pl.ds(brow, tl), :] = jnp.zeros((tl, MIX_W), f32)

    lax.fori_loop(0, nch, chunk, 0, unroll=True)

    @pl.when(ph == 1)
    def _():
        yt = yf_scr[pl.ds(brow, tl), :] + xs_ref[pl.ds(brow, tl), :] * dexp_ref[...]
        t = yt * _silu(z_ref[...].astype(f32))
        ms = jnp.mean(t * t, axis=-1, keepdims=True)
        o_ref[...] = (t * lax.rsqrt(ms + NORM_EPS) * ng_ref[...]).astype(bf16)

    @pl.when(i == nblk - 1)
    def _():
        sfin_ref[...] = _compact_state(st_scr[...])


def ssd_mixer(p, cw, cb, dtb, alog, dexp, ng, expand, s0, bsz, seq, tl=1024):
    tl = min(tl, seq)
    nblk = seq // tl
    n8 = bsz * seq // HALO
    per8 = tl // HALO
    blk = lambda ph, i: jnp.where(ph == 0, i, nblk - 1 - i)
    rows = lambda b, ph, i: b * nblk + blk(ph, i)
    const = lambda shape: pl.BlockSpec(shape, lambda b, ph, i: (0,) * len(shape))
    state = pl.BlockSpec((None, None, SSD_N, MIX_W), lambda b, ph, i: (b, ph, 0, 0))
    body = lambda tile, w: pl.BlockSpec((None, tl, w), lambda b, ph, i: (tile, rows(b, ph, i), 0))
    crow = lambda b, ph, i: jnp.where(ph == 0, rows(b, ph, i), b * nblk + nblk - 1)
    cbody = lambda tile, w: pl.BlockSpec((None, tl, w), lambda b, ph, i: (tile, crow(b, ph, i), 0))
    prev8 = lambda tile, w: pl.BlockSpec(
        (None, HALO, w), lambda b, ph, i: (tile, jnp.maximum(crow(b, ph, i) * per8 - 1, 0), 0))
    next8 = lambda tile, w: pl.BlockSpec(
        (None, HALO, w), lambda b, ph, i: (tile, jnp.minimum((crow(b, ph, i) + 1) * per8, n8 - 1), 0))
    bcw = 2 * SSD_G * SSD_N
    return pl.pallas_call(
        functools.partial(_ssd_kernel, nblk=nblk, tl=tl),
        out_shape=(jax.ShapeDtypeStruct((bsz * seq, MIX_W), bf16),
                   jax.ShapeDtypeStruct((bsz, 2, SSD_N, MIX_W), f32)),
        grid=(bsz, 2, nblk),
        in_specs=[cbody(T_SX, PT), cbody(T_MISC, bcw), prev8(T_SX, PT), prev8(T_MISC, bcw),
                  next8(T_SX, PT), next8(T_MISC, bcw), const((3, MIX_W + bcw)), const((1, MIX_W + bcw)),
                  pl.BlockSpec((None, tl, PT),
                               lambda b, ph, i: (T_SZ, b * nblk + jnp.where(ph == 0, nblk - 1, nblk - 1 - i), 0)),
                  pl.BlockSpec((None, tl, LANES), lambda b, ph, i: (T_MISC, rows(b, ph, i), MISC_DT // LANES)),
                  const((1, LANES)), const((1, LANES)), const((1, MIX_W)), const((1, MIX_W)),
                  const((LANES, MIX_W)), state],
        out_specs=(pl.BlockSpec((tl, MIX_W),
                                lambda b, ph, i: (b * nblk + jnp.where(ph == 0, nblk - 1, nblk - 1 - i), 0)),
                   state),
        scratch_shapes=[pltpu.VMEM((seq, MIX_W), f32), pltpu.VMEM((MIX_W, MIX_W), f32),
                        pltpu.VMEM((tl, LANES), f32), pltpu.VMEM((tl, LANES), f32),
                        pltpu.VMEM((seq, MIX_W), f32), pltpu.VMEM((seq, bcw), f32)],
        compiler_params=_cparams("parallel", "arbitrary", "arbitrary"),
        name="ssd_mixer",
    )(p, p, p, p, p, p, cw, cb.reshape(1, -1), p, p, dtb, alog, dexp, ng, expand, s0)


def _hy_filter_kernel(zf_ref, zb_ref, tf_ref, tb_ref, w1_ref, b1_ref, sw_ref, w2_ref, b2_ref,
                      w3f_ref, w3b_ref, b3f_ref, b3b_ref, df_ref, db_ref, skip_ref, r_ref, hf_scr, hb_scr, *, seq):
    first = jnp.logical_and(pl.program_id(0) == 0, pl.program_id(1) == 0)

    @pl.when(first)
    def _():
        for z_ref, h_scr in ((zf_ref, hf_scr), (zb_ref, hb_scr)):
            h = jnp.sin(sw_ref[...] * (_dot(w1_ref[...], z_ref[...], precision=HI) + b1_ref[...]))
            h_scr[...] = jnp.sin(sw_ref[...] * (_dot(w2_ref[...], h, precision=HI) + b2_ref[...]))

    hf = (_dot(w3f_ref[...], hf_scr[...], precision=HI) + b3f_ref[...]) * jnp.exp(-tf_ref[...] * jnp.abs(df_ref[...]))
    hb = (_dot(w3b_ref[...], hb_scr[...], precision=HI) + b3b_ref[...]) * jnp.exp(-tb_ref[...] * jnp.abs(db_ref[...]))
    den = (jnp.sum(jnp.abs(hf), axis=-1, keepdims=True) + jnp.sum(jnp.abs(hb), axis=-1, keepdims=True)) + NORM_EPS
    col = lax.broadcasted_iota(jnp.int32, (1, seq), 1)
    r_ref[:, 0:seq] = jnp.where(col == 0, 0.0, hb / den)
    r_ref[:, seq:2 * seq] = hf / den + jnp.where(col == 0, skip_ref[...], 0.0)


def hy_filters(fp, seq, cbf=128):
    t = jnp.arange(seq, dtype=f32) / seq
    t_rev = jnp.concatenate([t[:1], t[:0:-1]])

    def feats(tt):
        tc = tt[:, None]
        ang = 2.0 * math.pi * tc * jnp.arange(1, HY_BANDS + 1, dtype=f32)
        z = jnp.concatenate([tc, jnp.cos(ang), jnp.sin(ang)], axis=-1)
        return jnp.pad(z, ((0, 0), (0, HY_EMB_PAD - HY_EMB))).T

    ch = MIX_W
    nb = ch // cbf
    col = lambda a: a.reshape(-1, 1)
    w3t = fp['w3'].T.reshape(2, 2, ch, HY_HID)
    b3 = fp['b3'].reshape(2, 2, ch, 1)
    dec = fp['decay'].reshape(2, 2, ch, 1)
    const = lambda shape: pl.BlockSpec(shape, lambda o, j: (0,) * len(shape))
    sel = lambda d, last: pl.BlockSpec((None, None, cbf, last), lambda o, j: (o, d, j, 0))
    return pl.pallas_call(
        functools.partial(_hy_filter_kernel, seq=seq),
        out_shape=jax.ShapeDtypeStruct((2, ch, 2 * seq), f32),
        grid=(2, nb),
        in_specs=[const((HY_EMB_PAD, seq)), const((HY_EMB_PAD, seq)), const((1, seq)), const((1, seq)),
                  const((HY_HID, HY_EMB_PAD)), const((HY_HID, 1)), const((HY_HID, 1)),
                  const((HY_HID, HY_HID)), const((HY_HID, 1)),
                  sel(0, HY_HID), sel(1, HY_HID), sel(0, 1), sel(1, 1), sel(0, 1), sel(1, 1),
                  pl.BlockSpec((None, cbf, 1), lambda o, j: (o, j, 0))],
        out_specs=pl.BlockSpec((None, cbf, 2 * seq), lambda o, j: (o, j, 0)),
        scratch_shapes=[pltpu.VMEM((HY_HID, seq), f32), pltpu.VMEM((HY_HID, seq), f32)],
        compiler_params=_cparams("arbitrary", "arbitrary"),
        name="hy_filters",
    )(feats(t), feats(t_rev), t.reshape(1, seq), t_rev.reshape(1, seq),
      jnp.pad(fp['w1'], ((0, HY_EMB_PAD - HY_EMB), (0, 0))).T, col(fp['b1']), col(fp['sin_w']),
      fp['w2'].T, col(fp['b2']), w3t, w3t, b3, b3, dec, dec, fp['skip'].reshape(2, ch, 1))


def _hyena_kernel(cw_ref, cb_ref, hv_ref, h1_ref, h2_ref, r_ref, o_ref, acc_scr, *, bsz, n_i, cb_n):
    tb = HY_TB
    gr = tb // 2
    sub = tb // LANES
    seq = n_i * tb
    rows = n_i * bsz
    base = pl.program_id(0) * cb_n
    n_ch = MIX_W
    lane = lax.broadcasted_iota(jnp.int32, (1, tb), 1)
    zblk = jnp.zeros((bsz, tb), f32)

    def load(ref, ch):
        return jnp.concatenate(
            [jnp.concatenate([ref[ch, pl.ds(sub * ib + k, bsz, stride=sub * n_i), :] for k in range(sub)], axis=1)
             for ib in range(n_i)], axis=0)

    def short_conv(a, stream, ch):
        idx = stream * n_ch + base + ch
        prev = jnp.concatenate([zblk, a[:rows - bsz]], axis=0) if n_i > 1 else zblk
        nxt = jnp.concatenate([a[bsz:], zblk], axis=0) if n_i > 1 else zblk
        am = pltpu.roll(jnp.where(lane == tb - 1, prev, a), 1, 1)
        ap = pltpu.roll(jnp.where(lane == 0, nxt, a), tb - 1, 1)
        return (am * cw_ref[idx] + a * cw_ref[3 * n_ch + idx] + ap * cw_ref[6 * n_ch + idx]) + cb_ref[idx]

    def long_conv(u, order, ch):
        rrow = r_ref[order, pl.ds(ch, 1), :]
        bits = lambda a: lax.bitcast_convert_type(a.astype(bf16).astype(f32), jnp.int32)
        word = (bits(pltpu.roll(rrow, 1, 1)) & jnp.int32(-65536)) | lax.shift_right_logical(bits(rrow), 16)
        g = pltpu.bitcast(
            pltpu.roll(jnp.broadcast_to(word, (gr // 2, 2 * seq)), 0, 1, stride=2, stride_axis=0), bf16)
        for d in [0] + [e for e in range(-(n_i - 1), n_i) if e != 0]:
            n = n_i - abs(d)
            src = max(0, -d) * bsz
            dst = max(0, d) * bsz
            c0 = seq + d * tb
            m = jnp.concatenate([g[:, c0:c0 + tb], g[:, c0 - gr:c0 - gr + tb]], axis=0)
            part = _dot(u[src:src + n * bsz].astype(bf16), m)
            if d == 0:
                acc_scr[order] = part
            else:
                acc_scr[order, dst:dst + n * bsz, :] += part
        return acc_scr[order]

    def body(ch, carry):
        hv = short_conv(load(hv_ref, ch), 0, ch)
        hx1 = short_conv(load(h1_ref, ch), 1, ch)
        hx2 = short_conv(load(h2_ref, ch), 2, ch)
        zz = hx1 * long_conv(hv, 0, ch)
        out = hx2 * long_conv(zz, 1, ch)
        for ib in range(n_i):
            for k in range(sub):
                o_ref[ch, pl.ds(sub * ib + k, bsz, stride=sub * n_i), :] = (
                    out[ib * bsz:(ib + 1) * bsz, k * LANES:(k + 1) * LANES])
        return carry

    lax.fori_loop(0, cb_n, body, 0, unroll=8)


def hyena_mixer(hy_t, r, cw, cb, bsz, seq, cb_n=8):
    n_i = seq // HY_TB
    ch = MIX_W
    nblk = ch // cb_n
    n_rows = bsz * seq // LANES
    x3 = hy_t.reshape(3 * ch, n_rows, LANES)
    stream = lambda s: pl.BlockSpec((cb_n, n_rows, LANES), lambda j: (s * nblk + j, 0, 0))
    smem = pl.BlockSpec(memory_space=pltpu.SMEM)
    out = pl.pallas_call(
        functools.partial(_hyena_kernel, bsz=bsz, n_i=n_i, cb_n=cb_n),
        out_shape=jax.ShapeDtypeStruct((ch, n_rows, LANES), f32),
        grid=(nblk,),
        in_specs=[smem, smem, stream(0), stream(1), stream(2),
                  pl.BlockSpec((2, cb_n, 2 * seq), lambda j: (0, j, 0))],
        out_specs=pl.BlockSpec((cb_n, n_rows, LANES), lambda j: (j, 0, 0)),
        scratch_shapes=[pltpu.VMEM((2, bsz * n_i, HY_TB), f32)],
        compiler_params=_cparams("parallel"),
        name="hyena_mixer",
    )(cw.reshape(-1), cb, x3, x3, x3, r)
    return out.reshape(ch, bsz * seq)


_IN_SIZES = (NH * GLA_DK, NH * GLA_DK, MIX_W, MIX_W, 2 * GLA_RANK, MIX_W, MIX_W, MIX_W,
             MIX_W, MIX_W + 2 * SSD_G * SSD_N, 2 * NH, 3 * MIX_W)
_IN_OFFS = np.concatenate([[0], np.cumsum(_IN_SIZES)]).tolist()


def _prep_w_in(w_in):
    (gq, gk, gv, gg, gr, dq, dk, dv, sz, sxbc, sdt, hy) = [w_in[:, _IN_OFFS[i]:_IN_OFFS[i + 1]]
                                                          for i in range(len(_IN_SIZES))]
    d = w_in.shape[0]
    zeros = lambda n: jnp.zeros((d, n), w_in.dtype)
    misc = jnp.concatenate([sxbc[:, MIX_W:], gr, zeros(LANES - 2 * GLA_RANK), sdt, zeros(LANES - 2 * NH)], axis=1)
    w_tok = jnp.concatenate([gq, gk, gv, gg, dq, dk, dv, sz, sxbc[:, :MIX_W], misc], axis=1)
    w_tok = w_tok.astype(bf16).reshape(d, N_PT, PT).transpose(1, 0, 2)
    return w_tok, hy.T.astype(bf16)


def _prep_gate_w(gate_w):
    w = jnp.zeros((2, LANES, gate_w.shape[2]), f32)
    for d in range(2):
        w = w.at[d, d * GLA_RANK:(d + 1) * GLA_RANK, :].set(gate_w[d])
    hi = w.astype(bf16)
    return hi, (w - hi.astype(f32)).astype(bf16)


def _rope_tables(seq):
    rows = seq // GRID_W
    r, col = jnp.meshgrid(jnp.arange(rows), jnp.arange(GRID_W), indexing='ij')
    r = r.reshape(-1).astype(f32)
    col = col.reshape(-1).astype(f32)
    nf = DIFF_DK // 4
    inv = ROPE_BASE ** (-jnp.arange(nf, dtype=f32) / nf)
    ang = jnp.concatenate([r[:, None] * inv, col[:, None] * inv], axis=-1)
    cos, sin = jnp.cos(ang), jnp.sin(ang)
    reps = LANES // DIFF_DK
    return jnp.tile(jnp.concatenate([cos, cos], axis=-1), (1, reps)), jnp.tile(jnp.concatenate([-sin, sin], axis=-1), (1, reps))


def _const_tables():
    hsum = np.kron(np.eye(NH, dtype=np.float32), np.ones((HEAD_DIM, HEAD_DIM), np.float32))
    expand = np.zeros((LANES, MIX_W), np.float32)
    for h in range(NH):
        expand[h, h * HEAD_DIM:(h + 1) * HEAD_DIM] = 1.0
    return tuple(jnp.asarray(a, dtype=bf16) for a in (hsum, expand))


def _pack_state(s):
    bsz, _, h, a, b = s.shape
    return s.transpose(0, 1, 4, 2, 3).reshape(bsz, 2, b, h * a)


def _unpack_state(st, a):
    bsz, _, b, _ = st.shape
    return st.reshape(bsz, 2, b, NH, a).transpose(0, 1, 3, 4, 2)


def _layer(x, mod, rows_per_mod, lw, layer, consts, bsz, seq, lam_init, r_filt, ctx, rope_tabs, final_g,
           final_norm):
    hsum, expand = consts
    p, hy_t = in_proj(x, mod, lw['norm1_g'], lw['w_tok'], lw['w_hyT'], rows_per_mod)
    if ctx is None:
        gla_s0 = jnp.zeros((bsz, 2, HEAD_DIM, NH * GLA_DK), f32)
        ssd_s0 = jnp.zeros((bsz, 2, SSD_N, MIX_W), f32)
        dctx = None
    else:
        ctx_k, ctx_v, gla_s, ssd_s = ctx
        gla_s0 = _pack_state(gla_s)
        ssd_s0 = _pack_state(ssd_s)
        dctx = (ctx_k.reshape(bsz, -1, MIX_W), ctx_v.reshape(bsz, -1, MIX_W))
    o_gla, gla_fin = gla_mixer(p, lw['gla_gw_hi'], lw['gla_gw_lo'], lw['gla_gb'], lw['gla_ng'], hsum, gla_s0,
                               bsz, seq)
    o_diff = diff_mixer(p, lw['lam_p'], lw['diff_ng'], bsz, seq, lam_init, dctx, rope_tabs)
    o_ssd, ssd_fin = ssd_mixer(p, lw['ssd_conv_w'], lw['ssd_conv_b'], lw['ssd_dtb'], lw['ssd_alog'], lw['ssd_dexp'], lw['ssd_ng'],
                               expand, ssd_s0, bsz, seq)
    o_hy = hyena_mixer(hy_t, r_filt, lw['hy_cw'], lw['hy_cb'], bsz, seq)
    x = out_proj(x, mod, o_gla, o_diff, o_ssd, o_hy, lw['w_out'], layer, rows_per_mod)
    x = mlp(x, mod, lw['norm2_g'], lw['mlp_w1'], lw['mlp_w2'], layer, final_g, rows_per_mod, final_norm)
    return x, p, gla_fin, ssd_fin


def kernel(x_prompt, x_sample, cache_diff_k, cache_diff_v, state_gla, state_ssd, c, c_ctx, ada_w, ada_b, norm1_g, norm2_g, w_in, w_out, gla_gate_w, gla_gate_b, gla_norm_g, diff_lambda, diff_norm_g, ssd_conv_w, ssd_conv_b, ssd_dt_bias, ssd_a_log, ssd_d, ssd_norm_g, hy_conv_w, hy_conv_b, hy_f_w1, hy_f_b1, hy_f_w2, hy_f_b2, hy_f_w3, hy_f_b3, hy_sin_w, hy_decay, hy_skip, mlp_w1, mlp_w2, final_g):
    bp, lp, d = x_prompt.shape
    bs, ls, _ = x_sample.shape
    depth = w_in.shape[0]
    consts = _const_tables()
    rope_tabs = _rope_tables(ls)
    n_c = 16
    cvec = jnp.concatenate([c_ctx[None], c, jnp.zeros((n_c - 1 - bs, d), f32)], axis=0)
    hp = x_prompt.reshape(bp * lp, d)
    hs = x_sample.reshape(bs * ls, d)
    ks_, vs_, gs_, ss_ = [], [], [], []
    w_out_b, mlp_w1_b, mlp_w2_b = w_out.astype(bf16), mlp_w1.astype(bf16), mlp_w2.astype(bf16)
    for l in range(depth):
        w_tok, w_hyT = _prep_w_in(w_in[l])
        pad_l = lambda a: jnp.pad(a, (0, LANES - a.shape[0])).reshape(1, LANES)
        gw_hi, gw_lo = _prep_gate_w(gla_gate_w[l])
        lw = dict(
            norm1_g=norm1_g[l], norm2_g=norm2_g[l], w_tok=w_tok, w_hyT=w_hyT, w_out=w_out_b,
            mlp_w1=mlp_w1_b, mlp_w2=mlp_w2_b,
            gla_gw_hi=gw_hi, gla_gw_lo=gw_lo, gla_gb=gla_gate_b[l].reshape(2, 1, -1), gla_ng=jnp.tile(gla_norm_g[l], NH).reshape(1, MIX_W),
            lam_p=jnp.pad(diff_lambda[l], ((0, 4), (0, LANES - DIFF_DK))),
            diff_ng=jnp.tile(diff_norm_g[l], LANES // HEAD_DIM).reshape(1, LANES),
            ssd_conv_w=ssd_conv_w[l], ssd_conv_b=ssd_conv_b[l],
            ssd_dtb=pad_l(ssd_dt_bias[l].reshape(-1)), ssd_alog=pad_l(ssd_a_log[l].reshape(-1)),
            ssd_dexp=jnp.repeat(ssd_d[l], HEAD_DIM).reshape(1, MIX_W), ssd_ng=ssd_norm_g[l].reshape(1, MIX_W),
            hy_cw=hy_conv_w[l], hy_cb=hy_conv_b[l],
        )
        fp = dict(w1=hy_f_w1[l], b1=hy_f_b1[l], w2=hy_f_w2[l], b2=hy_f_b2[l], w3=hy_f_w3[l], b3=hy_f_b3[l],
                  sin_w=hy_sin_w[l], decay=hy_decay[l], skip=hy_skip[l])
        mod = ada_mod(cvec, ada_w, ada_b, l).reshape(n_c, 6, d)
        lam_init = 0.8 - 0.6 * math.exp(-0.3 * l)
        last = l == depth - 1
        hp, p_p, g_l, s_l = _layer(hp, mod[0:1], bp * lp, lw, l, consts, bp, lp, lam_init, hy_filters(fp, lp),
                                   None, None, final_g, last)
        ks_.append(p_p[T_DK].astype(f32).reshape(bp, lp, NH, 2 * DIFF_DK))
        vs_.append(p_p[T_DV].astype(f32).reshape(bp, lp, NH, HEAD_DIM))
        gs_.append(_unpack_state(g_l, GLA_DK))
        ss_.append(_unpack_state(s_l, HEAD_DIM))
        hs, _, _, _ = _layer(hs, mod[1:1 + bs], ls, lw, l, consts, bs, ls, lam_init, hy_filters(fp, ls),
                             (cache_diff_k[:, l], cache_diff_v[:, l], state_gla[:, l], state_ssd[:, l]),
                             rope_tabs, final_g, last)
    return (hp.reshape(bp, lp, d), hs.reshape(bs, ls, d), jnp.stack(ks_, axis=1), jnp.stack(vs_, axis=1),
            jnp.stack(gs_, axis=1), jnp.stack(ss_, axis=1))
```

```python
import functools
import math

import numpy as np
import jax
import jax.numpy as jnp
from jax import lax
from jax.experimental import pallas as pl
from jax.experimental.pallas import tpu as pltpu

f32 = jnp.float32
bf16 = jnp.bfloat16
HI = lax.Precision.HIGHEST

LANES = 128
VMEM_LIMIT = 56 * 2**20

D_MODEL = 2048
GRID_W = 64
MIX_W = D_MODEL // 4
HEAD_DIM = 64
NH = MIX_W // HEAD_DIM
D_FF = 4 * D_MODEL
NORM_EPS = 1e-6
LOG2E = 1.0 / math.log(2.0)
GLA_DK = HEAD_DIM // 2
GLA_RANK = 16
GLA_TAU = 16.0
GLA_CHUNK = 128
DIFF_DK = HEAD_DIM // 2
ROPE_BASE = 10000.0
SSD_N = 64
SSD_G = 2
SSD_CHUNK = 128
HY_BANDS = 16
HY_EMB = 2 * HY_BANDS + 1
HY_EMB_PAD = 40
HY_HID = 64
DIFF_KEY_CHUNK = 512
DIFF_Q_SUB = 512
HALO = 16
MLP_TF = 1024
HY_TB = 256

PT = 512
(T_GQK, T_GV, T_GG, T_DQ, T_DK, T_DV, T_SZ, T_SX, T_MISC) = range(9)
N_PT = 9
PT_GROUP = 3
MISC_GR = 256
MISC_DT = 384
SH1, SC1, G1, SH2, SC2, G2 = range(6)

NT_DIMS = (((1,), (1,)), ((), ()))
TN_DIMS = (((0,), (0,)), ((), ()))


def _cparams(*sem):
    return pltpu.CompilerParams(dimension_semantics=sem, vmem_limit_bytes=VMEM_LIMIT)


def _sigmoid(x):
    return 1.0 / (1.0 + jnp.exp(-x))


def _silu(x):
    return x * _sigmoid(x)


def _softplus(x):
    return jnp.maximum(x, 0.0) + jnp.log1p(jnp.exp(-jnp.abs(x)))


def _log_sigmoid(x):
    return jnp.minimum(x, 0.0) - jnp.log1p(jnp.exp(-jnp.abs(x)))


def _tri_mask(rows, c, sgn):
    r_i = lax.broadcasted_iota(jnp.int32, (rows, c), 0) & (c - 1)
    c_i = lax.broadcasted_iota(jnp.int32, (rows, c), 1)
    return (r_i - c_i) * sgn >= 0


def _expand_state(s, row_w, col_w):
    rows, cols = NH * row_w, NH * col_w
    bd = (lax.broadcasted_iota(jnp.int32, (rows, cols), 0) // row_w
          == lax.broadcasted_iota(jnp.int32, (rows, cols), 1) // col_w)
    return jnp.where(bd, jnp.concatenate([s] * NH, axis=0), 0.0)


def _compact_state(st):
    row_w = st.shape[0] // NH
    out = st[0:row_w]
    for h in range(1, NH):
        out += st[h * row_w:(h + 1) * row_w]
    return out


def _split(x, n):
    parts = []
    for _ in range(n - 1):
        hi = x.astype(bf16)
        parts.append(hi)
        x = x - hi.astype(f32)
    parts.append(x.astype(bf16))
    return parts


def _sel_dot(a, b, dims=None, terms=3):
    if a.dtype == bf16:
        prods = [_dot(a, p, dims) for p in _split(b, terms)]
    else:
        prods = [_dot(p, b, dims) for p in _split(a, terms)]
    return functools.reduce(lambda x, y: x + y, prods)


def _block_diag_rows(x, rows, col_w):
    shape = (NH * rows, NH * col_w)
    keep = (lax.broadcasted_iota(jnp.int32, shape, 0) // rows == lax.broadcasted_iota(jnp.int32, shape, 1) // col_w)
    return jnp.where(keep, jnp.concatenate([x] * NH, axis=0), jnp.zeros((), x.dtype))


def _per_head_groups(g2):
    assert SSD_G == 2 and SSD_N == HEAD_DIM and LANES == 2 * SSD_N
    first = lax.broadcasted_iota(jnp.int32, (1, LANES), 1) < SSD_N
    swapped = pltpu.roll(g2, SSD_N, 1)
    g0 = jnp.where(first, g2, swapped)
    g1 = jnp.where(first, swapped, g2)
    rep = NH * HEAD_DIM // (SSD_G * LANES)
    return jnp.concatenate([g0] * rep + [g1] * rep, axis=1)


def _dot(a, b, dims=None, precision=None):
    if dims is None:
        return jnp.dot(a, b, preferred_element_type=f32, precision=precision)
    return lax.dot_general(a, b, dims, preferred_element_type=f32, precision=precision)


def _ada_kernel(c_ref, w_ref, b_ref, o_ref):
    s = _silu(c_ref[...]).astype(bf16)
    o_ref[...] = _dot(s, w_ref[...].astype(bf16)) + b_ref[...]


def ada_mod(cvec, w, b, layer, tn=1024):
    m, d = cvec.shape
    n = w.shape[2]
    return pl.pallas_call(
        _ada_kernel,
        out_shape=jax.ShapeDtypeStruct((m, n), f32),
        grid=(n // tn,),
        in_specs=[pl.BlockSpec((m, d), lambda j: (0, 0)),
                  pl.BlockSpec((None, d, tn), lambda j: (layer, 0, j)),
                  pl.BlockSpec((None, 1, tn), lambda j: (layer, 0, j))],
        out_specs=pl.BlockSpec((m, tn), lambda j: (0, j)),
        compiler_params=_cparams("arbitrary"),
        name="ada_mod",
    )(cvec, w, b.reshape(b.shape[0], 1, n))


def _norm_mod(x, g, sc, sh):
    ms = jnp.mean(x * x, axis=-1, keepdims=True)
    return (x * lax.rsqrt(ms + NORM_EPS) * g) * (1.0 + sc) + sh


def _norm_next_chunk(xn_ref, modn_ref, g_ref, sc, sh, u_scr, slot, row0):
    xn = xn_ref[...]
    u = _norm_mod(xn, g_ref[...], modn_ref[sc:sc + 1, :], modn_ref[sh:sh + 1, :])
    u_scr[slot, pl.ds(row0, xn.shape[0]), :] = u.astype(bf16)


def _inproj_kernel(x_ref, mod_ref, g_ref, wt_ref, wh_ref, p_ref, h_ref, u_scr, *, n_hy):
    j = pl.program_id(1)

    @pl.when(j == 0)
    def _():
        u = _norm_mod(x_ref[...], g_ref[...], mod_ref[SC1:SC1 + 1, :], mod_ref[SH1:SH1 + 1, :])
        u_scr[...] = u.astype(bf16)

    @pl.when(j < n_hy)
    def _():
        h_ref[...] = _dot(wh_ref[...], u_scr[...], NT_DIMS)

    @pl.when(j >= n_hy)
    def _():
        for k in range(PT_GROUP):
            p_ref[k] = _dot(u_scr[...], wt_ref[k]).astype(p_ref.dtype)


def in_proj(x, mod, g, w_tok, w_hyT, rows_per_mod, tm=1024):
    t, d = x.shape
    assert t % tm == 0 and rows_per_mod % tm == 0, (t, rows_per_mod, tm)
    n_hy = w_hyT.shape[0] // PT
    n_grp = N_PT // PT_GROUP
    return pl.pallas_call(
        functools.partial(_inproj_kernel, n_hy=n_hy),
        out_shape=(jax.ShapeDtypeStruct((N_PT, t, PT), bf16), jax.ShapeDtypeStruct((n_hy * PT, t), f32)),
        grid=(t // tm, n_hy + n_grp),
        in_specs=[pl.BlockSpec((tm, d), lambda i, j: (i, 0)),
                  pl.BlockSpec((None, 6, d), lambda i, j: ((i * tm) // rows_per_mod, 0, 0)),
                  pl.BlockSpec((1, d), lambda i, j: (0, 0)),
                  pl.BlockSpec((PT_GROUP, d, PT), lambda i, j: (jnp.maximum(j - n_hy, 0), 0, 0)),
                  pl.BlockSpec((PT, d), lambda i, j: (jnp.minimum(j, n_hy - 1), 0))],
        out_specs=(pl.BlockSpec((PT_GROUP, tm, PT), lambda i, j: (jnp.maximum(j - n_hy, 0), i, 0)),
                   pl.BlockSpec((PT, tm), lambda i, j: (jnp.minimum(j, n_hy - 1), i))),
        scratch_shapes=[pltpu.VMEM((tm, d), bf16)],
        compiler_params=_cparams("parallel", "arbitrary"),
        name="in_proj",
    )(x, mod, g.reshape(1, d), w_tok, w_hyT)


def _outproj_kernel(x_ref, mod_ref, a_ref, b_ref, c_ref, dT_ref, w_ref, o_ref):
    acc = _dot(a_ref[...], w_ref[0:MIX_W, :])
    acc += _dot(b_ref[...], w_ref[MIX_W:2 * MIX_W, :])
    acc += _dot(c_ref[...], w_ref[2 * MIX_W:3 * MIX_W, :])
    acc += _dot(dT_ref[...].astype(bf16), w_ref[3 * MIX_W:4 * MIX_W, :], TN_DIMS)
    o_ref[...] = x_ref[...] + mod_ref[G1:G1 + 1, :] * acc


def out_proj(x, mod, o_a, o_b, o_c, o_dT, w_out, layer, rows_per_mod, tm=512):
    t, d = x.shape
    assert t % tm == 0 and rows_per_mod % tm == 0, (t, rows_per_mod, tm)
    tok = lambda i: (i, 0)
    return pl.pallas_call(
        _outproj_kernel,
        out_shape=jax.ShapeDtypeStruct((t, d), f32),
        grid=(t // tm,),
        in_specs=[pl.BlockSpec((tm, d), tok),
                  pl.BlockSpec((None, 6, d), lambda i: ((i * tm) // rows_per_mod, 0, 0)),
                  pl.BlockSpec((tm, MIX_W), tok), pl.BlockSpec((tm, MIX_W), tok), pl.BlockSpec((tm, MIX_W), tok),
                  pl.BlockSpec((MIX_W, tm), lambda i: (0, i)),
                  pl.BlockSpec((None, d, d), lambda i: (layer, 0, 0))],
        out_specs=pl.BlockSpec((tm, d), tok),
        compiler_params=_cparams("parallel"),
        name="out_proj",
    )(x, mod, o_a, o_b, o_c, o_dT, w_out)


def _mlp_kernel(xc_ref, mod_ref, xn_ref, modn_ref, g_ref, w1_ref, w2_ref, fg_ref, o_ref, u_scr, *, final_norm):
    io = pl.program_id(0)
    j = pl.program_id(1)
    rows = xn_ref.shape[0]
    row0 = pl.multiple_of(j * rows, rows)

    @pl.when(io == 0)
    def _():
        _norm_next_chunk(xn_ref, modn_ref, g_ref, SC2, SH2, u_scr, 0, row0)

    @pl.when(jnp.logical_and(io > 0, j == 0))
    def _():
        o_ref[...] = jnp.zeros_like(o_ref)

    @pl.when(io > 0)
    def _():
        _norm_next_chunk(xn_ref, modn_ref, g_ref, SC2, SH2, u_scr, io % 2, row0)
        h = jnp.maximum(_dot(u_scr[1 - io % 2], w1_ref[...]), 0.0)
        hb = (h * h).astype(bf16)
        d = o_ref.shape[1]
        for n in range(d // PT):
            cols = slice(n * PT, (n + 1) * PT)
            o_ref[:, cols] += mod_ref[G2:G2 + 1, cols] * _dot(hb, w2_ref[:, cols])
        o_ref[pl.ds(row0, rows), :] += xc_ref[...]

    if final_norm:
        @pl.when(jnp.logical_and(io > 0, j == pl.num_programs(1) - 1))
        def _():
            y = o_ref[...]
            ms = jnp.mean(y * y, axis=-1, keepdims=True)
            o_ref[...] = y * lax.rsqrt(ms + NORM_EPS) * fg_ref[...]


def mlp(x, mod, g, w1, w2, layer, final_g, rows_per_mod, final_norm, tm=1024, tf=MLP_TF):
    t, d = x.shape
    assert t % tm == 0 and rows_per_mod % tm == 0, (t, rows_per_mod, tm)
    n_f = w1.shape[2] // tf
    rows = tm // n_f
    n_tiles = t // tm
    tile = lambda io: jnp.maximum(io - 1, 0)
    wj = lambda io, j: jnp.where(io == 0, 0, j)
    chunk_of = lambda ti, j: jnp.minimum(ti * n_f + j, t // rows - 1)
    mod_of = lambda ti: jnp.minimum((ti * tm) // rows_per_mod, mod.shape[0] - 1)
    return pl.pallas_call(
        functools.partial(_mlp_kernel, final_norm=final_norm),
        out_shape=jax.ShapeDtypeStruct((t, d), f32),
        grid=(n_tiles + 1, n_f),
        in_specs=[pl.BlockSpec((rows, d), lambda io, j: (chunk_of(tile(io), j), 0)),
                  pl.BlockSpec((None, 6, d), lambda io, j: (mod_of(tile(io)), 0, 0)),
                  pl.BlockSpec((rows, d), lambda io, j: (chunk_of(jnp.minimum(io, n_tiles - 1), j), 0)),
                  pl.BlockSpec((None, 6, d), lambda io, j: (mod_of(jnp.minimum(io, n_tiles - 1)), 0, 0)),
                  pl.BlockSpec((1, d), lambda io, j: (0, 0)),
                  pl.BlockSpec((None, d, tf), lambda io, j: (layer, 0, wj(io, j))),
                  pl.BlockSpec((None, tf, d), lambda io, j: (layer, wj(io, j), 0)),
                  pl.BlockSpec((1, d), lambda io, j: (0, 0))],
        out_specs=pl.BlockSpec((tm, d), lambda io, j: (tile(io), 0)),
        scratch_shapes=[pltpu.VMEM((2, tm, d), bf16)],
        compiler_params=_cparams("arbitrary", "arbitrary"),
        name="mlp",
    )(x, mod, x, mod, g.reshape(1, d), w1, w2, final_g.reshape(1, d))


def _gla_kernel(qk_ref, v_ref, gg_ref, misc_ref, gwh_ref, gwl_ref, gb_ref, ng_ref, hsum_ref, s0_ref,
                o_ref, sfin_ref, of_scr, st_scr, la_scr, *, nblk, tl):
    ph = pl.program_id(1)
    i = pl.program_id(2)
    fwd = ph == 0
    c = GLA_CHUNK
    nch = tl // c
    qkw = NH * GLA_DK

    @pl.when(i == 0)
    def _():
        st_scr[...] = _expand_state(s0_ref[...], HEAD_DIM, GLA_DK)

    gr = misc_ref[:, MISC_GR:MISC_GR + LANES]
    logits = (_dot(gr, gwh_ref[...]) + _dot(gr, gwl_ref[...])) + gb_ref[...]
    la_scr[...] = _log_sigmoid(logits) * (1.0 / GLA_TAU)

    sgn = jnp.where(fwd, 1, -1)
    trib = jnp.where(_tri_mask(c, c, sgn), 1.0, 0.0).astype(bf16)
    tri8 = _tri_mask(NH * c, c, sgn)
    qk_head = lax.broadcasted_iota(jnp.int32, (1, qkw), 1) // GLA_DK
    bd = (lax.broadcasted_iota(jnp.int32, (MIX_W, qkw), 0) // HEAD_DIM
          == lax.broadcasted_iota(jnp.int32, (MIX_W, qkw), 1) // GLA_DK)
    brow = pl.multiple_of(jnp.where(fwd, i, nblk - 1 - i) * tl, tl)

    def chunk(j, carry):
        cj = jnp.where(fwd, j, nch - 1 - j)
        off = pl.multiple_of(cj * c, c)
        q = qk_ref[pl.ds(off, c), 0:qkw].astype(f32) * (GLA_DK ** -0.5)
        k = qk_ref[pl.ds(off, c), qkw:2 * qkw].astype(f32)
        vb = v_ref[pl.ds(off, c), :]
        b = _sel_dot(trib, la_scr[pl.ds(off, c), :])
        btot = jnp.where(fwd, b[c - 1:c, :], b[0:1, :])
        bmid = b[c // 2 - 1:c // 2, :]
        qd = q * jnp.exp(b)
        qi = q * jnp.exp(b - bmid)
        kd = (k * jnp.exp(bmid - b)).astype(bf16)
        kt = (k * jnp.exp(btot - b)).astype(bf16)
        qst = jnp.concatenate([jnp.where(qk_head == h, qi, 0.0) for h in range(NH)], axis=0).astype(bf16)
        att = jnp.where(tri8, _dot(qst, kd, NT_DIMS), 0.0)
        attb = att.astype(bf16)
        att_cat = jnp.concatenate([attb[h * c:(h + 1) * c] for h in range(NH)], axis=1)
        o = _dot(att_cat, _block_diag_rows(vb, c, HEAD_DIM))
        st = st_scr[...]
        o += _dot(qd.astype(bf16), st.astype(bf16), NT_DIMS)
        kv = _dot(vb, kt, TN_DIMS)
        st_scr[...] = st * jnp.exp(btot) + jnp.where(bd, kv, 0.0)
        of_scr[pl.ds(pl.multiple_of(brow + off, c), c), :] += o
        return carry

    @pl.when(fwd)
    def _():
        of_scr[pl.ds(brow, tl), :] = jnp.zeros((tl, MIX_W), f32)

    lax.fori_loop(0, nch, chunk, 0, unroll=True)

    @pl.when(ph == 1)
    def _():
        ot = of_scr[pl.ds(brow, tl), :]
        ss = _sel_dot(ot * ot, hsum_ref[...], terms=2) * (1.0 / HEAD_DIM)
        o_ref[...] = (ot * lax.rsqrt(ss + NORM_EPS) * ng_ref[...] * _silu(gg_ref[...].astype(f32))).astype(bf16)

    @pl.when(i == nblk - 1)
    def _():
        sfin_ref[...] = _compact_state(st_scr[...])


def gla_mixer(p, gw_hi, gw_lo, gb, ng, hsum, s0, bsz, seq, tl=1024):
    tl = min(tl, seq)
    nblk = seq // tl
    qkw = NH * GLA_DK
    blk = lambda ph, i: jnp.where(ph == 0, i, nblk - 1 - i)
    tile = lambda col: pl.BlockSpec((None, tl, PT), lambda b, ph, i: (col, b * nblk + blk(ph, i), 0))
    const = lambda shape: pl.BlockSpec(shape, lambda b, ph, i: (0,) * len(shape))
    per_dir = lambda rows: pl.BlockSpec((None, rows, qkw), lambda b, ph, i: (ph, 0, 0))
    state = pl.BlockSpec((None, None, HEAD_DIM, qkw), lambda b, ph, i: (b, ph, 0, 0))
    return pl.pallas_call(
        functools.partial(_gla_kernel, nblk=nblk, tl=tl),
        out_shape=(jax.ShapeDtypeStruct((bsz * seq, MIX_W), bf16),
                   jax.ShapeDtypeStruct((bsz, 2, HEAD_DIM, qkw), f32)),
        grid=(bsz, 2, nblk),
        in_specs=[tile(T_GQK), tile(T_GV),
                  pl.BlockSpec((None, tl, PT),
                               lambda b, ph, i: (T_GG, b * nblk + jnp.where(ph == 0, nblk - 1, nblk - 1 - i), 0)),
                  tile(T_MISC),
                  per_dir(LANES), per_dir(LANES), per_dir(1), const((1, MIX_W)), const((MIX_W, MIX_W)), state],
        out_specs=(pl.BlockSpec((tl, MIX_W),
                                lambda b, ph, i: (b * nblk + jnp.where(ph == 0, nblk - 1, nblk - 1 - i), 0)),
                   state),
        scratch_shapes=[pltpu.VMEM((seq, MIX_W), f32), pltpu.VMEM((MIX_W, qkw), f32), pltpu.VMEM((tl, qkw), f32)],
        compiler_params=_cparams("parallel", "arbitrary", "arbitrary"),
        name="gla_mixer",
    )(p, p, p, p, gw_hi, gw_lo, gb, ng, hsum, s0)


def _rope(x, cos, sin_signed):
    lane = lax.broadcasted_iota(jnp.int32, (1, LANES), 1)
    half = DIFF_DK // 2
    rot = jnp.where((lane % DIFF_DK) < half, pltpu.roll(x, LANES - half, 1), pltpu.roll(x, half, 1))
    return x * cos + rot * sin_signed


def _diff_kernel(*refs, seq, ctx_len, rope, lam_init):
    if rope:
        (q_ref, k_ref, v_ref, ck_ref, cv_ref, cq_ref, sq_ref, ckk_ref, skk_ref, lam_ref, ng_ref,
         o_ref, kb_scr, vt_scr) = refs
    else:
        q_ref, k_ref, v_ref, lam_ref, ng_ref, o_ref, kb_scr, vt_scr = refs
    qi = pl.program_id(2)
    lane = lax.broadcasted_iota(jnp.int32, (1, LANES), 1)
    first = lane < HEAD_DIM

    def put_v(v, lo, hi):
        vt = v.T
        ones = jnp.ones((HEAD_DIM, hi - lo), f32)
        vt_scr[0, :, lo:hi] = jnp.concatenate([vt[0:HEAD_DIM], ones], axis=0).astype(bf16)
        vt_scr[1, :, lo:hi] = jnp.concatenate([vt[HEAD_DIM:], ones], axis=0).astype(bf16)

    @pl.when(qi == 0)
    def _():
        k = k_ref[...].astype(f32)
        if rope:
            k = _rope(k, ckk_ref[...], skk_ref[...])
        kb_scr[0:seq, :] = k.astype(bf16)
        put_v(v_ref[...].astype(f32), 0, seq)
        if ctx_len:
            kb_scr[seq:seq + ctx_len, :] = ck_ref[...].astype(bf16)
            put_v(cv_ref[...], seq, seq + ctx_len)

    q = q_ref[...].astype(f32)
    if rope:
        q = _rope(q, cq_ref[...], sq_ref[...])
    q = q * (DIFF_DK ** -0.5 * LOG2E)
    n_keys = seq + ctx_len
    ck = max(w for w in range(LANES, DIFF_KEY_CHUNK + 1, LANES) if n_keys % w == 0)
    nck = n_keys // ck
    tq = q.shape[0]
    sub = min(DIFF_Q_SUB, tq)
    jobs = [(qs, g) for qs in range(tq // sub) for g in range(4)]
    lp = lam_ref[...]
    lam = (jnp.exp(jnp.sum(lp[0:1] * lp[1:2], keepdims=True)) - jnp.exp(jnp.sum(lp[2:3] * lp[3:4], keepdims=True))
           + lam_init)

    def finish(qs, res):
        o = jnp.concatenate([res[0] - lam * res[1], res[2] - lam * res[3]], axis=0).T
        o2 = o * o
        ss_a = jnp.sum(jnp.where(first, o2, 0.0), axis=-1, keepdims=True)
        ss_b = jnp.sum(jnp.where(first, 0.0, o2), axis=-1, keepdims=True)
        inv = jnp.where(first, lax.rsqrt(ss_a * (1.0 / HEAD_DIM) + NORM_EPS),
                        lax.rsqrt(ss_b * (1.0 / HEAD_DIM) + NORM_EPS))
        o_ref[qs * sub:(qs + 1) * sub, :] = (o * inv * ng_ref[...] * (1.0 - lam_init)).astype(bf16)

    res = []
    s_prev = m_prev = None
    for j in range(len(jobs) + 1):
        s_cur, mvec, acc = [], None, None
        if j < len(jobs):
            qs, g = jobs[j]
            qm = jnp.where(lane // DIFF_DK == g, q[qs * sub:(qs + 1) * sub], 0.0).astype(bf16)
        for c in range(nck):
            if j < len(jobs):
                s = _dot(qm, kb_scr[c * ck:(c + 1) * ck, :], NT_DIMS)
                s_cur.append(s)
                for k in range(ck // LANES):
                    blk = s[:, k * LANES:(k + 1) * LANES]
                    mvec = blk if mvec is None else jnp.maximum(mvec, blk)
            if j > 0:
                p = jnp.exp2(s_prev[c] - m_prev).astype(bf16)
                part = _dot(vt_scr[jobs[j - 1][1] // 2, :, c * ck:(c + 1) * ck], p, NT_DIMS)
                acc = part if acc is None else acc + part
        if j > 0:
            res.append(acc[0:HEAD_DIM] / acc[HEAD_DIM:HEAD_DIM + 1])
            if len(res) == 4:
                finish(jobs[j - 1][0], res)
                res = []
        if j < len(jobs):
            s_prev, m_prev = s_cur, jnp.max(mvec, axis=-1, keepdims=True)


def diff_mixer(p, lam_p, ng, bsz, seq, lam_init, ctx=None, rope_tabs=None, tq=1024):
    tq = min(tq, seq)
    nq = seq // tq
    npair = MIX_W // LANES
    rope = rope_tabs is not None
    ctx_len = ctx[0].shape[1] if ctx is not None else 0
    assert rope == (ctx is not None)
    q_spec = pl.BlockSpec((None, tq, LANES), lambda b, hp, i: (T_DQ, b * nq + i, hp))
    k_spec = pl.BlockSpec((None, seq, LANES), lambda b, hp, i: (T_DK, b, hp))
    v_spec = pl.BlockSpec((None, seq, LANES), lambda b, hp, i: (T_DV, b, hp))
    const = lambda shape: pl.BlockSpec(shape, lambda b, hp, i: (0,) * len(shape))
    args = [p, p, p]
    specs = [q_spec, k_spec, v_spec]
    if rope:
        c_spec = pl.BlockSpec((None, ctx_len, LANES), lambda b, hp, i: (b, 0, hp))
        tab_q = pl.BlockSpec((tq, LANES), lambda b, hp, i: (i, 0))
        args += [ctx[0], ctx[1], rope_tabs[0], rope_tabs[1], rope_tabs[0], rope_tabs[1]]
        specs += [c_spec, c_spec, tab_q, tab_q, const((seq, LANES)), const((seq, LANES))]
    args += [lam_p, ng]
    specs += [const((8, LANES)), const((1, LANES))]
    return pl.pallas_call(
        functools.partial(_diff_kernel, seq=seq, ctx_len=ctx_len, rope=rope, lam_init=lam_init),
        out_shape=jax.ShapeDtypeStruct((bsz * seq, MIX_W), bf16),
        grid=(bsz, npair, nq),
        in_specs=specs,
        out_specs=pl.BlockSpec((tq, LANES), lambda b, hp, i: (b * nq + i, hp)),
        scratch_shapes=[pltpu.VMEM((seq + ctx_len, LANES), bf16), pltpu.VMEM((2, LANES, seq + ctx_len), bf16)],
        compiler_params=_cparams("parallel", "parallel", "arbitrary"),
        name="diff_mixer",
    )(*args)


def _ssd_kernel(sx_ref, bcr_ref, sxp_ref, bcp_ref, sxn_ref, bcn_ref, cw_ref, cb_ref, z_ref, dt_ref, dtb_ref, alog_ref,
                dexp_ref, ng_ref, exp_ref, s0_ref, o_ref, sfin_ref,
                yf_scr, st_scr, dt_scr, dta_scr, xs_ref, bc_ref, *, nblk, tl):
    ph = pl.program_id(1)
    i = pl.program_id(2)
    fwd = ph == 0
    c = SSD_CHUNK
    nch = tl // c
    rep = NH // SSD_G
    blk = jnp.where(fwd, i, nblk - 1 - i)

    @pl.when(i == 0)
    def _():
        st_scr[...] = _expand_state(s0_ref[...], SSD_N, HEAD_DIM)

    brow = pl.multiple_of(blk * tl, tl)

    @pl.when(fwd)
    def _():
        row = lax.broadcasted_iota(jnp.int32, (tl, 1), 0)
        for raw_ref, prev_ref, next_ref, dst_ref, lo in ((sx_ref, sxp_ref, sxn_ref, xs_ref, 0),
                                                        (bcr_ref, bcp_ref, bcn_ref, bc_ref, MIX_W)):
            x = raw_ref[...].astype(f32)
            w = cw_ref[:, lo:lo + x.shape[1]]
            prev_row = jnp.where(blk == 0, 0.0, prev_ref[HALO - 1:HALO, :].astype(f32))
            next_row = jnp.where(blk == nblk - 1, 0.0, next_ref[0:1, :].astype(f32))
            xm = jnp.where(row == 0, prev_row, pltpu.roll(x, 1, 0))
            xp = jnp.where(row == tl - 1, next_row, pltpu.roll(x, tl - 1, 0))
            dst_ref[pl.ds(brow, tl), :] = _silu(xm * w[0:1] + x * w[1:2] + xp * w[2:3]
                                                + cb_ref[:, lo:lo + x.shape[1]])

    dtv = _softplus(dt_ref[...].astype(f32) + dtb_ref[...])
    dtv = jnp.where(fwd, dtv, pltpu.roll(dtv, LANES - NH, 1))
    av = -jnp.exp(alog_ref[...])
    av = jnp.where(fwd, av, pltpu.roll(av, LANES - NH, 1))
    dt_scr[...] = dtv
    dta_scr[...] = dtv * av

    tri = _tri_mask(c, c, jnp.where(fwd, 1, -1))
    trib = jnp.where(tri, 1.0, 0.0).astype(bf16)
    head = lax.broadcasted_iota(jnp.int32, (1, MIX_W), 1) // HEAD_DIM
    bd = (lax.broadcasted_iota(jnp.int32, (MIX_W, MIX_W), 0) // SSD_N
          == lax.broadcasted_iota(jnp.int32, (MIX_W, MIX_W), 1) // HEAD_DIM)
    def chunk(j, carry):
        cj = jnp.where(fwd, j, nch - 1 - j)
        off = pl.multiple_of(cj * c, c)
        srow = pl.multiple_of(brow + off, c)
        dtc = dt_scr[pl.ds(off, c), :]
        cum = _sel_dot(trib, dta_scr[pl.ds(off, c), :])
        cum_t = cum.T
        dt_t = dtc.T
        earg = _sel_dot(cum, exp_ref[...])
        elast = jnp.where(fwd, earg[c - 1:c, :], earg[0:1, :])
        dtx = _sel_dot(dtc, exp_ref[...])
        bt = _per_head_groups(bc_ref[pl.ds(srow, c), 0:LANES])
        ct = _per_head_groups(bc_ref[pl.ds(srow, c), LANES:2 * LANES])
        x = xs_ref[pl.ds(srow, c), :]
        xb = x.astype(bf16)
        st = st_scr[...]
        y = _dot((ct * jnp.exp(earg)).astype(bf16), st.astype(bf16))
        btail = (bt * jnp.exp(elast - earg) * dtx).astype(bf16)
        new = _dot(btail, xb, TN_DIMS)
        st_scr[...] = st * jnp.exp(elast) + jnp.where(bd, new, 0.0)
        btb = bt.astype(bf16)
        cb = [_dot(jnp.where(head == g * rep, ct, 0.0).astype(bf16), btb, NT_DIMS) for g in range(SSD_G)]
        ws = []
        for h in range(NH):
            seg = cum[:, h:h + 1] - cum_t[h:h + 1, :]
            lm = jnp.exp(jnp.where(tri, seg, -jnp.inf))
            ws.append((cb[h // rep] * lm * dt_t[h:h + 1, :]).astype(bf16))
        y += _dot(jnp.concatenate(ws, axis=1), _block_diag_rows(xb, c, HEAD_DIM))
        yf_scr[pl.ds(pl.multiple_of(brow + off, c), c), :] += y
        return carry

    @pl.when(fwd)
    def _():
        yf_scr[pl.ds(brow, tl), :] = jnp.zeros((tl, MIX_W), f32)

    lax.fori_loop(0, nch, chunk, 0, unroll=True)

    @pl.when(ph == 1)
    def _():
        yt = yf_scr[pl.ds(brow, tl), :] + xs_ref[pl.ds(brow, tl), :] * dexp_ref[...]
        t = yt * _silu(z_ref[...].astype(f32))
        ms = jnp.mean(t * t, axis=-1, keepdims=True)
        o_ref[...] = (t * lax.rsqrt(ms + NORM_EPS) * ng_ref[...]).astype(bf16)

    @pl.when(i == nblk - 1)
    def _():
        sfin_ref[...] = _compact_state(st_scr[...])


def ssd_mixer(p, cw, cb, dtb, alog, dexp, ng, expand, s0, bsz, seq, tl=1024):
    tl = min(tl, seq)
    nblk = seq // tl
    n8 = bsz * seq // HALO
    per8 = tl // HALO
    blk = lambda ph, i: jnp.where(ph == 0, i, nblk - 1 - i)
    rows = lambda b, ph, i: b * nblk + blk(ph, i)
    const = lambda shape: pl.BlockSpec(shape, lambda b, ph, i: (0,) * len(shape))
    state = pl.BlockSpec((None, None, SSD_N, MIX_W), lambda b, ph, i: (b, ph, 0, 0))
    body = lambda tile, w: pl.BlockSpec((None, tl, w), lambda b, ph, i: (tile, rows(b, ph, i), 0))
    crow = lambda b, ph, i: jnp.where(ph == 0, rows(b, ph, i), b * nblk + nblk - 1)
    cbody = lambda tile, w: pl.BlockSpec((None, tl, w), lambda b, ph, i: (tile, crow(b, ph, i), 0))
    prev8 = lambda tile, w: pl.BlockSpec(
        (None, HALO, w), lambda b, ph, i: (tile, jnp.maximum(crow(b, ph, i) * per8 - 1, 0), 0))
    next8 = lambda tile, w: pl.BlockSpec(
        (None, HALO, w), lambda b, ph, i: (tile, jnp.minimum((crow(b, ph, i) + 1) * per8, n8 - 1), 0))
    bcw = 2 * SSD_G * SSD_N
    return pl.pallas_call(
        functools.partial(_ssd_kernel, nblk=nblk, tl=tl),
        out_shape=(jax.ShapeDtypeStruct((bsz * seq, MIX_W), bf16),
                   jax.ShapeDtypeStruct((bsz, 2, SSD_N, MIX_W), f32)),
        grid=(bsz, 2, nblk),
        in_specs=[cbody(T_SX, PT), cbody(T_MISC, bcw), prev8(T_SX, PT), prev8(T_MISC, bcw),
                  next8(T_SX, PT), next8(T_MISC, bcw), const((3, MIX_W + bcw)), const((1, MIX_W + bcw)),
                  pl.BlockSpec((None, tl, PT),
                               lambda b, ph, i: (T_SZ, b * nblk + jnp.where(ph == 0, nblk - 1, nblk - 1 - i), 0)),
                  pl.BlockSpec((None, tl, LANES), lambda b, ph, i: (T_MISC, rows(b, ph, i), MISC_DT // LANES)),
                  const((1, LANES)), const((1, LANES)), const((1, MIX_W)), const((1, MIX_W)),
                  const((LANES, MIX_W)), state],
        out_specs=(pl.BlockSpec((tl, MIX_W),
                                lambda b, ph, i: (b * nblk + jnp.where(ph == 0, nblk - 1, nblk - 1 - i), 0)),
                   state),
        scratch_shapes=[pltpu.VMEM((seq, MIX_W), f32), pltpu.VMEM((MIX_W, MIX_W), f32),
                        pltpu.VMEM((tl, LANES), f32), pltpu.VMEM((tl, LANES), f32),
                        pltpu.VMEM((seq, MIX_W), f32), pltpu.VMEM((seq, bcw), f32)],
        compiler_params=_cparams("parallel", "arbitrary", "arbitrary"),
        name="ssd_mixer",
    )(p, p, p, p, p, p, cw, cb.reshape(1, -1), p, p, dtb, alog, dexp, ng, expand, s0)


def _hy_filter_kernel(zf_ref, zb_ref, tf_ref, tb_ref, w1_ref, b1_ref, sw_ref, w2_ref, b2_ref,
                      w3f_ref, w3b_ref, b3f_ref, b3b_ref, df_ref, db_ref, skip_ref, r_ref, hf_scr, hb_scr, *, seq):
    first = jnp.logical_and(pl.program_id(0) == 0, pl.program_id(1) == 0)

    @pl.when(first)
    def _():
        for z_ref, h_scr in ((zf_ref, hf_scr), (zb_ref, hb_scr)):
            h = jnp.sin(sw_ref[...] * (_dot(w1_ref[...], z_ref[...], precision=HI) + b1_ref[...]))
            h_scr[...] = jnp.sin(sw_ref[...] * (_dot(w2_ref[...], h, precision=HI) + b2_ref[...]))

    hf = (_dot(w3f_ref[...], hf_scr[...], precision=HI) + b3f_ref[...]) * jnp.exp(-tf_ref[...] * jnp.abs(df_ref[...]))
    hb = (_dot(w3b_ref[...], hb_scr[...], precision=HI) + b3b_ref[...]) * jnp.exp(-tb_ref[...] * jnp.abs(db_ref[...]))
    den = (jnp.sum(jnp.abs(hf), axis=-1, keepdims=True) + jnp.sum(jnp.abs(hb), axis=-1, keepdims=True)) + NORM_EPS
    col = lax.broadcasted_iota(jnp.int32, (1, seq), 1)
    r_ref[:, 0:seq] = jnp.where(col == 0, 0.0, hb / den)
    r_ref[:, seq:2 * seq] = hf / den + jnp.where(col == 0, skip_ref[...], 0.0)


def hy_filters(fp, seq, cbf=128):
    t = jnp.arange(seq, dtype=f32) / seq
    t_rev = jnp.concatenate([t[:1], t[:0:-1]])

    def feats(tt):
        tc = tt[:, None]
        ang = 2.0 * math.pi * tc * jnp.arange(1, HY_BANDS + 1, dtype=f32)
        z = jnp.concatenate([tc, jnp.cos(ang), jnp.sin(ang)], axis=-1)
        return jnp.pad(z, ((0, 0), (0, HY_EMB_PAD - HY_EMB))).T

    ch = MIX_W
    nb = ch // cbf
    col = lambda a: a.reshape(-1, 1)
    w3t = fp['w3'].T.reshape(2, 2, ch, HY_HID)
    b3 = fp['b3'].reshape(2, 2, ch, 1)
    dec = fp['decay'].reshape(2, 2, ch, 1)
    const = lambda shape: pl.BlockSpec(shape, lambda o, j: (0,) * len(shape))
    sel = lambda d, last: pl.BlockSpec((None, None, cbf, last), lambda o, j: (o, d, j, 0))
    return pl.pallas_call(
        functools.partial(_hy_filter_kernel, seq=seq),
        out_shape=jax.ShapeDtypeStruct((2, ch, 2 * seq), f32),
        grid=(2, nb),
        in_specs=[const((HY_EMB_PAD, seq)), const((HY_EMB_PAD, seq)), const((1, seq)), const((1, seq)),
                  const((HY_HID, HY_EMB_PAD)), const((HY_HID, 1)), const((HY_HID, 1)),
                  const((HY_HID, HY_HID)), const((HY_HID, 1)),
                  sel(0, HY_HID), sel(1, HY_HID), sel(0, 1), sel(1, 1), sel(0, 1), sel(1, 1),
                  pl.BlockSpec((None, cbf, 1), lambda o, j: (o, j, 0))],
        out_specs=pl.BlockSpec((None, cbf, 2 * seq), lambda o, j: (o, j, 0)),
        scratch_shapes=[pltpu.VMEM((HY_HID, seq), f32), pltpu.VMEM((HY_HID, seq), f32)],
        compiler_params=_cparams("arbitrary", "arbitrary"),
        name="hy_filters",
    )(feats(t), feats(t_rev), t.reshape(1, seq), t_rev.reshape(1, seq),
      jnp.pad(fp['w1'], ((0, HY_EMB_PAD - HY_EMB), (0, 0))).T, col(fp['b1']), col(fp['sin_w']),
      fp['w2'].T, col(fp['b2']), w3t, w3t, b3, b3, dec, dec, fp['skip'].reshape(2, ch, 1))


def _hyena_kernel(cw_ref, cb_ref, hv_ref, h1_ref, h2_ref, r_ref, o_ref, acc_scr, *, bsz, n_i, cb_n):
    tb = HY_TB
    gr = tb // 2
    sub = tb // LANES
    seq = n_i * tb
    rows = n_i * bsz
    base = pl.program_id(0) * cb_n
    n_ch = MIX_W
    lane = lax.broadcasted_iota(jnp.int32, (1, tb), 1)
    zblk = jnp.zeros((bsz, tb), f32)

    def load(ref, ch):
        return jnp.concatenate(
            [jnp.concatenate([ref[ch, pl.ds(sub * ib + k, bsz, stride=sub * n_i), :] for k in range(sub)], axis=1)
             for ib in range(n_i)], axis=0)

    def short_conv(a, stream, ch):
        idx = stream * n_ch + base + ch
        prev = jnp.concatenate([zblk, a[:rows - bsz]], axis=0) if n_i > 1 else zblk
        nxt = jnp.concatenate([a[bsz:], zblk], axis=0) if n_i > 1 else zblk
        am = pltpu.roll(jnp.where(lane == tb - 1, prev, a), 1, 1)
        ap = pltpu.roll(jnp.where(lane == 0, nxt, a), tb - 1, 1)
        return (am * cw_ref[idx] + a * cw_ref[3 * n_ch + idx] + ap * cw_ref[6 * n_ch + idx]) + cb_ref[idx]

    def long_conv(u, order, ch):
        rrow = r_ref[order, pl.ds(ch, 1), :]
        bits = lambda a: lax.bitcast_convert_type(a.astype(bf16).astype(f32), jnp.int32)
        word = (bits(pltpu.roll(rrow, 1, 1)) & jnp.int32(-65536)) | lax.shift_right_logical(bits(rrow), 16)
        g = pltpu.bitcast(
            pltpu.roll(jnp.broadcast_to(word, (gr // 2, 2 * seq)), 0, 1, stride=2, stride_axis=0), bf16)
        for d in [0] + [e for e in range(-(n_i - 1), n_i) if e != 0]:
            n = n_i - abs(d)
            src = max(0, -d) * bsz
            dst = max(0, d) * bsz
            c0 = seq + d * tb
            m = jnp.concatenate([g[:, c0:c0 + tb], g[:, c0 - gr:c0 - gr + tb]], axis=0)
            part = _dot(u[src:src + n * bsz].astype(bf16), m)
            if d == 0:
                acc_scr[order] = part
            else:
                acc_scr[order, dst:dst + n * bsz, :] += part
        return acc_scr[order]

    def body(ch, carry):
        hv = short_conv(load(hv_ref, ch), 0, ch)
        hx1 = short_conv(load(h1_ref, ch), 1, ch)
        hx2 = short_conv(load(h2_ref, ch), 2, ch)
        zz = hx1 * long_conv(hv, 0, ch)
        out = hx2 * long_conv(zz, 1, ch)
        for ib in range(n_i):
            for k in range(sub):
                o_ref[ch, pl.ds(sub * ib + k, bsz, stride=sub * n_i), :] = (
                    out[ib * bsz:(ib + 1) * bsz, k * LANES:(k + 1) * LANES])
        return carry

    lax.fori_loop(0, cb_n, body, 0, unroll=8)


def hyena_mixer(hy_t, r, cw, cb, bsz, seq, cb_n=8):
    n_i = seq // HY_TB
    ch = MIX_W
    nblk = ch // cb_n
    n_rows = bsz * seq // LANES
    x3 = hy_t.reshape(3 * ch, n_rows, LANES)
    stream = lambda s: pl.BlockSpec((cb_n, n_rows, LANES), lambda j: (s * nblk + j, 0, 0))
    smem = pl.BlockSpec(memory_space=pltpu.SMEM)
    out = pl.pallas_call(
        functools.partial(_hyena_kernel, bsz=bsz, n_i=n_i, cb_n=cb_n),
        out_shape=jax.ShapeDtypeStruct((ch, n_rows, LANES), f32),
        grid=(nblk,),
        in_specs=[smem, smem, stream(0), stream(1), stream(2),
                  pl.BlockSpec((2, cb_n, 2 * seq), lambda j: (0, j, 0))],
        out_specs=pl.BlockSpec((cb_n, n_rows, LANES), lambda j: (j, 0, 0)),
        scratch_shapes=[pltpu.VMEM((2, bsz * n_i, HY_TB), f32)],
        compiler_params=_cparams("parallel"),
        name="hyena_mixer",
    )(cw.reshape(-1), cb, x3, x3, x3, r)
    return out.reshape(ch, bsz * seq)


_IN_SIZES = (NH * GLA_DK, NH * GLA_DK, MIX_W, MIX_W, 2 * GLA_RANK, MIX_W, MIX_W, MIX_W,
             MIX_W, MIX_W + 2 * SSD_G * SSD_N, 2 * NH, 3 * MIX_W)
_IN_OFFS = np.concatenate([[0], np.cumsum(_IN_SIZES)]).tolist()


def _prep_w_in(w_in):
    (gq, gk, gv, gg, gr, dq, dk, dv, sz, sxbc, sdt, hy) = [w_in[:, _IN_OFFS[i]:_IN_OFFS[i + 1]]
                                                          for i in range(len(_IN_SIZES))]
    d = w_in.shape[0]
    zeros = lambda n: jnp.zeros((d, n), w_in.dtype)
    misc = jnp.concatenate([sxbc[:, MIX_W:], gr, zeros(LANES - 2 * GLA_RANK), sdt, zeros(LANES - 2 * NH)], axis=1)
    w_tok = jnp.concatenate([gq, gk, gv, gg, dq, dk, dv, sz, sxbc[:, :MIX_W], misc], axis=1)
    w_tok = w_tok.astype(bf16).reshape(d, N_PT, PT).transpose(1, 0, 2)
    return w_tok, hy.T.astype(bf16)


def _prep_gate_w(gate_w):
    w = jnp.zeros((2, LANES, gate_w.shape[2]), f32)
    for d in range(2):
        w = w.at[d, d * GLA_RANK:(d + 1) * GLA_RANK, :].set(gate_w[d])
    hi = w.astype(bf16)
    return hi, (w - hi.astype(f32)).astype(bf16)


def _rope_tables(seq):
    rows = seq // GRID_W
    r, col = jnp.meshgrid(jnp.arange(rows), jnp.arange(GRID_W), indexing='ij')
    r = r.reshape(-1).astype(f32)
    col = col.reshape(-1).astype(f32)
    nf = DIFF_DK // 4
    inv = ROPE_BASE ** (-jnp.arange(nf, dtype=f32) / nf)
    ang = jnp.concatenate([r[:, None] * inv, col[:, None] * inv], axis=-1)
    cos, sin = jnp.cos(ang), jnp.sin(ang)
    reps = LANES // DIFF_DK
    return jnp.tile(jnp.concatenate([cos, cos], axis=-1), (1, reps)), jnp.tile(jnp.concatenate([-sin, sin], axis=-1), (1, reps))


def _const_tables():
    hsum = np.kron(np.eye(NH, dtype=np.float32), np.ones((HEAD_DIM, HEAD_DIM), np.float32))
    expand = np.zeros((LANES, MIX_W), np.float32)
    for h in range(NH):
        expand[h, h * HEAD_DIM:(h + 1) * HEAD_DIM] = 1.0
    return tuple(jnp.asarray(a, dtype=bf16) for a in (hsum, expand))


def _pack_state(s):
    bsz, _, h, a, b = s.shape
    return s.transpose(0, 1, 4, 2, 3).reshape(bsz, 2, b, h * a)


def _unpack_state(st, a):
    bsz, _, b, _ = st.shape
    return st.reshape(bsz, 2, b, NH, a).transpose(0, 1, 3, 4, 2)


def _layer(x, mod, rows_per_mod, lw, layer, consts, bsz, seq, lam_init, r_filt, ctx, rope_tabs, final_g,
           final_norm):
    hsum, expand = consts
    p, hy_t = in_proj(x, mod, lw['norm1_g'], lw['w_tok'], lw['w_hyT'], rows_per_mod)
    if ctx is None:
        gla_s0 = jnp.zeros((bsz, 2, HEAD_DIM, NH * GLA_DK), f32)
        ssd_s0 = jnp.zeros((bsz, 2, SSD_N, MIX_W), f32)
        dctx = None
    else:
        ctx_k, ctx_v, gla_s, ssd_s = ctx
        gla_s0 = _pack_state(gla_s)
        ssd_s0 = _pack_state(ssd_s)
        dctx = (ctx_k.reshape(bsz, -1, MIX_W), ctx_v.reshape(bsz, -1, MIX_W))
    o_gla, gla_fin = gla_mixer(p, lw['gla_gw_hi'], lw['gla_gw_lo'], lw['gla_gb'], lw['gla_ng'], hsum, gla_s0,
                               bsz, seq)
    o_diff = diff_mixer(p, lw['lam_p'], lw['diff_ng'], bsz, seq, lam_init, dctx, rope_tabs)
    o_ssd, ssd_fin = ssd_mixer(p, lw['ssd_conv_w'], lw['ssd_conv_b'], lw['ssd_dtb'], lw['ssd_alog'], lw['ssd_dexp'], lw['ssd_ng'],
                               expand, ssd_s0, bsz, seq)
    o_hy = hyena_mixer(hy_t, r_filt, lw['hy_cw'], lw['hy_cb'], bsz, seq)
    x = out_proj(x, mod, o_gla, o_diff, o_ssd, o_hy, lw['w_out'], layer, rows_per_mod)
    x = mlp(x, mod, lw['norm2_g'], lw['mlp_w1'], lw['mlp_w2'], layer, final_g, rows_per_mod, final_norm)
    return x, p, gla_fin, ssd_fin


def kernel(x_prompt, x_sample, cache_diff_k, cache_diff_v, state_gla, state_ssd, c, c_ctx, ada_w, ada_b, norm1_g, norm2_g, w_in, w_out, gla_gate_w, gla_gate_b, gla_norm_g, diff_lambda, diff_norm_g, ssd_conv_w, ssd_conv_b, ssd_dt_bias, ssd_a_log, ssd_d, ssd_norm_g, hy_conv_w, hy_conv_b, hy_f_w1, hy_f_b1, hy_f_w2, hy_f_b2, hy_f_w3, hy_f_b3, hy_sin_w, hy_decay, hy_skip, mlp_w1, mlp_w2, final_g):
    bp, lp, d = x_prompt.shape
    bs, ls, _ = x_sample.shape
    depth = w_in.shape[0]
    consts = _const_tables()
    rope_tabs = _rope_tables(ls)
    n_c = 16
    cvec = jnp.concatenate([c_ctx[None], c, jnp.zeros((n_c - 1 - bs, d), f32)], axis=0)
    hp = x_prompt.reshape(bp * lp, d)
    hs = x_sample.reshape(bs * ls, d)
    ks_, vs_, gs_, ss_ = [], [], [], []
    w_out_b, mlp_w1_b, mlp_w2_b = w_out.astype(bf16), mlp_w1.astype(bf16), mlp_w2.astype(bf16)
    for l in range(depth):
        w_tok, w_hyT = _prep_w_in(w_in[l])
        pad_l = lambda a: jnp.pad(a, (0, LANES - a.shape[0])).reshape(1, LANES)
        gw_hi, gw_lo = _prep_gate_w(gla_gate_w[l])
        lw = dict(
            norm1_g=norm1_g[l], norm2_g=norm2_g[l], w_tok=w_tok, w_hyT=w_hyT, w_out=w_out_b,
            mlp_w1=mlp_w1_b, mlp_w2=mlp_w2_b,
            gla_gw_hi=gw_hi, gla_gw_lo=gw_lo, gla_gb=gla_gate_b[l].reshape(2, 1, -1), gla_ng=jnp.tile(gla_norm_g[l], NH).reshape(1, MIX_W),
            lam_p=jnp.pad(diff_lambda[l], ((0, 4), (0, LANES - DIFF_DK))),
            diff_ng=jnp.tile(diff_norm_g[l], LANES // HEAD_DIM).reshape(1, LANES),
            ssd_conv_w=ssd_conv_w[l], ssd_conv_b=ssd_conv_b[l],
            ssd_dtb=pad_l(ssd_dt_bias[l].reshape(-1)), ssd_alog=pad_l(ssd_a_log[l].reshape(-1)),
            ssd_dexp=jnp.repeat(ssd_d[l], HEAD_DIM).reshape(1, MIX_W), ssd_ng=ssd_norm_g[l].reshape(1, MIX_W),
            hy_cw=hy_conv_w[l], hy_cb=hy_conv_b[l],
        )
        fp = dict(w1=hy_f_w1[l], b1=hy_f_b1[l], w2=hy_f_w2[l], b2=hy_f_b2[l], w3=hy_f_w3[l], b3=hy_f_b3[l],
                  sin_w=hy_sin_w[l], decay=hy_decay[l], skip=hy_skip[l])
        mod = ada_mod(cvec, ada_w, ada_b, l).reshape(n_c, 6, d)
        lam_init = 0.8 - 0.6 * math.exp(-0.3 * l)
        last = l == depth - 1
        hp, p_p, g_l, s_l = _layer(hp, mod[0:1], bp * lp, lw, l, consts, bp, lp, lam_init, hy_filters(fp, lp),
                                   None, None, final_g, last)
        ks_.append(p_p[T_DK].astype(f32).reshape(bp, lp, NH, 2 * DIFF_DK))
        vs_.append(p_p[T_DV].astype(f32).reshape(bp, lp, NH, HEAD_DIM))
        gs_.append(_unpack_state(g_l, GLA_DK))
        ss_.append(_unpack_state(s_l, HEAD_DIM))
        hs, _, _, _ = _layer(hs, mod[1:1 + bs], ls, lw, l, consts, bs, ls, lam_init, hy_filters(fp, ls),
                             (cache_diff_k[:, l], cache_diff_v[:, l], state_gla[:, l], state_ssd[:, l]),
                             rope_tabs, final_g, last)
    return (hp.reshape(bp, lp, d), hs.reshape(bs, ls, d), jnp.stack(ks_, axis=1), jnp.stack(vs_, axis=1),
            jnp.stack(gs_, axis=1), jnp.stack(ss_, axis=1))
```

```python
import functools
import math

import numpy as np
import jax
import jax.numpy as jnp
from jax import lax
from jax.experimental import pallas as pl
from jax.experimental.pallas import tpu as pltpu

f32 = jnp.float32
bf16 = jnp.bfloat16
HI = lax.Precision.HIGHEST

LANES = 128
VMEM_LIMIT = 56 * 2**20

D_MODEL = 2048
GRID_W = 64
MIX_W = D_MODEL // 4
HEAD_DIM = 64
NH = MIX_W // HEAD_DIM
D_FF = 4 * D_MODEL
NORM_EPS = 1e-6
LOG2E = 1.0 / math.log(2.0)
GLA_DK = HEAD_DIM // 2
GLA_RANK = 16
GLA_TAU = 16.0
GLA_CHUNK = 128
DIFF_DK = HEAD_DIM // 2
ROPE_BASE = 10000.0
SSD_N = 64
SSD_G = 2
SSD_CHUNK = 128
HY_BANDS = 16
HY_EMB = 2 * HY_BANDS + 1
HY_EMB_PAD = 40
HY_HID = 64
DIFF_KEY_CHUNK = 512
DIFF_Q_SUB = 512
HALO = 16
MLP_TF = 1024
HY_TB = 256

PT = 512
(T_GQK, T_GV, T_GG, T_DQ, T_DK, T_DV, T_SZ, T_SX, T_MISC) = range(9)
N_PT = 9
PT_GROUP = 3
MISC_GR = 256
MISC_DT = 384
SH1, SC1, G1, SH2, SC2, G2 = range(6)

NT_DIMS = (((1,), (1,)), ((), ()))
TN_DIMS = (((0,), (0,)), ((), ()))


def _cparams(*sem):
    return pltpu.CompilerParams(dimension_semantics=sem, vmem_limit_bytes=VMEM_LIMIT)


def _sigmoid(x):
    return 1.0 / (1.0 + jnp.exp(-x))


def _silu(x):
    return x * _sigmoid(x)


def _softplus(x):
    return jnp.maximum(x, 0.0) + jnp.log1p(jnp.exp(-jnp.abs(x)))


def _log_sigmoid(x):
    return jnp.minimum(x, 0.0) - jnp.log1p(jnp.exp(-jnp.abs(x)))


def _tri_mask(rows, c, sgn):
    r_i = lax.broadcasted_iota(jnp.int32, (rows, c), 0) & (c - 1)
    c_i = lax.broadcasted_iota(jnp.int32, (rows, c), 1)
    return (r_i - c_i) * sgn >= 0


def _expand_state(s, row_w, col_w):
    rows, cols = NH * row_w, NH * col_w
    bd = (lax.broadcasted_iota(jnp.int32, (rows, cols), 0) // row_w
          == lax.broadcasted_iota(jnp.int32, (rows, cols), 1) // col_w)
    return jnp.where(bd, jnp.concatenate([s] * NH, axis=0), 0.0)


def _compact_state(st):
    row_w = st.shape[0] // NH
    out = st[0:row_w]
    for h in range(1, NH):
        out += st[h * row_w:(h + 1) * row_w]
    return out


def _split(x, n):
    parts = []
    for _ in range(n - 1):
        hi = x.astype(bf16)
        parts.append(hi)
        x = x - hi.astype(f32)
    parts.append(x.astype(bf16))
    return parts


def _sel_dot(a, b, dims=None, terms=3):
    if a.dtype == bf16:
        prods = [_dot(a, p, dims) for p in _split(b, terms)]
    else:
        prods = [_dot(p, b, dims) for p in _split(a, terms)]
    return functools.reduce(lambda x, y: x + y, prods)


def _block_diag_rows(x, rows, col_w):
    shape = (NH * rows, NH * col_w)
    keep = (lax.broadcasted_iota(jnp.int32, shape, 0) // rows == lax.broadcasted_iota(jnp.int32, shape, 1) // col_w)
    return jnp.where(keep, jnp.concatenate([x] * NH, axis=0), jnp.zeros((), x.dtype))


def _per_head_groups(g2):
    assert SSD_G == 2 and SSD_N == HEAD_DIM and LANES == 2 * SSD_N
    first = lax.broadcasted_iota(jnp.int32, (1, LANES), 1) < SSD_N
    swapped = pltpu.roll(g2, SSD_N, 1)
    g0 = jnp.where(first, g2, swapped)
    g1 = jnp.where(first, swapped, g2)
    rep = NH * HEAD_DIM // (SSD_G * LANES)
    return jnp.concatenate([g0] * rep + [g1] * rep, axis=1)


def _dot(a, b, dims=None, precision=None):
    if dims is None:
        return jnp.dot(a, b, preferred_element_type=f32, precision=precision)
    return lax.dot_general(a, b, dims, preferred_element_type=f32, precision=precision)


def _ada_kernel(c_ref, w_ref, b_ref, o_ref):
    s = _silu(c_ref[...]).astype(bf16)
    o_ref[...] = _dot(s, w_ref[...].astype(bf16)) + b_ref[...]


def ada_mod(cvec, w, b, layer, tn=1024):
    m, d = cvec.shape
    n = w.shape[2]
    return pl.pallas_call(
        _ada_kernel,
        out_shape=jax.ShapeDtypeStruct((m, n), f32),
        grid=(n // tn,),
        in_specs=[pl.BlockSpec((m, d), lambda j: (0, 0)),
                  pl.BlockSpec((None, d, tn), lambda j: (layer, 0, j)),
                  pl.BlockSpec((None, 1, tn), lambda j: (layer, 0, j))],
        out_specs=pl.BlockSpec((m, tn), lambda j: (0, j)),
        compiler_params=_cparams("arbitrary"),
        name="ada_mod",
    )(cvec, w, b.reshape(b.shape[0], 1, n))


def _norm_mod(x, g, sc, sh):
    ms = jnp.mean(x * x, axis=-1, keepdims=True)
    return (x * lax.rsqrt(ms + NORM_EPS) * g) * (1.0 + sc) + sh


def _norm_next_chunk(xn_ref, modn_ref, g_ref, sc, sh, u_scr, slot, row0):
    xn = xn_ref[...]
    u = _norm_mod(xn, g_ref[...], modn_ref[sc:sc + 1, :], modn_ref[sh:sh + 1, :])
    u_scr[slot, pl.ds(row0, xn.shape[0]), :] = u.astype(bf16)


def _inproj_kernel(x_ref, mod_ref, g_ref, wt_ref, wh_ref, p_ref, h_ref, u_scr, *, n_hy):
    j = pl.program_id(1)

    @pl.when(j == 0)
    def _():
        u = _norm_mod(x_ref[...], g_ref[...], mod_ref[SC1:SC1 + 1, :], mod_ref[SH1:SH1 + 1, :])
        u_scr[...] = u.astype(bf16)

    @pl.when(j < n_hy)
    def _():
        h_ref[...] = _dot(wh_ref[...], u_scr[...], NT_DIMS)

    @pl.when(j >= n_hy)
    def _():
        for k in range(PT_GROUP):
            p_ref[k] = _dot(u_scr[...], wt_ref[k]).astype(p_ref.dtype)


def in_proj(x, mod, g, w_tok, w_hyT, rows_per_mod, tm=1024):
    t, d = x.shape
    assert t % tm == 0 and rows_per_mod % tm == 0, (t, rows_per_mod, tm)
    n_hy = w_hyT.shape[0] // PT
    n_grp = N_PT // PT_GROUP
    return pl.pallas_call(
        functools.partial(_inproj_kernel, n_hy=n_hy),
        out_shape=(jax.ShapeDtypeStruct((N_PT, t, PT), bf16), jax.ShapeDtypeStruct((n_hy * PT, t), f32)),
        grid=(t // tm, n_hy + n_grp),
        in_specs=[pl.BlockSpec((tm, d), lambda i, j: (i, 0)),
                  pl.BlockSpec((None, 6, d), lambda i, j: ((i * tm) // rows_per_mod, 0, 0)),
                  pl.BlockSpec((1, d), lambda i, j: (0, 0)),
                  pl.BlockSpec((PT_GROUP, d, PT), lambda i, j: (jnp.maximum(j - n_hy, 0), 0, 0)),
                  pl.BlockSpec((PT, d), lambda i, j: (jnp.minimum(j, n_hy - 1), 0))],
        out_specs=(pl.BlockSpec((PT_GROUP, tm, PT), lambda i, j: (jnp.maximum(j - n_hy, 0), i, 0)),
                   pl.BlockSpec((PT, tm), lambda i, j: (jnp.minimum(j, n_hy - 1), i))),
        scratch_shapes=[pltpu.VMEM((tm, d), bf16)],
        compiler_params=_cparams("parallel", "arbitrary"),
        name="in_proj",
    )(x, mod, g.reshape(1, d), w_tok, w_hyT)


def _outproj_kernel(x_ref, mod_ref, a_ref, b_ref, c_ref, dT_ref, w_ref, o_ref):
    acc = _dot(a_ref[...], w_ref[0:MIX_W, :])
    acc += _dot(b_ref[...], w_ref[MIX_W:2 * MIX_W, :])
    acc += _dot(c_ref[...], w_ref[2 * MIX_W:3 * MIX_W, :])
    acc += _dot(dT_ref[...].astype(bf16), w_ref[3 * MIX_W:4 * MIX_W, :], TN_DIMS)
    o_ref[...] = x_ref[...] + mod_ref[G1:G1 + 1, :] * acc


def out_proj(x, mod, o_a, o_b, o_c, o_dT, w_out, layer, rows_per_mod, tm=512):
    t, d = x.shape
    assert t % tm == 0 and rows_per_mod % tm == 0, (t, rows_per_mod, tm)
    tok = lambda i: (i, 0)
    return pl.pallas_call(
        _outproj_kernel,
        out_shape=jax.ShapeDtypeStruct((t, d), f32),
        grid=(t // tm,),
        in_specs=[pl.BlockSpec((tm, d), tok),
                  pl.BlockSpec((None, 6, d), lambda i: ((i * tm) // rows_per_mod, 0, 0)),
                  pl.BlockSpec((tm, MIX_W), tok), pl.BlockSpec((tm, MIX_W), tok), pl.BlockSpec((tm, MIX_W), tok),
                  pl.BlockSpec((MIX_W, tm), lambda i: (0, i)),
                  pl.BlockSpec((None, d, d), lambda i: (layer, 0, 0))],
        out_specs=pl.BlockSpec((tm, d), tok),
        compiler_params=_cparams("parallel"),
        name="out_proj",
    )(x, mod, o_a, o_b, o_c, o_dT, w_out)


def _mlp_kernel(xc_ref, mod_ref, xn_ref, modn_ref, g_ref, w1_ref, w2_ref, fg_ref, o_ref, u_scr, *, final_norm):
    io = pl.program_id(0)
    j = pl.program_id(1)
    rows = xn_ref.shape[0]
    row0 = pl.multiple_of(j * rows, rows)

    @pl.when(io == 0)
    def _():
        _norm_next_chunk(xn_ref, modn_ref, g_ref, SC2, SH2, u_scr, 0, row0)

    @pl.when(jnp.logical_and(io > 0, j == 0))
    def _():
        o_ref[...] = jnp.zeros_like(o_ref)

    @pl.when(io > 0)
    def _():
        _norm_next_chunk(xn_ref, modn_ref, g_ref, SC2, SH2, u_scr, io % 2, row0)
        h = jnp.maximum(_dot(u_scr[1 - io % 2], w1_ref[...]), 0.0)
        hb = (h * h).astype(bf16)
        d = o_ref.shape[1]
        for n in range(d // PT):
            cols = slice(n * PT, (n + 1) * PT)
            o_ref[:, cols] += mod_ref[G2:G2 + 1, cols] * _dot(hb, w2_ref[:, cols])
        o_ref[pl.ds(row0, rows), :] += xc_ref[...]

    if final_norm:
        @pl.when(jnp.logical_and(io > 0, j == pl.num_programs(1) - 1))
        def _():
            y = o_ref[...]
            ms = jnp.mean(y * y, axis=-1, keepdims=True)
            o_ref[...] = y * lax.rsqrt(ms + NORM_EPS) * fg_ref[...]


def mlp(x, mod, g, w1, w2, layer, final_g, rows_per_mod, final_norm, tm=1024, tf=MLP_TF):
    t, d = x.shape
    assert t % tm == 0 and rows_per_mod % tm == 0, (t, rows_per_mod, tm)
    n_f = w1.shape[2] // tf
    rows = tm // n_f
    n_tiles = t // tm
    tile = lambda io: jnp.maximum(io - 1, 0)
    wj = lambda io, j: jnp.where(io == 0, 0, j)
    chunk_of = lambda ti, j: jnp.minimum(ti * n_f + j, t // rows - 1)
    mod_of = lambda ti: jnp.minimum((ti * tm) // rows_per_mod, mod.shape[0] - 1)
    return pl.pallas_call(
        functools.partial(_mlp_kernel, final_norm=final_norm),
        out_shape=jax.ShapeDtypeStruct((t, d), f32),
        grid=(n_tiles + 1, n_f),
        in_specs=[pl.BlockSpec((rows, d), lambda io, j: (chunk_of(tile(io), j), 0)),
                  pl.BlockSpec((None, 6, d), lambda io, j: (mod_of(tile(io)), 0, 0)),
                  pl.BlockSpec((rows, d), lambda io, j: (chunk_of(jnp.minimum(io, n_tiles - 1), j), 0)),
                  pl.BlockSpec((None, 6, d), lambda io, j: (mod_of(jnp.minimum(io, n_tiles - 1)), 0, 0)),
                  pl.BlockSpec((1, d), lambda io, j: (0, 0)),
                  pl.BlockSpec((None, d, tf), lambda io, j: (layer, 0, wj(io, j))),
                  pl.BlockSpec((None, tf, d), lambda io, j: (layer, wj(io, j), 0)),
                  pl.BlockSpec((1, d), lambda io, j: (0, 0))],
        out_specs=pl.BlockSpec((tm, d), lambda io, j: (tile(io), 0)),
        scratch_shapes=[pltpu.VMEM((2, tm, d), bf16)],
        compiler_params=_cparams("arbitrary", "arbitrary"),
        name="mlp",
    )(x, mod, x, mod, g.reshape(1, d), w1, w2, final_g.reshape(1, d))


def _gla_kernel(qk_ref, v_ref, gg_ref, misc_ref, gwh_ref, gwl_ref, gb_ref, ng_ref, hsum_ref, s0_ref,
                o_ref, sfin_ref, of_scr, st_scr, la_scr, *, nblk, tl):
    ph = pl.program_id(1)
    i = pl.program_id(2)
    fwd = ph == 0
    c = GLA_CHUNK
    nch = tl // c
    qkw = NH * GLA_DK

    @pl.when(i == 0)
    def _():
        st_scr[...] = _expand_state(s0_ref[...], HEAD_DIM, GLA_DK)

    gr = misc_ref[:, MISC_GR:MISC_GR + LANES]
    logits = (_dot(gr, gwh_ref[...]) + _dot(gr, gwl_ref[...])) + gb_ref[...]
    la_scr[...] = _log_sigmoid(logits) * (1.0 / GLA_TAU)

    sgn = jnp.where(fwd, 1, -1)
    trib = jnp.where(_tri_mask(c, c, sgn), 1.0, 0.0).astype(bf16)
    tri8 = _tri_mask(NH * c, c, sgn)
    qk_head = lax.broadcasted_iota(jnp.int32, (1, qkw), 1) // GLA_DK
    bd = (lax.broadcasted_iota(jnp.int32, (MIX_W, qkw), 0) // HEAD_DIM
          == lax.broadcasted_iota(jnp.int32, (MIX_W, qkw), 1) // GLA_DK)
    brow = pl.multiple_of(jnp.where(fwd, i, nblk - 1 - i) * tl, tl)

    def chunk(j, carry):
        cj = jnp.where(fwd, j, nch - 1 - j)
        off = pl.multiple_of(cj * c, c)
        q = qk_ref[pl.ds(off, c), 0:qkw].astype(f32) * (GLA_DK ** -0.5)
        k = qk_ref[pl.ds(off, c), qkw:2 * qkw].astype(f32)
        vb = v_ref[pl.ds(off, c), :]
        b = _sel_dot(trib, la_scr[pl.ds(off, c), :])
        btot = jnp.where(fwd, b[c - 1:c, :], b[0:1, :])
        bmid = b[c // 2 - 1:c // 2, :]
        qd = q * jnp.exp(b)
        qi = q * jnp.exp(b - bmid)
        kd = (k * jnp.exp(bmid - b)).astype(bf16)
        kt = (k * jnp.exp(btot - b)).astype(bf16)
        qst = jnp.concatenate([jnp.where(qk_head == h, qi, 0.0) for h in range(NH)], axis=0).astype(bf16)
        att = jnp.where(tri8, _dot(qst, kd, NT_DIMS), 0.0)
        attb = att.astype(bf16)
        att_cat = jnp.concatenate([attb[h * c:(h + 1) * c] for h in range(NH)], axis=1)
        o = _dot(att_cat, _block_diag_rows(vb, c, HEAD_DIM))
        st = st_scr[...]
        o += _dot(qd.astype(bf16), st.astype(bf16), NT_DIMS)
        kv = _dot(vb, kt, TN_DIMS)
        st_scr[...] = st * jnp.exp(btot) + jnp.where(bd, kv, 0.0)
        of_scr[pl.ds(pl.multiple_of(brow + off, c), c), :] += o
        return carry

    @pl.when(fwd)
    def _():
        of_scr[pl.ds(brow, tl), :] = jnp.zeros((tl, MIX_W), f32)

    lax.fori_loop(0, nch, chunk, 0, unroll=True)

    @pl.when(ph == 1)
    def _():
        ot = of_scr[pl.ds(brow, tl), :]
        ss = _sel_dot(ot * ot, hsum_ref[...], terms=2) * (1.0 / HEAD_DIM)
        o_ref[...] = (ot * lax.rsqrt(ss + NORM_EPS) * ng_ref[...] * _silu(gg_ref[...].astype(f32))).astype(bf16)

    @pl.when(i == nblk - 1)
    def _():
        sfin_ref[...] = _compact_state(st_scr[...])


def gla_mixer(p, gw_hi, gw_lo, gb, ng, hsum, s0, bsz, seq, tl=1024):
    tl = min(tl, seq)
    nblk = seq // tl
    qkw = NH * GLA_DK
    blk = lambda ph, i: jnp.where(ph == 0, i, nblk - 1 - i)
    tile = lambda col: pl.BlockSpec((None, tl, PT), lambda b, ph, i: (col, b * nblk + blk(ph, i), 0))
    const = lambda shape: pl.BlockSpec(shape, lambda b, ph, i: (0,) * len(shape))
    per_dir = lambda rows: pl.BlockSpec((None, rows, qkw), lambda b, ph, i: (ph, 0, 0))
    state = pl.BlockSpec((None, None, HEAD_DIM, qkw), lambda b, ph, i: (b, ph, 0, 0))
    return pl.pallas_call(
        functools.partial(_gla_kernel, nblk=nblk, tl=tl),
        out_shape=(jax.ShapeDtypeStruct((bsz * seq, MIX_W), bf16),
                   jax.ShapeDtypeStruct((bsz, 2, HEAD_DIM, qkw), f32)),
        grid=(bsz, 2, nblk),
        in_specs=[tile(T_GQK), tile(T_GV),
                  pl.BlockSpec((None, tl, PT),
                               lambda b, ph, i: (T_GG, b * nblk + jnp.where(ph == 0, nblk - 1, nblk - 1 - i), 0)),
                  tile(T_MISC),
                  per_dir(LANES), per_dir(LANES), per_dir(1), const((1, MIX_W)), const((MIX_W, MIX_W)), state],
        out_specs=(pl.BlockSpec((tl, MIX_W),
                                lambda b, ph, i: (b * nblk + jnp.where(ph == 0, nblk - 1, nblk - 1 - i), 0)),
                   state),
        scratch_shapes=[pltpu.VMEM((seq, MIX_W), f32), pltpu.VMEM((MIX_W, qkw), f32), pltpu.VMEM((tl, qkw), f32)],
        compiler_params=_cparams("parallel", "arbitrary", "arbitrary"),
        name="gla_mixer",
    )(p, p, p, p, gw_hi, gw_lo, gb, ng, hsum, s0)


def _rope(x, cos, sin_signed):
    lane = lax.broadcasted_iota(jnp.int32, (1, LANES), 1)
    half = DIFF_DK // 2
    rot = jnp.where((lane % DIFF_DK) < half, pltpu.roll(x, LANES - half, 1), pltpu.roll(x, half, 1))
    return x * cos + rot * sin_signed


def _diff_kernel(*refs, seq, ctx_len, rope, lam_init):
    if rope:
        (q_ref, k_ref, v_ref, ck_ref, cv_ref, cq_ref, sq_ref, ckk_ref, skk_ref, lam_ref, ng_ref,
         o_ref, kb_scr, vt_scr) = refs
    else:
        q_ref, k_ref, v_ref, lam_ref, ng_ref, o_ref, kb_scr, vt_scr = refs
    qi = pl.program_id(2)
    lane = lax.broadcasted_iota(jnp.int32, (1, LANES), 1)
    first = lane < HEAD_DIM

    def put_v(v, lo, hi):
        vt = v.T
        ones = jnp.ones((HEAD_DIM, hi - lo), f32)
        vt_scr[0, :, lo:hi] = jnp.concatenate([vt[0:HEAD_DIM], ones], axis=0).astype(bf16)
        vt_scr[1, :, lo:hi] = jnp.concatenate([vt[HEAD_DIM:], ones], axis=0).astype(bf16)

    @pl.when(qi == 0)
    def _():
        k = k_ref[...].astype(f32)
        if rope:
            k = _rope(k, ckk_ref[...], skk_ref[...])
        kb_scr[0:seq, :] = k.astype(bf16)
        put_v(v_ref[...].astype(f32), 0, seq)
        if ctx_len:
            kb_scr[seq:seq + ctx_len, :] = ck_ref[...].astype(bf16)
            put_v(cv_ref[...], seq, seq + ctx_len)

    q = q_ref[...].astype(f32)
    if rope:
        q = _rope(q, cq_ref[...], sq_ref[...])
    q = q * (DIFF_DK ** -0.5 * LOG2E)
    n_keys = seq + ctx_len
    ck = max(w for w in range(LANES, DIFF_KEY_CHUNK + 1, LANES) if n_keys % w == 0)
    nck = n_keys // ck
    tq = q.shape[0]
    sub = min(DIFF_Q_SUB, tq)
    jobs = [(qs, g) for qs in range(tq // sub) for g in range(4)]
    lp = lam_ref[...]
    lam = (jnp.exp(jnp.sum(lp[0:1] * lp[1:2], keepdims=True)) - jnp.exp(jnp.sum(lp[2:3] * lp[3:4], keepdims=True))
           + lam_init)

    def finish(qs, res):
        o = jnp.concatenate([res[0] - lam * res[1], res[2] - lam * res[3]], axis=0).T
        o2 = o * o
        ss_a = jnp.sum(jnp.where(first, o2, 0.0), axis=-1, keepdims=True)
        ss_b = jnp.sum(jnp.where(first, 0.0, o2), axis=-1, keepdims=True)
        inv = jnp.where(first, lax.rsqrt(ss_a * (1.0 / HEAD_DIM) + NORM_EPS),
                        lax.rsqrt(ss_b * (1.0 / HEAD_DIM) + NORM_EPS))
        o_ref[qs * sub:(qs + 1) * sub, :] = (o * inv * ng_ref[...] * (1.0 - lam_init)).astype(bf16)

    res = []
    s_prev = m_prev = None
    for j in range(len(jobs) + 1):
        s_cur, mvec, acc = [], None, None
        if j < len(jobs):
            qs, g = jobs[j]
            qm = jnp.where(lane // DIFF_DK == g, q[qs * sub:(qs + 1) * sub], 0.0).astype(bf16)
        for c in range(nck):
            if j < len(jobs):
                s = _dot(qm, kb_scr[c * ck:(c + 1) * ck, :], NT_DIMS)
                s_cur.append(s)
                for k in range(ck // LANES):
                    blk = s[:, k * LANES:(k + 1) * LANES]
                    mvec = blk if mvec is None else jnp.maximum(mvec, blk)
            if j > 0:
                p = jnp.exp2(s_prev[c] - m_prev).astype(bf16)
                part = _dot(vt_scr[jobs[j - 1][1] // 2, :, c * ck:(c + 1) * ck], p, NT_DIMS)
                acc = part if acc is None else acc + part
        if j > 0:
            res.append(acc[0:HEAD_DIM] / acc[HEAD_DIM:HEAD_DIM + 1])
            if len(res) == 4:
                finish(jobs[j - 1][0], res)
                res = []
        if j < len(jobs):
            s_prev, m_prev = s_cur, jnp.max(mvec, axis=-1, keepdims=True)


def diff_mixer(p, lam_p, ng, bsz, seq, lam_init, ctx=None, rope_tabs=None, tq=1024):
    tq = min(tq, seq)
    nq = seq // tq
    npair = MIX_W // LANES
    rope = rope_tabs is not None
    ctx_len = ctx[0].shape[1] if ctx is not None else 0
    assert rope == (ctx is not None)
    q_spec = pl.BlockSpec((None, tq, LANES), lambda b, hp, i: (T_DQ, b * nq + i, hp))
    k_spec = pl.BlockSpec((None, seq, LANES), lambda b, hp, i: (T_DK, b, hp))
    v_spec = pl.BlockSpec((None, seq, LANES), lambda b, hp, i: (T_DV, b, hp))
    const = lambda shape: pl.BlockSpec(shape, lambda b, hp, i: (0,) * len(shape))
    args = [p, p, p]
    specs = [q_spec, k_spec, v_spec]
    if rope:
        c_spec = pl.BlockSpec((None, ctx_len, LANES), lambda b, hp, i: (b, 0, hp))
        tab_q = pl.BlockSpec((tq, LANES), lambda b, hp, i: (i, 0))
        args += [ctx[0], ctx[1], rope_tabs[0], rope_tabs[1], rope_tabs[0], rope_tabs[1]]
        specs += [c_spec, c_spec, tab_q, tab_q, const((seq, LANES)), const((seq, LANES))]
    args += [lam_p, ng]
    specs += [const((8, LANES)), const((1, LANES))]
    return pl.pallas_call(
        functools.partial(_diff_kernel, seq=seq, ctx_len=ctx_len, rope=rope, lam_init=lam_init),
        out_shape=jax.ShapeDtypeStruct((bsz * seq, MIX_W), bf16),
        grid=(bsz, npair, nq),
        in_specs=specs,
        out_specs=pl.BlockSpec((tq, LANES), lambda b, hp, i: (b * nq + i, hp)),
        scratch_shapes=[pltpu.VMEM((seq + ctx_len, LANES), bf16), pltpu.VMEM((2, LANES, seq + ctx_len), bf16)],
        compiler_params=_cparams("parallel", "parallel", "arbitrary"),
        name="diff_mixer",
    )(*args)


def _ssd_kernel(sx_ref, bcr_ref, sxp_ref, bcp_ref, sxn_ref, bcn_ref, cw_ref, cb_ref, z_ref, dt_ref, dtb_ref, alog_ref,
                dexp_ref, ng_ref, exp_ref, s0_ref, o_ref, sfin_ref,
                yf_scr, st_scr, dt_scr, dta_scr, xs_ref, bc_ref, *, nblk, tl):
    ph = pl.program_id(1)
    i = pl.program_id(2)
    fwd = ph == 0
    c = SSD_CHUNK
    nch = tl // c
    rep = NH // SSD_G
    blk = jnp.where(fwd, i, nblk - 1 - i)

    @pl.when(i == 0)
    def _():
        st_scr[...] = _expand_state(s0_ref[...], SSD_N, HEAD_DIM)

    brow = pl.multiple_of(blk * tl, tl)

    @pl.when(fwd)
    def _():
        row = lax.broadcasted_iota(jnp.int32, (tl, 1), 0)
        for raw_ref, prev_ref, next_ref, dst_ref, lo in ((sx_ref, sxp_ref, sxn_ref, xs_ref, 0),
                                                        (bcr_ref, bcp_ref, bcn_ref, bc_ref, MIX_W)):
            x = raw_ref[...].astype(f32)
            w = cw_ref[:, lo:lo + x.shape[1]]
            prev_row = jnp.where(blk == 0, 0.0, prev_ref[HALO - 1:HALO, :].astype(f32))
            next_row = jnp.where(blk == nblk - 1, 0.0, next_ref[0:1, :].astype(f32))
            xm = jnp.where(row == 0, prev_row, pltpu.roll(x, 1, 0))
            xp = jnp.where(row == tl - 1, next_row, pltpu.roll(x, tl - 1, 0))
            dst_ref[pl.ds(brow, tl), :] = _silu(xm * w[0:1] + x * w[1:2] + xp * w[2:3]
                                                + cb_ref[:, lo:lo + x.shape[1]])

    dtv = _softplus(dt_ref[...].astype(f32) + dtb_ref[...])
    dtv = jnp.where(fwd, dtv, pltpu.roll(dtv, LANES - NH, 1))
    av = -jnp.exp(alog_ref[...])
    av = jnp.where(fwd, av, pltpu.roll(av, LANES - NH, 1))
    dt_scr[...] = dtv
    dta_scr[...] = dtv * av

    tri = _tri_mask(c, c, jnp.where(fwd, 1, -1))
    trib = jnp.where(tri, 1.0, 0.0).astype(bf16)
    head = lax.broadcasted_iota(jnp.int32, (1, MIX_W), 1) // HEAD_DIM
    bd = (lax.broadcasted_iota(jnp.int32, (MIX_W, MIX_W), 0) // SSD_N
          == lax.broadcasted_iota(jnp.int32, (MIX_W, MIX_W), 1) // HEAD_DIM)
    def chunk(j, carry):
        cj = jnp.where(fwd, j, nch - 1 - j)
        off = pl.multiple_of(cj * c, c)
        srow = pl.multiple_of(brow + off, c)
        dtc = dt_scr[pl.ds(off, c), :]
        cum = _sel_dot(trib, dta_scr[pl.ds(off, c), :])
        cum_t = cum.T
        dt_t = dtc.T
        earg = _sel_dot(cum, exp_ref[...])
        elast = jnp.where(fwd, earg[c - 1:c, :], earg[0:1, :])
        dtx = _sel_dot(dtc, exp_ref[...])
        bt = _per_head_groups(bc_ref[pl.ds(srow, c), 0:LANES])
        ct = _per_head_groups(bc_ref[pl.ds(srow, c), LANES:2 * LANES])
        x = xs_ref[pl.ds(srow, c), :]
        xb = x.astype(bf16)
        st = st_scr[...]
        y = _dot((ct * jnp.exp(earg)).astype(bf16), st.astype(bf16))
        btail = (bt * jnp.exp(elast - earg) * dtx).astype(bf16)
        new = _dot(btail, xb, TN_DIMS)
        st_scr[...] = st * jnp.exp(elast) + jnp.where(bd, new, 0.0)
        btb = bt.astype(bf16)
        cb = [_dot(jnp.where(head == g * rep, ct, 0.0).astype(bf16), btb, NT_DIMS) for g in range(SSD_G)]
        ws = []
        for h in range(NH):
            seg = cum[:, h:h + 1] - cum_t[h:h + 1, :]
            lm = jnp.exp(jnp.where(tri, seg, -jnp.inf))
            ws.append((cb[h // rep] * lm * dt_t[h:h + 1, :]).astype(bf16))
        y += _dot(jnp.concatenate(ws, axis=1), _block_diag_rows(xb, c, HEAD_DIM))
        yf_scr[pl.ds(pl.multiple_of(brow + off, c), c), :] += y
        return carry

    @pl.when(fwd)
    def _():
        yf_scr[pl.ds(brow, tl), :] = jnp.zeros((tl, MIX_W), f32)

    lax.fori_loop(0, nch, chunk, 0, unroll=True)

    @pl.when(ph == 1)
    def _():
        yt = yf_scr[pl.ds(brow, tl), :] + xs_ref[pl.ds(brow, tl), :] * dexp_ref[...]
        t = yt * _silu(z_ref[...].astype(f32))
        ms = jnp.mean(t * t, axis=-1, keepdims=True)
        o_ref[...] = (t * lax.rsqrt(ms + NORM_EPS) * ng_ref[...]).astype(bf16)

    @pl.when(i == nblk - 1)
    def _():
        sfin_ref[...] = _compact_state(st_scr[...])


def ssd_mixer(p, cw, cb, dtb, alog, dexp, ng, expand, s0, bsz, seq, tl=1024):
    tl = min(tl, seq)
    nblk = seq // tl
    n_halo = bsz * seq // HALO
    per_blk = tl // HALO
    blk = lambda ph, i: jnp.where(ph == 0, i, nblk - 1 - i)
    rows = lambda b, ph, i: b * nblk + blk(ph, i)
    const = lambda shape: pl.BlockSpec(shape, lambda b, ph, i: (0,) * len(shape))
    state = pl.BlockSpec((None, None, SSD_N, MIX_W), lambda b, ph, i: (b, ph, 0, 0))
    body = lambda tile, w: pl.BlockSpec((None, tl, w), lambda b, ph, i: (tile, rows(b, ph, i), 0))
    crow = lambda b, ph, i: jnp.where(ph == 0, rows(b, ph, i), b * nblk + nblk - 1)
    cbody = lambda tile, w: pl.BlockSpec((None, tl, w), lambda b, ph, i: (tile, crow(b, ph, i), 0))
    prev_halo = lambda tile, w: pl.BlockSpec(
        (None, HALO, w), lambda b, ph, i: (tile, jnp.maximum(crow(b, ph, i) * per_blk - 1, 0), 0))
    next_halo = lambda tile, w: pl.BlockSpec(
        (None, HALO, w), lambda b, ph, i: (tile, jnp.minimum((crow(b, ph, i) + 1) * per_blk, n_halo - 1), 0))
    bcw = 2 * SSD_G * SSD_N
    return pl.pallas_call(
        functools.partial(_ssd_kernel, nblk=nblk, tl=tl),
        out_shape=(jax.ShapeDtypeStruct((bsz * seq, MIX_W), bf16),
                   jax.ShapeDtypeStruct((bsz, 2, SSD_N, MIX_W), f32)),
        grid=(bsz, 2, nblk),
        in_specs=[cbody(T_SX, PT), cbody(T_MISC, bcw), prev_halo(T_SX, PT), prev_halo(T_MISC, bcw),
                  next_halo(T_SX, PT), next_halo(T_MISC, bcw), const((3, MIX_W + bcw)), const((1, MIX_W + bcw)),
                  pl.BlockSpec((None, tl, PT),
                               lambda b, ph, i: (T_SZ, b * nblk + jnp.where(ph == 0, nblk - 1, nblk - 1 - i), 0)),
                  pl.BlockSpec((None, tl, LANES), lambda b, ph, i: (T_MISC, rows(b, ph, i), MISC_DT // LANES)),
                  const((1, LANES)), const((1, LANES)), const((1, MIX_W)), const((1, MIX_W)),
                  const((LANES, MIX_W)), state],
        out_specs=(pl.BlockSpec((tl, MIX_W),
                                lambda b, ph, i: (b * nblk + jnp.where(ph == 0, nblk - 1, nblk - 1 - i), 0)),
                   state),
        scratch_shapes=[pltpu.VMEM((seq, MIX_W), f32), pltpu.VMEM((MIX_W, MIX_W), f32),
                        pltpu.VMEM((tl, LANES), f32), pltpu.VMEM((tl, LANES), f32),
                        pltpu.VMEM((seq, MIX_W), f32), pltpu.VMEM((seq, bcw), f32)],
        compiler_params=_cparams("parallel", "arbitrary", "arbitrary"),
        name="ssd_mixer",
    )(p, p, p, p, p, p, cw, cb.reshape(1, -1), p, p, dtb, alog, dexp, ng, expand, s0)


def _hy_filter_kernel(zf_ref, zb_ref, tf_ref, tb_ref, w1_ref, b1_ref, sw_ref, w2_ref, b2_ref,
                      w3f_ref, w3b_ref, b3f_ref, b3b_ref, df_ref, db_ref, skip_ref, r_ref, hf_scr, hb_scr, *, seq):
    first = jnp.logical_and(pl.program_id(0) == 0, pl.program_id(1) == 0)

    @pl.when(first)
    def _():
        for z_ref, h_scr in ((zf_ref, hf_scr), (zb_ref, hb_scr)):
            h = jnp.sin(sw_ref[...] * (_dot(w1_ref[...], z_ref[...], precision=HI) + b1_ref[...]))
            h_scr[...] = jnp.sin(sw_ref[...] * (_dot(w2_ref[...], h, precision=HI) + b2_ref[...]))

    hf = (_dot(w3f_ref[...], hf_scr[...], precision=HI) + b3f_ref[...]) * jnp.exp(-tf_ref[...] * jnp.abs(df_ref[...]))
    hb = (_dot(w3b_ref[...], hb_scr[...], precision=HI) + b3b_ref[...]) * jnp.exp(-tb_ref[...] * jnp.abs(db_ref[...]))
    den = (jnp.sum(jnp.abs(hf), axis=-1, keepdims=True) + jnp.sum(jnp.abs(hb), axis=-1, keepdims=True)) + NORM_EPS
    col = lax.broadcasted_iota(jnp.int32, (1, seq), 1)
    r_ref[:, 0:seq] = jnp.where(col == 0, 0.0, hb / den)
    r_ref[:, seq:2 * seq] = hf / den + jnp.where(col == 0, skip_ref[...], 0.0)


def hy_filters(fp, seq, cbf=128):
    t = jnp.arange(seq, dtype=f32) / seq
    t_rev = jnp.concatenate([t[:1], t[:0:-1]])

    def feats(tt):
        tc = tt[:, None]
        ang = 2.0 * math.pi * tc * jnp.arange(1, HY_BANDS + 1, dtype=f32)
        z = jnp.concatenate([tc, jnp.cos(ang), jnp.sin(ang)], axis=-1)
        return jnp.pad(z, ((0, 0), (0, HY_EMB_PAD - HY_EMB))).T

    ch = MIX_W
    nb = ch // cbf
    col = lambda a: a.reshape(-1, 1)
    w3t = fp['w3'].T.reshape(2, 2, ch, HY_HID)
    b3 = fp['b3'].reshape(2, 2, ch, 1)
    dec = fp['decay'].reshape(2, 2, ch, 1)
    const = lambda shape: pl.BlockSpec(shape, lambda o, j: (0,) * len(shape))
    sel = lambda d, last: pl.BlockSpec((None, None, cbf, last), lambda o, j: (o, d, j, 0))
    return pl.pallas_call(
        functools.partial(_hy_filter_kernel, seq=seq),
        out_shape=jax.ShapeDtypeStruct((2, ch, 2 * seq), f32),
        grid=(2, nb),
        in_specs=[const((HY_EMB_PAD, seq)), const((HY_EMB_PAD, seq)), const((1, seq)), const((1, seq)),
                  const((HY_HID, HY_EMB_PAD)), const((HY_HID, 1)), const((HY_HID, 1)),
                  const((HY_HID, HY_HID)), const((HY_HID, 1)),
                  sel(0, HY_HID), sel(1, HY_HID), sel(0, 1), sel(1, 1), sel(0, 1), sel(1, 1),
                  pl.BlockSpec((None, cbf, 1), lambda o, j: (o, j, 0))],
        out_specs=pl.BlockSpec((None, cbf, 2 * seq), lambda o, j: (o, j, 0)),
        scratch_shapes=[pltpu.VMEM((HY_HID, seq), f32), pltpu.VMEM((HY_HID, seq), f32)],
        compiler_params=_cparams("arbitrary", "arbitrary"),
        name="hy_filters",
    )(feats(t), feats(t_rev), t.reshape(1, seq), t_rev.reshape(1, seq),
      jnp.pad(fp['w1'], ((0, HY_EMB_PAD - HY_EMB), (0, 0))).T, col(fp['b1']), col(fp['sin_w']),
      fp['w2'].T, col(fp['b2']), w3t, w3t, b3, b3, dec, dec, fp['skip'].reshape(2, ch, 1))


def _hyena_kernel(cw_ref, cb_ref, hv_ref, h1_ref, h2_ref, r_ref, o_ref, acc_scr, *, bsz, n_i, cb_n):
    tb = HY_TB
    gr = tb // 2
    sub = tb // LANES
    seq = n_i * tb
    rows = n_i * bsz
    base = pl.program_id(0) * cb_n
    n_ch = MIX_W
    lane = lax.broadcasted_iota(jnp.int32, (1, tb), 1)
    zblk = jnp.zeros((bsz, tb), f32)

    def load(ref, ch):
        return jnp.concatenate(
            [jnp.concatenate([ref[ch, pl.ds(sub * ib + k, bsz, stride=sub * n_i), :] for k in range(sub)], axis=1)
             for ib in range(n_i)], axis=0)

    def short_conv(a, stream, ch):
        idx = stream * n_ch + base + ch
        prev = jnp.concatenate([zblk, a[:rows - bsz]], axis=0) if n_i > 1 else zblk
        nxt = jnp.concatenate([a[bsz:], zblk], axis=0) if n_i > 1 else zblk
        am = pltpu.roll(jnp.where(lane == tb - 1, prev, a), 1, 1)
        ap = pltpu.roll(jnp.where(lane == 0, nxt, a), tb - 1, 1)
        return (am * cw_ref[idx] + a * cw_ref[3 * n_ch + idx] + ap * cw_ref[6 * n_ch + idx]) + cb_ref[idx]

    def long_conv(u, order, ch):
        rrow = r_ref[order, pl.ds(ch, 1), :]
        bits = lambda a: lax.bitcast_convert_type(a.astype(bf16).astype(f32), jnp.int32)
        word = (bits(pltpu.roll(rrow, 1, 1)) & jnp.int32(-65536)) | lax.shift_right_logical(bits(rrow), 16)
        g = pltpu.bitcast(
            pltpu.roll(jnp.broadcast_to(word, (gr // 2, 2 * seq)), 0, 1, stride=2, stride_axis=0), bf16)
        for d in [0] + [e for e in range(-(n_i - 1), n_i) if e != 0]:
            n = n_i - abs(d)
            src = max(0, -d) * bsz
            dst = max(0, d) * bsz
            c0 = seq + d * tb
            m = jnp.concatenate([g[:, c0:c0 + tb], g[:, c0 - gr:c0 - gr + tb]], axis=0)
            part = _dot(u[src:src + n * bsz].astype(bf16), m)
            if d == 0:
                acc_scr[order] = part
            else:
                acc_scr[order, dst:dst + n * bsz, :] += part
        return acc_scr[order]

    def body(ch, carry):
        hv = short_conv(load(hv_ref, ch), 0, ch)
        hx1 = short_conv(load(h1_ref, ch), 1, ch)
        hx2 = short_conv(load(h2_ref, ch), 2, ch)
        zz = hx1 * long_conv(hv, 0, ch)
        out = hx2 * long_conv(zz, 1, ch)
        for ib in range(n_i):
            for k in range(sub):
                o_ref[ch, pl.ds(sub * ib + k, bsz, stride=sub * n_i), :] = (
                    out[ib * bsz:(ib + 1) * bsz, k * LANES:(k + 1) * LANES])
        return carry

    lax.fori_loop(0, cb_n, body, 0, unroll=8)


def hyena_mixer(hy_t, r, cw, cb, bsz, seq, cb_n=8):
    n_i = seq // HY_TB
    ch = MIX_W
    nblk = ch // cb_n
    n_rows = bsz * seq // LANES
    x3 = hy_t.reshape(3 * ch, n_rows, LANES)
    stream = lambda s: pl.BlockSpec((cb_n, n_rows, LANES), lambda j: (s * nblk + j, 0, 0))
    smem = pl.BlockSpec(memory_space=pltpu.SMEM)
    out = pl.pallas_call(
        functools.partial(_hyena_kernel, bsz=bsz, n_i=n_i, cb_n=cb_n),
        out_shape=jax.ShapeDtypeStruct((ch, n_rows, LANES), f32),
        grid=(nblk,),
        in_specs=[smem, smem, stream(0), stream(1), stream(2),
                  pl.BlockSpec((2, cb_n, 2 * seq), lambda j: (0, j, 0))],
        out_specs=pl.BlockSpec((cb_n, n_rows, LANES), lambda j: (j, 0, 0)),
        scratch_shapes=[pltpu.VMEM((2, bsz * n_i, HY_TB), f32)],
        compiler_params=_cparams("parallel"),
        name="hyena_mixer",
    )(cw.reshape(-1), cb, x3, x3, x3, r)
    return out.reshape(ch, bsz * seq)


_IN_SIZES = (NH * GLA_DK, NH * GLA_DK, MIX_W, MIX_W, 2 * GLA_RANK, MIX_W, MIX_W, MIX_W,
             MIX_W, MIX_W + 2 * SSD_G * SSD_N, 2 * NH, 3 * MIX_W)
_IN_OFFS = np.concatenate([[0], np.cumsum(_IN_SIZES)]).tolist()


def _prep_w_in(w_in):
    (gq, gk, gv, gg, gr, dq, dk, dv, sz, sxbc, sdt, hy) = [w_in[:, _IN_OFFS[i]:_IN_OFFS[i + 1]]
                                                          for i in range(len(_IN_SIZES))]
    d = w_in.shape[0]
    zeros = lambda n: jnp.zeros((d, n), w_in.dtype)
    misc = jnp.concatenate([sxbc[:, MIX_W:], gr, zeros(LANES - 2 * GLA_RANK), sdt, zeros(LANES - 2 * NH)], axis=1)
    w_tok = jnp.concatenate([gq, gk, gv, gg, dq, dk, dv, sz, sxbc[:, :MIX_W], misc], axis=1)
    w_tok = w_tok.astype(bf16).reshape(d, N_PT, PT).transpose(1, 0, 2)
    return w_tok, hy.T.astype(bf16)


def _prep_gate_w(gate_w):
    w = jnp.zeros((2, LANES, gate_w.shape[2]), f32)
    for d in range(2):
        w = w.at[d, d * GLA_RANK:(d + 1) * GLA_RANK, :].set(gate_w[d])
    hi = w.astype(bf16)
    return hi, (w - hi.astype(f32)).astype(bf16)


def _rope_tables(seq):
    rows = seq // GRID_W
    r, col = jnp.meshgrid(jnp.arange(rows), jnp.arange(GRID_W), indexing='ij')
    r = r.reshape(-1).astype(f32)
    col = col.reshape(-1).astype(f32)
    nf = DIFF_DK // 4
    inv = ROPE_BASE ** (-jnp.arange(nf, dtype=f32) / nf)
    ang = jnp.concatenate([r[:, None] * inv, col[:, None] * inv], axis=-1)
    cos, sin = jnp.cos(ang), jnp.sin(ang)
    reps = LANES // DIFF_DK
    return jnp.tile(jnp.concatenate([cos, cos], axis=-1), (1, reps)), jnp.tile(jnp.concatenate([-sin, sin], axis=-1), (1, reps))


def _const_tables():
    hsum = np.kron(np.eye(NH, dtype=np.float32), np.ones((HEAD_DIM, HEAD_DIM), np.float32))
    expand = np.zeros((LANES, MIX_W), np.float32)
    for h in range(NH):
        expand[h, h * HEAD_DIM:(h + 1) * HEAD_DIM] = 1.0
    return tuple(jnp.asarray(a, dtype=bf16) for a in (hsum, expand))


def _pack_state(s):
    bsz, _, h, a, b = s.shape
    return s.transpose(0, 1, 4, 2, 3).reshape(bsz, 2, b, h * a)


def _unpack_state(st, a):
    bsz, _, b, _ = st.shape
    return st.reshape(bsz, 2, b, NH, a).transpose(0, 1, 3, 4, 2)


def _layer(x, mod, rows_per_mod, lw, layer, consts, bsz, seq, lam_init, r_filt, ctx, rope_tabs, final_g,
           final_norm):
    hsum, expand = consts
    p, hy_t = in_proj(x, mod, lw['norm1_g'], lw['w_tok'], lw['w_hyT'], rows_per_mod)
    if ctx is None:
        gla_s0 = jnp.zeros((bsz, 2, HEAD_DIM, NH * GLA_DK), f32)
        ssd_s0 = jnp.zeros((bsz, 2, SSD_N, MIX_W), f32)
        dctx = None
    else:
        ctx_k, ctx_v, gla_s, ssd_s = ctx
        gla_s0 = _pack_state(gla_s)
        ssd_s0 = _pack_state(ssd_s)
        dctx = (ctx_k.reshape(bsz, -1, MIX_W), ctx_v.reshape(bsz, -1, MIX_W))
    o_gla, gla_fin = gla_mixer(p, lw['gla_gw_hi'], lw['gla_gw_lo'], lw['gla_gb'], lw['gla_ng'], hsum, gla_s0,
                               bsz, seq)
    o_diff = diff_mixer(p, lw['lam_p'], lw['diff_ng'], bsz, seq, lam_init, dctx, rope_tabs)
    o_ssd, ssd_fin = ssd_mixer(p, lw['ssd_conv_w'], lw['ssd_conv_b'], lw['ssd_dtb'], lw['ssd_alog'], lw['ssd_dexp'], lw['ssd_ng'],
                               expand, ssd_s0, bsz, seq)
    o_hy = hyena_mixer(hy_t, r_filt, lw['hy_cw'], lw['hy_cb'], bsz, seq)
    x = out_proj(x, mod, o_gla, o_diff, o_ssd, o_hy, lw['w_out'], layer, rows_per_mod)
    x = mlp(x, mod, lw['norm2_g'], lw['mlp_w1'], lw['mlp_w2'], layer, final_g, rows_per_mod, final_norm)
    return x, p, gla_fin, ssd_fin


def kernel(x_prompt, x_sample, cache_diff_k, cache_diff_v, state_gla, state_ssd, c, c_ctx, ada_w, ada_b, norm1_g, norm2_g, w_in, w_out, gla_gate_w, gla_gate_b, gla_norm_g, diff_lambda, diff_norm_g, ssd_conv_w, ssd_conv_b, ssd_dt_bias, ssd_a_log, ssd_d, ssd_norm_g, hy_conv_w, hy_conv_b, hy_f_w1, hy_f_b1, hy_f_w2, hy_f_b2, hy_f_w3, hy_f_b3, hy_sin_w, hy_decay, hy_skip, mlp_w1, mlp_w2, final_g):
    bp, lp, d = x_prompt.shape
    bs, ls, _ = x_sample.shape
    depth = w_in.shape[0]
    consts = _const_tables()
    rope_tabs = _rope_tables(ls)
    n_c = 16
    cvec = jnp.concatenate([c_ctx[None], c, jnp.zeros((n_c - 1 - bs, d), f32)], axis=0)
    hp = x_prompt.reshape(bp * lp, d)
    hs = x_sample.reshape(bs * ls, d)
    ks_, vs_, gs_, ss_ = [], [], [], []
    w_out_b, mlp_w1_b, mlp_w2_b = w_out.astype(bf16), mlp_w1.astype(bf16), mlp_w2.astype(bf16)
    for l in range(depth):
        w_tok, w_hyT = _prep_w_in(w_in[l])
        pad_l = lambda a: jnp.pad(a, (0, LANES - a.shape[0])).reshape(1, LANES)
        gw_hi, gw_lo = _prep_gate_w(gla_gate_w[l])
        lw = dict(
            norm1_g=norm1_g[l], norm2_g=norm2_g[l], w_tok=w_tok, w_hyT=w_hyT, w_out=w_out_b,
            mlp_w1=mlp_w1_b, mlp_w2=mlp_w2_b,
            gla_gw_hi=gw_hi, gla_gw_lo=gw_lo, gla_gb=gla_gate_b[l].reshape(2, 1, -1), gla_ng=jnp.tile(gla_norm_g[l], NH).reshape(1, MIX_W),
            lam_p=jnp.pad(diff_lambda[l], ((0, 4), (0, LANES - DIFF_DK))),
            diff_ng=jnp.tile(diff_norm_g[l], LANES // HEAD_DIM).reshape(1, LANES),
            ssd_conv_w=ssd_conv_w[l], ssd_conv_b=ssd_conv_b[l],
            ssd_dtb=pad_l(ssd_dt_bias[l].reshape(-1)), ssd_alog=pad_l(ssd_a_log[l].reshape(-1)),
            ssd_dexp=jnp.repeat(ssd_d[l], HEAD_DIM).reshape(1, MIX_W), ssd_ng=ssd_norm_g[l].reshape(1, MIX_W),
            hy_cw=hy_conv_w[l], hy_cb=hy_conv_b[l],
        )
        fp = dict(w1=hy_f_w1[l], b1=hy_f_b1[l], w2=hy_f_w2[l], b2=hy_f_b2[l], w3=hy_f_w3[l], b3=hy_f_b3[l],
                  sin_w=hy_sin_w[l], decay=hy_decay[l], skip=hy_skip[l])
        mod = ada_mod(cvec, ada_w, ada_b, l).reshape(n_c, 6, d)
        lam_init = 0.8 - 0.6 * math.exp(-0.3 * l)
        last = l == depth - 1
        hp, p_p, g_l, s_l = _layer(hp, mod[0:1], bp * lp, lw, l, consts, bp, lp, lam_init, hy_filters(fp, lp),
                                   None, None, final_g, last)
        ks_.append(p_p[T_DK].astype(f32).reshape(bp, lp, NH, 2 * DIFF_DK))
        vs_.append(p_p[T_DV].astype(f32).reshape(bp, lp, NH, HEAD_DIM))
        gs_.append(_unpack_state(g_l, GLA_DK))
        ss_.append(_unpack_state(s_l, HEAD_DIM))
        hs, _, _, _ = _layer(hs, mod[1:1 + bs], ls, lw, l, consts, bs, ls, lam_init, hy_filters(fp, ls),
                             (cache_diff_k[:, l], cache_diff_v[:, l], state_gla[:, l], state_ssd[:, l]),
                             rope_tabs, final_g, last)
    return (hp.reshape(bp, lp, d), hs.reshape(bs, ls, d), jnp.stack(ks_, axis=1), jnp.stack(vs_, axis=1),
            jnp.stack(gs_, axis=1), jnp.stack(ss_, axis=1))
```

```python
import functools
import math

import numpy as np
import jax
import jax.numpy as jnp
from jax import lax
from jax.experimental import pallas as pl
from jax.experimental.pallas import tpu as pltpu

f32 = jnp.float32
bf16 = jnp.bfloat16
HI = lax.Precision.HIGHEST

LANES = 128
VMEM_LIMIT = 56 * 2**20

D_MODEL = 2048
GRID_W = 64
MIX_W = D_MODEL // 4
HEAD_DIM = 64
NH = MIX_W // HEAD_DIM
D_FF = 4 * D_MODEL
NORM_EPS = 1e-6
LOG2E = 1.0 / math.log(2.0)
GLA_DK = HEAD_DIM // 2
GLA_RANK = 16
GLA_TAU = 16.0
GLA_CHUNK = 128
DIFF_DK = HEAD_DIM // 2
ROPE_BASE = 10000.0
SSD_N = 64
SSD_G = 2
SSD_CHUNK = 128
HY_BANDS = 16
HY_EMB = 2 * HY_BANDS + 1
HY_EMB_PAD = 40
HY_HID = 64
DIFF_KEY_CHUNK = 512
DIFF_Q_SUB = 512
HALO = 16
MLP_TF = 1024
HY_TB = 256

PT = 512
(T_GQK, T_GV, T_GG, T_DQ, T_DK, T_DV, T_SZ, T_SX, T_MISC) = range(9)
N_PT = 9
PT_GROUP = 3
MISC_GR = 256
MISC_DT = 384
SH1, SC1, G1, SH2, SC2, G2 = range(6)

NT_DIMS = (((1,), (1,)), ((), ()))
TN_DIMS = (((0,), (0,)), ((), ()))


def _cparams(*sem):
    return pltpu.CompilerParams(dimension_semantics=sem, vmem_limit_bytes=VMEM_LIMIT)


def _sigmoid(x):
    return 1.0 / (1.0 + jnp.exp(-x))


def _silu(x):
    return x * _sigmoid(x)


def _softplus(x):
    return jnp.maximum(x, 0.0) + jnp.log1p(jnp.exp(-jnp.abs(x)))


def _log_sigmoid(x):
    return jnp.minimum(x, 0.0) - jnp.log1p(jnp.exp(-jnp.abs(x)))


def _tri_mask(rows, c, sgn):
    r_i = lax.broadcasted_iota(jnp.int32, (rows, c), 0) & (c - 1)
    c_i = lax.broadcasted_iota(jnp.int32, (rows, c), 1)
    return (r_i - c_i) * sgn >= 0


def _expand_state(s, row_w, col_w):
    rows, cols = NH * row_w, NH * col_w
    bd = (lax.broadcasted_iota(jnp.int32, (rows, cols), 0) // row_w
          == lax.broadcasted_iota(jnp.int32, (rows, cols), 1) // col_w)
    return jnp.where(bd, jnp.concatenate([s] * NH, axis=0), 0.0)


def _compact_state(st):
    row_w = st.shape[0] // NH
    out = st[0:row_w]
    for h in range(1, NH):
        out += st[h * row_w:(h + 1) * row_w]
    return out


def _split(x, n):
    parts = []
    for _ in range(n - 1):
        hi = x.astype(bf16)
        parts.append(hi)
        x = x - hi.astype(f32)
    parts.append(x.astype(bf16))
    return parts


def _sel_dot(a, b, dims=None, terms=3):
    if a.dtype == bf16:
        prods = [_dot(a, p, dims) for p in _split(b, terms)]
    else:
        prods = [_dot(p, b, dims) for p in _split(a, terms)]
    return functools.reduce(lambda x, y: x + y, prods)


def _block_diag_rows(x, rows, col_w):
    shape = (NH * rows, NH * col_w)
    keep = (lax.broadcasted_iota(jnp.int32, shape, 0) // rows == lax.broadcasted_iota(jnp.int32, shape, 1) // col_w)
    return jnp.where(keep, jnp.concatenate([x] * NH, axis=0), jnp.zeros((), x.dtype))


def _per_head_groups(g2):
    assert SSD_G == 2 and SSD_N == HEAD_DIM and LANES == 2 * SSD_N
    first = lax.broadcasted_iota(jnp.int32, (1, LANES), 1) < SSD_N
    swapped = pltpu.roll(g2, SSD_N, 1)
    g0 = jnp.where(first, g2, swapped)
    g1 = jnp.where(first, swapped, g2)
    rep = NH * HEAD_DIM // (SSD_G * LANES)
    return jnp.concatenate([g0] * rep + [g1] * rep, axis=1)


def _dot(a, b, dims=None, precision=None):
    if dims is None:
        return jnp.dot(a, b, preferred_element_type=f32, precision=precision)
    return lax.dot_general(a, b, dims, preferred_element_type=f32, precision=precision)


def _ada_kernel(c_ref, w_ref, b_ref, o_ref):
    s = _silu(c_ref[...]).astype(bf16)
    o_ref[...] = _dot(s, w_ref[...].astype(bf16)) + b_ref[...]


def ada_mod(cvec, w, b, layer, tn=1024):
    m, d = cvec.shape
    n = w.shape[2]
    return pl.pallas_call(
        _ada_kernel,
        out_shape=jax.ShapeDtypeStruct((m, n), f32),
        grid=(n // tn,),
        in_specs=[pl.BlockSpec((m, d), lambda j: (0, 0)),
                  pl.BlockSpec((None, d, tn), lambda j: (layer, 0, j)),
                  pl.BlockSpec((None, 1, tn), lambda j: (layer, 0, j))],
        out_specs=pl.BlockSpec((m, tn), lambda j: (0, j)),
        compiler_params=_cparams("arbitrary"),
        name="ada_mod",
    )(cvec, w, b.reshape(b.shape[0], 1, n))


def _norm_mod(x, g, sc, sh):
    ms = jnp.mean(x * x, axis=-1, keepdims=True)
    return (x * lax.rsqrt(ms + NORM_EPS) * g) * (1.0 + sc) + sh


def _norm_next_chunk(xn_ref, modn_ref, g_ref, sc, sh, u_scr, slot, row0):
    xn = xn_ref[...]
    u = _norm_mod(xn, g_ref[...], modn_ref[sc:sc + 1, :], modn_ref[sh:sh + 1, :])
    u_scr[slot, pl.ds(row0, xn.shape[0]), :] = u.astype(bf16)


def _inproj_kernel(x_ref, mod_ref, g_ref, wt_ref, wh_ref, p_ref, h_ref, u_scr, *, n_hy):
    j = pl.program_id(1)

    @pl.when(j == 0)
    def _():
        u = _norm_mod(x_ref[...], g_ref[...], mod_ref[SC1:SC1 + 1, :], mod_ref[SH1:SH1 + 1, :])
        u_scr[...] = u.astype(bf16)

    @pl.when(j < n_hy)
    def _():
        h_ref[...] = _dot(wh_ref[...], u_scr[...], NT_DIMS)

    @pl.when(j >= n_hy)
    def _():
        for k in range(PT_GROUP):
            p_ref[k] = _dot(u_scr[...], wt_ref[k]).astype(p_ref.dtype)


def in_proj(x, mod, g, w_tok, w_hyT, rows_per_mod, tm=1024):
    t, d = x.shape
    assert t % tm == 0 and rows_per_mod % tm == 0, (t, rows_per_mod, tm)
    n_hy = w_hyT.shape[0] // PT
    n_grp = N_PT // PT_GROUP
    return pl.pallas_call(
        functools.partial(_inproj_kernel, n_hy=n_hy),
        out_shape=(jax.ShapeDtypeStruct((N_PT, t, PT), bf16), jax.ShapeDtypeStruct((n_hy * PT, t), f32)),
        grid=(t // tm, n_hy + n_grp),
        in_specs=[pl.BlockSpec((tm, d), lambda i, j: (i, 0)),
                  pl.BlockSpec((None, 6, d), lambda i, j: ((i * tm) // rows_per_mod, 0, 0)),
                  pl.BlockSpec((1, d), lambda i, j: (0, 0)),
                  pl.BlockSpec((PT_GROUP, d, PT), lambda i, j: (jnp.maximum(j - n_hy, 0), 0, 0)),
                  pl.BlockSpec((PT, d), lambda i, j: (jnp.minimum(j, n_hy - 1), 0))],
        out_specs=(pl.BlockSpec((PT_GROUP, tm, PT), lambda i, j: (jnp.maximum(j - n_hy, 0), i, 0)),
                   pl.BlockSpec((PT, tm), lambda i, j: (jnp.minimum(j, n_hy - 1), i))),
        scratch_shapes=[pltpu.VMEM((tm, d), bf16)],
        compiler_params=_cparams("parallel", "arbitrary"),
        name="in_proj",
    )(x, mod, g.reshape(1, d), w_tok, w_hyT)


def _outproj_kernel(x_ref, mod_ref, a_ref, b_ref, c_ref, dT_ref, w_ref, o_ref):
    acc = _dot(a_ref[...], w_ref[0:MIX_W, :])
    acc += _dot(b_ref[...], w_ref[MIX_W:2 * MIX_W, :])
    acc += _dot(c_ref[...], w_ref[2 * MIX_W:3 * MIX_W, :])
    acc += _dot(dT_ref[...].astype(bf16), w_ref[3 * MIX_W:4 * MIX_W, :], TN_DIMS)
    o_ref[...] = x_ref[...] + mod_ref[G1:G1 + 1, :] * acc


def out_proj(x, mod, o_a, o_b, o_c, o_dT, w_out, layer, rows_per_mod, tm=512):
    t, d = x.shape
    assert t % tm == 0 and rows_per_mod % tm == 0, (t, rows_per_mod, tm)
    tok = lambda i: (i, 0)
    return pl.pallas_call(
        _outproj_kernel,
        out_shape=jax.ShapeDtypeStruct((t, d), f32),
        grid=(t // tm,),
        in_specs=[pl.BlockSpec((tm, d), tok),
                  pl.BlockSpec((None, 6, d), lambda i: ((i * tm) // rows_per_mod, 0, 0)),
                  pl.BlockSpec((tm, MIX_W), tok), pl.BlockSpec((tm, MIX_W), tok), pl.BlockSpec((tm, MIX_W), tok),
                  pl.BlockSpec((MIX_W, tm), lambda i: (0, i)),
                  pl.BlockSpec((None, d, d), lambda i: (layer, 0, 0))],
        out_specs=pl.BlockSpec((tm, d), tok),
        compiler_params=_cparams("parallel"),
        name="out_proj",
    )(x, mod, o_a, o_b, o_c, o_dT, w_out)


def _mlp_kernel(xc_ref, mod_ref, xn_ref, modn_ref, g_ref, w1_ref, w2_ref, fg_ref, o_ref, u_scr, *, final_norm):
    io = pl.program_id(0)
    j = pl.program_id(1)
    rows = xn_ref.shape[0]
    row0 = pl.multiple_of(j * rows, rows)

    @pl.when(io == 0)
    def _():
        _norm_next_chunk(xn_ref, modn_ref, g_ref, SC2, SH2, u_scr, 0, row0)

    @pl.when(jnp.logical_and(io > 0, j == 0))
    def _():
        o_ref[...] = jnp.zeros_like(o_ref)

    @pl.when(io > 0)
    def _():
        _norm_next_chunk(xn_ref, modn_ref, g_ref, SC2, SH2, u_scr, io % 2, row0)
        h = jnp.maximum(_dot(u_scr[1 - io % 2], w1_ref[...]), 0.0)
        hb = (h * h).astype(bf16)
        d = o_ref.shape[1]
        for n in range(d // PT):
            cols = slice(n * PT, (n + 1) * PT)
            o_ref[:, cols] += mod_ref[G2:G2 + 1, cols] * _dot(hb, w2_ref[:, cols])
        o_ref[pl.ds(row0, rows), :] += xc_ref[...]

    if final_norm:
        @pl.when(jnp.logical_and(io > 0, j == pl.num_programs(1) - 1))
        def _():
            y = o_ref[...]
            ms = jnp.mean(y * y, axis=-1, keepdims=True)
            o_ref[...] = y * lax.rsqrt(ms + NORM_EPS) * fg_ref[...]


def mlp(x, mod, g, w1, w2, layer, final_g, rows_per_mod, final_norm, tm=1024, tf=MLP_TF):
    t, d = x.shape
    assert t % tm == 0 and rows_per_mod % tm == 0, (t, rows_per_mod, tm)
    n_f = w1.shape[2] // tf
    rows = tm // n_f
    n_tiles = t // tm
    tile = lambda io: jnp.maximum(io - 1, 0)
    wj = lambda io, j: jnp.where(io == 0, 0, j)
    chunk_of = lambda ti, j: jnp.minimum(ti * n_f + j, t // rows - 1)
    mod_of = lambda ti: jnp.minimum((ti * tm) // rows_per_mod, mod.shape[0] - 1)
    return pl.pallas_call(
        functools.partial(_mlp_kernel, final_norm=final_norm),
        out_shape=jax.ShapeDtypeStruct((t, d), f32),
        grid=(n_tiles + 1, n_f),
        in_specs=[pl.BlockSpec((rows, d), lambda io, j: (chunk_of(tile(io), j), 0)),
                  pl.BlockSpec((None, 6, d), lambda io, j: (mod_of(tile(io)), 0, 0)),
                  pl.BlockSpec((rows, d), lambda io, j: (chunk_of(jnp.minimum(io, n_tiles - 1), j), 0)),
                  pl.BlockSpec((None, 6, d), lambda io, j: (mod_of(jnp.minimum(io, n_tiles - 1)), 0, 0)),
                  pl.BlockSpec((1, d), lambda io, j: (0, 0)),
                  pl.BlockSpec((None, d, tf), lambda io, j: (layer, 0, wj(io, j))),
                  pl.BlockSpec((None, tf, d), lambda io, j: (layer, wj(io, j), 0)),
                  pl.BlockSpec((1, d), lambda io, j: (0, 0))],
        out_specs=pl.BlockSpec((tm, d), lambda io, j: (tile(io), 0)),
        scratch_shapes=[pltpu.VMEM((2, tm, d), bf16)],
        compiler_params=_cparams("arbitrary", "arbitrary"),
        name="mlp",
    )(x, mod, x, mod, g.reshape(1, d), w1, w2, final_g.reshape(1, d))


def _gla_kernel(qk_ref, v_ref, gg_ref, misc_ref, gwh_ref, gwl_ref, gb_ref, ng_ref, hsum_ref, s0_ref,
                o_ref, sfin_ref, of_scr, st_scr, la_scr, *, nblk, tl):
    ph = pl.program_id(1)
    i = pl.program_id(2)
    fwd = ph == 0
    c = GLA_CHUNK
    nch = tl // c
    qkw = NH * GLA_DK

    @pl.when(i == 0)
    def _():
        st_scr[...] = _expand_state(s0_ref[...], HEAD_DIM, GLA_DK)

    gr = misc_ref[:, MISC_GR:MISC_GR + LANES]
    logits = (_dot(gr, gwh_ref[...]) + _dot(gr, gwl_ref[...])) + gb_ref[...]
    la_scr[...] = _log_sigmoid(logits) * (1.0 / GLA_TAU)

    sgn = jnp.where(fwd, 1, -1)
    trib = jnp.where(_tri_mask(c, c, sgn), 1.0, 0.0).astype(bf16)
    tri_cat = ((lax.broadcasted_iota(jnp.int32, (c, NH * c), 0)
                - (lax.broadcasted_iota(jnp.int32, (c, NH * c), 1) & (c - 1))) * sgn >= 0)
    bd = (lax.broadcasted_iota(jnp.int32, (MIX_W, qkw), 0) // HEAD_DIM
          == lax.broadcasted_iota(jnp.int32, (MIX_W, qkw), 1) // GLA_DK)
    brow = pl.multiple_of(jnp.where(fwd, i, nblk - 1 - i) * tl, tl)

    def chunk(j, carry):
        cj = jnp.where(fwd, j, nch - 1 - j)
        off = pl.multiple_of(cj * c, c)
        q = qk_ref[pl.ds(off, c), 0:qkw].astype(f32) * (GLA_DK ** -0.5)
        k = qk_ref[pl.ds(off, c), qkw:2 * qkw].astype(f32)
        vb = v_ref[pl.ds(off, c), :]
        b = _sel_dot(trib, la_scr[pl.ds(off, c), :])
        btot = jnp.where(fwd, b[c - 1:c, :], b[0:1, :])
        bmid = b[c // 2 - 1:c // 2, :]
        qd = q * jnp.exp(b)
        qi = q * jnp.exp(b - bmid)
        kd = (k * jnp.exp(bmid - b)).astype(bf16)
        kt = (k * jnp.exp(btot - b)).astype(bf16)
        att_cat = jnp.where(tri_cat, _dot(qi.astype(bf16), _block_diag_rows(kd, c, GLA_DK), NT_DIMS), 0.0)
        o = _dot(att_cat.astype(bf16), _block_diag_rows(vb, c, HEAD_DIM))
        st = st_scr[...]
        o += _dot(qd.astype(bf16), st.astype(bf16), NT_DIMS)
        kv = _dot(vb, kt, TN_DIMS)
        st_scr[...] = st * jnp.exp(btot) + jnp.where(bd, kv, 0.0)
        of_scr[pl.ds(pl.multiple_of(brow + off, c), c), :] += o
        return carry

    @pl.when(fwd)
    def _():
        of_scr[pl.ds(brow, tl), :] = jnp.zeros((tl, MIX_W), f32)

    lax.fori_loop(0, nch, chunk, 0, unroll=True)

    @pl.when(ph == 1)
    def _():
        ot = of_scr[pl.ds(brow, tl), :]
        ss = _sel_dot(ot * ot, hsum_ref[...], terms=2) * (1.0 / HEAD_DIM)
        o_ref[...] = (ot * lax.rsqrt(ss + NORM_EPS) * ng_ref[...] * _silu(gg_ref[...].astype(f32))).astype(bf16)

    @pl.when(i == nblk - 1)
    def _():
        sfin_ref[...] = _compact_state(st_scr[...])


def gla_mixer(p, gw_hi, gw_lo, gb, ng, hsum, s0, bsz, seq, tl=1024):
    tl = min(tl, seq)
    nblk = seq // tl
    qkw = NH * GLA_DK
    blk = lambda ph, i: jnp.where(ph == 0, i, nblk - 1 - i)
    tile = lambda col: pl.BlockSpec((None, tl, PT), lambda b, ph, i: (col, b * nblk + blk(ph, i), 0))
    const = lambda shape: pl.BlockSpec(shape, lambda b, ph, i: (0,) * len(shape))
    per_dir = lambda rows: pl.BlockSpec((None, rows, qkw), lambda b, ph, i: (ph, 0, 0))
    state = pl.BlockSpec((None, None, HEAD_DIM, qkw), lambda b, ph, i: (b, ph, 0, 0))
    return pl.pallas_call(
        functools.partial(_gla_kernel, nblk=nblk, tl=tl),
        out_shape=(jax.ShapeDtypeStruct((bsz * seq, MIX_W), bf16),
                   jax.ShapeDtypeStruct((bsz, 2, HEAD_DIM, qkw), f32)),
        grid=(bsz, 2, nblk),
        in_specs=[tile(T_GQK), tile(T_GV),
                  pl.BlockSpec((None, tl, PT),
                               lambda b, ph, i: (T_GG, b * nblk + jnp.where(ph == 0, nblk - 1, nblk - 1 - i), 0)),
                  tile(T_MISC),
                  per_dir(LANES), per_dir(LANES), per_dir(1), const((1, MIX_W)), const((MIX_W, MIX_W)), state],
        out_specs=(pl.BlockSpec((tl, MIX_W),
                                lambda b, ph, i: (b * nblk + jnp.where(ph == 0, nblk - 1, nblk - 1 - i), 0)),
                   state),
        scratch_shapes=[pltpu.VMEM((seq, MIX_W), f32), pltpu.VMEM((MIX_W, qkw), f32), pltpu.VMEM((tl, qkw), f32)],
        compiler_params=_cparams("parallel", "arbitrary", "arbitrary"),
        name="gla_mixer",
    )(p, p, p, p, gw_hi, gw_lo, gb, ng, hsum, s0)


def _rope(x, cos, sin_signed):
    lane = lax.broadcasted_iota(jnp.int32, (1, LANES), 1)
    half = DIFF_DK // 2
    rot = jnp.where((lane % DIFF_DK) < half, pltpu.roll(x, LANES - half, 1), pltpu.roll(x, half, 1))
    return x * cos + rot * sin_signed


def _diff_kernel(*refs, seq, ctx_len, rope, lam_init):
    if rope:
        (q_ref, k_ref, v_ref, ck_ref, cv_ref, cq_ref, sq_ref, ckk_ref, skk_ref, lam_ref, ng_ref,
         o_ref, kb_scr, vt_scr) = refs
    else:
        q_ref, k_ref, v_ref, lam_ref, ng_ref, o_ref, kb_scr, vt_scr = refs
    qi = pl.program_id(2)
    lane = lax.broadcasted_iota(jnp.int32, (1, LANES), 1)
    first = lane < HEAD_DIM

    def put_v(v, lo, hi):
        vt = v.T
        ones = jnp.ones((HEAD_DIM, hi - lo), f32)
        vt_scr[0, :, lo:hi] = jnp.concatenate([vt[0:HEAD_DIM], ones], axis=0).astype(bf16)
        vt_scr[1, :, lo:hi] = jnp.concatenate([vt[HEAD_DIM:], ones], axis=0).astype(bf16)

    @pl.when(qi == 0)
    def _():
        k = k_ref[...].astype(f32)
        if rope:
            k = _rope(k, ckk_ref[...], skk_ref[...])
        kb_scr[0:seq, :] = k.astype(bf16)
        put_v(v_ref[...].astype(f32), 0, seq)
        if ctx_len:
            kb_scr[seq:seq + ctx_len, :] = ck_ref[...].astype(bf16)
            put_v(cv_ref[...], seq, seq + ctx_len)

    q = q_ref[...].astype(f32)
    if rope:
        q = _rope(q, cq_ref[...], sq_ref[...])
    q = q * (DIFF_DK ** -0.5 * LOG2E)
    n_keys = seq + ctx_len
    ck = max(w for w in range(LANES, DIFF_KEY_CHUNK + 1, LANES) if n_keys % w == 0)
    nck = n_keys // ck
    tq = q.shape[0]
    sub = min(DIFF_Q_SUB, tq)
    jobs = [(qs, g) for qs in range(tq // sub) for g in range(4)]
    lp = lam_ref[...]
    lam = (jnp.exp(jnp.sum(lp[0:1] * lp[1:2], keepdims=True)) - jnp.exp(jnp.sum(lp[2:3] * lp[3:4], keepdims=True))
           + lam_init)

    def finish(qs, res):
        o = jnp.concatenate([res[0] - lam * res[1], res[2] - lam * res[3]], axis=0).T
        o2 = o * o
        ss_a = jnp.sum(jnp.where(first, o2, 0.0), axis=-1, keepdims=True)
        ss_b = jnp.sum(jnp.where(first, 0.0, o2), axis=-1, keepdims=True)
        inv = jnp.where(first, lax.rsqrt(ss_a * (1.0 / HEAD_DIM) + NORM_EPS),
                        lax.rsqrt(ss_b * (1.0 / HEAD_DIM) + NORM_EPS))
        o_ref[qs * sub:(qs + 1) * sub, :] = (o * inv * ng_ref[...] * (1.0 - lam_init)).astype(bf16)

    res = []
    s_prev = m_prev = None
    for j in range(len(jobs) + 1):
        s_cur, mvec, acc = [], None, None
        if j < len(jobs):
            qs, g = jobs[j]
            qm = jnp.where(lane // DIFF_DK == g, q[qs * sub:(qs + 1) * sub], 0.0).astype(bf16)
        for c in range(nck):
            if j < len(jobs):
                s = _dot(qm, kb_scr[c * ck:(c + 1) * ck, :], NT_DIMS)
                s_cur.append(s)
                for k in range(ck // LANES):
                    blk = s[:, k * LANES:(k + 1) * LANES]
                    mvec = blk if mvec is None else jnp.maximum(mvec, blk)
            if j > 0:
                p = jnp.exp2(s_prev[c] - m_prev).astype(bf16)
                part = _dot(vt_scr[jobs[j - 1][1] // 2, :, c * ck:(c + 1) * ck], p, NT_DIMS)
                acc = part if acc is None else acc + part
        if j > 0:
            res.append(acc[0:HEAD_DIM] / acc[HEAD_DIM:HEAD_DIM + 1])
            if len(res) == 4:
                finish(jobs[j - 1][0], res)
                res = []
        if j < len(jobs):
            s_prev, m_prev = s_cur, jnp.max(mvec, axis=-1, keepdims=True)


def diff_mixer(p, lam_p, ng, bsz, seq, lam_init, ctx=None, rope_tabs=None, tq=1024):
    tq = min(tq, seq)
    nq = seq // tq
    npair = MIX_W // LANES
    rope = rope_tabs is not None
    ctx_len = ctx[0].shape[1] if ctx is not None else 0
    assert rope == (ctx is not None)
    q_spec = pl.BlockSpec((None, tq, LANES), lambda b, hp, i: (T_DQ, b * nq + i, hp))
    k_spec = pl.BlockSpec((None, seq, LANES), lambda b, hp, i: (T_DK, b, hp))
    v_spec = pl.BlockSpec((None, seq, LANES), lambda b, hp, i: (T_DV, b, hp))
    const = lambda shape: pl.BlockSpec(shape, lambda b, hp, i: (0,) * len(shape))
    args = [p, p, p]
    specs = [q_spec, k_spec, v_spec]
    if rope:
        c_spec = pl.BlockSpec((None, ctx_len, LANES), lambda b, hp, i: (b, 0, hp))
        tab_q = pl.BlockSpec((tq, LANES), lambda b, hp, i: (i, 0))
        args += [ctx[0], ctx[1], rope_tabs[0], rope_tabs[1], rope_tabs[0], rope_tabs[1]]
        specs += [c_spec, c_spec, tab_q, tab_q, const((seq, LANES)), const((seq, LANES))]
    args += [lam_p, ng]
    specs += [const((8, LANES)), const((1, LANES))]
    return pl.pallas_call(
        functools.partial(_diff_kernel, seq=seq, ctx_len=ctx_len, rope=rope, lam_init=lam_init),
        out_shape=jax.ShapeDtypeStruct((bsz * seq, MIX_W), bf16),
        grid=(bsz, npair, nq),
        in_specs=specs,
        out_specs=pl.BlockSpec((tq, LANES), lambda b, hp, i: (b * nq + i, hp)),
        scratch_shapes=[pltpu.VMEM((seq + ctx_len, LANES), bf16), pltpu.VMEM((2, LANES, seq + ctx_len), bf16)],
        compiler_params=_cparams("parallel", "parallel", "arbitrary"),
        name="diff_mixer",
    )(*args)


def _ssd_kernel(sx_ref, bcr_ref, sxp_ref, bcp_ref, sxn_ref, bcn_ref, cw_ref, cb_ref, z_ref, dt_ref, dtb_ref, alog_ref,
                dexp_ref, ng_ref, exp_ref, s0_ref, o_ref, sfin_ref,
                yf_scr, st_scr, dt_scr, dta_scr, xs_ref, bc_ref, *, nblk, tl):
    ph = pl.program_id(1)
    i = pl.program_id(2)
    fwd = ph == 0
    c = SSD_CHUNK
    nch = tl // c
    rep = NH // SSD_G
    blk = jnp.where(fwd, i, nblk - 1 - i)

    @pl.when(i == 0)
    def _():
        st_scr[...] = _expand_state(s0_ref[...], SSD_N, HEAD_DIM)

    brow = pl.multiple_of(blk * tl, tl)

    @pl.when(fwd)
    def _():
        row = lax.broadcasted_iota(jnp.int32, (tl, 1), 0)
        for raw_ref, prev_ref, next_ref, dst_ref, lo in ((sx_ref, sxp_ref, sxn_ref, xs_ref, 0),
                                                        (bcr_ref, bcp_ref, bcn_ref, bc_ref, MIX_W)):
            x = raw_ref[...].astype(f32)
            w = cw_ref[:, lo:lo + x.shape[1]]
            prev_row = jnp.where(blk == 0, 0.0, prev_ref[HALO - 1:HALO, :].astype(f32))
            next_row = jnp.where(blk == nblk - 1, 0.0, next_ref[0:1, :].astype(f32))
            xm = jnp.where(row == 0, prev_row, pltpu.roll(x, 1, 0))
            xp = jnp.where(row == tl - 1, next_row, pltpu.roll(x, tl - 1, 0))
            dst_ref[pl.ds(brow, tl), :] = _silu(xm * w[0:1] + x * w[1:2] + xp * w[2:3]
                                                + cb_ref[:, lo:lo + x.shape[1]])

    dtv = _softplus(dt_ref[...].astype(f32) + dtb_ref[...])
    dtv = jnp.where(fwd, dtv, pltpu.roll(dtv, LANES - NH, 1))
    av = -jnp.exp(alog_ref[...])
    av = jnp.where(fwd, av, pltpu.roll(av, LANES - NH, 1))
    dt_scr[...] = dtv
    dta_scr[...] = dtv * av

    tri = _tri_mask(c, c, jnp.where(fwd, 1, -1))
    trib = jnp.where(tri, 1.0, 0.0).astype(bf16)
    head = lax.broadcasted_iota(jnp.int32, (1, MIX_W), 1) // HEAD_DIM
    bd = (lax.broadcasted_iota(jnp.int32, (MIX_W, MIX_W), 0) // SSD_N
          == lax.broadcasted_iota(jnp.int32, (MIX_W, MIX_W), 1) // HEAD_DIM)
    def chunk(j, carry):
        cj = jnp.where(fwd, j, nch - 1 - j)
        off = pl.multiple_of(cj * c, c)
        srow = pl.multiple_of(brow + off, c)
        dtc = dt_scr[pl.ds(off, c), :]
        cum = _sel_dot(trib, dta_scr[pl.ds(off, c), :])
        cum_t = cum.T
        dt_t = dtc.T
        earg = _sel_dot(cum, exp_ref[...])
        elast = jnp.where(fwd, earg[c - 1:c, :], earg[0:1, :])
        dtx = _sel_dot(dtc, exp_ref[...])
        bt = _per_head_groups(bc_ref[pl.ds(srow, c), 0:LANES])
        ct = _per_head_groups(bc_ref[pl.ds(srow, c), LANES:2 * LANES])
        x = xs_ref[pl.ds(srow, c), :]
        xb = x.astype(bf16)
        st = st_scr[...]
        y = _dot((ct * jnp.exp(earg)).astype(bf16), st.astype(bf16))
        btail = (bt * jnp.exp(elast - earg) * dtx).astype(bf16)
        new = _dot(btail, xb, TN_DIMS)
        st_scr[...] = st * jnp.exp(elast) + jnp.where(bd, new, 0.0)
        btb = bt.astype(bf16)
        cb = [_dot(jnp.where(head == g * rep, ct, 0.0).astype(bf16), btb, NT_DIMS) for g in range(SSD_G)]
        ws = []
        for h in range(NH):
            seg = cum[:, h:h + 1] - cum_t[h:h + 1, :]
            lm = jnp.exp(jnp.where(tri, seg, -jnp.inf))
            ws.append((cb[h // rep] * lm * dt_t[h:h + 1, :]).astype(bf16))
        y += _dot(jnp.concatenate(ws, axis=1), _block_diag_rows(xb, c, HEAD_DIM))
        yf_scr[pl.ds(pl.multiple_of(brow + off, c), c), :] += y
        return carry

    @pl.when(fwd)
    def _():
        yf_scr[pl.ds(brow, tl), :] = jnp.zeros((tl, MIX_W), f32)

    lax.fori_loop(0, nch, chunk, 0, unroll=True)

    @pl.when(ph == 1)
    def _():
        yt = yf_scr[pl.ds(brow, tl), :] + xs_ref[pl.ds(brow, tl), :] * dexp_ref[...]
        t = yt * _silu(z_ref[...].astype(f32))
        ms = jnp.mean(t * t, axis=-1, keepdims=True)
        o_ref[...] = (t * lax.rsqrt(ms + NORM_EPS) * ng_ref[...]).astype(bf16)

    @pl.when(i == nblk - 1)
    def _():
        sfin_ref[...] = _compact_state(st_scr[...])


def ssd_mixer(p, cw, cb, dtb, alog, dexp, ng, expand, s0, bsz, seq, tl=1024):
    tl = min(tl, seq)
    nblk = seq // tl
    n_halo = bsz * seq // HALO
    per_blk = tl // HALO
    blk = lambda ph, i: jnp.where(ph == 0, i, nblk - 1 - i)
    rows = lambda b, ph, i: b * nblk + blk(ph, i)
    const = lambda shape: pl.BlockSpec(shape, lambda b, ph, i: (0,) * len(shape))
    state = pl.BlockSpec((None, None, SSD_N, MIX_W), lambda b, ph, i: (b, ph, 0, 0))
    body = lambda tile, w: pl.BlockSpec((None, tl, w), lambda b, ph, i: (tile, rows(b, ph, i), 0))
    crow = lambda b, ph, i: jnp.where(ph == 0, rows(b, ph, i), b * nblk + nblk - 1)
    cbody = lambda tile, w: pl.BlockSpec((None, tl, w), lambda b, ph, i: (tile, crow(b, ph, i), 0))
    prev_halo = lambda tile, w: pl.BlockSpec(
        (None, HALO, w), lambda b, ph, i: (tile, jnp.maximum(crow(b, ph, i) * per_blk - 1, 0), 0))
    next_halo = lambda tile, w: pl.BlockSpec(
        (None, HALO, w), lambda b, ph, i: (tile, jnp.minimum((crow(b, ph, i) + 1) * per_blk, n_halo - 1), 0))
    bcw = 2 * SSD_G * SSD_N
    return pl.pallas_call(
        functools.partial(_ssd_kernel, nblk=nblk, tl=tl),
        out_shape=(jax.ShapeDtypeStruct((bsz * seq, MIX_W), bf16),
                   jax.ShapeDtypeStruct((bsz, 2, SSD_N, MIX_W), f32)),
        grid=(bsz, 2, nblk),
        in_specs=[cbody(T_SX, PT), cbody(T_MISC, bcw), prev_halo(T_SX, PT), prev_halo(T_MISC, bcw),
                  next_halo(T_SX, PT), next_halo(T_MISC, bcw), const((3, MIX_W + bcw)), const((1, MIX_W + bcw)),
                  pl.BlockSpec((None, tl, PT),
                               lambda b, ph, i: (T_SZ, b * nblk + jnp.where(ph == 0, nblk - 1, nblk - 1 - i), 0)),
                  pl.BlockSpec((None, tl, LANES), lambda b, ph, i: (T_MISC, rows(b, ph, i), MISC_DT // LANES)),
                  const((1, LANES)), const((1, LANES)), const((1, MIX_W)), const((1, MIX_W)),
                  const((LANES, MIX_W)), state],
        out_specs=(pl.BlockSpec((tl, MIX_W),
                                lambda b, ph, i: (b * nblk + jnp.where(ph == 0, nblk - 1, nblk - 1 - i), 0)),
                   state),
        scratch_shapes=[pltpu.VMEM((seq, MIX_W), f32), pltpu.VMEM((MIX_W, MIX_W), f32),
                        pltpu.VMEM((tl, LANES), f32), pltpu.VMEM((tl, LANES), f32),
                        pltpu.VMEM((seq, MIX_W), f32), pltpu.VMEM((seq, bcw), f32)],
        compiler_params=_cparams("parallel", "arbitrary", "arbitrary"),
        name="ssd_mixer",
    )(p, p, p, p, p, p, cw, cb.reshape(1, -1), p, p, dtb, alog, dexp, ng, expand, s0)


def _hy_filter_kernel(zf_ref, zb_ref, tf_ref, tb_ref, w1_ref, b1_ref, sw_ref, w2_ref, b2_ref,
                      w3f_ref, w3b_ref, b3f_ref, b3b_ref, df_ref, db_ref, skip_ref, r_ref, hf_scr, hb_scr, *, seq):
    first = jnp.logical_and(pl.program_id(0) == 0, pl.program_id(1) == 0)

    @pl.when(first)
    def _():
        for z_ref, h_scr in ((zf_ref, hf_scr), (zb_ref, hb_scr)):
            h = jnp.sin(sw_ref[...] * (_dot(w1_ref[...], z_ref[...], precision=HI) + b1_ref[...]))
            h_scr[...] = jnp.sin(sw_ref[...] * (_dot(w2_ref[...], h, precision=HI) + b2_ref[...]))

    hf = (_dot(w3f_ref[...], hf_scr[...], precision=HI) + b3f_ref[...]) * jnp.exp(-tf_ref[...] * jnp.abs(df_ref[...]))
    hb = (_dot(w3b_ref[...], hb_scr[...], precision=HI) + b3b_ref[...]) * jnp.exp(-tb_ref[...] * jnp.abs(db_ref[...]))
    den = (jnp.sum(jnp.abs(hf), axis=-1, keepdims=True) + jnp.sum(jnp.abs(hb), axis=-1, keepdims=True)) + NORM_EPS
    col = lax.broadcasted_iota(jnp.int32, (1, seq), 1)
    r_ref[:, 0:seq] = jnp.where(col == 0, 0.0, hb / den)
    r_ref[:, seq:2 * seq] = hf / den + jnp.where(col == 0, skip_ref[...], 0.0)


def hy_filters(fp, seq, cbf=128):
    t = jnp.arange(seq, dtype=f32) / seq
    t_rev = jnp.concatenate([t[:1], t[:0:-1]])

    def feats(tt):
        tc = tt[:, None]
        ang = 2.0 * math.pi * tc * jnp.arange(1, HY_BANDS + 1, dtype=f32)
        z = jnp.concatenate([tc, jnp.cos(ang), jnp.sin(ang)], axis=-1)
        return jnp.pad(z, ((0, 0), (0, HY_EMB_PAD - HY_EMB))).T

    ch = MIX_W
    nb = ch // cbf
    col = lambda a: a.reshape(-1, 1)
    w3t = fp['w3'].T.reshape(2, 2, ch, HY_HID)
    b3 = fp['b3'].reshape(2, 2, ch, 1)
    dec = fp['decay'].reshape(2, 2, ch, 1)
    const = lambda shape: pl.BlockSpec(shape, lambda o, j: (0,) * len(shape))
    sel = lambda d, last: pl.BlockSpec((None, None, cbf, last), lambda o, j: (o, d, j, 0))
    return pl.pallas_call(
        functools.partial(_hy_filter_kernel, seq=seq),
        out_shape=jax.ShapeDtypeStruct((2, ch, 2 * seq), f32),
        grid=(2, nb),
        in_specs=[const((HY_EMB_PAD, seq)), const((HY_EMB_PAD, seq)), const((1, seq)), const((1, seq)),
                  const((HY_HID, HY_EMB_PAD)), const((HY_HID, 1)), const((HY_HID, 1)),
                  const((HY_HID, HY_HID)), const((HY_HID, 1)),
                  sel(0, HY_HID), sel(1, HY_HID), sel(0, 1), sel(1, 1), sel(0, 1), sel(1, 1),
                  pl.BlockSpec((None, cbf, 1), lambda o, j: (o, j, 0))],
        out_specs=pl.BlockSpec((None, cbf, 2 * seq), lambda o, j: (o, j, 0)),
        scratch_shapes=[pltpu.VMEM((HY_HID, seq), f32), pltpu.VMEM((HY_HID, seq), f32)],
        compiler_params=_cparams("arbitrary", "arbitrary"),
        name="hy_filters",
    )(feats(t), feats(t_rev), t.reshape(1, seq), t_rev.reshape(1, seq),
      jnp.pad(fp['w1'], ((0, HY_EMB_PAD - HY_EMB), (0, 0))).T, col(fp['b1']), col(fp['sin_w']),
      fp['w2'].T, col(fp['b2']), w3t, w3t, b3, b3, dec, dec, fp['skip'].reshape(2, ch, 1))


def _hyena_kernel(cw_ref, cb_ref, hv_ref, h1_ref, h2_ref, r_ref, o_ref, acc_scr, *, bsz, n_i, cb_n):
    tb = HY_TB
    gr = tb // 2
    sub = tb // LANES
    seq = n_i * tb
    rows = n_i * bsz
    base = pl.program_id(0) * cb_n
    n_ch = MIX_W
    lane = lax.broadcasted_iota(jnp.int32, (1, tb), 1)
    zblk = jnp.zeros((bsz, tb), f32)

    def load(ref, ch):
        return jnp.concatenate(
            [jnp.concatenate([ref[ch, pl.ds(sub * ib + k, bsz, stride=sub * n_i), :] for k in range(sub)], axis=1)
             for ib in range(n_i)], axis=0)

    def short_conv(a, stream, ch):
        idx = stream * n_ch + base + ch
        prev = jnp.concatenate([zblk, a[:rows - bsz]], axis=0) if n_i > 1 else zblk
        nxt = jnp.concatenate([a[bsz:], zblk], axis=0) if n_i > 1 else zblk
        am = pltpu.roll(jnp.where(lane == tb - 1, prev, a), 1, 1)
        ap = pltpu.roll(jnp.where(lane == 0, nxt, a), tb - 1, 1)
        return (am * cw_ref[idx] + a * cw_ref[3 * n_ch + idx] + ap * cw_ref[6 * n_ch + idx]) + cb_ref[idx]

    def long_conv(u, order, ch):
        rrow = r_ref[order, pl.ds(ch, 1), :]
        bits = lambda a: lax.bitcast_convert_type(a.astype(bf16).astype(f32), jnp.int32)
        word = (bits(pltpu.roll(rrow, 1, 1)) & jnp.int32(-65536)) | lax.shift_right_logical(bits(rrow), 16)
        g = pltpu.bitcast(
            pltpu.roll(jnp.broadcast_to(word, (gr // 2, 2 * seq)), 0, 1, stride=2, stride_axis=0), bf16)
        for d in [0] + [e for e in range(-(n_i - 1), n_i) if e != 0]:
            n = n_i - abs(d)
            src = max(0, -d) * bsz
            dst = max(0, d) * bsz
            c0 = seq + d * tb
            m = jnp.concatenate([g[:, c0:c0 + tb], g[:, c0 - gr:c0 - gr + tb]], axis=0)
            part = _dot(u[src:src + n * bsz].astype(bf16), m)
            if d == 0:
                acc_scr[order] = part
            else:
                acc_scr[order, dst:dst + n * bsz, :] += part
        return acc_scr[order]

    def body(ch, carry):
        hv = short_conv(load(hv_ref, ch), 0, ch)
        hx1 = short_conv(load(h1_ref, ch), 1, ch)
        hx2 = short_conv(load(h2_ref, ch), 2, ch)
        zz = hx1 * long_conv(hv, 0, ch)
        out = hx2 * long_conv(zz, 1, ch)
        for ib in range(n_i):
            for k in range(sub):
                o_ref[ch, pl.ds(sub * ib + k, bsz, stride=sub * n_i), :] = (
                    out[ib * bsz:(ib + 1) * bsz, k * LANES:(k + 1) * LANES])
        return carry

    lax.fori_loop(0, cb_n, body, 0, unroll=8)


def hyena_mixer(hy_t, r, cw, cb, bsz, seq, cb_n=8):
    n_i = seq // HY_TB
    ch = MIX_W
    nblk = ch // cb_n
    n_rows = bsz * seq // LANES
    x3 = hy_t.reshape(3 * ch, n_rows, LANES)
    stream = lambda s: pl.BlockSpec((cb_n, n_rows, LANES), lambda j: (s * nblk + j, 0, 0))
    smem = pl.BlockSpec(memory_space=pltpu.SMEM)
    out = pl.pallas_call(
        functools.partial(_hyena_kernel, bsz=bsz, n_i=n_i, cb_n=cb_n),
        out_shape=jax.ShapeDtypeStruct((ch, n_rows, LANES), f32),
        grid=(nblk,),
        in_specs=[smem, smem, stream(0), stream(1), stream(2),
                  pl.BlockSpec((2, cb_n, 2 * seq), lambda j: (0, j, 0))],
        out_specs=pl.BlockSpec((cb_n, n_rows, LANES), lambda j: (j, 0, 0)),
        scratch_shapes=[pltpu.VMEM((2, bsz * n_i, HY_TB), f32)],
        compiler_params=_cparams("parallel"),
        name="hyena_mixer",
    )(cw.reshape(-1), cb, x3, x3, x3, r)
    return out.reshape(ch, bsz * seq)


_IN_SIZES = (NH * GLA_DK, NH * GLA_DK, MIX_W, MIX_W, 2 * GLA_RANK, MIX_W, MIX_W, MIX_W,
             MIX_W, MIX_W + 2 * SSD_G * SSD_N, 2 * NH, 3 * MIX_W)
_IN_OFFS = np.concatenate([[0], np.cumsum(_IN_SIZES)]).tolist()


def _prep_w_in(w_in):
    (gq, gk, gv, gg, gr, dq, dk, dv, sz, sxbc, sdt, hy) = [w_in[:, _IN_OFFS[i]:_IN_OFFS[i + 1]]
                                                          for i in range(len(_IN_SIZES))]
    d = w_in.shape[0]
    zeros = lambda n: jnp.zeros((d, n), w_in.dtype)
    misc = jnp.concatenate([sxbc[:, MIX_W:], gr, zeros(LANES - 2 * GLA_RANK), sdt, zeros(LANES - 2 * NH)], axis=1)
    w_tok = jnp.concatenate([gq, gk, gv, gg, dq, dk, dv, sz, sxbc[:, :MIX_W], misc], axis=1)
    w_tok = w_tok.astype(bf16).reshape(d, N_PT, PT).transpose(1, 0, 2)
    return w_tok, hy.T.astype(bf16)


def _prep_gate_w(gate_w):
    w = jnp.zeros((2, LANES, gate_w.shape[2]), f32)
    for d in range(2):
        w = w.at[d, d * GLA_RANK:(d + 1) * GLA_RANK, :].set(gate_w[d])
    hi = w.astype(bf16)
    return hi, (w - hi.astype(f32)).astype(bf16)


def _rope_tables(seq):
    rows = seq // GRID_W
    r, col = jnp.meshgrid(jnp.arange(rows), jnp.arange(GRID_W), indexing='ij')
    r = r.reshape(-1).astype(f32)
    col = col.reshape(-1).astype(f32)
    nf = DIFF_DK // 4
    inv = ROPE_BASE ** (-jnp.arange(nf, dtype=f32) / nf)
    ang = jnp.concatenate([r[:, None] * inv, col[:, None] * inv], axis=-1)
    cos, sin = jnp.cos(ang), jnp.sin(ang)
    reps = LANES // DIFF_DK
    return jnp.tile(jnp.concatenate([cos, cos], axis=-1), (1, reps)), jnp.tile(jnp.concatenate([-sin, sin], axis=-1), (1, reps))


def _const_tables():
    hsum = np.kron(np.eye(NH, dtype=np.float32), np.ones((HEAD_DIM, HEAD_DIM), np.float32))
    expand = np.zeros((LANES, MIX_W), np.float32)
    for h in range(NH):
        expand[h, h * HEAD_DIM:(h + 1) * HEAD_DIM] = 1.0
    return tuple(jnp.asarray(a, dtype=bf16) for a in (hsum, expand))


def _pack_state(s):
    bsz, _, h, a, b = s.shape
    return s.transpose(0, 1, 4, 2, 3).reshape(bsz, 2, b, h * a)


def _unpack_state(st, a):
    bsz, _, b, _ = st.shape
    return st.reshape(bsz, 2, b, NH, a).transpose(0, 1, 3, 4, 2)


def _layer(x, mod, rows_per_mod, lw, layer, consts, bsz, seq, lam_init, r_filt, ctx, rope_tabs, final_g,
           final_norm):
    hsum, expand = consts
    p, hy_t = in_proj(x, mod, lw['norm1_g'], lw['w_tok'], lw['w_hyT'], rows_per_mod)
    if ctx is None:
        gla_s0 = jnp.zeros((bsz, 2, HEAD_DIM, NH * GLA_DK), f32)
        ssd_s0 = jnp.zeros((bsz, 2, SSD_N, MIX_W), f32)
        dctx = None
    else:
        ctx_k, ctx_v, gla_s, ssd_s = ctx
        gla_s0 = _pack_state(gla_s)
        ssd_s0 = _pack_state(ssd_s)
        dctx = (ctx_k.reshape(bsz, -1, MIX_W), ctx_v.reshape(bsz, -1, MIX_W))
    o_gla, gla_fin = gla_mixer(p, lw['gla_gw_hi'], lw['gla_gw_lo'], lw['gla_gb'], lw['gla_ng'], hsum, gla_s0,
                               bsz, seq)
    o_diff = diff_mixer(p, lw['lam_p'], lw['diff_ng'], bsz, seq, lam_init, dctx, rope_tabs)
    o_ssd, ssd_fin = ssd_mixer(p, lw['ssd_conv_w'], lw['ssd_conv_b'], lw['ssd_dtb'], lw['ssd_alog'], lw['ssd_dexp'], lw['ssd_ng'],
                               expand, ssd_s0, bsz, seq)
    o_hy = hyena_mixer(hy_t, r_filt, lw['hy_cw'], lw['hy_cb'], bsz, seq)
    x = out_proj(x, mod, o_gla, o_diff, o_ssd, o_hy, lw['w_out'], layer, rows_per_mod)
    x = mlp(x, mod, lw['norm2_g'], lw['mlp_w1'], lw['mlp_w2'], layer, final_g, rows_per_mod, final_norm)
    return x, p, gla_fin, ssd_fin


def kernel(x_prompt, x_sample, cache_diff_k, cache_diff_v, state_gla, state_ssd, c, c_ctx, ada_w, ada_b, norm1_g, norm2_g, w_in, w_out, gla_gate_w, gla_gate_b, gla_norm_g, diff_lambda, diff_norm_g, ssd_conv_w, ssd_conv_b, ssd_dt_bias, ssd_a_log, ssd_d, ssd_norm_g, hy_conv_w, hy_conv_b, hy_f_w1, hy_f_b1, hy_f_w2, hy_f_b2, hy_f_w3, hy_f_b3, hy_sin_w, hy_decay, hy_skip, mlp_w1, mlp_w2, final_g):
    bp, lp, d = x_prompt.shape
    bs, ls, _ = x_sample.shape
    depth = w_in.shape[0]
    consts = _const_tables()
    rope_tabs = _rope_tables(ls)
    n_c = 16
    cvec = jnp.concatenate([c_ctx[None], c, jnp.zeros((n_c - 1 - bs, d), f32)], axis=0)
    hp = x_prompt.reshape(bp * lp, d)
    hs = x_sample.reshape(bs * ls, d)
    ks_, vs_, gs_, ss_ = [], [], [], []
    w_out_b, mlp_w1_b, mlp_w2_b = w_out.astype(bf16), mlp_w1.astype(bf16), mlp_w2.astype(bf16)
    for l in range(depth):
        w_tok, w_hyT = _prep_w_in(w_in[l])
        pad_l = lambda a: jnp.pad(a, (0, LANES - a.shape[0])).reshape(1, LANES)
        gw_hi, gw_lo = _prep_gate_w(gla_gate_w[l])
        lw = dict(
            norm1_g=norm1_g[l], norm2_g=norm2_g[l], w_tok=w_tok, w_hyT=w_hyT, w_out=w_out_b,
            mlp_w1=mlp_w1_b, mlp_w2=mlp_w2_b,
            gla_gw_hi=gw_hi, gla_gw_lo=gw_lo, gla_gb=gla_gate_b[l].reshape(2, 1, -1), gla_ng=jnp.tile(gla_norm_g[l], NH).reshape(1, MIX_W),
            lam_p=jnp.pad(diff_lambda[l], ((0, 4), (0, LANES - DIFF_DK))),
            diff_ng=jnp.tile(diff_norm_g[l], LANES // HEAD_DIM).reshape(1, LANES),
            ssd_conv_w=ssd_conv_w[l], ssd_conv_b=ssd_conv_b[l],
            ssd_dtb=pad_l(ssd_dt_bias[l].reshape(-1)), ssd_alog=pad_l(ssd_a_log[l].reshape(-1)),
            ssd_dexp=jnp.repeat(ssd_d[l], HEAD_DIM).reshape(1, MIX_W), ssd_ng=ssd_norm_g[l].reshape(1, MIX_W),
            hy_cw=hy_conv_w[l], hy_cb=hy_conv_b[l],
        )
        fp = dict(w1=hy_f_w1[l], b1=hy_f_b1[l], w2=hy_f_w2[l], b2=hy_f_b2[l], w3=hy_f_w3[l], b3=hy_f_b3[l],
                  sin_w=hy_sin_w[l], decay=hy_decay[l], skip=hy_skip[l])
        mod = ada_mod(cvec, ada_w, ada_b, l).reshape(n_c, 6, d)
        lam_init = 0.8 - 0.6 * math.exp(-0.3 * l)
        last = l == depth - 1
        hp, p_p, g_l, s_l = _layer(hp, mod[0:1], bp * lp, lw, l, consts, bp, lp, lam_init, hy_filters(fp, lp),
                                   None, None, final_g, last)
        ks_.append(p_p[T_DK].astype(f32).reshape(bp, lp, NH, 2 * DIFF_DK))
        vs_.append(p_p[T_DV].astype(f32).reshape(bp, lp, NH, HEAD_DIM))
        gs_.append(_unpack_state(g_l, GLA_DK))
        ss_.append(_unpack_state(s_l, HEAD_DIM))
        hs, _, _, _ = _layer(hs, mod[1:1 + bs], ls, lw, l, consts, bs, ls, lam_init, hy_filters(fp, ls),
                             (cache_diff_k[:, l], cache_diff_v[:, l], state_gla[:, l], state_ssd[:, l]),
                             rope_tabs, final_g, last)
    return (hp.reshape(bp, lp, d), hs.reshape(bs, ls, d), jnp.stack(ks_, axis=1), jnp.stack(vs_, axis=1),
            jnp.stack(gs_, axis=1), jnp.stack(ss_, axis=1))
```
